```python
import math
import jax
import jax.numpy as jnp
from jax import lax
import numpy as np

D_MODEL = 1024
BATCH = 8
SEQ = 4096
DEPTH = 4

GRID_W = 64
CTX_LEN = 256
N_BRANCH = 4
BR_W = D_MODEL // N_BRANCH

HY_ORDER = 2
HY_BANDS = 8
HY_FEAT = 1 + 2 * HY_BANDS
HY_HID = 64
HY_CONV = 3
HY_TARGET = 1e-2
HY_FAST = 0.3
HY_SLOW = 1.5
HY_IN = (HY_ORDER + 1) * BR_W

RW_HEAD = 64
RW_HEADS = BR_W // RW_HEAD
RW_LORA = 64
RW_IN = 3 * BR_W + 3 * RW_LORA
RW_SPLIT = (BR_W, 2 * BR_W, 3 * BR_W, 3 * BR_W + RW_LORA, 3 * BR_W + 2 * RW_LORA)
RW_LN_EPS = 64e-5

LRU_BLOCKS = 4
LRU_BLOCK = BR_W // LRU_BLOCKS
LRU_CONV = 4
LRU_C = 8.0
LRU_IN = 2 * BR_W

RET_HEADS = 4
RET_HEAD = BR_W // RET_HEADS
RET_CHUNK = 128
ROPE_BASE = 10000.0
RET_IN = 4 * BR_W

GATE_IN = N_BRANCH * D_MODEL
N_IN = HY_IN + RW_IN + LRU_IN + RET_IN + GATE_IN
SPLIT_IN = (HY_IN, HY_IN + RW_IN, HY_IN + RW_IN + LRU_IN, HY_IN + RW_IN + LRU_IN + RET_IN)

D_FF = ((8 * D_MODEL // 3 + 255) // 256) * 256
EPS = 1e-6

kernel_name = 'hybrid_prefix_dit_trunk'


def rmsnorm(x, g):
    xf = x.astype(jnp.float32)
    y = xf * lax.rsqrt(jnp.mean(xf * xf, axis=-1, keepdims=True) + EPS)
    return (y * g.astype(jnp.float32)).astype(x.dtype)


def head_norm(y, eps=1e-5):
    mu = jnp.mean(y, axis=-1, keepdims=True)
    var = jnp.mean(jnp.square(y - mu), axis=-1, keepdims=True)
    return (y - mu) * lax.rsqrt(var + eps)


def dwconv(x, w, b, pad_left):
    K = w.shape[0]
    T = x.shape[1]
    xp = jnp.pad(x, ((0, 0), (pad_left, K - 1 - pad_left), (0, 0)))
    y = b + w[0] * xp[:, 0:T]
    for j in range(1, K):
        y = y + w[j] * xp[:, j:j + T]
    return y


def hyena_filter_spectrum(L, f_w1, f_b1, f_w2, f_b2, f_w3, freq):
    f32 = jnp.float32
    t = jnp.arange(L, dtype=f32) / L
    ang = (2.0 * math.pi) * t[:, None] * jnp.arange(1, HY_BANDS + 1, dtype=f32)
    feat = jnp.concatenate([t[:, None], jnp.sin(ang), jnp.cos(ang)], axis=-1)
    h = jnp.sin(freq[0].astype(f32) * (feat @ f_w1.astype(f32) + f_b1.astype(f32)))
    h = jnp.sin(freq[1].astype(f32) * (h @ f_w2.astype(f32) + f_b2.astype(f32)))
    h = (h @ f_w3.astype(f32)).reshape(L, HY_ORDER, 2, BR_W)
    rates = jnp.abs(jnp.linspace(math.log(HY_TARGET) / HY_SLOW, math.log(HY_TARGET) / HY_FAST, BR_W, dtype=f32))
    h = h * jnp.exp(-t[:, None] * rates)[:, None, None, :]
    fwd, bwd = h[:, :, 0], h[:, :, 1]
    two = jnp.concatenate([fwd, jnp.zeros((1, HY_ORDER, BR_W), f32), jnp.flip(bwd[1:], 0)], axis=0)
    two = two * lax.rsqrt(jnp.sum(two * two, axis=0, keepdims=True) + EPS)
    return jnp.fft.rfft(two, axis=0)


def hyena_branch(p, conv_w, conv_b, f_w1, f_b1, f_w2, f_b2, f_w3, freq, bias):
    L = p.shape[1]
    u = dwconv(p, conv_w, conv_b, 1).astype(jnp.float32)
    v, g1, g2 = jnp.split(u, 3, axis=-1)
    spec = hyena_filter_spectrum(L, f_w1, f_b1, f_w2, f_b2, f_w3, freq)
    z = v
    for o, gate in enumerate((g1, g2)):
        conv = jnp.fft.irfft(jnp.fft.rfft(z, n=2 * L, axis=1) * spec[:, o], n=2 * L, axis=1)[:, :L]
        z = gate * (conv + bias[o].astype(jnp.float32) * z)
    return z


def shift_grid(x, rows):
    B, T, C = x.shape
    g = x.reshape(B, rows, GRID_W, C // 4, 4)
    left = jnp.pad(g[:, :, :-1, :, 0], ((0, 0), (0, 0), (1, 0), (0, 0)))
    right = jnp.pad(g[:, :, 1:, :, 1], ((0, 0), (0, 0), (0, 1), (0, 0)))
    up = jnp.pad(g[:, :-1, :, :, 2], ((0, 0), (1, 0), (0, 0), (0, 0)))
    down = jnp.pad(g[:, 1:, :, :, 3], ((0, 0), (0, 1), (0, 0), (0, 0)))
    return jnp.stack([left, right, up, down], axis=-1).reshape(B, T, C)


def shift_seq(x):
    B, T, C = x.shape
    g = x.reshape(B, T, C // 4, 4)
    prev = jnp.pad(g[:, :-1], ((0, 0), (1, 0), (0, 0), (0, 0)))
    nxt = jnp.pad(g[:, 1:], ((0, 0), (0, 1), (0, 0), (0, 0)))
    return jnp.stack([prev[..., 0], nxt[..., 1], prev[..., 2], nxt[..., 3]], axis=-1).reshape(B, T, C)


def wkv_scan(r, w, k, v, a, b, s0):
    def step(s, inp):
        rt, wt, kt, vt, at, bt = inp
        sa = jnp.einsum('bhvk,bhk->bhv', s, at)
        s = s * wt[:, :, None, :] + sa[..., None] * bt[:, :, None, :] + vt[..., None] * kt[:, :, None, :]
        return s, jnp.einsum('bhvk,bhk->bhv', s, rt)
    xs = tuple(jnp.moveaxis(t, 1, 0) for t in (r, w, k, v, a, b))
    s_fin, y = lax.scan(step, s0, xs)
    return jnp.moveaxis(y, 0, 1), s_fin


def rwkv_branch(p, rows, states, mu, w0, w2, a0, a2, g2, k_k, k_a, r_k, ln_g):
    B, T, _ = p.shape
    shifted = shift_grid(p, rows) if rows is not None else shift_seq(p)
    xx = (p + (shifted - p) * mu).astype(jnp.float32)
    r, k, v, cw, ca, cg = jnp.split(xx, RW_SPLIT, axis=-1)
    hd = lambda t: t.reshape(B, T, RW_HEADS, RW_HEAD)
    kk = hd(k * k_k)
    kk = kk * lax.rsqrt(jnp.sum(kk * kk, axis=-1, keepdims=True) + 1e-12)
    ys, fins = [], []
    for d in range(2):
        logw = -jax.nn.softplus(-(w0[d] + jnp.tanh(cw) @ w2[d])) - 0.5
        decay = jnp.exp(-jnp.exp(logw))
        a = jax.nn.sigmoid(a0[d] + ca @ a2[d])
        k_t = k * (1.0 + (a - 1.0) * k_a)
        ins = (hd(r), hd(decay), hd(k_t), hd(v), -kk, kk * hd(a))
        if d == 1:
            ins = tuple(jnp.flip(t, 1) for t in ins)
        y_d, s_d = wkv_scan(*ins, states[d])
        ys.append(jnp.flip(y_d, 1) if d == 1 else y_d)
        fins.append(s_d)
    o = head_norm(ys[0] + ys[1], RW_LN_EPS) * ln_g.reshape(RW_HEADS, RW_HEAD)
    o = o + jnp.sum(hd(r) * hd(k) * r_k.reshape(RW_HEADS, RW_HEAD), axis=-1, keepdims=True) * hd(v)
    g = jax.nn.sigmoid(cg) @ g2
    return o.reshape(B, T, BR_W) * g, (fins[0], fins[1])


def linear_scan(a, b, h0):
    b = b.at[:, 0].add(a[:, 0] * h0)
    comb = lambda lhs, rhs: (lhs[0] * rhs[0], rhs[0] * lhs[1] + rhs[1])
    _, h = lax.associative_scan(comb, (a, b), axis=1)
    return h, h[:, -1]


def rglru_branch(p, states, conv_w, conv_b, wa, ba, wx, bx, lam):
    B, T, _ = p.shape
    xb, gb = jnp.split(p.astype(jnp.float32), 2, axis=-1)
    xc = dwconv(xb, conv_w, conv_b, LRU_CONV // 2)
    xg = xc.reshape(B, T, LRU_BLOCKS, LRU_BLOCK)
    hs, fins = [], []
    for d in range(2):
        r = jax.nn.sigmoid(jnp.einsum('btgi,gio->btgo', xg, wa[d].astype(jnp.float32)).reshape(B, T, BR_W) + ba[d])
        i = jax.nn.sigmoid(jnp.einsum('btgi,gio->btgo', xg, wx[d].astype(jnp.float32)).reshape(B, T, BR_W) + bx[d])
        log_a = -LRU_C * r * jax.nn.softplus(-lam[d].astype(jnp.float32))
        a = jnp.exp(log_a)
        b = jnp.sqrt(-jnp.expm1(2.0 * log_a)) * (i * xc)
        if d == 1:
            a, b = jnp.flip(a, 1), jnp.flip(b, 1)
        h, h_fin = linear_scan(a, b, states[d])
        hs.append(jnp.flip(h, 1) if d == 1 else h)
        fins.append(h_fin)
    return (hs[0] + hs[1]) * jax.nn.gelu(gb), (fins[0], fins[1])


def axial_rope(x):
    T = x.shape[1]
    pos = jnp.arange(T)
    q4 = RET_HEAD // 4
    half = RET_HEAD // 2
    inv = ROPE_BASE ** (-jnp.arange(q4, dtype=jnp.float32) / q4)
    def rot(xp, coord):
        ang = coord.astype(jnp.float32)[:, None] * inv
        cos, sin = jnp.cos(ang)[:, None, :], jnp.sin(ang)[:, None, :]
        x1, x2 = xp[..., :q4], xp[..., q4:]
        return jnp.concatenate([x1 * cos - x2 * sin, x1 * sin + x2 * cos], axis=-1)
    return jnp.concatenate([rot(x[..., :half], pos // GRID_W), rot(x[..., half:], pos % GRID_W)], axis=-1)


def retention_scan(q, k, v, log_g, s0):
    B, H, T, _ = q.shape
    C = RET_CHUNK
    n = T // C
    idx = jnp.arange(C, dtype=jnp.float32)
    diff = idx[:, None] - idx[None, :]
    dmat = jnp.exp(jnp.where(diff >= 0, diff * log_g[:, None, None], -jnp.inf))
    q_dec = jnp.exp((idx + 1.0) * log_g[:, None])[..., None]
    k_dec = jnp.exp((C - 1.0 - idx) * log_g[:, None])[..., None]
    c_dec = jnp.exp(C * log_g)[:, None, None]
    chunks = lambda t: jnp.moveaxis(t.reshape(B, H, n, C, t.shape[-1]), 2, 0)
    def step(s, inp):
        qc, kc, vc = inp
        sc = jnp.einsum('bhid,bhjd->bhij', qc, kc) * dmat
        y = jnp.einsum('bhij,bhjv->bhiv', sc, vc) + jnp.einsum('bhid,bhdv->bhiv', qc, s) * q_dec
        s = s * c_dec + jnp.einsum('bhjd,bhjv->bhdv', kc * k_dec, vc)
        return s, y
    s_fin, y = lax.scan(step, s0, (chunks(q), chunks(k), chunks(v)))
    return jnp.moveaxis(y, 0, 2).reshape(B, H, T, v.shape[-1]), s_fin


def retention_branch(p, rows, states, gamma_logit):
    B, T, _ = p.shape
    q, k, v, g = jnp.split(p.astype(jnp.float32), 4, axis=-1)
    hd = lambda t: t.reshape(B, T, RET_HEADS, RET_HEAD)
    q, k, v = hd(q), hd(k), hd(v)
    if rows is not None:
        q, k = axial_rope(q), axial_rope(k)
    k = k * (RET_HEAD ** -0.5)
    q, k, v = (jnp.swapaxes(t, 1, 2) for t in (q, k, v))
    log_g = jax.nn.log_sigmoid(gamma_logit.astype(jnp.float32))
    y_f, s_f = retention_scan(q, k, v, log_g[0], states[0])
    y_b, s_b = retention_scan(jnp.flip(q, 2), jnp.flip(k, 2), jnp.flip(v, 2), log_g[1], states[1])
    y = head_norm(jnp.swapaxes(y_f + jnp.flip(y_b, 2), 1, 2)).reshape(B, T, BR_W)
    return y * jax.nn.silu(g), (s_f, s_b)


def zero_states(batch):
    f32 = jnp.float32
    rw = jnp.zeros((batch, RW_HEADS, RW_HEAD, RW_HEAD), f32)
    lru = jnp.zeros((batch, BR_W), f32)
    ret = jnp.zeros((batch, RET_HEADS, RET_HEAD, RET_HEAD), f32)
    return ((rw, rw), (lru, lru), (ret, ret))


def token_mixer(p, rows, states, lp, with_output):
    hy_p, rw_p, lru_p, ret_p, gate_p = jnp.split(p, SPLIT_IN, axis=-1)
    y_rw, rw_st = rwkv_branch(rw_p, rows, states[0], *lp['rw'])
    y_lru, lru_st = rglru_branch(lru_p, states[1], *lp['lru'])
    y_ret, ret_st = retention_branch(ret_p, rows, states[2], lp['ret'])
    new_states = (rw_st, lru_st, ret_st)
    if not with_output:
        return None, new_states
    y_hy = hyena_branch(hy_p, *lp['hy'])
    B, T, _ = p.shape
    gates = jax.nn.sigmoid(gate_p.astype(jnp.float32)).reshape(B, T, N_BRANCH, D_MODEL)
    br = lp['br_proj'].astype(jnp.float32)
    branches = (y_hy, y_rw, y_lru, y_ret)
    merged = gates[:, :, 0] * (branches[0] @ br[0])
    for n in range(1, N_BRANCH):
        merged = merged + gates[:, :, n] * (branches[n] @ br[n])
    return merged.astype(p.dtype) @ lp['w_out'], new_states


def swiglu(u, w1, w2):
    gate, up = jnp.split(u @ w1, 2, axis=-1)
    return (jax.nn.silu(gate) * up) @ w2


def setup_inputs(seed: int = 0) -> dict:
    key = jax.random.key(seed)
    ks = iter(jax.random.split(key, 48))
    f32 = jnp.float32
    L = DEPTH
    def nrm(shape, scale):
        return scale * jax.random.normal(next(ks), shape, f32)
    def unif(shape, lo, hi):
        return jax.random.uniform(next(ks), shape, f32, lo, hi)
    x = nrm((BATCH, SEQ, D_MODEL), 1.0)
    c = nrm((BATCH, D_MODEL), 1.0)
    ctx = nrm((BATCH, CTX_LEN, D_MODEL), 1.0)
    c_ctx = nrm((D_MODEL,), 1.0)
    w_mod = nrm((L, D_MODEL, 6 * D_MODEL), 0.5 * D_MODEL ** -0.5)
    b_mod = nrm((L, 6 * D_MODEL), 0.02)
    norm1_g = 1.0 + nrm((L, D_MODEL), 0.02)
    norm2_g = 1.0 + nrm((L, D_MODEL), 0.02)
    w_in = nrm((L, D_MODEL, N_IN), D_MODEL ** -0.5)
    hy_conv_w = nrm((L, HY_CONV, HY_IN), HY_CONV ** -0.5)
    hy_conv_b = nrm((L, HY_IN), 0.02)
    hy_f_w1 = nrm((L, HY_FEAT, HY_HID), HY_FEAT ** -0.5)
    hy_f_b1 = nrm((L, HY_HID), 0.1)
    hy_f_w2 = nrm((L, HY_HID, HY_HID), HY_HID ** -0.5)
    hy_f_b2 = nrm((L, HY_HID), 0.1)
    hy_f_w3 = nrm((L, HY_HID, HY_ORDER * 2 * BR_W), HY_HID ** -0.5)
    hy_freq = 1.0 + nrm((L, 2, HY_HID), 0.02)
    hy_bias = nrm((L, HY_ORDER, BR_W), 0.5)
    rw_mu = unif((L, RW_IN), 0.0, 1.0)
    rw_w0 = jnp.linspace(-5.0, 1.0, BR_W, dtype=f32) + nrm((L, 2, BR_W), 0.1)
    rw_w2 = nrm((L, 2, RW_LORA, BR_W), 0.1 * RW_LORA ** -0.5)
    rw_a0 = nrm((L, 2, BR_W), 0.1)
    rw_a2 = nrm((L, 2, RW_LORA, BR_W), 0.5 * RW_LORA ** -0.5)
    rw_g2 = nrm((L, RW_LORA, BR_W), RW_LORA ** -0.5)
    rw_kk = 0.85 + nrm((L, BR_W), 0.02)
    rw_ka = 1.0 + nrm((L, BR_W), 0.02)
    rw_rk = nrm((L, BR_W), 0.1)
    rw_ln_g = 1.0 + nrm((L, BR_W), 0.02)
    lru_conv_w = nrm((L, LRU_CONV, BR_W), LRU_CONV ** -0.5)
    lru_conv_b = nrm((L, BR_W), 0.02)
    lru_wa = nrm((L, 2, LRU_BLOCKS, LRU_BLOCK, LRU_BLOCK), LRU_BLOCK ** -0.5)
    lru_ba = nrm((L, 2, BR_W), 0.02)
    lru_wx = nrm((L, 2, LRU_BLOCKS, LRU_BLOCK, LRU_BLOCK), LRU_BLOCK ** -0.5)
    lru_bx = nrm((L, 2, BR_W), 0.02)
    a_pow_c = unif((L, 2, BR_W), 0.9, 0.999)
    a_base = a_pow_c ** (1.0 / LRU_C)
    lru_lam = jnp.log(a_base) - jnp.log1p(-a_base)
    ret_base = 1.0 - 2.0 ** (-5.0 - jnp.arange(RET_HEADS, dtype=f32))
    ret_gamma = jnp.log(ret_base) - jnp.log1p(-ret_base) + nrm((L, 2, RET_HEADS), 0.1)
    br_proj = nrm((L, N_BRANCH, BR_W, D_MODEL), BR_W ** -0.5)
    w_out = nrm((L, D_MODEL, D_MODEL), D_MODEL ** -0.5)
    ffn_w1 = nrm((L, D_MODEL, 2 * D_FF), D_MODEL ** -0.5)
    ffn_w2 = nrm((L, D_FF, D_MODEL), D_FF ** -0.5)
    final_g = 1.0 + nrm((D_MODEL,), 0.02)
    return {'x': x, 'c': c, 'ctx': ctx, 'c_ctx': c_ctx, 'w_mod': w_mod, 'b_mod': b_mod,
            'norm1_g': norm1_g, 'norm2_g': norm2_g, 'w_in': w_in,
            'hy_conv_w': hy_conv_w, 'hy_conv_b': hy_conv_b, 'hy_f_w1': hy_f_w1, 'hy_f_b1': hy_f_b1,
            'hy_f_w2': hy_f_w2, 'hy_f_b2': hy_f_b2, 'hy_f_w3': hy_f_w3, 'hy_freq': hy_freq, 'hy_bias': hy_bias,
            'rw_mu': rw_mu, 'rw_w0': rw_w0, 'rw_w2': rw_w2, 'rw_a0': rw_a0, 'rw_a2': rw_a2, 'rw_g2': rw_g2,
            'rw_kk': rw_kk, 'rw_ka': rw_ka, 'rw_rk': rw_rk, 'rw_ln_g': rw_ln_g,
            'lru_conv_w': lru_conv_w, 'lru_conv_b': lru_conv_b, 'lru_wa': lru_wa, 'lru_ba': lru_ba,
            'lru_wx': lru_wx, 'lru_bx': lru_bx, 'lru_lam': lru_lam, 'ret_gamma': ret_gamma,
            'br_proj': br_proj, 'w_out': w_out, 'ffn_w1': ffn_w1, 'ffn_w2': ffn_w2, 'final_g': final_g}


def reference(x, c, ctx, c_ctx, w_mod, b_mod, norm1_g, norm2_g, w_in,
              hy_conv_w, hy_conv_b, hy_f_w1, hy_f_b1, hy_f_w2, hy_f_b2, hy_f_w3, hy_freq, hy_bias,
              rw_mu, rw_w0, rw_w2, rw_a0, rw_a2, rw_g2, rw_kk, rw_ka, rw_rk, rw_ln_g,
              lru_conv_w, lru_conv_b, lru_wa, lru_ba, lru_wx, lru_bx, lru_lam, ret_gamma,
              br_proj, w_out, ffn_w1, ffn_w2, final_g):
    rows = x.shape[1] // GRID_W
    xc = ctx
    for l in range(DEPTH):
        last = l == DEPTH - 1
        lp = {
            'hy': (hy_conv_w[l], hy_conv_b[l], hy_f_w1[l], hy_f_b1[l], hy_f_w2[l], hy_f_b2[l],
                   hy_f_w3[l], hy_freq[l], hy_bias[l]),
            'rw': (rw_mu[l], rw_w0[l], rw_w2[l], rw_a0[l], rw_a2[l], rw_g2[l], rw_kk[l], rw_ka[l],
                   rw_rk[l], rw_ln_g[l]),
            'lru': (lru_conv_w[l], lru_conv_b[l], lru_wa[l], lru_ba[l], lru_wx[l], lru_bx[l], lru_lam[l]),
            'ret': ret_gamma[l],
            'br_proj': br_proj[l],
            'w_out': w_out[l],
        }
        m_lat = jnp.split((jax.nn.silu(c) @ w_mod[l] + b_mod[l])[:, None, :], 6, axis=-1)
        m_ctx = jnp.split(jax.nn.silu(c_ctx) @ w_mod[l] + b_mod[l], 6, axis=-1)
        u_c = rmsnorm(xc, norm1_g[l]) * (1.0 + m_ctx[1]) + m_ctx[0]
        h_c, ctx_states = token_mixer(u_c @ w_in[l], None, zero_states(x.shape[0]), lp, not last)
        u_l = rmsnorm(x, norm1_g[l]) * (1.0 + m_lat[1]) + m_lat[0]
        h_l, _ = token_mixer(u_l @ w_in[l], rows, ctx_states, lp, True)
        x = x + m_lat[2] * h_l
        v_l = rmsnorm(x, norm2_g[l]) * (1.0 + m_lat[4]) + m_lat[3]
        x = x + m_lat[5] * swiglu(v_l, ffn_w1[l], ffn_w2[l])
        if not last:
            xc = xc + m_ctx[2] * h_c
            v_c = rmsnorm(xc, norm2_g[l]) * (1.0 + m_ctx[4]) + m_ctx[3]
            xc = xc + m_ctx[5] * swiglu(v_c, ffn_w1[l], ffn_w2[l])
    return rmsnorm(x, final_g)
```

```python
import functools
import math

import numpy as np
import jax
import jax.numpy as jnp
from jax import lax
from jax.experimental import pallas as pl
from jax.experimental.pallas import tpu as pltpu

F32 = jnp.float32
BF16 = jnp.bfloat16

D_MODEL = 1024
DEPTH = 4
GRID_W = 64
N_BRANCH = 4
BR_W = D_MODEL // N_BRANCH

HY_BANDS = 8
HY_FEAT = 1 + 2 * HY_BANDS
HY_FEAT_PAD = 32
HY_HID = 64
HY_TARGET = 1e-2
HY_FAST = 0.3
HY_SLOW = 1.5
HY_IN = 3 * BR_W

RW_HEAD = 64
RW_HEADS = BR_W // RW_HEAD
RW_LORA = 64
RW_IN = 3 * BR_W + 3 * RW_LORA
RW_PAD = 4 * BR_W
RW_LN_EPS = 64e-5

LRU_BLOCKS = 4
LRU_BLOCK = BR_W // LRU_BLOCKS
LRU_C = 8.0
LRU_IN = 2 * BR_W

RET_HEADS = 4
RET_HEAD = BR_W // RET_HEADS
RET_CHUNK = 128
ROPE_BASE = 10000.0
RET_IN = 4 * BR_W
RET_LN_EPS = 1e-5

GATE_IN = N_BRANCH * D_MODEL
D_FF = ((8 * D_MODEL // 3 + 255) // 256) * 256
EPS = 1e-6

OFF_RW = 0
OFF_RET = OFF_RW + RW_PAD
OFF_LRU = OFF_RET + RET_IN
OFF_HY = OFF_LRU + LRU_IN
N_BR = OFF_HY + HY_IN

VMEM_LIMIT = 56 * 1024 * 1024
HI = lax.Precision.HIGHEST


def _cparams(*sem):
    return pltpu.CompilerParams(dimension_semantics=sem, vmem_limit_bytes=VMEM_LIMIT)


def _const_spec(shape):
    nd = len(shape)
    return pl.BlockSpec(shape, lambda *_: (0,) * nd, pipeline_mode=pl.Buffered(1))


def _dot(a, b, **kw):
    return jnp.dot(a, b, preferred_element_type=F32, **kw)


def _norm_mod(x, g, sh, sc):
    ms = jnp.mean(x * x, axis=-1, keepdims=True)
    return x * lax.rsqrt(ms + EPS) * g * (1.0 + sc) + sh


def _sigmoid(x):
    return 1.0 / (1.0 + jnp.exp(-x))


def _silu(x):
    return x * _sigmoid(x)


def _softplus(x):
    return jnp.maximum(x, 0.0) + jnp.log(1.0 + jnp.exp(-jnp.abs(x)))


def _mod_kernel(c_ref, w_ref, b_ref, o_ref):
    c = c_ref[...]
    o_ref[0] = _dot(_silu(c).astype(BF16), w_ref[0]) + b_ref[0]


def modulation(cc, w_mod, b_mod):
    L, D, N = w_mod.shape
    tn = 1536
    return pl.pallas_call(
        _mod_kernel,
        grid=(L, N // tn),
        in_specs=[pl.BlockSpec((16, D), lambda l, j: (0, 0)),
                  pl.BlockSpec((1, D, tn), lambda l, j: (l, 0, j)),
                  pl.BlockSpec((1, 1, tn), lambda l, j: (l, 0, j))],
        out_specs=pl.BlockSpec((1, 16, tn), lambda l, j: (l, 0, j)),
        out_shape=jax.ShapeDtypeStruct((L, 16, N), F32),
        compiler_params=_cparams("parallel", "parallel"),
        name="modulation",
    )(cc, w_mod, b_mod)


def _inproj_kernel(x_ref, g_ref, sh_ref, sc_ref, w_ref, o_ref):
    u = _norm_mod(x_ref[0], g_ref[...], sh_ref[0], sc_ref[0]).astype(BF16)
    n = w_ref.shape[1]
    cw = 256
    for c in range(n // cw):
        o_ref[0, :, c * cw:(c + 1) * cw] = _dot(u, w_ref[:, c * cw:(c + 1) * cw])


def in_projection(x, g, sh, sc, w, tm):
    B, T, D = x.shape
    N = w.shape[1]
    vec = pl.BlockSpec((1, 1, D), lambda b, i: (b, 0, 0))
    return pl.pallas_call(
        _inproj_kernel,
        grid=(B, T // tm),
        in_specs=[pl.BlockSpec((1, tm, D), lambda b, i: (b, i, 0)),
                  _const_spec((1, D)), vec, vec, _const_spec((D, N))],
        out_specs=pl.BlockSpec((1, tm, N), lambda b, i: (b, i, 0)),
        out_shape=jax.ShapeDtypeStruct((B, T, N), F32),
        compiler_params=_cparams("parallel", "parallel"),
        name="in_projection",
    )(x, g, sh, sc, w)


def _ffn1_kernel(x_ref, g_ref, sh_ref, sc_ref, wg_ref, wu_ref, o_ref):
    u = _norm_mod(x_ref[0], g_ref[...], sh_ref[0], sc_ref[0]).astype(BF16)
    n = wg_ref.shape[1]
    cw = 256
    for c in range(n // cw):
        sl = slice(c * cw, (c + 1) * cw)
        gate = _dot(u, wg_ref[:, sl])
        up = _dot(u, wu_ref[:, sl])
        o_ref[0, :, sl] = (_silu(gate) * up).astype(BF16)


def ffn_up(x, g, sh, sc, w_gate, w_up, tm):
    B, T, D = x.shape
    N = w_gate.shape[1]
    vec = pl.BlockSpec((1, 1, D), lambda b, i: (b, 0, 0))
    return pl.pallas_call(
        _ffn1_kernel,
        grid=(B, T // tm),
        in_specs=[pl.BlockSpec((1, tm, D), lambda b, i: (b, i, 0)),
                  _const_spec((1, D)), vec, vec, _const_spec((D, N)), _const_spec((D, N))],
        out_specs=pl.BlockSpec((1, tm, N), lambda b, i: (b, i, 0)),
        out_shape=jax.ShapeDtypeStruct((B, T, N), BF16),
        compiler_params=_cparams("parallel", "parallel"),
        name="ffn_up",
    )(x, g, sh, sc, w_gate, w_up)


def _ffn2_kernel(h_ref, x_ref, gt_ref, w_ref, fg_ref, o_ref, *, final_norm):
    y = x_ref[0] + gt_ref[0] * _dot(h_ref[0], w_ref[...])
    if final_norm:
        ms = jnp.mean(y * y, axis=-1, keepdims=True)
        y = y * lax.rsqrt(ms + EPS) * fg_ref[...]
    o_ref[0] = y


def ffn_down(h, x, gate, w, final_g, final_norm, tm):
    B, T, D = x.shape
    N = h.shape[2]
    return pl.pallas_call(
        functools.partial(_ffn2_kernel, final_norm=final_norm),
        grid=(B, T // tm),
        in_specs=[pl.BlockSpec((1, tm, N), lambda b, i: (b, i, 0)),
                  pl.BlockSpec((1, tm, D), lambda b, i: (b, i, 0)),
                  pl.BlockSpec((1, 1, D), lambda b, i: (b, 0, 0)),
                  _const_spec((N, D)), _const_spec((1, D))],
        out_specs=pl.BlockSpec((1, tm, D), lambda b, i: (b, i, 0)),
        out_shape=jax.ShapeDtypeStruct((B, T, D), F32),
        compiler_params=_cparams("parallel", "parallel"),
        name="ffn_down",
    )(h, x, gate, w, final_g)


def _halo_specs(tm, T, width, col_fn, halo):
    r = tm // halo
    last = T // halo - 1
    main = pl.BlockSpec((1, tm, width), lambda *g: (g[0], g[1], col_fn(*g)))
    prev = pl.BlockSpec((1, halo, width), lambda *g: (g[0], jnp.maximum(g[1] * r - 1, 0), col_fn(*g)))
    nxt = pl.BlockSpec((1, halo, width), lambda *g: (g[0], jnp.minimum((g[1] + 1) * r, last), col_fn(*g)))
    return main, prev, nxt


def _shift_rows(x, prev, nxt, s, first, last):
    tm = x.shape[0]
    if s > 0:
        head = jnp.where(first, 0.0, prev[prev.shape[0] - s:, :])
        return jnp.concatenate([head, x[:tm - s, :]], axis=0)
    s = -s
    tail = jnp.where(last, 0.0, nxt[:s, :])
    return jnp.concatenate([x[s:, :], tail], axis=0)


def _hy_prep_kernel(p_ref, pp_ref, pn_ref, w_ref, b_ref, o_ref):
    i = pl.program_id(1)
    first = i == 0
    last = i == pl.num_programs(1) - 1
    x = p_ref[0]
    xm = _shift_rows(x, pp_ref[0], pn_ref[0], 1, first, last)
    xp = _shift_rows(x, pp_ref[0], pn_ref[0], -1, first, last)
    o_ref[0, 0] = b_ref[...] + w_ref[0:1, :] * xm + w_ref[1:2, :] * x + w_ref[2:3, :] * xp


def hyena_prep(p, conv_w, conv_b, tm):
    B, T, _ = p.shape
    main, prev, nxt = _halo_specs(tm, T, BR_W, lambda b, i, j: OFF_HY // BR_W + j, 8)
    return pl.pallas_call(
        _hy_prep_kernel,
        grid=(B, T // tm, 3),
        in_specs=[main, prev, nxt, pl.BlockSpec((3, BR_W), lambda b, i, j: (0, j)),
                  pl.BlockSpec((1, BR_W), lambda b, i, j: (0, j))],
        out_specs=pl.BlockSpec((1, 1, tm, BR_W), lambda b, i, j: (j, b, i, 0)),
        out_shape=jax.ShapeDtypeStruct((3, B, T, BR_W), F32),
        compiler_params=_cparams("parallel", "parallel", "parallel"),
        name="hyena_prep",
    )(p, p, p, conv_w, conv_b)


def _hy_filter_kernel(feat_ref, w1_ref, b1_ref, w2_ref, b2_ref, w3_ref, fq_ref, rates_ref, h_ref, ss_ref):
    i = pl.program_id(0)
    feat = feat_ref[...]
    t = feat[:, 0:1]
    h = jnp.sin(fq_ref[0:1, :] * (_dot(feat, w1_ref[...], precision=HI) + b1_ref[...]))
    h = jnp.sin(fq_ref[1:2, :] * (_dot(h, w2_ref[...], precision=HI) + b2_ref[...]))
    h = _dot(h, w3_ref[...], precision=HI) * jnp.exp(-t * rates_ref[...])
    row = lax.broadcasted_iota(jnp.int32, h.shape, 0) + i * h.shape[0]
    col = lax.broadcasted_iota(jnp.int32, h.shape, 1)
    h = jnp.where((row == 0) & ((col // BR_W) % 2 == 1), 0.0, h)
    h_ref[...] = h

    @pl.when(i == 0)
    def _():
        ss_ref[...] = jnp.zeros_like(ss_ref)

    ss_ref[...] += jnp.sum(h * h, axis=0, keepdims=True)


def hyena_filter(feat, w1, b1, w2, b2, w3, freq, rates, tl):
    L = feat.shape[0]
    C = w3.shape[1]
    return pl.pallas_call(
        _hy_filter_kernel,
        grid=(L // tl,),
        in_specs=[pl.BlockSpec((tl, HY_FEAT_PAD), lambda i: (i, 0)),
                  _const_spec((HY_FEAT_PAD, HY_HID)), _const_spec((1, HY_HID)),
                  _const_spec((HY_HID, HY_HID)), _const_spec((1, HY_HID)),
                  _const_spec((HY_HID, C)), _const_spec((2, HY_HID)), _const_spec((1, C))],
        out_specs=[pl.BlockSpec((tl, C), lambda i: (i, 0)), pl.BlockSpec((1, C), lambda i: (0, 0))],
        out_shape=[jax.ShapeDtypeStruct((L, C), F32), jax.ShapeDtypeStruct((1, C), F32)],
        compiler_params=_cparams("arbitrary"),
        name="hyena_filter",
    )(feat, w1, b1, w2, b2, w3, freq, rates)


def _filter_scale(ss_ref, o):
    e = ss_ref[:, 2 * o * BR_W:(2 * o + 1) * BR_W] + ss_ref[:, (2 * o + 1) * BR_W:(2 * o + 2) * BR_W]
    return lax.rsqrt(e + EPS)


def _combine_spectrum(x, ss_ref, o, half):
    xf = x[:, 2 * o * BR_W:(2 * o + 1) * BR_W]
    xb = x[:, (2 * o + 1) * BR_W:(2 * o + 2) * BR_W]
    sc = _filter_scale(ss_ref, o)
    hr = (xf[:half] + xb[:half]) * sc
    hi = (xf[half:] - xb[half:]) * sc
    return jnp.concatenate([hr, hi], axis=0)


def _cmul(x, h, half):
    xr, xi = x[:half], x[half:]
    hr, hi = h[:half], h[half:]
    return jnp.concatenate([xr * hr - xi * hi, xr * hi + xi * hr], axis=0)


def _dft1_kernel(z_ref, f_ref, a_ref):
    a_ref[0] = _dot(f_ref[...], z_ref[0].astype(BF16)).astype(BF16)


def dft_stage1(z, f1, tn):
    B, n1, W = z.shape
    M = f1.shape[0]
    return pl.pallas_call(
        _dft1_kernel,
        grid=(B, W // tn),
        in_specs=[pl.BlockSpec((1, n1, tn), lambda b, j: (b, 0, j)), _const_spec((M, n1))],
        out_specs=pl.BlockSpec((1, M, tn), lambda b, j: (b, 0, j)),
        out_shape=jax.ShapeDtypeStruct((B, M, W), BF16),
        compiler_params=_cparams("parallel", "parallel"),
        name="dft_stage1",
    )(z, f1)


def _spec2_kernel(a_ref, g_ref, ss_ref, h_ref):
    kb = g_ref.shape[0]
    n2 = a_ref.shape[3]
    for k in range(kb):
        a = a_ref[0, :, k].reshape(2 * n2, a_ref.shape[4])
        x = _dot(g_ref[k], a)
        for o in range(2):
            h_ref[o, k] = _combine_spectrum(x, ss_ref, o, n2)


def filter_spectrum(a, g, ss, kb):
    _, _, N1, N2, C = a.shape
    return pl.pallas_call(
        _spec2_kernel,
        grid=(N1 // kb,),
        in_specs=[pl.BlockSpec((1, 2, kb, N2, C), lambda i: (0, 0, i, 0, 0)),
                  pl.BlockSpec((kb, 2 * N2, 2 * N2), lambda i: (i, 0, 0)),
                  _const_spec((1, C))],
        out_specs=pl.BlockSpec((2, kb, 2 * N2, BR_W), lambda i: (0, i, 0, 0)),
        out_shape=jax.ShapeDtypeStruct((2, N1, 2 * N2, BR_W), F32),
        compiler_params=_cparams("parallel"),
        name="filter_spectrum",
    )(a, g, ss)


def _conv2_kernel(a_ref, g_ref, gi_ref, h_ref, o_ref):
    kb = g_ref.shape[0]
    n2 = a_ref.shape[3]
    C = a_ref.shape[4]
    for k in range(kb):
        a = a_ref[0, :, k].reshape(2 * n2, C)
        y = _cmul(_dot(g_ref[k], a), h_ref[0, k], n2).astype(BF16)
        o_ref[0, :, k] = _dot(gi_ref[k], y).astype(BF16).reshape(2, n2, C)


def spectral_multiply(a, g, gi, h, o, kb):
    B, _, N1, N2, C = a.shape
    blk = pl.BlockSpec((1, 2, kb, N2, C), lambda i, b: (b, 0, i, 0, 0))
    mat = pl.BlockSpec((kb, 2 * N2, 2 * N2), lambda i, b: (i, 0, 0))
    return pl.pallas_call(
        _conv2_kernel,
        grid=(N1 // kb, B),
        in_specs=[blk, mat, mat, pl.BlockSpec((1, kb, 2 * N2, C), lambda i, b: (o, i, 0, 0))],
        out_specs=blk,
        out_shape=jax.ShapeDtypeStruct(a.shape, BF16),
        compiler_params=_cparams("parallel", "parallel"),
        name="spectral_multiply",
    )(a, g, gi, h)


def _idft1_kernel(b_ref, f_ref, z_ref, gate_ref, bias_ref, o_ref):
    y = _dot(f_ref[...], b_ref[0])
    z = z_ref[0]
    o_ref[0] = gate_ref[0] * (y + bias_ref[...] * z)


def idft_stage1(bm, fi, z, gate, bias, tn):
    B, M, W = bm.shape
    n1 = fi.shape[0]
    blk = pl.BlockSpec((1, n1, tn), lambda b, j: (b, 0, j))
    return pl.pallas_call(
        _idft1_kernel,
        grid=(B, W // tn),
        in_specs=[pl.BlockSpec((1, M, tn), lambda b, j: (b, 0, j)), _const_spec((n1, M)),
                  blk, blk, _const_spec((1, tn))],
        out_specs=blk,
        out_shape=jax.ShapeDtypeStruct((B, n1, W), F32),
        compiler_params=_cparams("parallel", "parallel"),
        name="idft_stage1",
    )(bm, fi, z, gate, bias)


@functools.lru_cache(maxsize=None)
def _dft_tables(L):
    N = 2 * L
    N2 = 128
    N1 = N // N2
    nz = L // N2
    k1 = np.arange(N1)[:, None]
    n1 = np.arange(nz)[None, :]
    th = 2 * np.pi * ((k1 * n1) % N1) / N1
    f1 = np.concatenate([np.cos(th), -np.sin(th)], axis=0)
    fi = np.concatenate([np.cos(th).T, -np.sin(th).T], axis=1) / N
    kk1 = np.arange(N1)[:, None, None]
    k2 = np.arange(N2)[None, :, None]
    n2 = np.arange(N2)[None, None, :]
    ph = 2 * np.pi * ((n2 * k2 * N1 + n2 * kk1) % N) / N
    gr, gim = np.cos(ph), -np.sin(ph)
    g = np.concatenate([np.concatenate([gr, -gim], axis=2), np.concatenate([gim, gr], axis=2)], axis=1)
    hr, him = np.swapaxes(gr, 1, 2), -np.swapaxes(gim, 1, 2)
    gi = np.concatenate([np.concatenate([hr, -him], axis=2), np.concatenate([him, hr], axis=2)], axis=1)
    return tuple(np.asarray(t, np.float32) for t in (f1, fi, g, gi))


def _spec_direct_kernel(hf_ref, f_ref, ss_ref, h_ref):
    x = _dot(f_ref[...], hf_ref[...].astype(BF16))
    half = x.shape[0] // 2
    for o in range(2):
        h_ref[o] = _combine_spectrum(x, ss_ref, o, half)


def filter_spectrum_direct(hf, f, ss):
    L, C = hf.shape
    return pl.pallas_call(
        _spec_direct_kernel,
        grid=(1,),
        in_specs=[_const_spec((L, C)), _const_spec((4 * L, L)), _const_spec((1, C))],
        out_specs=pl.BlockSpec((2, 4 * L, BR_W), lambda i: (0, 0, 0)),
        out_shape=jax.ShapeDtypeStruct((2, 4 * L, BR_W), F32),
        compiler_params=_cparams("arbitrary"),
        name="filter_spectrum_direct",
    )(hf, f, ss)


def _conv_direct_kernel(z_ref, gate_ref, bias_ref, f_ref, fi_ref, h_ref, o_ref):
    z = z_ref[0]
    x = _dot(f_ref[...], z.astype(BF16))
    y = _cmul(x, h_ref[0], x.shape[0] // 2).astype(BF16)
    o_ref[0] = gate_ref[0] * (_dot(fi_ref[...], y) + bias_ref[...] * z)


def conv_direct(z, gate, bias, f, fi, h, o):
    B, L, C = z.shape
    blk = pl.BlockSpec((1, L, C), lambda b: (b, 0, 0))
    return pl.pallas_call(
        _conv_direct_kernel,
        grid=(B,),
        in_specs=[blk, blk, _const_spec((1, C)), _const_spec((4 * L, L)), _const_spec((L, 4 * L)),
                  pl.BlockSpec((1, 4 * L, C), lambda b: (o, 0, 0))],
        out_specs=blk,
        out_shape=jax.ShapeDtypeStruct((B, L, C), F32),
        compiler_params=_cparams("parallel"),
        name="conv_direct",
    )(z, gate, bias, f, fi, h)


@functools.lru_cache(maxsize=None)
def _dft_direct_tables(L):
    N = 2 * L
    k = np.arange(N)[:, None]
    n = np.arange(L)[None, :]
    th = 2 * np.pi * ((k * n) % N) / N
    f = np.concatenate([np.cos(th), -np.sin(th)], axis=0)
    fi = np.concatenate([np.cos(th).T, -np.sin(th).T], axis=1) / N
    return np.asarray(f, np.float32), np.asarray(fi, np.float32)


@functools.lru_cache(maxsize=None)
def _filter_features(L):
    t = np.arange(L, dtype=np.float32) / np.float32(L)
    ang = (2.0 * math.pi) * t[:, None].astype(np.float64) * np.arange(1, HY_BANDS + 1)
    feat = np.zeros((L, HY_FEAT_PAD), np.float32)
    feat[:, 0] = t
    feat[:, 1:1 + HY_BANDS] = np.sin(ang)
    feat[:, 1 + HY_BANDS:HY_FEAT] = np.cos(ang)
    rates = np.abs(np.linspace(math.log(HY_TARGET) / HY_SLOW, math.log(HY_TARGET) / HY_FAST, BR_W))
    return feat, np.tile(np.asarray(rates, np.float32), 4)[None, :]


def hyena_branch(v, g1, g2, hp):
    f_w1, f_b1, f_w2, f_b2, f_w3, freq, bias = hp
    B, L, C = v.shape
    feat, rates = _filter_features(L)
    hf, ss = hyena_filter(jnp.asarray(feat), f_w1, f_b1, f_w2, f_b2, f_w3, freq, jnp.asarray(rates),
                          min(L, 512))
    if L <= 512:
        f, fi = (jnp.asarray(t).astype(BF16) for t in _dft_direct_tables(L))
        spec = filter_spectrum_direct(hf, f, ss)
        z = v
        for o, gate in enumerate((g1, g2)):
            z = conv_direct(z, gate, bias[o:o + 1], f, fi, spec, o)
        return z
    f1, fi1, g, gi = (jnp.asarray(t).astype(BF16) for t in _dft_tables(L))
    N2 = 128
    N1 = 2 * L // N2
    nz = L // N2
    kb = 8
    a = dft_stage1(hf.reshape(1, nz, N2 * 4 * C), f1, 8192)
    spec = filter_spectrum(a.reshape(1, 2, N1, N2, 4 * C), g, ss, kb)
    W = N2 * C
    tn = 8192
    z = v.reshape(B, nz, W)
    for o, gate in enumerate((g1, g2)):
        a = dft_stage1(z, f1, tn).reshape(B, 2, N1, N2, C)
        bm = spectral_multiply(a, g, gi, spec, o, kb).reshape(B, 2 * N1, W)
        z = idft_stage1(bm, fi1, z, gate.reshape(B, nz, W), jnp.tile(bias[o:o + 1], (1, tn // C)), tn)
    return z.reshape(B, L, C)


def _head_sum(x, seg_ref):
    return _dot(x, seg_ref[...], precision=HI)


def _rw_prep_kernel(p_ref, pp_ref, pn_ref, mu_ref, wl_ref, w0_ref, a0_ref, kv_ref, seg_ref,
                    r_ref, v_ref, na_ref, w_ref, kt_ref, b_ref, bonus_ref, g_ref, *, on_grid):
    i = pl.program_id(1)
    first = i == 0
    last = i == pl.num_programs(1) - 1
    x = p_ref[0]
    tm = x.shape[0]
    grp = lax.broadcasted_iota(jnp.int32, x.shape, 1) % 4
    prev, nxt = pp_ref[0], pn_ref[0]
    if on_grid:
        col = (lax.broadcasted_iota(jnp.int32, (tm, 1), 0) + i * tm) % GRID_W
        left = jnp.where(col == 0, 0.0, pltpu.roll(x, 1, 0))
        right = jnp.where(col == GRID_W - 1, 0.0, pltpu.roll(x, tm - 1, 0))
        up = _shift_rows(x, prev, nxt, GRID_W, first, last)
        down = _shift_rows(x, prev, nxt, -GRID_W, first, last)
        shifted = jnp.where(grp == 0, left, jnp.where(grp == 1, right, jnp.where(grp == 2, up, down)))
    else:
        before = _shift_rows(x, prev, nxt, 1, first, last)
        after = _shift_rows(x, prev, nxt, -1, first, last)
        shifted = jnp.where(grp % 2 == 0, before, after)
    xx = x + (shifted - x) * mu_ref[...]
    r = xx[:, 0:BR_W]
    k = xx[:, BR_W:2 * BR_W]
    v = xx[:, 2 * BR_W:3 * BR_W]
    lo = xx[:, 3 * BR_W:4 * BR_W]
    ll = lax.broadcasted_iota(jnp.int32, lo.shape, 1)
    act = jnp.where(ll < RW_LORA, jnp.tanh(lo), jnp.where(ll < 2 * RW_LORA, lo, _sigmoid(lo)))
    z = _dot(act.astype(BF16), wl_ref[...])
    kk = k * kv_ref[0:1, :]
    kk = kk * lax.rsqrt(_head_sum(kk * kk, seg_ref) + 1e-12)
    r_ref[0] = r
    v_ref[0] = v
    na_ref[0] = -kk
    bonus_ref[0] = _head_sum(r * k * kv_ref[2:3, :], seg_ref) * v
    g_ref[0] = z[:, 4 * BR_W:5 * BR_W]
    for d in range(2):
        logw = -_softplus(-(w0_ref[d:d + 1, :] + z[:, d * BR_W:(d + 1) * BR_W])) - 0.5
        w_ref[d, 0] = jnp.exp(-jnp.exp(logw))
        a = _sigmoid(a0_ref[d:d + 1, :] + z[:, (2 + d) * BR_W:(3 + d) * BR_W])
        kt_ref[d, 0] = k * (1.0 + (a - 1.0) * kv_ref[1:2, :])
        b_ref[d, 0] = kk * a


def rwkv_prep(p, mu, w_lora, w0, a0, kvec, seg, on_grid, tm):
    B, T, _ = p.shape
    halo = GRID_W if on_grid else 8
    main, prev, nxt = _halo_specs(tm, T, RW_PAD, lambda b, i: OFF_RW // RW_PAD, halo)
    one = pl.BlockSpec((1, tm, BR_W), lambda b, i: (b, i, 0))
    two = pl.BlockSpec((2, 1, tm, BR_W), lambda b, i: (0, b, i, 0))
    s1 = jax.ShapeDtypeStruct((B, T, BR_W), F32)
    s2 = jax.ShapeDtypeStruct((2, B, T, BR_W), F32)
    return pl.pallas_call(
        functools.partial(_rw_prep_kernel, on_grid=on_grid),
        grid=(B, T // tm),
        in_specs=[main, prev, nxt, _const_spec((1, RW_PAD)), _const_spec((BR_W, 5 * BR_W)),
                  _const_spec((2, BR_W)), _const_spec((2, BR_W)), _const_spec((3, BR_W)),
                  _const_spec((BR_W, BR_W))],
        out_specs=[one, one, one, two, two, two, one, one],
        out_shape=[s1, s1, s1, s2, s2, s2, s1, s1],
        compiler_params=_cparams("parallel", "parallel"),
        name="rwkv_prep",
    )(p, p, p, mu, w_lora, w0, a0, kvec, seg)


RW_VH = RW_HEAD // 2


def _rw_scan_kernel(r_ref, w_ref, kt_ref, a_ref, b_ref, v_ref, s0_ref, y_ref, s_ref):
    @pl.when(pl.program_id(0) == 0)
    def _():
        s_ref[...] = s0_ref[...]

    def step(t, carry):
        a, w, b, kt, r = a_ref[t], w_ref[t], b_ref[t], kt_ref[t], r_ref[t]
        for v8 in range(RW_VH // 8):
            vt = v_ref[t, v8 * 8:(v8 + 1) * 8, :]
            ys = []
            for j in range(8):
                vi = v8 * 8 + j
                s = s_ref[vi]
                sa = jnp.sum(s * a, axis=0, keepdims=True)
                s = s * w + sa * b + vt[j:j + 1, :] * kt
                s_ref[vi] = s
                ys.append(jnp.sum(s * r, axis=0, keepdims=True))
            y_ref[t, v8 * 8:(v8 + 1) * 8, :] = jnp.concatenate(ys, axis=0)
        return carry

    lax.fori_loop(0, r_ref.shape[0], step, 0)


def rwkv_scan(r, w, kt, a, b, v, s0, tt):
    T, K, NL = r.shape
    kblk = pl.BlockSpec((tt, K, NL), lambda i: (i, 0, 0))
    vblk = pl.BlockSpec((tt, RW_VH, NL), lambda i: (i, 0, 0))
    sblk = pl.BlockSpec((RW_VH, K, NL), lambda i: (0, 0, 0))
    return pl.pallas_call(
        _rw_scan_kernel,
        grid=(T // tt,),
        in_specs=[kblk, kblk, kblk, kblk, kblk, vblk, sblk],
        out_specs=[vblk, sblk],
        out_shape=[jax.ShapeDtypeStruct((T, RW_VH, NL), F32), jax.ShapeDtypeStruct((RW_VH, K, NL), F32)],
        compiler_params=_cparams("arbitrary"),
        name="rwkv_scan",
    )(r, w, kt, a, b, v, s0)


def _to_scan_keys(xf, xb):
    B, T, _ = xf.shape
    st = jnp.stack([xf, jnp.flip(xb, 1)], 0).reshape(2, B, T, RW_HEADS, RW_HEAD)
    st = st.transpose(2, 4, 0, 1, 3).reshape(T, RW_HEAD, 2 * B * RW_HEADS)
    return jnp.concatenate([st, st], axis=-1)


def _to_scan_values(v):
    B, T, _ = v.shape
    st = jnp.stack([v, jnp.flip(v, 1)], 0).reshape(2, B, T, RW_HEADS, 2, RW_VH)
    return st.transpose(2, 5, 4, 0, 1, 3).reshape(T, RW_VH, 4 * B * RW_HEADS)


def _from_scan(y, B):
    T = y.shape[0]
    st = y.reshape(T, RW_VH, 2, 2, B, RW_HEADS).transpose(3, 4, 0, 5, 2, 1).reshape(2, B, T, BR_W)
    return st[0], jnp.flip(st[1], 1)


def rwkv_mix(prep, s0, tt):
    r, v, na, w, kt, b = prep
    B = r.shape[0]
    y, s_fin = rwkv_scan(_to_scan_keys(r, r), _to_scan_keys(w[0], w[1]), _to_scan_keys(kt[0], kt[1]),
                         _to_scan_keys(na, na), _to_scan_keys(b[0], b[1]), _to_scan_values(v), s0, tt)
    y_f, y_b = _from_scan(y, B)
    return y_f, y_b, s_fin


def _gelu_tanh(x):
    return 0.5 * x * (1.0 + jnp.tanh(math.sqrt(2.0 / math.pi) * (x + 0.044715 * (x * x * x))))


def _lru_prep_kernel(p_ref, pp_ref, pn_ref, cw_ref, cb_ref, w_ref, bias_ref, lam_ref, a_ref, b_ref, gg_ref):
    i = pl.program_id(1)
    first = i == 0
    last = i == pl.num_programs(1) - 1
    x = p_ref[0][:, 0:BR_W]
    prev = pp_ref[0][:, 0:BR_W]
    nxt = pn_ref[0][:, 0:BR_W]
    xc = cb_ref[...] + cw_ref[0:1, :] * _shift_rows(x, prev, nxt, 2, first, last)
    xc = xc + cw_ref[1:2, :] * _shift_rows(x, prev, nxt, 1, first, last)
    xc = xc + cw_ref[2:3, :] * x
    xc = xc + cw_ref[3:4, :] * _shift_rows(x, prev, nxt, -1, first, last)
    z = _dot(xc.astype(BF16), w_ref[...])
    for d in range(2):
        r = _sigmoid(z[:, 2 * d * BR_W:(2 * d + 1) * BR_W] + bias_ref[2 * d:2 * d + 1, :])
        gi = _sigmoid(z[:, (2 * d + 1) * BR_W:(2 * d + 2) * BR_W] + bias_ref[2 * d + 1:2 * d + 2, :])
        log_a = -LRU_C * r * _softplus(-lam_ref[d:d + 1, :])
        a_ref[d, 0] = jnp.exp(log_a)
        b_ref[d, 0] = jnp.sqrt(1.0 - jnp.exp(2.0 * log_a)) * (gi * xc)
    gg_ref[0] = _gelu_tanh(p_ref[0][:, BR_W:2 * BR_W])


def lru_prep(p, conv_w, conv_b, w_blk, bias, lam, tm):
    B, T, _ = p.shape
    main, prev, nxt = _halo_specs(tm, T, LRU_IN, lambda b, i: OFF_LRU // LRU_IN, 8)
    two = pl.BlockSpec((2, 1, tm, BR_W), lambda b, i: (0, b, i, 0))
    s2 = jax.ShapeDtypeStruct((2, B, T, BR_W), F32)
    return pl.pallas_call(
        _lru_prep_kernel,
        grid=(B, T // tm),
        in_specs=[main, prev, nxt, _const_spec((4, BR_W)), _const_spec((1, BR_W)),
                  _const_spec((BR_W, 4 * BR_W)), _const_spec((4, BR_W)), _const_spec((2, BR_W))],
        out_specs=[two, two, pl.BlockSpec((1, tm, BR_W), lambda b, i: (b, i, 0))],
        out_shape=[s2, s2, jax.ShapeDtypeStruct((B, T, BR_W), F32)],
        compiler_params=_cparams("parallel", "parallel"),
        name="lru_prep",
    )(p, p, p, conv_w, conv_b, w_blk, bias, lam)


def _affine_scan(a, b, reverse):
    tb = a.shape[0]
    row = lax.broadcasted_iota(jnp.int32, (tb, 1), 0)
    s = 1
    while s < tb:
        sh = tb - s if reverse else s
        ok = (row < tb - s) if reverse else (row >= s)
        a_s = pltpu.roll(a, sh, 0)
        b_s = pltpu.roll(b, sh, 0)
        b = jnp.where(ok, a * b_s + b, b)
        a = jnp.where(ok, a * a_s, a)
        s *= 2
    return a, b


def _lru_scan_kernel(af_ref, bf_ref, ab_ref, bb_ref, h0_ref, hf_ref, hb_ref, fin_ref):
    @pl.when(pl.program_id(1) == 0)
    def _():
        fin_ref[...] = h0_ref[...]

    tb = af_ref.shape[2]
    a, b = _affine_scan(af_ref[0, 0], bf_ref[0, 0], False)
    h = b + a * fin_ref[0, 0]
    hf_ref[0] = h
    fin_ref[0, 0] = h[tb - 1:tb, :]
    a, b = _affine_scan(ab_ref[0, 0], bb_ref[0, 0], True)
    h = b + a * fin_ref[1, 0]
    hb_ref[0] = h
    fin_ref[1, 0] = h[0:1, :]


def lru_scan(a, b, h0, tb):
    _, B, T, C = a.shape
    nb = T // tb
    fwd = pl.BlockSpec((1, 1, tb, C), lambda bi, i: (0, bi, i, 0))
    bwd = pl.BlockSpec((1, 1, tb, C), lambda bi, i: (1, bi, nb - 1 - i, 0))
    st = pl.BlockSpec((2, 1, 1, C), lambda bi, i: (0, bi, 0, 0))
    return pl.pallas_call(
        _lru_scan_kernel,
        grid=(B, nb),
        in_specs=[fwd, fwd, bwd, bwd, st],
        out_specs=[pl.BlockSpec((1, tb, C), lambda bi, i: (bi, i, 0)),
                   pl.BlockSpec((1, tb, C), lambda bi, i: (bi, nb - 1 - i, 0)), st],
        out_shape=[jax.ShapeDtypeStruct((B, T, C), F32), jax.ShapeDtypeStruct((B, T, C), F32),
                   jax.ShapeDtypeStruct((2, B, 1, C), F32)],
        compiler_params=_cparams("parallel", "arbitrary"),
        name="lru_scan",
    )(a, b, a, b, h0)


def _rope(x, cos, sin):
    q4 = RET_HEAD // 4
    lane = lax.broadcasted_iota(jnp.int32, x.shape, 1) % (2 * q4)
    partner = jnp.where(lane < q4, pltpu.roll(x, x.shape[1] - q4, 1), pltpu.roll(x, q4, 1))
    return x * cos + partner * sin


def _ret_dir(x, cos, sin, s, glane, gtile_ref, d, reverse):
    C = x.shape[0]
    q = x[:, 0:BR_W]
    k = x[:, BR_W:2 * BR_W]
    v = x[:, 2 * BR_W:3 * BR_W].astype(BF16)
    if cos is not None:
        q = _rope(q, cos, sin)
        k = _rope(k, cos, sin)
    k = k * (RET_HEAD ** -0.5)
    lg = -_softplus(-glane)
    idx = lax.broadcasted_iota(jnp.int32, (C, 1), 0).astype(F32)
    steps_in = (C - idx) if reverse else (idx + 1.0)
    steps_out = idx if reverse else (C - 1.0 - idx)
    ri = lax.broadcasted_iota(jnp.int32, (C, C), 0)
    ci = lax.broadcasted_iota(jnp.int32, (C, C), 1)
    diff = ((ci - ri) if reverse else (ri - ci)).astype(F32)
    lane_head = lax.broadcasted_iota(jnp.int32, (1, BR_W), 1) // RET_HEAD
    qb = q.astype(BF16)
    kb = k.astype(BF16)
    y = _dot(qb, s.astype(BF16)) * jnp.exp(steps_in * lg)
    for h in range(RET_HEADS):
        lg_h = -_softplus(-gtile_ref[d, h][0:1, :])
        dm = jnp.where(diff >= 0, jnp.exp(diff * lg_h), 0.0)
        mh = lane_head == h
        sc = lax.dot_general(jnp.where(mh, qb, jnp.zeros_like(qb)), kb, (((1,), (1,)), ((), ())),
                             preferred_element_type=F32)
        y = y + jnp.where(mh, _dot((sc * dm).astype(BF16), v), 0.0)
    kd = (k * jnp.exp(steps_out * lg)).astype(BF16)
    upd = lax.dot_general(kd, v, (((0,), (0,)), ((), ())), preferred_element_type=F32)
    rh = lax.broadcasted_iota(jnp.int32, (BR_W, BR_W), 0) // RET_HEAD
    ch = lax.broadcasted_iota(jnp.int32, (BR_W, BR_W), 1) // RET_HEAD
    s = s * jnp.exp(C * lg) + jnp.where(rh == ch, upd, 0.0)
    return y, s


def _ret_kernel(*refs, rope):
    if rope:
        xf_ref, xb_ref, cf_ref, sf_ref, cb_ref, sb_ref, gl_ref, gt_ref, s0_ref, yf_ref, yb_ref, s_ref = refs
    else:
        xf_ref, xb_ref, gl_ref, gt_ref, s0_ref, yf_ref, yb_ref, s_ref = refs

    @pl.when(pl.program_id(1) == 0)
    def _():
        s_ref[...] = s0_ref[...]

    y, s = _ret_dir(xf_ref[0], cf_ref[...] if rope else None, sf_ref[...] if rope else None,
                    s_ref[0, 0], gl_ref[0:1, :], gt_ref, 0, False)
    yf_ref[0] = y
    s_ref[0, 0] = s
    y, s = _ret_dir(xb_ref[0], cb_ref[...] if rope else None, sb_ref[...] if rope else None,
                    s_ref[1, 0], gl_ref[1:2, :], gt_ref, 1, True)
    yb_ref[0] = y
    s_ref[1, 0] = s


def retention(p, cos, sin, glane, gtile, s0, rope):
    B, T, _ = p.shape
    C = RET_CHUNK
    nc = T // C
    cb = OFF_RET // RET_IN
    xf = pl.BlockSpec((1, C, RET_IN), lambda b, i: (b, i, cb))
    xb = pl.BlockSpec((1, C, RET_IN), lambda b, i: (b, nc - 1 - i, cb))
    tf = pl.BlockSpec((C, BR_W), lambda b, i: (i, 0))
    tb = pl.BlockSpec((C, BR_W), lambda b, i: (nc - 1 - i, 0))
    st = pl.BlockSpec((2, 1, BR_W, BR_W), lambda b, i: (0, b, 0, 0))
    ins = [xf, xb] + ([tf, tf, tb, tb] if rope else []) + [
        _const_spec((2, BR_W)), _const_spec((2, RET_HEADS, 8, C)), st]
    args = [p, p] + ([cos, sin, cos, sin] if rope else []) + [glane, gtile, s0]
    return pl.pallas_call(
        functools.partial(_ret_kernel, rope=rope),
        grid=(B, nc),
        in_specs=ins,
        out_specs=[pl.BlockSpec((1, C, BR_W), lambda b, i: (b, i, 0)),
                   pl.BlockSpec((1, C, BR_W), lambda b, i: (b, nc - 1 - i, 0)), st],
        out_shape=[jax.ShapeDtypeStruct((B, T, BR_W), F32), jax.ShapeDtypeStruct((B, T, BR_W), F32),
                   jax.ShapeDtypeStruct((2, B, BR_W, BR_W), F32)],
        compiler_params=_cparams("parallel", "arbitrary"),
        name="retention",
    )(*args)


@functools.lru_cache(maxsize=None)
def _rope_tables(T):
    pos = np.arange(T)
    q4 = RET_HEAD // 4
    inv = ROPE_BASE ** (-np.arange(q4, dtype=np.float64) / q4)
    cos = np.zeros((T, RET_HEAD))
    sin = np.zeros((T, RET_HEAD))
    for part, coord in enumerate((pos // GRID_W, pos % GRID_W)):
        ang = coord[:, None] * inv
        base = part * 2 * q4
        cos[:, base:base + q4] = np.cos(ang)
        cos[:, base + q4:base + 2 * q4] = np.cos(ang)
        sin[:, base:base + q4] = -np.sin(ang)
        sin[:, base + q4:base + 2 * q4] = np.sin(ang)
    tile = lambda t: np.asarray(np.tile(t, (1, RET_HEADS)), np.float32)
    return tile(cos), tile(sin)


def _head_norm(y, seg_ref, eps):
    mu = _head_sum(y, seg_ref) * (1.0 / RW_HEAD)
    yc = y - mu
    var = _head_sum(yc * yc, seg_ref) * (1.0 / RW_HEAD)
    return yc * lax.rsqrt(var + eps)


def _merge_kernel(x_ref, g_ref, sh_ref, sc_ref, gt_ref, hy_ref, ryf_ref, ryb_ref, rbon_ref, rg_ref,
                  lhf_ref, lhb_ref, lgg_ref, tyf_ref, tyb_ref, tg_ref, lng_ref, seg_ref,
                  wg_ref, br_ref, wo_ref, o_ref, m_ref):
    x = x_ref[0]
    u = _norm_mod(x, g_ref[...], sh_ref[0], sc_ref[0]).astype(BF16)
    y_rw = (_head_norm(ryf_ref[0] + ryb_ref[0], seg_ref, RW_LN_EPS) * lng_ref[...] + rbon_ref[0]) * rg_ref[0]
    y_lru = (lhf_ref[0] + lhb_ref[0]) * lgg_ref[0]
    y_ret = _head_norm(tyf_ref[0] + tyb_ref[0], seg_ref, RET_LN_EPS) * _silu(tg_ref[0])
    ys = [y.astype(BF16) for y in (hy_ref[0], y_rw, y_lru, y_ret)]
    D = x.shape[1]
    cw = 256
    for c in range(D // cw):
        acc = None
        for n in range(N_BRANCH):
            gate = _sigmoid(_dot(u, wg_ref[:, n * D + c * cw:n * D + (c + 1) * cw]))
            t = gate * _dot(ys[n], br_ref[n, :, c * cw:(c + 1) * cw])
            acc = t if acc is None else acc + t
        m_ref[:, c * cw:(c + 1) * cw] = acc.astype(BF16)
    o_ref[0] = x + gt_ref[0] * _dot(m_ref[...], wo_ref[...])


def merge(x, g, sh, sc, gt, p, y_hy, rw, lru, ret, ln_g, seg, w_gate, br, w_out, tm):
    B, T, D = x.shape
    vec = pl.BlockSpec((1, 1, D), lambda b, i: (b, 0, 0))
    row = pl.BlockSpec((1, tm, D), lambda b, i: (b, i, 0))
    brn = pl.BlockSpec((1, tm, BR_W), lambda b, i: (b, i, 0))
    tg = pl.BlockSpec((1, tm, BR_W), lambda b, i: (b, i, (OFF_RET + 3 * BR_W) // BR_W))
    return pl.pallas_call(
        _merge_kernel,
        grid=(B, T // tm),
        in_specs=[row, _const_spec((1, D)), vec, vec, vec] + [brn] * 10 + [tg] + [
            _const_spec((1, BR_W)), _const_spec((BR_W, BR_W)), _const_spec((D, GATE_IN)),
            _const_spec((N_BRANCH, BR_W, D)), _const_spec((D, D))],
        out_specs=row,
        out_shape=jax.ShapeDtypeStruct((B, T, D), F32),
        scratch_shapes=[pltpu.VMEM((tm, D), BF16)],
        compiler_params=_cparams("parallel", "parallel"),
        name="merge",
    )(x, g, sh, sc, gt, y_hy, *rw, *lru, *ret, p, ln_g, seg, w_gate, br, w_out)


def _block_diag(w):
    G = w.shape[-3]
    eye = jnp.eye(G, dtype=w.dtype)
    full = w[..., :, :, None, :] * eye[:, None, :, None]
    return full.reshape(*w.shape[:-3], G * w.shape[-2], G * w.shape[-1])


def _mixers(p, lp, states, on_grid, with_output, tiles):
    B, T, _ = p.shape
    tm, tt, _ = tiles
    r, v, na, w, kt, b, bonus, g = rwkv_prep(p, *lp['rw'], on_grid, tm)
    y_f, y_b, rw_fin = rwkv_mix((r, v, na, w, kt, b), states[0], tt)
    a, bb, gg = lru_prep(p, *lp['lru'], tm)
    h_f, h_b, lru_fin = lru_scan(a, bb, states[1], tm)
    cos, sin = (jnp.asarray(t) for t in _rope_tables(T)) if on_grid else (None, None)
    t_f, t_b, ret_fin = retention(p, cos, sin, *lp['ret'], states[2], on_grid)
    fins = (rw_fin, lru_fin, ret_fin)
    if not with_output:
        return None, fins
    vg = hyena_prep(p, *lp['hy_conv'], tm)
    y_hy = hyena_branch(vg[0], vg[1], vg[2], lp['hy'])
    return (y_hy, (y_f, y_b, bonus, g), (h_f, h_b, gg), (t_f, t_b)), fins


def kernel(x, c, ctx, c_ctx, w_mod, b_mod, norm1_g, norm2_g, w_in, hy_conv_w, hy_conv_b, hy_f_w1, hy_f_b1, hy_f_w2, hy_f_b2, hy_f_w3, hy_freq, hy_bias, rw_mu, rw_w0, rw_w2, rw_a0, rw_a2, rw_g2, rw_kk, rw_ka, rw_rk, rw_ln_g, lru_conv_w, lru_conv_b, lru_wa, lru_ba, lru_wx, lru_bx, lru_lam, ret_gamma, br_proj, w_out, ffn_w1, ffn_w2, final_g):
    B, T, D = x.shape
    TC = ctx.shape[1]
    L = w_in.shape[0]

    s0, s1, s2, s3 = HY_IN, HY_IN + RW_IN, HY_IN + RW_IN + LRU_IN, HY_IN + RW_IN + LRU_IN + RET_IN
    w_branch = jnp.concatenate([w_in[:, :, s0:s1], jnp.zeros((L, D, RW_PAD - RW_IN), w_in.dtype),
                                w_in[:, :, s2:s3], w_in[:, :, s1:s2], w_in[:, :, 0:s0]], axis=2).astype(BF16)
    w_gate = w_in[:, :, s3:].astype(BF16)
    mu = jnp.pad(rw_mu, ((0, 0), (0, RW_PAD - RW_IN)))[:, None, :]
    w_lora = jnp.zeros((L, BR_W, 5 * BR_W), F32)
    w_lora = w_lora.at[:, 0:RW_LORA, 0:BR_W].set(rw_w2[:, 0]).at[:, 0:RW_LORA, BR_W:2 * BR_W].set(rw_w2[:, 1])
    w_lora = w_lora.at[:, RW_LORA:2 * RW_LORA, 2 * BR_W:3 * BR_W].set(rw_a2[:, 0])
    w_lora = w_lora.at[:, RW_LORA:2 * RW_LORA, 3 * BR_W:4 * BR_W].set(rw_a2[:, 1])
    w_lora = w_lora.at[:, 2 * RW_LORA:3 * RW_LORA, 4 * BR_W:5 * BR_W].set(rw_g2).astype(BF16)
    kvec = jnp.stack([rw_kk, rw_ka, rw_rk], axis=1)
    seg = jnp.asarray(np.kron(np.eye(RW_HEADS), np.ones((RW_HEAD, RW_HEAD))), F32)
    lru_w = jnp.concatenate([_block_diag(lru_wa[:, 0]), _block_diag(lru_wx[:, 0]),
                             _block_diag(lru_wa[:, 1]), _block_diag(lru_wx[:, 1])], axis=2).astype(BF16)
    lru_bias = jnp.stack([lru_ba[:, 0], lru_bx[:, 0], lru_ba[:, 1], lru_bx[:, 1]], axis=1)
    glane = jnp.repeat(ret_gamma, RET_HEAD, axis=2)
    gtile = jnp.broadcast_to(ret_gamma[:, :, :, None, None], (L, 2, RET_HEADS, 8, RET_CHUNK))
    f_w1 = jnp.pad(hy_f_w1, ((0, 0), (0, HY_FEAT_PAD - HY_FEAT), (0, 0)))
    br_b = br_proj.astype(BF16)
    w_out_b = w_out.astype(BF16)
    ffn_gate = ffn_w1[:, :, :D_FF].astype(BF16)
    ffn_up_w = ffn_w1[:, :, D_FF:].astype(BF16)
    ffn_w2_b = ffn_w2.astype(BF16)

    cc = jnp.concatenate([c, c_ctx[None, :], jnp.zeros((16 - B - 1, D), F32)], axis=0)
    mods = modulation(cc, w_mod.astype(BF16), b_mod[:, None, :])

    zero_states = (jnp.zeros((RW_VH, RW_HEAD, 4 * B * RW_HEADS), F32), jnp.zeros((2, B, 1, BR_W), F32),
                   jnp.zeros((2, B, BR_W, BR_W), F32))
    xc = ctx
    for l in range(L):
        last = l == L - 1
        lp = {
            'rw': (mu[l], w_lora[l], rw_w0[l], rw_a0[l], kvec[l], seg),
            'lru': (lru_conv_w[l], lru_conv_b[l][None, :], lru_w[l], lru_bias[l], lru_lam[l]),
            'ret': (glane[l], gtile[l]),
            'hy_conv': (hy_conv_w[l], hy_conv_b[l][None, :]),
            'hy': (f_w1[l], hy_f_b1[l][None, :], hy_f_w2[l], hy_f_b2[l][None, :], hy_f_w3[l], hy_freq[l],
                   hy_bias[l]),
        }
        g1 = norm1_g[l][None, :]
        g2 = norm2_g[l][None, :]
        m_lat = [m[:, None, :] for m in jnp.split(mods[l, :B], 6, axis=-1)]
        m_ctx = [jnp.broadcast_to(m[None, :, :], (B, 1, D)) for m in jnp.split(mods[l, B:B + 1], 6, axis=-1)]
        ln_g = rw_ln_g[l][None, :]

        def layer(xs, m, states, on_grid, with_output, tiles, final):
            p = in_projection(xs, g1, m[0], m[1], w_branch[l], tiles[0])
            br, fins = _mixers(p, lp, states, on_grid, with_output, tiles)
            if not with_output:
                return None, fins
            xs = merge(xs, g1, m[0], m[1], m[2], p, br[0], br[1], br[2], br[3], ln_g, seg,
                       w_gate[l], br_b[l], w_out_b[l], tiles[2])
            h = ffn_up(xs, g2, m[3], m[4], ffn_gate[l], ffn_up_w[l], tiles[0])
            xs = ffn_down(h, xs, m[5], ffn_w2_b[l], final_g[None, :], final, tiles[0])
            return xs, fins

        xc_new, ctx_states = layer(xc, m_ctx, zero_states, False, not last, (TC, 32, TC), False)
        x, _ = layer(x, m_lat, ctx_states, True, True, (512, 32, 256), last)
        if not last:
            xc = xc_new
    return x
```

```python
import functools
import math

import numpy as np
import jax
import jax.numpy as jnp
from jax import lax
from jax.experimental import pallas as pl
from jax.experimental.pallas import tpu as pltpu

F32 = jnp.float32
BF16 = jnp.bfloat16

D_MODEL = 1024
DEPTH = 4
GRID_W = 64
N_BRANCH = 4
BR_W = D_MODEL // N_BRANCH

HY_BANDS = 8
HY_FEAT = 1 + 2 * HY_BANDS
HY_FEAT_PAD = 32
HY_HID = 64
HY_TARGET = 1e-2
HY_FAST = 0.3
HY_SLOW = 1.5
HY_IN = 3 * BR_W

RW_HEAD = 64
RW_HEADS = BR_W // RW_HEAD
RW_LORA = 64
RW_IN = 3 * BR_W + 3 * RW_LORA
RW_PAD = 4 * BR_W
RW_LN_EPS = 64e-5

LRU_BLOCKS = 4
LRU_BLOCK = BR_W // LRU_BLOCKS
LRU_C = 8.0
LRU_IN = 2 * BR_W

RET_HEADS = 4
RET_HEAD = BR_W // RET_HEADS
RET_CHUNK = 128
ROPE_BASE = 10000.0
RET_IN = 4 * BR_W
RET_LN_EPS = 1e-5

GATE_IN = N_BRANCH * D_MODEL
D_FF = ((8 * D_MODEL // 3 + 255) // 256) * 256
EPS = 1e-6

OFF_RW = 0
OFF_RET = OFF_RW + RW_PAD
OFF_LRU = OFF_RET + RET_IN
OFF_HY = OFF_LRU + LRU_IN
N_BR = OFF_HY + HY_IN

VMEM_LIMIT = 56 * 1024 * 1024
HI = lax.Precision.HIGHEST


def _cparams(*sem):
    return pltpu.CompilerParams(dimension_semantics=sem, vmem_limit_bytes=VMEM_LIMIT)


def _const_spec(shape):
    nd = len(shape)
    return pl.BlockSpec(shape, lambda *_: (0,) * nd, pipeline_mode=pl.Buffered(1))


def _dot(a, b, **kw):
    return jnp.dot(a, b, preferred_element_type=F32, **kw)


def _norm_mod(x, g, sh, sc):
    ms = jnp.mean(x * x, axis=-1, keepdims=True)
    return x * lax.rsqrt(ms + EPS) * g * (1.0 + sc) + sh


def _sigmoid(x):
    return 1.0 / (1.0 + jnp.exp(-x))


def _silu(x):
    return x * _sigmoid(x)


def _softplus(x):
    return jnp.maximum(x, 0.0) + jnp.log(1.0 + jnp.exp(-jnp.abs(x)))


def _mod_kernel(c_ref, w_ref, b_ref, o_ref):
    c = c_ref[...]
    o_ref[0] = _dot(_silu(c).astype(BF16), w_ref[0]) + b_ref[0]


def modulation(cc, w_mod, b_mod):
    L, D, N = w_mod.shape
    tn = 1536
    return pl.pallas_call(
        _mod_kernel,
        grid=(L, N // tn),
        in_specs=[pl.BlockSpec((16, D), lambda l, j: (0, 0)),
                  pl.BlockSpec((1, D, tn), lambda l, j: (l, 0, j)),
                  pl.BlockSpec((1, 1, tn), lambda l, j: (l, 0, j))],
        out_specs=pl.BlockSpec((1, 16, tn), lambda l, j: (l, 0, j)),
        out_shape=jax.ShapeDtypeStruct((L, 16, N), F32),
        compiler_params=_cparams("parallel", "parallel"),
        name="modulation",
    )(cc, w_mod, b_mod)


def _inproj_kernel(x_ref, g_ref, sh_ref, sc_ref, w_ref, o_ref):
    u = _norm_mod(x_ref[0], g_ref[...], sh_ref[0], sc_ref[0]).astype(BF16)
    n = w_ref.shape[1]
    cw = 256
    for c in range(n // cw):
        o_ref[0, :, c * cw:(c + 1) * cw] = _dot(u, w_ref[:, c * cw:(c + 1) * cw])


def in_projection(x, g, sh, sc, w, tm):
    B, T, D = x.shape
    N = w.shape[1]
    vec = pl.BlockSpec((1, 1, D), lambda b, i: (b, 0, 0))
    return pl.pallas_call(
        _inproj_kernel,
        grid=(B, T // tm),
        in_specs=[pl.BlockSpec((1, tm, D), lambda b, i: (b, i, 0)),
                  _const_spec((1, D)), vec, vec, _const_spec((D, N))],
        out_specs=pl.BlockSpec((1, tm, N), lambda b, i: (b, i, 0)),
        out_shape=jax.ShapeDtypeStruct((B, T, N), F32),
        compiler_params=_cparams("parallel", "parallel"),
        name="in_projection",
    )(x, g, sh, sc, w)


def _ffn1_kernel(x_ref, g_ref, sh_ref, sc_ref, wg_ref, wu_ref, o_ref):
    u = _norm_mod(x_ref[0], g_ref[...], sh_ref[0], sc_ref[0]).astype(BF16)
    n = wg_ref.shape[1]
    cw = 256
    for c in range(n // cw):
        sl = slice(c * cw, (c + 1) * cw)
        gate = _dot(u, wg_ref[:, sl])
        up = _dot(u, wu_ref[:, sl])
        o_ref[0, :, sl] = (_silu(gate) * up).astype(BF16)


def ffn_up(x, g, sh, sc, w_gate, w_up, tm):
    B, T, D = x.shape
    N = w_gate.shape[1]
    vec = pl.BlockSpec((1, 1, D), lambda b, i: (b, 0, 0))
    return pl.pallas_call(
        _ffn1_kernel,
        grid=(B, T // tm),
        in_specs=[pl.BlockSpec((1, tm, D), lambda b, i: (b, i, 0)),
                  _const_spec((1, D)), vec, vec, _const_spec((D, N)), _const_spec((D, N))],
        out_specs=pl.BlockSpec((1, tm, N), lambda b, i: (b, i, 0)),
        out_shape=jax.ShapeDtypeStruct((B, T, N), BF16),
        compiler_params=_cparams("parallel", "parallel"),
        name="ffn_up",
    )(x, g, sh, sc, w_gate, w_up)


def _ffn2_kernel(h_ref, x_ref, gt_ref, w_ref, fg_ref, o_ref, *, final_norm):
    y = x_ref[0] + gt_ref[0] * _dot(h_ref[0], w_ref[...])
    if final_norm:
        ms = jnp.mean(y * y, axis=-1, keepdims=True)
        y = y * lax.rsqrt(ms + EPS) * fg_ref[...]
    o_ref[0] = y


def ffn_down(h, x, gate, w, final_g, final_norm, tm):
    B, T, D = x.shape
    N = h.shape[2]
    return pl.pallas_call(
        functools.partial(_ffn2_kernel, final_norm=final_norm),
        grid=(B, T // tm),
        in_specs=[pl.BlockSpec((1, tm, N), lambda b, i: (b, i, 0)),
                  pl.BlockSpec((1, tm, D), lambda b, i: (b, i, 0)),
                  pl.BlockSpec((1, 1, D), lambda b, i: (b, 0, 0)),
                  _const_spec((N, D)), _const_spec((1, D))],
        out_specs=pl.BlockSpec((1, tm, D), lambda b, i: (b, i, 0)),
        out_shape=jax.ShapeDtypeStruct((B, T, D), F32),
        compiler_params=_cparams("parallel", "parallel"),
        name="ffn_down",
    )(h, x, gate, w, final_g)


def _halo_specs(tm, T, width, col_fn, halo):
    r = tm // halo
    last = T // halo - 1
    main = pl.BlockSpec((1, tm, width), lambda *g: (g[0], g[1], col_fn(*g)))
    prev = pl.BlockSpec((1, halo, width), lambda *g: (g[0], jnp.maximum(g[1] * r - 1, 0), col_fn(*g)))
    nxt = pl.BlockSpec((1, halo, width), lambda *g: (g[0], jnp.minimum((g[1] + 1) * r, last), col_fn(*g)))
    return main, prev, nxt


def _shift_rows(x, prev, nxt, s, first, last):
    tm = x.shape[0]
    if s > 0:
        head = jnp.where(first, 0.0, prev[prev.shape[0] - s:, :])
        return jnp.concatenate([head, x[:tm - s, :]], axis=0)
    s = -s
    tail = jnp.where(last, 0.0, nxt[:s, :])
    return jnp.concatenate([x[s:, :], tail], axis=0)


def _hy_prep_kernel(p_ref, pp_ref, pn_ref, w_ref, b_ref, o_ref):
    i = pl.program_id(1)
    first = i == 0
    last = i == pl.num_programs(1) - 1
    x = p_ref[0]
    xm = _shift_rows(x, pp_ref[0], pn_ref[0], 1, first, last)
    xp = _shift_rows(x, pp_ref[0], pn_ref[0], -1, first, last)
    o_ref[0, 0] = b_ref[...] + w_ref[0:1, :] * xm + w_ref[1:2, :] * x + w_ref[2:3, :] * xp


def hyena_prep(p, conv_w, conv_b, tm):
    B, T, _ = p.shape
    main, prev, nxt = _halo_specs(tm, T, BR_W, lambda b, i, j: OFF_HY // BR_W + j, 8)
    return pl.pallas_call(
        _hy_prep_kernel,
        grid=(B, T // tm, 3),
        in_specs=[main, prev, nxt, pl.BlockSpec((3, BR_W), lambda b, i, j: (0, j)),
                  pl.BlockSpec((1, BR_W), lambda b, i, j: (0, j))],
        out_specs=pl.BlockSpec((1, 1, tm, BR_W), lambda b, i, j: (j, b, i, 0)),
        out_shape=jax.ShapeDtypeStruct((3, B, T, BR_W), F32),
        compiler_params=_cparams("parallel", "parallel", "parallel"),
        name="hyena_prep",
    )(p, p, p, conv_w, conv_b)


def _hy_filter_kernel(feat_ref, w1_ref, b1_ref, w2_ref, b2_ref, w3_ref, fq_ref, rates_ref, h_ref, ss_ref):
    i = pl.program_id(0)
    feat = feat_ref[...]
    t = feat[:, 0:1]
    h = jnp.sin(fq_ref[0:1, :] * (_dot(feat, w1_ref[...], precision=HI) + b1_ref[...]))
    h = jnp.sin(fq_ref[1:2, :] * (_dot(h, w2_ref[...], precision=HI) + b2_ref[...]))
    h = _dot(h, w3_ref[...], precision=HI) * jnp.exp(-t * rates_ref[...])
    row = lax.broadcasted_iota(jnp.int32, h.shape, 0) + i * h.shape[0]
    col = lax.broadcasted_iota(jnp.int32, h.shape, 1)
    h = jnp.where((row == 0) & ((col // BR_W) % 2 == 1), 0.0, h)
    h_ref[...] = h

    @pl.when(i == 0)
    def _():
        ss_ref[...] = jnp.zeros_like(ss_ref)

    ss_ref[...] += jnp.sum(h * h, axis=0, keepdims=True)


def hyena_filter(feat, w1, b1, w2, b2, w3, freq, rates, tl):
    L = feat.shape[0]
    C = w3.shape[1]
    return pl.pallas_call(
        _hy_filter_kernel,
        grid=(L // tl,),
        in_specs=[pl.BlockSpec((tl, HY_FEAT_PAD), lambda i: (i, 0)),
                  _const_spec((HY_FEAT_PAD, HY_HID)), _const_spec((1, HY_HID)),
                  _const_spec((HY_HID, HY_HID)), _const_spec((1, HY_HID)),
                  _const_spec((HY_HID, C)), _const_spec((2, HY_HID)), _const_spec((1, C))],
        out_specs=[pl.BlockSpec((tl, C), lambda i: (i, 0)), pl.BlockSpec((1, C), lambda i: (0, 0))],
        out_shape=[jax.ShapeDtypeStruct((L, C), F32), jax.ShapeDtypeStruct((1, C), F32)],
        compiler_params=_cparams("arbitrary"),
        name="hyena_filter",
    )(feat, w1, b1, w2, b2, w3, freq, rates)


def _filter_scale(ss_ref, o):
    e = ss_ref[:, 2 * o * BR_W:(2 * o + 1) * BR_W] + ss_ref[:, (2 * o + 1) * BR_W:(2 * o + 2) * BR_W]
    return lax.rsqrt(e + EPS)


def _combine_spectrum(x, ss_ref, o, half):
    xf = x[:, 2 * o * BR_W:(2 * o + 1) * BR_W]
    xb = x[:, (2 * o + 1) * BR_W:(2 * o + 2) * BR_W]
    sc = _filter_scale(ss_ref, o)
    hr = (xf[:half] + xb[:half]) * sc
    hi = (xf[half:] - xb[half:]) * sc
    return jnp.concatenate([hr, hi], axis=0)


def _cmul(x, h, half):
    xr, xi = x[:half], x[half:]
    hr, hi = h[:half], h[half:]
    return jnp.concatenate([xr * hr - xi * hi, xr * hi + xi * hr], axis=0)


def _dft1_kernel(z_ref, f_ref, a_ref):
    a_ref[0] = _dot(f_ref[...], z_ref[0].astype(BF16)).astype(BF16)


def dft_stage1(z, f1, tn):
    B, n1, W = z.shape
    M = f1.shape[0]
    return pl.pallas_call(
        _dft1_kernel,
        grid=(B, W // tn),
        in_specs=[pl.BlockSpec((1, n1, tn), lambda b, j: (b, 0, j)), _const_spec((M, n1))],
        out_specs=pl.BlockSpec((1, M, tn), lambda b, j: (b, 0, j)),
        out_shape=jax.ShapeDtypeStruct((B, M, W), BF16),
        compiler_params=_cparams("parallel", "parallel"),
        name="dft_stage1",
    )(z, f1)


def _spec2_kernel(a_ref, g_ref, ss_ref, h_ref):
    kb = g_ref.shape[0]
    n2 = a_ref.shape[3]
    for k in range(kb):
        a = a_ref[0, :, k].reshape(2 * n2, a_ref.shape[4])
        x = _dot(g_ref[k], a)
        for o in range(2):
            h_ref[o, k] = _combine_spectrum(x, ss_ref, o, n2)


def filter_spectrum(a, g, ss, kb):
    _, _, N1, N2, C = a.shape
    return pl.pallas_call(
        _spec2_kernel,
        grid=(N1 // kb,),
        in_specs=[pl.BlockSpec((1, 2, kb, N2, C), lambda i: (0, 0, i, 0, 0)),
                  pl.BlockSpec((kb, 2 * N2, 2 * N2), lambda i: (i, 0, 0)),
                  _const_spec((1, C))],
        out_specs=pl.BlockSpec((2, kb, 2 * N2, BR_W), lambda i: (0, i, 0, 0)),
        out_shape=jax.ShapeDtypeStruct((2, N1, 2 * N2, BR_W), F32),
        compiler_params=_cparams("parallel"),
        name="filter_spectrum",
    )(a, g, ss)


def _conv2_kernel(a_ref, g_ref, gi_ref, h_ref, o_ref):
    kb = g_ref.shape[0]
    n2 = a_ref.shape[3]
    C = a_ref.shape[4]
    for k in range(kb):
        a = a_ref[0, :, k].reshape(2 * n2, C)
        y = _cmul(_dot(g_ref[k], a), h_ref[0, k], n2).astype(BF16)
        o_ref[0, :, k] = _dot(gi_ref[k], y).astype(BF16).reshape(2, n2, C)


def spectral_multiply(a, g, gi, h, o, kb):
    B, _, N1, N2, C = a.shape
    blk = pl.BlockSpec((1, 2, kb, N2, C), lambda i, b: (b, 0, i, 0, 0))
    mat = pl.BlockSpec((kb, 2 * N2, 2 * N2), lambda i, b: (i, 0, 0))
    return pl.pallas_call(
        _conv2_kernel,
        grid=(N1 // kb, B),
        in_specs=[blk, mat, mat, pl.BlockSpec((1, kb, 2 * N2, C), lambda i, b: (o, i, 0, 0))],
        out_specs=blk,
        out_shape=jax.ShapeDtypeStruct(a.shape, BF16),
        compiler_params=_cparams("parallel", "parallel"),
        name="spectral_multiply",
    )(a, g, gi, h)


def _idft1_kernel(b_ref, f_ref, z_ref, gate_ref, bias_ref, o_ref):
    y = _dot(f_ref[...], b_ref[0])
    z = z_ref[0]
    o_ref[0] = gate_ref[0] * (y + bias_ref[...] * z)


def idft_stage1(bm, fi, z, gate, bias, tn):
    B, M, W = bm.shape
    n1 = fi.shape[0]
    blk = pl.BlockSpec((1, n1, tn), lambda b, j: (b, 0, j))
    return pl.pallas_call(
        _idft1_kernel,
        grid=(B, W // tn),
        in_specs=[pl.BlockSpec((1, M, tn), lambda b, j: (b, 0, j)), _const_spec((n1, M)),
                  blk, blk, _const_spec((1, tn))],
        out_specs=blk,
        out_shape=jax.ShapeDtypeStruct((B, n1, W), F32),
        compiler_params=_cparams("parallel", "parallel"),
        name="idft_stage1",
    )(bm, fi, z, gate, bias)


@functools.lru_cache(maxsize=None)
def _dft_tables(L):
    N = 2 * L
    N2 = 128
    N1 = N // N2
    nz = L // N2
    k1 = np.arange(N1)[:, None]
    n1 = np.arange(nz)[None, :]
    th = 2 * np.pi * ((k1 * n1) % N1) / N1
    f1 = np.concatenate([np.cos(th), -np.sin(th)], axis=0)
    fi = np.concatenate([np.cos(th).T, -np.sin(th).T], axis=1) / N
    kk1 = np.arange(N1)[:, None, None]
    k2 = np.arange(N2)[None, :, None]
    n2 = np.arange(N2)[None, None, :]
    ph = 2 * np.pi * ((n2 * k2 * N1 + n2 * kk1) % N) / N
    gr, gim = np.cos(ph), -np.sin(ph)
    g = np.concatenate([np.concatenate([gr, -gim], axis=2), np.concatenate([gim, gr], axis=2)], axis=1)
    hr, him = np.swapaxes(gr, 1, 2), -np.swapaxes(gim, 1, 2)
    gi = np.concatenate([np.concatenate([hr, -him], axis=2), np.concatenate([him, hr], axis=2)], axis=1)
    return tuple(np.asarray(t, np.float32) for t in (f1, fi, g, gi))


def _spec_direct_kernel(hf_ref, f_ref, ss_ref, h_ref):
    x = _dot(f_ref[...], hf_ref[...].astype(BF16))
    half = x.shape[0] // 2
    for o in range(2):
        h_ref[o] = _combine_spectrum(x, ss_ref, o, half)


def filter_spectrum_direct(hf, f, ss):
    L, C = hf.shape
    return pl.pallas_call(
        _spec_direct_kernel,
        grid=(1,),
        in_specs=[_const_spec((L, C)), _const_spec((4 * L, L)), _const_spec((1, C))],
        out_specs=pl.BlockSpec((2, 4 * L, BR_W), lambda i: (0, 0, 0)),
        out_shape=jax.ShapeDtypeStruct((2, 4 * L, BR_W), F32),
        compiler_params=_cparams("arbitrary"),
        name="filter_spectrum_direct",
    )(hf, f, ss)


def _conv_direct_kernel(z_ref, gate_ref, bias_ref, f_ref, fi_ref, h_ref, o_ref):
    z = z_ref[0]
    x = _dot(f_ref[...], z.astype(BF16))
    y = _cmul(x, h_ref[0], x.shape[0] // 2).astype(BF16)
    o_ref[0] = gate_ref[0] * (_dot(fi_ref[...], y) + bias_ref[...] * z)


def conv_direct(z, gate, bias, f, fi, h, o):
    B, L, C = z.shape
    blk = pl.BlockSpec((1, L, C), lambda b: (b, 0, 0))
    return pl.pallas_call(
        _conv_direct_kernel,
        grid=(B,),
        in_specs=[blk, blk, _const_spec((1, C)), _const_spec((4 * L, L)), _const_spec((L, 4 * L)),
                  pl.BlockSpec((1, 4 * L, C), lambda b: (o, 0, 0))],
        out_specs=blk,
        out_shape=jax.ShapeDtypeStruct((B, L, C), F32),
        compiler_params=_cparams("parallel"),
        name="conv_direct",
    )(z, gate, bias, f, fi, h)


@functools.lru_cache(maxsize=None)
def _dft_direct_tables(L):
    N = 2 * L
    k = np.arange(N)[:, None]
    n = np.arange(L)[None, :]
    th = 2 * np.pi * ((k * n) % N) / N
    f = np.concatenate([np.cos(th), -np.sin(th)], axis=0)
    fi = np.concatenate([np.cos(th).T, -np.sin(th).T], axis=1) / N
    return np.asarray(f, np.float32), np.asarray(fi, np.float32)


@functools.lru_cache(maxsize=None)
def _filter_features(L):
    t = np.arange(L, dtype=np.float32) / np.float32(L)
    ang = (2.0 * math.pi) * t[:, None].astype(np.float64) * np.arange(1, HY_BANDS + 1)
    feat = np.zeros((L, HY_FEAT_PAD), np.float32)
    feat[:, 0] = t
    feat[:, 1:1 + HY_BANDS] = np.sin(ang)
    feat[:, 1 + HY_BANDS:HY_FEAT] = np.cos(ang)
    rates = np.abs(np.linspace(math.log(HY_TARGET) / HY_SLOW, math.log(HY_TARGET) / HY_FAST, BR_W))
    return feat, np.tile(np.asarray(rates, np.float32), 4)[None, :]


def hyena_branch(v, g1, g2, hp):
    f_w1, f_b1, f_w2, f_b2, f_w3, freq, bias = hp
    B, L, C = v.shape
    feat, rates = _filter_features(L)
    hf, ss = hyena_filter(jnp.asarray(feat), f_w1, f_b1, f_w2, f_b2, f_w3, freq, jnp.asarray(rates),
                          min(L, 512))
    if L <= 512:
        f, fi = (jnp.asarray(t).astype(BF16) for t in _dft_direct_tables(L))
        spec = filter_spectrum_direct(hf, f, ss)
        z = v
        for o, gate in enumerate((g1, g2)):
            z = conv_direct(z, gate, bias[o:o + 1], f, fi, spec, o)
        return z
    f1, fi1, g, gi = (jnp.asarray(t).astype(BF16) for t in _dft_tables(L))
    N2 = 128
    N1 = 2 * L // N2
    nz = L // N2
    kb = 8
    a = dft_stage1(hf.reshape(1, nz, N2 * 4 * C), f1, 8192)
    spec = filter_spectrum(a.reshape(1, 2, N1, N2, 4 * C), g, ss, kb)
    W = N2 * C
    tn = 8192
    z = v.reshape(B, nz, W)
    for o, gate in enumerate((g1, g2)):
        a = dft_stage1(z, f1, tn).reshape(B, 2, N1, N2, C)
        bm = spectral_multiply(a, g, gi, spec, o, kb).reshape(B, 2 * N1, W)
        z = idft_stage1(bm, fi1, z, gate.reshape(B, nz, W), jnp.tile(bias[o:o + 1], (1, tn // C)), tn)
    return z.reshape(B, L, C)


def _head_sum(x, seg_ref):
    return _dot(x, seg_ref[...], precision=HI)


def _rw_prep_kernel(p_ref, pp_ref, pn_ref, mu_ref, wl_ref, w0_ref, a0_ref, kv_ref, seg_ref,
                    r_ref, v_ref, na_ref, w_ref, kt_ref, b_ref, bonus_ref, g_ref, *, on_grid):
    i = pl.program_id(1)
    first = i == 0
    last = i == pl.num_programs(1) - 1
    x = p_ref[0]
    tm = x.shape[0]
    grp = lax.broadcasted_iota(jnp.int32, x.shape, 1) % 4
    prev, nxt = pp_ref[0], pn_ref[0]
    if on_grid:
        col = (lax.broadcasted_iota(jnp.int32, (tm, 1), 0) + i * tm) % GRID_W
        left = jnp.where(col == 0, 0.0, pltpu.roll(x, 1, 0))
        right = jnp.where(col == GRID_W - 1, 0.0, pltpu.roll(x, tm - 1, 0))
        up = _shift_rows(x, prev, nxt, GRID_W, first, last)
        down = _shift_rows(x, prev, nxt, -GRID_W, first, last)
        shifted = jnp.where(grp == 0, left, jnp.where(grp == 1, right, jnp.where(grp == 2, up, down)))
    else:
        before = _shift_rows(x, prev, nxt, 1, first, last)
        after = _shift_rows(x, prev, nxt, -1, first, last)
        shifted = jnp.where(grp % 2 == 0, before, after)
    xx = x + (shifted - x) * mu_ref[...]
    r = xx[:, 0:BR_W]
    k = xx[:, BR_W:2 * BR_W]
    v = xx[:, 2 * BR_W:3 * BR_W]
    lo = xx[:, 3 * BR_W:4 * BR_W]
    ll = lax.broadcasted_iota(jnp.int32, lo.shape, 1)
    act = jnp.where(ll < RW_LORA, jnp.tanh(lo), jnp.where(ll < 2 * RW_LORA, lo, _sigmoid(lo)))
    z = _dot(act.astype(BF16), wl_ref[...])
    kk = k * kv_ref[0:1, :]
    kk = kk * lax.rsqrt(_head_sum(kk * kk, seg_ref) + 1e-12)
    r_ref[0] = r
    v_ref[0] = v
    na_ref[0] = -kk
    bonus_ref[0] = _head_sum(r * k * kv_ref[2:3, :], seg_ref) * v
    g_ref[0] = z[:, 4 * BR_W:5 * BR_W]
    for d in range(2):
        logw = -_softplus(-(w0_ref[d:d + 1, :] + z[:, d * BR_W:(d + 1) * BR_W])) - 0.5
        w_ref[d, 0] = jnp.exp(-jnp.exp(logw))
        a = _sigmoid(a0_ref[d:d + 1, :] + z[:, (2 + d) * BR_W:(3 + d) * BR_W])
        kt_ref[d, 0] = k * (1.0 + (a - 1.0) * kv_ref[1:2, :])
        b_ref[d, 0] = kk * a


def rwkv_prep(p, mu, w_lora, w0, a0, kvec, seg, on_grid, tm):
    B, T, _ = p.shape
    halo = GRID_W if on_grid else 8
    main, prev, nxt = _halo_specs(tm, T, RW_PAD, lambda b, i: OFF_RW // RW_PAD, halo)
    one = pl.BlockSpec((1, tm, BR_W), lambda b, i: (b, i, 0))
    two = pl.BlockSpec((2, 1, tm, BR_W), lambda b, i: (0, b, i, 0))
    s1 = jax.ShapeDtypeStruct((B, T, BR_W), F32)
    s2 = jax.ShapeDtypeStruct((2, B, T, BR_W), F32)
    return pl.pallas_call(
        functools.partial(_rw_prep_kernel, on_grid=on_grid),
        grid=(B, T // tm),
        in_specs=[main, prev, nxt, _const_spec((1, RW_PAD)), _const_spec((BR_W, 5 * BR_W)),
                  _const_spec((2, BR_W)), _const_spec((2, BR_W)), _const_spec((3, BR_W)),
                  _const_spec((BR_W, BR_W))],
        out_specs=[one, one, one, two, two, two, one, one],
        out_shape=[s1, s1, s1, s2, s2, s2, s1, s1],
        compiler_params=_cparams("parallel", "parallel"),
        name="rwkv_prep",
    )(p, p, p, mu, w_lora, w0, a0, kvec, seg)


RW_VH = RW_HEAD // 2
LANE_W = 128
N_KEYED = 5
RW_UNROLL = 4


def _chain_rows(refs, t, tb, shifts):
    blocks = []
    for d in range(2):
        full = refs[d][t if d == 0 else tb]
        for half in range(2):
            x = full[:, half * LANE_W:(half + 1) * LANE_W]
            for sh in shifts:
                blocks.append(x if sh == 0 else pltpu.roll(x, LANE_W - sh, 1))
    return blocks


def _rw_scan_kernel(*refs, B, Tt):
    nin = 2 * (N_KEYED + 1)
    ins = refs[:nin]
    s0_ref = refs[nin]
    y_refs = refs[nin + 1:nin + 3]
    s_ref = refs[nin + 3]
    bufs = [refs[nin + 4 + 3 * u:nin + 7 + 3 * u] for u in range(RW_UNROLL)]
    NL = 16 * B
    A_TILE = 3

    @pl.when(pl.program_id(0) == 0)
    def _():
        s_ref[...] = s0_ref[...]

    def keyed_job(t, slot, n):
        def issue():
            rows = _chain_rows(ins[2 * n:2 * n + 2], t, Tt - 1 - t, (0, RW_HEAD))
            return jnp.concatenate(rows + rows, axis=0).T[0:RW_HEAD, :]

        def commit(val):
            bufs[slot][0][n] = val
        return issue, commit

    def values_job(t, slot):
        def issue():
            rows = []
            for vs in range(2):
                rows += _chain_rows(ins[2 * N_KEYED:], t, Tt - 1 - t, (vs * RW_VH, RW_HEAD + vs * RW_VH))
            return jnp.concatenate(rows, axis=0).T[0:RW_VH, :]

        def commit(val):
            bufs[slot][1][...] = val
        return issue, commit

    def output_job(t, slot):
        def issue():
            ys = bufs[slot][2][...]
            y = jnp.concatenate([ys, jnp.zeros((LANE_W - RW_VH, NL), F32)], axis=0).T
            out = []
            for d in range(2):
                halves = []
                for half in range(2):
                    acc = None
                    for vs in range(2):
                        for h2 in range(2):
                            row0 = (((vs * 2 + d) * 2 + half) * 2 + h2) * B
                            blk = y[row0:row0 + B, :]
                            sh = h2 * RW_HEAD + vs * RW_VH
                            blk = blk if sh == 0 else pltpu.roll(blk, sh, 1)
                            acc = blk if acc is None else acc + blk
                    halves.append(acc)
                out.append(jnp.concatenate(halves, axis=1))
            return out

        def commit(val):
            y_refs[0][t] = val[0]
            y_refs[1][Tt - 1 - t] = val[1]
        return issue, commit

    def input_jobs(t, slot):
        order = (A_TILE,) + tuple(n for n in range(N_KEYED) if n != A_TILE)
        return [keyed_job(t, slot, n) for n in order] + [values_job(t, slot)]

    def step_pieces(slot):
        tile_ref, vt_ref, ys_ref = bufs[slot]
        row = lambda n, k: tile_ref[n, pl.ds(k, 1), :]
        groups = range(RW_VH // 8)
        sa = [[None, None] for _ in groups]
        ys = [[None, None] for _ in groups]
        tot, vt = {}, {}

        def dot_a(k0, k1):
            for k in range(k0, k1):
                a_k = row(A_TILE, k)
                for g in groups:
                    p = s_ref[g, k] * a_k
                    sa[g][k % 2] = p if sa[g][k % 2] is None else sa[g][k % 2] + p

        def update(k0, k1):
            for k in range(k0, k1):
                r_k, w_k, kt_k, b_k = row(0, k), row(1, k), row(2, k), row(4, k)
                for g in groups:
                    if g not in tot:
                        tot[g] = sa[g][0] + sa[g][1]
                        vt[g] = vt_ref[g * 8:(g + 1) * 8, :]
                    s = s_ref[g, k] * w_k + tot[g] * b_k + vt[g] * kt_k
                    s_ref[g, k] = s
                    p = s * r_k
                    ys[g][k % 2] = p if ys[g][k % 2] is None else ys[g][k % 2] + p
            if k1 == RW_HEAD:
                for g in groups:
                    ys_ref[g * 8:(g + 1) * 8, :] = ys[g][0] + ys[g][1]

        half = RW_HEAD // 2
        cuts = [0, 13, 26, 39, 52, RW_HEAD]
        return ([functools.partial(dot_a, 0, half), functools.partial(dot_a, half, RW_HEAD)]
                + [functools.partial(update, cuts[i], cuts[i + 1]) for i in range(5)])

    def run(pieces, jobs):
        for i, piece in enumerate(pieces):
            val = jobs[i][0]() if i < len(jobs) else None
            piece()
            if i < len(jobs):
                jobs[i][1](val)

    for issue, commit in input_jobs(0, 0):
        commit(issue())

    def body(j, carry):
        t0 = RW_UNROLL * j
        for u in range(RW_UNROLL):
            t_next = t0 + u + 1 if u + 1 < RW_UNROLL else jnp.minimum(t0 + RW_UNROLL, Tt - 1)
            jobs = input_jobs(t_next, (u + 1) % RW_UNROLL)
            if u > 0:
                jobs.append(output_job(t0 + u - 1, u - 1))
            run(step_pieces(u), jobs)
        issue, commit = output_job(t0 + RW_UNROLL - 1, RW_UNROLL - 1)
        commit(issue())
        return carry

    lax.fori_loop(0, Tt // RW_UNROLL, body, 0)


def rwkv_mix(prep, s0, tt):
    r, v, na, w, kt, b = prep
    T, B, _ = r.shape
    nb = T // tt
    tblk = lambda i, d: i if d == 0 else nb - 1 - i

    in_specs, args = [], []
    for x, per_dir in ((r, False), (w, True), (kt, True), (na, False), (b, True), (v, False)):
        for d in range(2):
            if per_dir:
                in_specs.append(pl.BlockSpec((None, tt, B, BR_W), lambda i, d=d: (d, tblk(i, d), 0, 0)))
            else:
                in_specs.append(pl.BlockSpec((tt, B, BR_W), lambda i, d=d: (tblk(i, d), 0, 0)))
            args.append(x)
    NL = 16 * B
    sblk = pl.BlockSpec((RW_VH // 8, RW_HEAD, 8, NL), lambda i: (0, 0, 0, 0))
    yspecs = [pl.BlockSpec((tt, B, BR_W), lambda i, d=d: (tblk(i, d), 0, 0)) for d in range(2)]
    ysh = jax.ShapeDtypeStruct((T, B, BR_W), F32)
    out = pl.pallas_call(
        functools.partial(_rw_scan_kernel, B=B, Tt=tt),
        grid=(nb,),
        in_specs=in_specs + [sblk],
        out_specs=yspecs + [sblk],
        out_shape=[ysh] * 2 + [jax.ShapeDtypeStruct((RW_VH // 8, RW_HEAD, 8, NL), F32)],
        scratch_shapes=[pltpu.VMEM((N_KEYED, RW_HEAD, NL), F32), pltpu.VMEM((RW_VH, NL), F32),
                        pltpu.VMEM((RW_VH, NL), F32)] * RW_UNROLL,
        compiler_params=_cparams("arbitrary"),
        name="rwkv_scan",
    )(*args, s0)
    return out[0], out[1], out[2]


def _gelu_tanh(x):
    return 0.5 * x * (1.0 + jnp.tanh(math.sqrt(2.0 / math.pi) * (x + 0.044715 * (x * x * x))))


def _lru_prep_kernel(p_ref, pp_ref, pn_ref, cw_ref, cb_ref, w_ref, bias_ref, lam_ref, a_ref, b_ref, gg_ref):
    i = pl.program_id(1)
    first = i == 0
    last = i == pl.num_programs(1) - 1
    x = p_ref[0][:, 0:BR_W]
    prev = pp_ref[0][:, 0:BR_W]
    nxt = pn_ref[0][:, 0:BR_W]
    xc = cb_ref[...] + cw_ref[0:1, :] * _shift_rows(x, prev, nxt, 2, first, last)
    xc = xc + cw_ref[1:2, :] * _shift_rows(x, prev, nxt, 1, first, last)
    xc = xc + cw_ref[2:3, :] * x
    xc = xc + cw_ref[3:4, :] * _shift_rows(x, prev, nxt, -1, first, last)
    z = _dot(xc.astype(BF16), w_ref[...])
    for d in range(2):
        r = _sigmoid(z[:, 2 * d * BR_W:(2 * d + 1) * BR_W] + bias_ref[2 * d:2 * d + 1, :])
        gi = _sigmoid(z[:, (2 * d + 1) * BR_W:(2 * d + 2) * BR_W] + bias_ref[2 * d + 1:2 * d + 2, :])
        log_a = -LRU_C * r * _softplus(-lam_ref[d:d + 1, :])
        a_ref[d, 0] = jnp.exp(log_a)
        b_ref[d, 0] = jnp.sqrt(1.0 - jnp.exp(2.0 * log_a)) * (gi * xc)
    gg_ref[0] = _gelu_tanh(p_ref[0][:, BR_W:2 * BR_W])


def lru_prep(p, conv_w, conv_b, w_blk, bias, lam, tm):
    B, T, _ = p.shape
    main, prev, nxt = _halo_specs(tm, T, LRU_IN, lambda b, i: OFF_LRU // LRU_IN, 8)
    two = pl.BlockSpec((2, 1, tm, BR_W), lambda b, i: (0, b, i, 0))
    s2 = jax.ShapeDtypeStruct((2, B, T, BR_W), F32)
    return pl.pallas_call(
        _lru_prep_kernel,
        grid=(B, T // tm),
        in_specs=[main, prev, nxt, _const_spec((4, BR_W)), _const_spec((1, BR_W)),
                  _const_spec((BR_W, 4 * BR_W)), _const_spec((4, BR_W)), _const_spec((2, BR_W))],
        out_specs=[two, two, pl.BlockSpec((1, tm, BR_W), lambda b, i: (b, i, 0))],
        out_shape=[s2, s2, jax.ShapeDtypeStruct((B, T, BR_W), F32)],
        compiler_params=_cparams("parallel", "parallel"),
        name="lru_prep",
    )(p, p, p, conv_w, conv_b, w_blk, bias, lam)


def _affine_scan(a, b, reverse):
    tb = a.shape[0]
    row = lax.broadcasted_iota(jnp.int32, (tb, 1), 0)
    s = 1
    while s < tb:
        sh = tb - s if reverse else s
        ok = (row < tb - s) if reverse else (row >= s)
        a_s = pltpu.roll(a, sh, 0)
        b_s = pltpu.roll(b, sh, 0)
        b = jnp.where(ok, a * b_s + b, b)
        a = jnp.where(ok, a * a_s, a)
        s *= 2
    return a, b


def _lru_scan_kernel(af_ref, bf_ref, ab_ref, bb_ref, h0_ref, hf_ref, hb_ref, fin_ref):
    @pl.when(pl.program_id(1) == 0)
    def _():
        fin_ref[...] = h0_ref[...]

    tb = af_ref.shape[2]
    a, b = _affine_scan(af_ref[0, 0], bf_ref[0, 0], False)
    h = b + a * fin_ref[0, 0]
    hf_ref[0] = h
    fin_ref[0, 0] = h[tb - 1:tb, :]
    a, b = _affine_scan(ab_ref[0, 0], bb_ref[0, 0], True)
    h = b + a * fin_ref[1, 0]
    hb_ref[0] = h
    fin_ref[1, 0] = h[0:1, :]


def lru_scan(a, b, h0, tb):
    _, B, T, C = a.shape
    nb = T // tb
    fwd = pl.BlockSpec((1, 1, tb, C), lambda bi, i: (0, bi, i, 0))
    bwd = pl.BlockSpec((1, 1, tb, C), lambda bi, i: (1, bi, nb - 1 - i, 0))
    st = pl.BlockSpec((2, 1, 1, C), lambda bi, i: (0, bi, 0, 0))
    return pl.pallas_call(
        _lru_scan_kernel,
        grid=(B, nb),
        in_specs=[fwd, fwd, bwd, bwd, st],
        out_specs=[pl.BlockSpec((1, tb, C), lambda bi, i: (bi, i, 0)),
                   pl.BlockSpec((1, tb, C), lambda bi, i: (bi, nb - 1 - i, 0)), st],
        out_shape=[jax.ShapeDtypeStruct((B, T, C), F32), jax.ShapeDtypeStruct((B, T, C), F32),
                   jax.ShapeDtypeStruct((2, B, 1, C), F32)],
        compiler_params=_cparams("parallel", "arbitrary"),
        name="lru_scan",
    )(a, b, a, b, h0)


def _rope(x, cos, sin):
    q4 = RET_HEAD // 4
    lane = lax.broadcasted_iota(jnp.int32, x.shape, 1) % (2 * q4)
    partner = jnp.where(lane < q4, pltpu.roll(x, x.shape[1] - q4, 1), pltpu.roll(x, q4, 1))
    return x * cos + partner * sin


def _ret_dir(x, cos, sin, s, glane, gtile_ref, d, reverse):
    C = x.shape[0]
    q = x[:, 0:BR_W]
    k = x[:, BR_W:2 * BR_W]
    v = x[:, 2 * BR_W:3 * BR_W].astype(BF16)
    if cos is not None:
        q = _rope(q, cos, sin)
        k = _rope(k, cos, sin)
    k = k * (RET_HEAD ** -0.5)
    lg = -_softplus(-glane)
    idx = lax.broadcasted_iota(jnp.int32, (C, 1), 0).astype(F32)
    steps_in = (C - idx) if reverse else (idx + 1.0)
    steps_out = idx if reverse else (C - 1.0 - idx)
    ri = lax.broadcasted_iota(jnp.int32, (C, C), 0)
    ci = lax.broadcasted_iota(jnp.int32, (C, C), 1)
    diff = ((ci - ri) if reverse else (ri - ci)).astype(F32)
    lane_head = lax.broadcasted_iota(jnp.int32, (1, BR_W), 1) // RET_HEAD
    qb = q.astype(BF16)
    kb = k.astype(BF16)
    y = _dot(qb, s.astype(BF16)) * jnp.exp(steps_in * lg)
    for h in range(RET_HEADS):
        lg_h = -_softplus(-gtile_ref[d, h][0:1, :])
        dm = jnp.where(diff >= 0, jnp.exp(diff * lg_h), 0.0)
        mh = lane_head == h
        sc = lax.dot_general(jnp.where(mh, qb, jnp.zeros_like(qb)), kb, (((1,), (1,)), ((), ())),
                             preferred_element_type=F32)
        y = y + jnp.where(mh, _dot((sc * dm).astype(BF16), v), 0.0)
    kd = (k * jnp.exp(steps_out * lg)).astype(BF16)
    upd = lax.dot_general(kd, v, (((0,), (0,)), ((), ())), preferred_element_type=F32)
    rh = lax.broadcasted_iota(jnp.int32, (BR_W, BR_W), 0) // RET_HEAD
    ch = lax.broadcasted_iota(jnp.int32, (BR_W, BR_W), 1) // RET_HEAD
    s = s * jnp.exp(C * lg) + jnp.where(rh == ch, upd, 0.0)
    return y, s


def _ret_kernel(*refs, rope):
    if rope:
        xf_ref, xb_ref, cf_ref, sf_ref, cb_ref, sb_ref, gl_ref, gt_ref, s0_ref, yf_ref, yb_ref, s_ref = refs
    else:
        xf_ref, xb_ref, gl_ref, gt_ref, s0_ref, yf_ref, yb_ref, s_ref = refs

    @pl.when(pl.program_id(1) == 0)
    def _():
        s_ref[...] = s0_ref[...]

    y, s = _ret_dir(xf_ref[0], cf_ref[...] if rope else None, sf_ref[...] if rope else None,
                    s_ref[0, 0], gl_ref[0:1, :], gt_ref, 0, False)
    yf_ref[0] = y
    s_ref[0, 0] = s
    y, s = _ret_dir(xb_ref[0], cb_ref[...] if rope else None, sb_ref[...] if rope else None,
                    s_ref[1, 0], gl_ref[1:2, :], gt_ref, 1, True)
    yb_ref[0] = y
    s_ref[1, 0] = s


def retention(p, cos, sin, glane, gtile, s0, rope):
    B, T, _ = p.shape
    C = RET_CHUNK
    nc = T // C
    cb = OFF_RET // RET_IN
    xf = pl.BlockSpec((1, C, RET_IN), lambda b, i: (b, i, cb))
    xb = pl.BlockSpec((1, C, RET_IN), lambda b, i: (b, nc - 1 - i, cb))
    tf = pl.BlockSpec((C, BR_W), lambda b, i: (i, 0))
    tb = pl.BlockSpec((C, BR_W), lambda b, i: (nc - 1 - i, 0))
    st = pl.BlockSpec((2, 1, BR_W, BR_W), lambda b, i: (0, b, 0, 0))
    ins = [xf, xb] + ([tf, tf, tb, tb] if rope else []) + [
        _const_spec((2, BR_W)), _const_spec((2, RET_HEADS, 8, C)), st]
    args = [p, p] + ([cos, sin, cos, sin] if rope else []) + [glane, gtile, s0]
    return pl.pallas_call(
        functools.partial(_ret_kernel, rope=rope),
        grid=(B, nc),
        in_specs=ins,
        out_specs=[pl.BlockSpec((1, C, BR_W), lambda b, i: (b, i, 0)),
                   pl.BlockSpec((1, C, BR_W), lambda b, i: (b, nc - 1 - i, 0)), st],
        out_shape=[jax.ShapeDtypeStruct((B, T, BR_W), F32), jax.ShapeDtypeStruct((B, T, BR_W), F32),
                   jax.ShapeDtypeStruct((2, B, BR_W, BR_W), F32)],
        compiler_params=_cparams("parallel", "arbitrary"),
        name="retention",
    )(*args)


@functools.lru_cache(maxsize=None)
def _rope_tables(T):
    pos = np.arange(T)
    q4 = RET_HEAD // 4
    inv = ROPE_BASE ** (-np.arange(q4, dtype=np.float64) / q4)
    cos = np.zeros((T, RET_HEAD))
    sin = np.zeros((T, RET_HEAD))
    for part, coord in enumerate((pos // GRID_W, pos % GRID_W)):
        ang = coord[:, None] * inv
        base = part * 2 * q4
        cos[:, base:base + q4] = np.cos(ang)
        cos[:, base + q4:base + 2 * q4] = np.cos(ang)
        sin[:, base:base + q4] = -np.sin(ang)
        sin[:, base + q4:base + 2 * q4] = np.sin(ang)
    tile = lambda t: np.asarray(np.tile(t, (1, RET_HEADS)), np.float32)
    return tile(cos), tile(sin)


def _head_norm(y, seg_ref, eps):
    mu = _head_sum(y, seg_ref) * (1.0 / RW_HEAD)
    yc = y - mu
    var = _head_sum(yc * yc, seg_ref) * (1.0 / RW_HEAD)
    return yc * lax.rsqrt(var + eps)


def _merge_kernel(x_ref, g_ref, sh_ref, sc_ref, gt_ref, hy_ref, ryf_ref, ryb_ref, rbon_ref, rg_ref,
                  lhf_ref, lhb_ref, lgg_ref, tyf_ref, tyb_ref, tg_ref, lng_ref, seg_ref,
                  wg_ref, br_ref, wo_ref, o_ref, m_ref):
    x = x_ref[0]
    u = _norm_mod(x, g_ref[...], sh_ref[0], sc_ref[0]).astype(BF16)
    y_rw = (_head_norm(ryf_ref[0] + ryb_ref[0], seg_ref, RW_LN_EPS) * lng_ref[...] + rbon_ref[0]) * rg_ref[0]
    y_lru = (lhf_ref[0] + lhb_ref[0]) * lgg_ref[0]
    y_ret = _head_norm(tyf_ref[0] + tyb_ref[0], seg_ref, RET_LN_EPS) * _silu(tg_ref[0])
    ys = [y.astype(BF16) for y in (hy_ref[0], y_rw, y_lru, y_ret)]
    D = x.shape[1]
    cw = 256
    for c in range(D // cw):
        acc = None
        for n in range(N_BRANCH):
            gate = _sigmoid(_dot(u, wg_ref[:, n * D + c * cw:n * D + (c + 1) * cw]))
            t = gate * _dot(ys[n], br_ref[n, :, c * cw:(c + 1) * cw])
            acc = t if acc is None else acc + t
        m_ref[:, c * cw:(c + 1) * cw] = acc.astype(BF16)
    o_ref[0] = x + gt_ref[0] * _dot(m_ref[...], wo_ref[...])


def merge(x, g, sh, sc, gt, p, y_hy, rw, lru, ret, ln_g, seg, w_gate, br, w_out, tm):
    B, T, D = x.shape
    vec = pl.BlockSpec((1, 1, D), lambda b, i: (b, 0, 0))
    row = pl.BlockSpec((1, tm, D), lambda b, i: (b, i, 0))
    brn = pl.BlockSpec((1, tm, BR_W), lambda b, i: (b, i, 0))
    tg = pl.BlockSpec((1, tm, BR_W), lambda b, i: (b, i, (OFF_RET + 3 * BR_W) // BR_W))
    return pl.pallas_call(
        _merge_kernel,
        grid=(B, T // tm),
        in_specs=[row, _const_spec((1, D)), vec, vec, vec] + [brn] * 10 + [tg] + [
            _const_spec((1, BR_W)), _const_spec((BR_W, BR_W)), _const_spec((D, GATE_IN)),
            _const_spec((N_BRANCH, BR_W, D)), _const_spec((D, D))],
        out_specs=row,
        out_shape=jax.ShapeDtypeStruct((B, T, D), F32),
        scratch_shapes=[pltpu.VMEM((tm, D), BF16)],
        compiler_params=_cparams("parallel", "parallel"),
        name="merge",
    )(x, g, sh, sc, gt, y_hy, *rw, *lru, *ret, p, ln_g, seg, w_gate, br, w_out)


def _block_diag(w):
    G = w.shape[-3]
    eye = jnp.eye(G, dtype=w.dtype)
    full = w[..., :, :, None, :] * eye[:, None, :, None]
    return full.reshape(*w.shape[:-3], G * w.shape[-2], G * w.shape[-1])


def _mixers(p, lp, states, on_grid, with_output, tiles):
    B, T, _ = p.shape
    tm, tt, _ = tiles
    r, v, na, w, kt, b, bonus, g = rwkv_prep(p, *lp['rw'], on_grid, tm)
    time_major = lambda t: jnp.swapaxes(t, -3, -2)
    y_f, y_b, rw_fin = rwkv_mix(tuple(time_major(t) for t in (r, v, na, w, kt, b)), states[0], tt)
    y_f, y_b = time_major(y_f), time_major(y_b)
    a, bb, gg = lru_prep(p, *lp['lru'], tm)
    h_f, h_b, lru_fin = lru_scan(a, bb, states[1], tm)
    cos, sin = (jnp.asarray(t) for t in _rope_tables(T)) if on_grid else (None, None)
    t_f, t_b, ret_fin = retention(p, cos, sin, *lp['ret'], states[2], on_grid)
    fins = (rw_fin, lru_fin, ret_fin)
    if not with_output:
        return None, fins
    vg = hyena_prep(p, *lp['hy_conv'], tm)
    y_hy = hyena_branch(vg[0], vg[1], vg[2], lp['hy'])
    return (y_hy, (y_f, y_b, bonus, g), (h_f, h_b, gg), (t_f, t_b)), fins


def kernel(x, c, ctx, c_ctx, w_mod, b_mod, norm1_g, norm2_g, w_in, hy_conv_w, hy_conv_b, hy_f_w1, hy_f_b1, hy_f_w2, hy_f_b2, hy_f_w3, hy_freq, hy_bias, rw_mu, rw_w0, rw_w2, rw_a0, rw_a2, rw_g2, rw_kk, rw_ka, rw_rk, rw_ln_g, lru_conv_w, lru_conv_b, lru_wa, lru_ba, lru_wx, lru_bx, lru_lam, ret_gamma, br_proj, w_out, ffn_w1, ffn_w2, final_g):
    B, T, D = x.shape
    TC = ctx.shape[1]
    L = w_in.shape[0]

    s0, s1, s2, s3 = HY_IN, HY_IN + RW_IN, HY_IN + RW_IN + LRU_IN, HY_IN + RW_IN + LRU_IN + RET_IN
    w_branch = jnp.concatenate([w_in[:, :, s0:s1], jnp.zeros((L, D, RW_PAD - RW_IN), w_in.dtype),
                                w_in[:, :, s2:s3], w_in[:, :, s1:s2], w_in[:, :, 0:s0]], axis=2).astype(BF16)
    w_gate = w_in[:, :, s3:].astype(BF16)
    mu = jnp.pad(rw_mu, ((0, 0), (0, RW_PAD - RW_IN)))[:, None, :]
    w_lora = jnp.zeros((L, BR_W, 5 * BR_W), F32)
    w_lora = w_lora.at[:, 0:RW_LORA, 0:BR_W].set(rw_w2[:, 0]).at[:, 0:RW_LORA, BR_W:2 * BR_W].set(rw_w2[:, 1])
    w_lora = w_lora.at[:, RW_LORA:2 * RW_LORA, 2 * BR_W:3 * BR_W].set(rw_a2[:, 0])
    w_lora = w_lora.at[:, RW_LORA:2 * RW_LORA, 3 * BR_W:4 * BR_W].set(rw_a2[:, 1])
    w_lora = w_lora.at[:, 2 * RW_LORA:3 * RW_LORA, 4 * BR_W:5 * BR_W].set(rw_g2).astype(BF16)
    kvec = jnp.stack([rw_kk, rw_ka, rw_rk], axis=1)
    seg = jnp.asarray(np.kron(np.eye(RW_HEADS), np.ones((RW_HEAD, RW_HEAD))), F32)
    lru_w = jnp.concatenate([_block_diag(lru_wa[:, 0]), _block_diag(lru_wx[:, 0]),
                             _block_diag(lru_wa[:, 1]), _block_diag(lru_wx[:, 1])], axis=2).astype(BF16)
    lru_bias = jnp.stack([lru_ba[:, 0], lru_bx[:, 0], lru_ba[:, 1], lru_bx[:, 1]], axis=1)
    glane = jnp.repeat(ret_gamma, RET_HEAD, axis=2)
    gtile = jnp.broadcast_to(ret_gamma[:, :, :, None, None], (L, 2, RET_HEADS, 8, RET_CHUNK))
    f_w1 = jnp.pad(hy_f_w1, ((0, 0), (0, HY_FEAT_PAD - HY_FEAT), (0, 0)))
    br_b = br_proj.astype(BF16)
    w_out_b = w_out.astype(BF16)
    ffn_gate = ffn_w1[:, :, :D_FF].astype(BF16)
    ffn_up_w = ffn_w1[:, :, D_FF:].astype(BF16)
    ffn_w2_b = ffn_w2.astype(BF16)

    cc = jnp.concatenate([c, c_ctx[None, :], jnp.zeros((16 - B - 1, D), F32)], axis=0)
    mods = modulation(cc, w_mod.astype(BF16), b_mod[:, None, :])

    zero_states = (jnp.zeros((RW_VH // 8, RW_HEAD, 8, 16 * B), F32), jnp.zeros((2, B, 1, BR_W), F32),
                   jnp.zeros((2, B, BR_W, BR_W), F32))
    xc = ctx
    for l in range(L):
        last = l == L - 1
        lp = {
            'rw': (mu[l], w_lora[l], rw_w0[l], rw_a0[l], kvec[l], seg),
            'lru': (lru_conv_w[l], lru_conv_b[l][None, :], lru_w[l], lru_bias[l], lru_lam[l]),
            'ret': (glane[l], gtile[l]),
            'hy_conv': (hy_conv_w[l], hy_conv_b[l][None, :]),
            'hy': (f_w1[l], hy_f_b1[l][None, :], hy_f_w2[l], hy_f_b2[l][None, :], hy_f_w3[l], hy_freq[l],
                   hy_bias[l]),
        }
        g1 = norm1_g[l][None, :]
        g2 = norm2_g[l][None, :]
        m_lat = [m[:, None, :] for m in jnp.split(mods[l, :B], 6, axis=-1)]
        m_ctx = [jnp.broadcast_to(m[None, :, :], (B, 1, D)) for m in jnp.split(mods[l, B:B + 1], 6, axis=-1)]
        ln_g = rw_ln_g[l][None, :]

        def layer(xs, m, states, on_grid, with_output, tiles, final):
            p = in_projection(xs, g1, m[0], m[1], w_branch[l], tiles[0])
            br, fins = _mixers(p, lp, states, on_grid, with_output, tiles)
            if not with_output:
                return None, fins
            xs = merge(xs, g1, m[0], m[1], m[2], p, br[0], br[1], br[2], br[3], ln_g, seg,
                       w_gate[l], br_b[l], w_out_b[l], tiles[2])
            h = ffn_up(xs, g2, m[3], m[4], ffn_gate[l], ffn_up_w[l], tiles[0])
            xs = ffn_down(h, xs, m[5], ffn_w2_b[l], final_g[None, :], final, tiles[0])
            return xs, fins

        xc_new, ctx_states = layer(xc, m_ctx, zero_states, False, not last, (TC, 64, TC), False)
        x, _ = layer(x, m_lat, ctx_states, True, True, (512, 64, 256), last)
        if not last:
            xc = xc_new
    return x
```

```python
import functools
import math

import numpy as np
import jax
import jax.numpy as jnp
from jax import lax
from jax.experimental import pallas as pl
from jax.experimental.pallas import tpu as pltpu

F32 = jnp.float32
BF16 = jnp.bfloat16

D_MODEL = 1024
DEPTH = 4
GRID_W = 64
N_BRANCH = 4
BR_W = D_MODEL // N_BRANCH

HY_BANDS = 8
HY_FEAT = 1 + 2 * HY_BANDS
HY_FEAT_PAD = 32
HY_HID = 64
HY_TARGET = 1e-2
HY_FAST = 0.3
HY_SLOW = 1.5
HY_IN = 3 * BR_W

RW_HEAD = 64
RW_HEADS = BR_W // RW_HEAD
RW_LORA = 64
RW_IN = 3 * BR_W + 3 * RW_LORA
RW_PAD = 4 * BR_W
RW_LN_EPS = 64e-5

LRU_BLOCKS = 4
LRU_BLOCK = BR_W // LRU_BLOCKS
LRU_C = 8.0
LRU_IN = 2 * BR_W

RET_HEADS = 4
RET_HEAD = BR_W // RET_HEADS
RET_CHUNK = 128
ROPE_BASE = 10000.0
RET_IN = 4 * BR_W
RET_LN_EPS = 1e-5

GATE_IN = N_BRANCH * D_MODEL
D_FF = ((8 * D_MODEL // 3 + 255) // 256) * 256
EPS = 1e-6

OFF_RW = 0
OFF_RET = OFF_RW + RW_PAD
OFF_HY = 3 * HY_IN
OFF_LRU = OFF_HY + HY_IN
N_BR = OFF_LRU + LRU_IN
HY_GAP = OFF_HY - (OFF_RET + RET_IN)

VMEM_LIMIT = 56 * 1024 * 1024
HI = lax.Precision.HIGHEST


def _cparams(*sem):
    return pltpu.CompilerParams(dimension_semantics=sem, vmem_limit_bytes=VMEM_LIMIT)


def _const_spec(shape):
    nd = len(shape)
    return pl.BlockSpec(shape, lambda *_: (0,) * nd, pipeline_mode=pl.Buffered(1))


def _dot(a, b, **kw):
    return jnp.dot(a, b, preferred_element_type=F32, **kw)


def _norm_mod(x, g, sh, sc):
    ms = jnp.mean(x * x, axis=-1, keepdims=True)
    return x * lax.rsqrt(ms + EPS) * g * (1.0 + sc) + sh


def _sigmoid(x):
    return 1.0 / (1.0 + jnp.exp(-x))


def _silu(x):
    return x * _sigmoid(x)


def _softplus(x):
    return jnp.maximum(x, 0.0) + jnp.log(1.0 + jnp.exp(-jnp.abs(x)))


def _mod_kernel(c_ref, w_ref, b_ref, o_ref):
    c = c_ref[...]
    o_ref[0] = _dot(_silu(c).astype(BF16), w_ref[0]) + b_ref[0]


def modulation(cc, w_mod, b_mod):
    L, D, N = w_mod.shape
    tn = 1536
    return pl.pallas_call(
        _mod_kernel,
        grid=(L, N // tn),
        in_specs=[pl.BlockSpec((16, D), lambda l, j: (0, 0)),
                  pl.BlockSpec((1, D, tn), lambda l, j: (l, 0, j)),
                  pl.BlockSpec((1, 1, tn), lambda l, j: (l, 0, j))],
        out_specs=pl.BlockSpec((1, 16, tn), lambda l, j: (l, 0, j)),
        out_shape=jax.ShapeDtypeStruct((L, 16, N), F32),
        compiler_params=_cparams("parallel", "parallel"),
        name="modulation",
    )(cc, w_mod, b_mod)


def _inproj_kernel(x_ref, g_ref, sh_ref, sc_ref, w_ref, o_ref):
    u = _norm_mod(x_ref[0], g_ref[...], sh_ref[0], sc_ref[0]).astype(BF16)
    n = w_ref.shape[1]
    cw = 256
    for c in range(n // cw):
        o_ref[0, :, c * cw:(c + 1) * cw] = _dot(u, w_ref[:, c * cw:(c + 1) * cw])


def in_projection(x, g, sh, sc, w, tm):
    B, T, D = x.shape
    N = w.shape[1]
    vec = pl.BlockSpec((1, 1, D), lambda b, i: (b, 0, 0))
    return pl.pallas_call(
        _inproj_kernel,
        grid=(B, T // tm),
        in_specs=[pl.BlockSpec((1, tm, D), lambda b, i: (b, i, 0)),
                  _const_spec((1, D)), vec, vec, _const_spec((D, N))],
        out_specs=pl.BlockSpec((1, tm, N), lambda b, i: (b, i, 0)),
        out_shape=jax.ShapeDtypeStruct((B, T, N), F32),
        compiler_params=_cparams("parallel", "parallel"),
        name="in_projection",
    )(x, g, sh, sc, w)


def _ffn1_kernel(x_ref, g_ref, sh_ref, sc_ref, wg_ref, wu_ref, o_ref):
    u = _norm_mod(x_ref[0], g_ref[...], sh_ref[0], sc_ref[0]).astype(BF16)
    n = wg_ref.shape[1]
    cw = 256
    for c in range(n // cw):
        sl = slice(c * cw, (c + 1) * cw)
        gate = _dot(u, wg_ref[:, sl])
        up = _dot(u, wu_ref[:, sl])
        o_ref[0, :, sl] = (_silu(gate) * up).astype(BF16)


def ffn_up(x, g, sh, sc, w_gate, w_up, tm):
    B, T, D = x.shape
    N = w_gate.shape[1]
    vec = pl.BlockSpec((1, 1, D), lambda b, i: (b, 0, 0))
    return pl.pallas_call(
        _ffn1_kernel,
        grid=(B, T // tm),
        in_specs=[pl.BlockSpec((1, tm, D), lambda b, i: (b, i, 0)),
                  _const_spec((1, D)), vec, vec, _const_spec((D, N)), _const_spec((D, N))],
        out_specs=pl.BlockSpec((1, tm, N), lambda b, i: (b, i, 0)),
        out_shape=jax.ShapeDtypeStruct((B, T, N), BF16),
        compiler_params=_cparams("parallel", "parallel"),
        name="ffn_up",
    )(x, g, sh, sc, w_gate, w_up)


def _ffn2_kernel(h_ref, x_ref, gt_ref, w_ref, fg_ref, o_ref, *, final_norm):
    y = x_ref[0] + gt_ref[0] * _dot(h_ref[0], w_ref[...])
    if final_norm:
        ms = jnp.mean(y * y, axis=-1, keepdims=True)
        y = y * lax.rsqrt(ms + EPS) * fg_ref[...]
    o_ref[0] = y


def ffn_down(h, x, gate, w, final_g, final_norm, tm):
    B, T, D = x.shape
    N = h.shape[2]
    return pl.pallas_call(
        functools.partial(_ffn2_kernel, final_norm=final_norm),
        grid=(B, T // tm),
        in_specs=[pl.BlockSpec((1, tm, N), lambda b, i: (b, i, 0)),
                  pl.BlockSpec((1, tm, D), lambda b, i: (b, i, 0)),
                  pl.BlockSpec((1, 1, D), lambda b, i: (b, 0, 0)),
                  _const_spec((N, D)), _const_spec((1, D))],
        out_specs=pl.BlockSpec((1, tm, D), lambda b, i: (b, i, 0)),
        out_shape=jax.ShapeDtypeStruct((B, T, D), F32),
        compiler_params=_cparams("parallel", "parallel"),
        name="ffn_down",
    )(h, x, gate, w, final_g)


def _halo_specs(tm, T, width, col_fn, halo):
    r = tm // halo
    last = T // halo - 1
    main = pl.BlockSpec((1, tm, width), lambda *g: (g[0], g[1], col_fn(*g)))
    prev = pl.BlockSpec((1, halo, width), lambda *g: (g[0], jnp.maximum(g[1] * r - 1, 0), col_fn(*g)))
    nxt = pl.BlockSpec((1, halo, width), lambda *g: (g[0], jnp.minimum((g[1] + 1) * r, last), col_fn(*g)))
    return main, prev, nxt


def _shift_rows(x, prev, nxt, s, first, last):
    tm = x.shape[0]
    if s > 0:
        head = jnp.where(first, 0.0, prev[prev.shape[0] - s:, :])
        return jnp.concatenate([head, x[:tm - s, :]], axis=0)
    s = -s
    tail = jnp.where(last, 0.0, nxt[:s, :])
    return jnp.concatenate([x[s:, :], tail], axis=0)


def _hy_prep_kernel(p_ref, pp_ref, pn_ref, w_ref, b_ref, v_ref, g1_ref, g2_ref):
    i = pl.program_id(1)
    first = i == 0
    last = i == pl.num_programs(1) - 1
    x = p_ref[0]
    xm = _shift_rows(x, pp_ref[0], pn_ref[0], 1, first, last)
    xp = _shift_rows(x, pp_ref[0], pn_ref[0], -1, first, last)
    u = b_ref[...] + w_ref[0:1, :] * xm + w_ref[1:2, :] * x + w_ref[2:3, :] * xp
    v_ref[0] = u[:, 0:BR_W]
    g1_ref[0] = u[:, BR_W:2 * BR_W]
    g2_ref[0] = u[:, 2 * BR_W:3 * BR_W]


def hyena_prep(p, conv_w, conv_b, tm):
    B, T, _ = p.shape
    main, prev, nxt = _halo_specs(tm, T, HY_IN, lambda b, i: OFF_HY // HY_IN, 8)
    out = pl.BlockSpec((1, tm, BR_W), lambda b, i: (b, i, 0))
    shp = jax.ShapeDtypeStruct((B, T, BR_W), F32)
    return pl.pallas_call(
        _hy_prep_kernel,
        grid=(B, T // tm),
        in_specs=[main, prev, nxt, _const_spec((3, HY_IN)), _const_spec((1, HY_IN))],
        out_specs=[out, out, out],
        out_shape=[shp, shp, shp],
        compiler_params=_cparams("parallel", "parallel"),
        name="hyena_prep",
    )(p, p, p, conv_w, conv_b)


def _hy_filter_kernel(feat_ref, w1_ref, b1_ref, w2_ref, b2_ref, w3_ref, fq_ref, rates_ref, h_ref, ss_ref):
    i = pl.program_id(0)
    feat = feat_ref[...]
    t = feat[:, 0:1]
    h = jnp.sin(fq_ref[0:1, :] * (_dot(feat, w1_ref[...], precision=HI) + b1_ref[...]))
    h = jnp.sin(fq_ref[1:2, :] * (_dot(h, w2_ref[...], precision=HI) + b2_ref[...]))
    h = _dot(h, w3_ref[...], precision=HI) * jnp.exp(-t * rates_ref[...])
    row = lax.broadcasted_iota(jnp.int32, h.shape, 0) + i * h.shape[0]
    col = lax.broadcasted_iota(jnp.int32, h.shape, 1)
    h = jnp.where((row == 0) & ((col // BR_W) % 2 == 1), 0.0, h)
    h_ref[...] = h

    @pl.when(i == 0)
    def _():
        ss_ref[...] = jnp.zeros_like(ss_ref)

    ss_ref[...] += jnp.sum(h * h, axis=0, keepdims=True)


def hyena_filter(feat, w1, b1, w2, b2, w3, freq, rates, tl):
    L = feat.shape[0]
    C = w3.shape[1]
    return pl.pallas_call(
        _hy_filter_kernel,
        grid=(L // tl,),
        in_specs=[pl.BlockSpec((tl, HY_FEAT_PAD), lambda i: (i, 0)),
                  _const_spec((HY_FEAT_PAD, HY_HID)), _const_spec((1, HY_HID)),
                  _const_spec((HY_HID, HY_HID)), _const_spec((1, HY_HID)),
                  _const_spec((HY_HID, C)), _const_spec((2, HY_HID)), _const_spec((1, C))],
        out_specs=[pl.BlockSpec((tl, C), lambda i: (i, 0)), pl.BlockSpec((1, C), lambda i: (0, 0))],
        out_shape=[jax.ShapeDtypeStruct((L, C), F32), jax.ShapeDtypeStruct((1, C), F32)],
        compiler_params=_cparams("arbitrary"),
        name="hyena_filter",
    )(feat, w1, b1, w2, b2, w3, freq, rates)


def _filter_scale(ss_ref, o):
    e = ss_ref[:, 2 * o * BR_W:(2 * o + 1) * BR_W] + ss_ref[:, (2 * o + 1) * BR_W:(2 * o + 2) * BR_W]
    return lax.rsqrt(e + EPS)


def _combine_spectrum(x, ss_ref, o, half):
    xf = x[:, 2 * o * BR_W:(2 * o + 1) * BR_W]
    xb = x[:, (2 * o + 1) * BR_W:(2 * o + 2) * BR_W]
    sc = _filter_scale(ss_ref, o)
    hr = (xf[:half] + xb[:half]) * sc
    hi = (xf[half:] - xb[half:]) * sc
    return jnp.concatenate([hr, hi], axis=0)


def _cmul(x, h, half):
    xr, xi = x[:half], x[half:]
    hr, hi = h[:half], h[half:]
    return jnp.concatenate([xr * hr - xi * hi, xr * hi + xi * hr], axis=0)


def _dft1_kernel(z_ref, f_ref, a_ref):
    a_ref[0] = _dot(f_ref[...], z_ref[0].astype(BF16)).astype(BF16)


def dft_stage1(z, f1, tn):
    B, n1, W = z.shape
    M = f1.shape[0]
    return pl.pallas_call(
        _dft1_kernel,
        grid=(B, W // tn),
        in_specs=[pl.BlockSpec((1, n1, tn), lambda b, j: (b, 0, j)), _const_spec((M, n1))],
        out_specs=pl.BlockSpec((1, M, tn), lambda b, j: (b, 0, j)),
        out_shape=jax.ShapeDtypeStruct((B, M, W), BF16),
        compiler_params=_cparams("parallel", "parallel"),
        name="dft_stage1",
    )(z, f1)


def _spec2_kernel(a_ref, g_ref, ss_ref, h_ref):
    kb = g_ref.shape[0]
    n2 = a_ref.shape[3]
    for k in range(kb):
        a = a_ref[0, :, k].reshape(2 * n2, a_ref.shape[4])
        x = _dot(g_ref[k], a)
        for o in range(2):
            h_ref[o, k] = _combine_spectrum(x, ss_ref, o, n2)


def filter_spectrum(a, g, ss, kb):
    _, _, N1, N2, C = a.shape
    return pl.pallas_call(
        _spec2_kernel,
        grid=(N1 // kb,),
        in_specs=[pl.BlockSpec((1, 2, kb, N2, C), lambda i: (0, 0, i, 0, 0)),
                  pl.BlockSpec((kb, 2 * N2, 2 * N2), lambda i: (i, 0, 0)),
                  _const_spec((1, C))],
        out_specs=pl.BlockSpec((2, kb, 2 * N2, BR_W), lambda i: (0, i, 0, 0)),
        out_shape=jax.ShapeDtypeStruct((2, N1, 2 * N2, BR_W), F32),
        compiler_params=_cparams("parallel"),
        name="filter_spectrum",
    )(a, g, ss)


def _conv2_kernel(a_ref, g_ref, gi_ref, h_ref, o_ref):
    kb = g_ref.shape[0]
    n2 = a_ref.shape[3]
    C = a_ref.shape[4]
    for k in range(kb):
        a = a_ref[0, :, k].reshape(2 * n2, C)
        y = _cmul(_dot(g_ref[k], a), h_ref[0, k], n2).astype(BF16)
        o_ref[0, :, k] = _dot(gi_ref[k], y).astype(BF16).reshape(2, n2, C)


def spectral_multiply(a, g, gi, h, o, kb):
    B, _, N1, N2, C = a.shape
    blk = pl.BlockSpec((1, 2, kb, N2, C), lambda i, b: (b, 0, i, 0, 0))
    mat = pl.BlockSpec((kb, 2 * N2, 2 * N2), lambda i, b: (i, 0, 0))
    return pl.pallas_call(
        _conv2_kernel,
        grid=(N1 // kb, B),
        in_specs=[blk, mat, mat, pl.BlockSpec((1, kb, 2 * N2, C), lambda i, b: (o, i, 0, 0))],
        out_specs=blk,
        out_shape=jax.ShapeDtypeStruct(a.shape, BF16),
        compiler_params=_cparams("parallel", "parallel"),
        name="spectral_multiply",
    )(a, g, gi, h)


def _idft1_kernel(b_ref, f_ref, z_ref, gate_ref, bias_ref, o_ref):
    y = _dot(f_ref[...], b_ref[0])
    z = z_ref[0]
    o_ref[0] = gate_ref[0] * (y + bias_ref[...] * z)


def idft_stage1(bm, fi, z, gate, bias, tn):
    B, M, W = bm.shape
    n1 = fi.shape[0]
    blk = pl.BlockSpec((1, n1, tn), lambda b, j: (b, 0, j))
    return pl.pallas_call(
        _idft1_kernel,
        grid=(B, W // tn),
        in_specs=[pl.BlockSpec((1, M, tn), lambda b, j: (b, 0, j)), _const_spec((n1, M)),
                  blk, blk, _const_spec((1, tn))],
        out_specs=blk,
        out_shape=jax.ShapeDtypeStruct((B, n1, W), F32),
        compiler_params=_cparams("parallel", "parallel"),
        name="idft_stage1",
    )(bm, fi, z, gate, bias)


@functools.lru_cache(maxsize=None)
def _dft_tables(L):
    N = 2 * L
    N2 = 128
    N1 = N // N2
    nz = L // N2
    k1 = np.arange(N1)[:, None]
    n1 = np.arange(nz)[None, :]
    th = 2 * np.pi * ((k1 * n1) % N1) / N1
    f1 = np.concatenate([np.cos(th), -np.sin(th)], axis=0)
    fi = np.concatenate([np.cos(th).T, -np.sin(th).T], axis=1) / N
    kk1 = np.arange(N1)[:, None, None]
    k2 = np.arange(N2)[None, :, None]
    n2 = np.arange(N2)[None, None, :]
    ph = 2 * np.pi * ((n2 * k2 * N1 + n2 * kk1) % N) / N
    gr, gim = np.cos(ph), -np.sin(ph)
    g = np.concatenate([np.concatenate([gr, -gim], axis=2), np.concatenate([gim, gr], axis=2)], axis=1)
    hr, him = np.swapaxes(gr, 1, 2), -np.swapaxes(gim, 1, 2)
    gi = np.concatenate([np.concatenate([hr, -him], axis=2), np.concatenate([him, hr], axis=2)], axis=1)
    return tuple(np.asarray(t, np.float32) for t in (f1, fi, g, gi))


def _spec_direct_kernel(hf_ref, f_ref, ss_ref, h_ref):
    x = _dot(f_ref[...], hf_ref[...].astype(BF16))
    half = x.shape[0] // 2
    for o in range(2):
        h_ref[o] = _combine_spectrum(x, ss_ref, o, half)


def filter_spectrum_direct(hf, f, ss):
    L, C = hf.shape
    return pl.pallas_call(
        _spec_direct_kernel,
        grid=(1,),
        in_specs=[_const_spec((L, C)), _const_spec((4 * L, L)), _const_spec((1, C))],
        out_specs=pl.BlockSpec((2, 4 * L, BR_W), lambda i: (0, 0, 0)),
        out_shape=jax.ShapeDtypeStruct((2, 4 * L, BR_W), F32),
        compiler_params=_cparams("arbitrary"),
        name="filter_spectrum_direct",
    )(hf, f, ss)


def _conv_direct_kernel(z_ref, gate_ref, bias_ref, f_ref, fi_ref, h_ref, o_ref):
    z = z_ref[0]
    x = _dot(f_ref[...], z.astype(BF16))
    y = _cmul(x, h_ref[0], x.shape[0] // 2).astype(BF16)
    o_ref[0] = gate_ref[0] * (_dot(fi_ref[...], y) + bias_ref[...] * z)


def conv_direct(z, gate, bias, f, fi, h, o):
    B, L, C = z.shape
    blk = pl.BlockSpec((1, L, C), lambda b: (b, 0, 0))
    return pl.pallas_call(
        _conv_direct_kernel,
        grid=(B,),
        in_specs=[blk, blk, _const_spec((1, C)), _const_spec((4 * L, L)), _const_spec((L, 4 * L)),
                  pl.BlockSpec((1, 4 * L, C), lambda b: (o, 0, 0))],
        out_specs=blk,
        out_shape=jax.ShapeDtypeStruct((B, L, C), F32),
        compiler_params=_cparams("parallel"),
        name="conv_direct",
    )(z, gate, bias, f, fi, h)


@functools.lru_cache(maxsize=None)
def _dft_direct_tables(L):
    N = 2 * L
    k = np.arange(N)[:, None]
    n = np.arange(L)[None, :]
    th = 2 * np.pi * ((k * n) % N) / N
    f = np.concatenate([np.cos(th), -np.sin(th)], axis=0)
    fi = np.concatenate([np.cos(th).T, -np.sin(th).T], axis=1) / N
    return np.asarray(f, np.float32), np.asarray(fi, np.float32)


@functools.lru_cache(maxsize=None)
def _filter_features(L):
    t = np.arange(L, dtype=np.float32) / np.float32(L)
    ang = (2.0 * math.pi) * t[:, None].astype(np.float64) * np.arange(1, HY_BANDS + 1)
    feat = np.zeros((L, HY_FEAT_PAD), np.float32)
    feat[:, 0] = t
    feat[:, 1:1 + HY_BANDS] = np.sin(ang)
    feat[:, 1 + HY_BANDS:HY_FEAT] = np.cos(ang)
    rates = np.abs(np.linspace(math.log(HY_TARGET) / HY_SLOW, math.log(HY_TARGET) / HY_FAST, BR_W))
    return feat, np.tile(np.asarray(rates, np.float32), 4)[None, :]


def hyena_branch(v, g1, g2, hp):
    f_w1, f_b1, f_w2, f_b2, f_w3, freq, bias = hp
    B, L, C = v.shape
    feat, rates = _filter_features(L)
    hf, ss = hyena_filter(jnp.asarray(feat), f_w1, f_b1, f_w2, f_b2, f_w3, freq, jnp.asarray(rates),
                          min(L, 512))
    if L <= 512:
        f, fi = (jnp.asarray(t).astype(BF16) for t in _dft_direct_tables(L))
        spec = filter_spectrum_direct(hf, f, ss)
        z = v
        for o, gate in enumerate((g1, g2)):
            z = conv_direct(z, gate, bias[o:o + 1], f, fi, spec, o)
        return z
    f1, fi1, g, gi = (jnp.asarray(t).astype(BF16) for t in _dft_tables(L))
    N2 = 128
    N1 = 2 * L // N2
    nz = L // N2
    kb = 8
    a = dft_stage1(hf.reshape(1, nz, N2 * 4 * C), f1, 8192)
    spec = filter_spectrum(a.reshape(1, 2, N1, N2, 4 * C), g, ss, kb)
    W = N2 * C
    tn = 8192
    z = v.reshape(B, nz, W)
    for o, gate in enumerate((g1, g2)):
        a = dft_stage1(z, f1, tn).reshape(B, 2, N1, N2, C)
        bm = spectral_multiply(a, g, gi, spec, o, kb).reshape(B, 2 * N1, W)
        z = idft_stage1(bm, fi1, z, gate.reshape(B, nz, W), jnp.tile(bias[o:o + 1], (1, tn // C)), tn)
    return z.reshape(B, L, C)


def _head_sum(x, seg_ref):
    hi = x.astype(BF16)
    lo = (x - hi.astype(F32)).astype(BF16)
    return _dot(hi, seg_ref[...]) + _dot(lo, seg_ref[...])


def _rw_prep_kernel(p_ref, pp_ref, pn_ref, mu_ref, wl_ref, w0_ref, a0_ref, kv_ref, seg_ref,
                    r_ref, v_ref, na_ref, w_ref, kt_ref, b_ref, bonus_ref, g_ref, *, on_grid):
    i = pl.program_id(1)
    first = i == 0
    last = i == pl.num_programs(1) - 1
    x = p_ref[0]
    tm = x.shape[0]
    grp = lax.broadcasted_iota(jnp.int32, x.shape, 1) % 4
    prev, nxt = pp_ref[0], pn_ref[0]
    if on_grid:
        col = (lax.broadcasted_iota(jnp.int32, (tm, 1), 0) + i * tm) % GRID_W
        left = jnp.where(col == 0, 0.0, pltpu.roll(x, 1, 0))
        right = jnp.where(col == GRID_W - 1, 0.0, pltpu.roll(x, tm - 1, 0))
        up = _shift_rows(x, prev, nxt, GRID_W, first, last)
        down = _shift_rows(x, prev, nxt, -GRID_W, first, last)
        shifted = jnp.where(grp == 0, left, jnp.where(grp == 1, right, jnp.where(grp == 2, up, down)))
    else:
        before = _shift_rows(x, prev, nxt, 1, first, last)
        after = _shift_rows(x, prev, nxt, -1, first, last)
        shifted = jnp.where(grp % 2 == 0, before, after)
    xx = x + (shifted - x) * mu_ref[...]
    r = xx[:, 0:BR_W]
    k = xx[:, BR_W:2 * BR_W]
    v = xx[:, 2 * BR_W:3 * BR_W]
    lo = xx[:, 3 * BR_W:4 * BR_W]
    ll = lax.broadcasted_iota(jnp.int32, lo.shape, 1)
    act = jnp.where(ll < RW_LORA, jnp.tanh(lo), jnp.where(ll < 2 * RW_LORA, lo, _sigmoid(lo)))
    z = _dot(act.astype(BF16), wl_ref[...])
    kk = k * kv_ref[0:1, :]
    kk = kk * lax.rsqrt(_head_sum(kk * kk, seg_ref) + 1e-12)
    r_ref[0] = r
    v_ref[0] = v
    na_ref[0] = -kk
    bonus_ref[0] = _head_sum(r * k * kv_ref[2:3, :], seg_ref) * v
    g_ref[0] = z[:, 4 * BR_W:5 * BR_W]
    for d in range(2):
        logw = -_softplus(-(w0_ref[d:d + 1, :] + z[:, d * BR_W:(d + 1) * BR_W])) - 0.5
        w_ref[d, 0] = jnp.exp(-jnp.exp(logw))
        a = _sigmoid(a0_ref[d:d + 1, :] + z[:, (2 + d) * BR_W:(3 + d) * BR_W])
        kt_ref[d, 0] = k * (1.0 + (a - 1.0) * kv_ref[1:2, :])
        b_ref[d, 0] = kk * a


def rwkv_prep(p, mu, w_lora, w0, a0, kvec, seg, on_grid, tm):
    B, T, _ = p.shape
    halo = GRID_W if on_grid else 8
    main, prev, nxt = _halo_specs(tm, T, RW_PAD, lambda b, i: OFF_RW // RW_PAD, halo)
    one = pl.BlockSpec((1, tm, BR_W), lambda b, i: (b, i, 0))
    two = pl.BlockSpec((2, 1, tm, BR_W), lambda b, i: (0, b, i, 0))
    s1 = jax.ShapeDtypeStruct((B, T, BR_W), F32)
    s2 = jax.ShapeDtypeStruct((2, B, T, BR_W), F32)
    return pl.pallas_call(
        functools.partial(_rw_prep_kernel, on_grid=on_grid),
        grid=(B, T // tm),
        in_specs=[main, prev, nxt, _const_spec((1, RW_PAD)), _const_spec((BR_W, 5 * BR_W)),
                  _const_spec((2, BR_W)), _const_spec((2, BR_W)), _const_spec((3, BR_W)),
                  _const_spec((BR_W, BR_W))],
        out_specs=[one, one, one, two, two, two, one, one],
        out_shape=[s1, s1, s1, s2, s2, s2, s1, s1],
        compiler_params=_cparams("parallel", "parallel"),
        name="rwkv_prep",
    )(p, p, p, mu, w_lora, w0, a0, kvec, seg)


RW_VH = RW_HEAD // 2
LANE_W = 128
N_KEYED = 5
RW_UNROLL = 4


def _chain_rows(refs, t, tb, shifts):
    blocks = []
    for d in range(2):
        full = refs[d][t if d == 0 else tb]
        for half in range(2):
            x = full[:, half * LANE_W:(half + 1) * LANE_W]
            for sh in shifts:
                blocks.append(x if sh == 0 else pltpu.roll(x, LANE_W - sh, 1))
    return blocks


def _rw_scan_kernel(*refs, B, Tt):
    nin = 2 * (N_KEYED + 1)
    ins = refs[:nin]
    s0_ref = refs[nin]
    y_refs = refs[nin + 1:nin + 3]
    s_ref = refs[nin + 3]
    bufs = [refs[nin + 4 + 3 * u:nin + 7 + 3 * u] for u in range(RW_UNROLL)]
    NL = 16 * B
    A_TILE = 3

    @pl.when(pl.program_id(0) == 0)
    def _():
        s_ref[...] = s0_ref[...]

    def keyed_job(t, slot, n):
        def issue():
            rows = _chain_rows(ins[2 * n:2 * n + 2], t, Tt - 1 - t, (0, RW_HEAD))
            return jnp.concatenate(rows + rows, axis=0).T[0:RW_HEAD, :]

        def commit(val):
            bufs[slot][0][n] = val
        return issue, commit

    def values_job(t, slot):
        def issue():
            rows = []
            for vs in range(2):
                rows += _chain_rows(ins[2 * N_KEYED:], t, Tt - 1 - t, (vs * RW_VH, RW_HEAD + vs * RW_VH))
            return jnp.concatenate(rows, axis=0).T[0:RW_VH, :]

        def commit(val):
            bufs[slot][1][...] = val
        return issue, commit

    def output_job(t, slot):
        def issue():
            ys = bufs[slot][2][...]
            y = jnp.concatenate([ys, jnp.zeros((LANE_W - RW_VH, NL), F32)], axis=0).T
            out = []
            for d in range(2):
                halves = []
                for half in range(2):
                    acc = None
                    for vs in range(2):
                        for h2 in range(2):
                            row0 = (((vs * 2 + d) * 2 + half) * 2 + h2) * B
                            blk = y[row0:row0 + B, :]
                            sh = h2 * RW_HEAD + vs * RW_VH
                            blk = blk if sh == 0 else pltpu.roll(blk, sh, 1)
                            acc = blk if acc is None else acc + blk
                    halves.append(acc)
                out.append(jnp.concatenate(halves, axis=1))
            return out

        def commit(val):
            y_refs[0][t] = val[0]
            y_refs[1][Tt - 1 - t] = val[1]
        return issue, commit

    def input_jobs(t, slot):
        order = (A_TILE,) + tuple(n for n in range(N_KEYED) if n != A_TILE)
        return [keyed_job(t, slot, n) for n in order] + [values_job(t, slot)]

    def step_pieces(slot):
        tile_ref, vt_ref, ys_ref = bufs[slot]
        row = lambda n, k: tile_ref[n, pl.ds(k, 1), :]
        groups = range(RW_VH // 8)
        sa = [[None, None] for _ in groups]
        ys = [[None, None] for _ in groups]
        tot, vt = {}, {}

        def dot_a(k0, k1):
            for k in range(k0, k1):
                a_k = row(A_TILE, k)
                for g in groups:
                    p = s_ref[g, k] * a_k
                    sa[g][k % 2] = p if sa[g][k % 2] is None else sa[g][k % 2] + p

        def update(k0, k1):
            if not tot:
                for g in groups:
                    tot[g] = sa[g][0] + sa[g][1]
                    vt[g] = vt_ref[g * 8:(g + 1) * 8, :]
            fetch = lambda ks: [(k, row(0, k), row(1, k), row(2, k), row(4, k), [s_ref[g, k] for g in groups])
                                for k in ks]
            pairs = [list(range(c, min(c + 2, k1))) for c in range(k0, k1, 2)]
            cur = fetch(pairs[0])
            for i in range(len(pairs)):
                nxt = fetch(pairs[i + 1]) if i + 1 < len(pairs) else None
                done = []
                for k, r_k, w_k, kt_k, b_k, ss in cur:
                    for g in groups:
                        s = ss[g] * w_k + tot[g] * b_k + vt[g] * kt_k
                        done.append((g, k, s))
                        p = s * r_k
                        ys[g][k % 2] = p if ys[g][k % 2] is None else ys[g][k % 2] + p
                for g, k, s in done:
                    s_ref[g, k] = s
                cur = nxt
            if k1 == RW_HEAD:
                for g in groups:
                    ys_ref[g * 8:(g + 1) * 8, :] = ys[g][0] + ys[g][1]

        half = RW_HEAD // 2
        cuts = [0, 13, 26, 39, 52, RW_HEAD]
        return ([functools.partial(dot_a, 0, half), functools.partial(dot_a, half, RW_HEAD)]
                + [functools.partial(update, cuts[i], cuts[i + 1]) for i in range(5)])

    def run(pieces, jobs):
        for i, piece in enumerate(pieces):
            val = jobs[i][0]() if i < len(jobs) else None
            piece()
            if i < len(jobs):
                jobs[i][1](val)

    for issue, commit in input_jobs(0, 0):
        commit(issue())

    def body(j, carry):
        t0 = RW_UNROLL * j
        for u in range(RW_UNROLL):
            t_next = t0 + u + 1 if u + 1 < RW_UNROLL else jnp.minimum(t0 + RW_UNROLL, Tt - 1)
            jobs = input_jobs(t_next, (u + 1) % RW_UNROLL)
            if u > 0:
                jobs.append(output_job(t0 + u - 1, u - 1))
            run(step_pieces(u), jobs)
        issue, commit = output_job(t0 + RW_UNROLL - 1, RW_UNROLL - 1)
        commit(issue())
        return carry

    lax.fori_loop(0, Tt // RW_UNROLL, body, 0)


def rwkv_mix(prep, s0, tt):
    r, v, na, w, kt, b = prep
    T, B, _ = r.shape
    nb = T // tt
    tblk = lambda i, d: i if d == 0 else nb - 1 - i

    in_specs, args = [], []
    for x, per_dir in ((r, False), (w, True), (kt, True), (na, False), (b, True), (v, False)):
        for d in range(2):
            if per_dir:
                in_specs.append(pl.BlockSpec((None, tt, B, BR_W), lambda i, d=d: (d, tblk(i, d), 0, 0)))
            else:
                in_specs.append(pl.BlockSpec((tt, B, BR_W), lambda i, d=d: (tblk(i, d), 0, 0)))
            args.append(x)
    NL = 16 * B
    sblk = pl.BlockSpec((RW_VH // 8, RW_HEAD, 8, NL), lambda i: (0, 0, 0, 0))
    yspecs = [pl.BlockSpec((tt, B, BR_W), lambda i, d=d: (tblk(i, d), 0, 0)) for d in range(2)]
    ysh = jax.ShapeDtypeStruct((T, B, BR_W), F32)
    out = pl.pallas_call(
        functools.partial(_rw_scan_kernel, B=B, Tt=tt),
        grid=(nb,),
        in_specs=in_specs + [sblk],
        out_specs=yspecs + [sblk],
        out_shape=[ysh] * 2 + [jax.ShapeDtypeStruct((RW_VH // 8, RW_HEAD, 8, NL), F32)],
        scratch_shapes=[pltpu.VMEM((N_KEYED, RW_HEAD, NL), F32), pltpu.VMEM((RW_VH, NL), F32),
                        pltpu.VMEM((RW_VH, NL), F32)] * RW_UNROLL,
        compiler_params=_cparams("arbitrary"),
        name="rwkv_scan",
    )(*args, s0)
    return out[0], out[1], out[2]


def _gelu_tanh(x):
    return 0.5 * x * (1.0 + jnp.tanh(math.sqrt(2.0 / math.pi) * (x + 0.044715 * (x * x * x))))


def _lru_prep_kernel(p_ref, pp_ref, pn_ref, cw_ref, cb_ref, w_ref, bias_ref, lam_ref, a_ref, b_ref, gg_ref):
    i = pl.program_id(1)
    first = i == 0
    last = i == pl.num_programs(1) - 1
    x = p_ref[0][:, 0:BR_W]
    prev = pp_ref[0][:, 0:BR_W]
    nxt = pn_ref[0][:, 0:BR_W]
    xc = cb_ref[...] + cw_ref[0:1, :] * _shift_rows(x, prev, nxt, 2, first, last)
    xc = xc + cw_ref[1:2, :] * _shift_rows(x, prev, nxt, 1, first, last)
    xc = xc + cw_ref[2:3, :] * x
    xc = xc + cw_ref[3:4, :] * _shift_rows(x, prev, nxt, -1, first, last)
    z = _dot(xc.astype(BF16), w_ref[...])
    for d in range(2):
        r = _sigmoid(z[:, 2 * d * BR_W:(2 * d + 1) * BR_W] + bias_ref[2 * d:2 * d + 1, :])
        gi = _sigmoid(z[:, (2 * d + 1) * BR_W:(2 * d + 2) * BR_W] + bias_ref[2 * d + 1:2 * d + 2, :])
        log_a = -LRU_C * r * _softplus(-lam_ref[d:d + 1, :])
        a_ref[d, 0] = jnp.exp(log_a)
        b_ref[d, 0] = jnp.sqrt(1.0 - jnp.exp(2.0 * log_a)) * (gi * xc)
    gg_ref[0] = _gelu_tanh(p_ref[0][:, BR_W:2 * BR_W])


def lru_prep(p, conv_w, conv_b, w_blk, bias, lam, tm):
    B, T, _ = p.shape
    main, prev, nxt = _halo_specs(tm, T, LRU_IN, lambda b, i: OFF_LRU // LRU_IN, 8)
    two = pl.BlockSpec((2, 1, tm, BR_W), lambda b, i: (0, b, i, 0))
    s2 = jax.ShapeDtypeStruct((2, B, T, BR_W), F32)
    return pl.pallas_call(
        _lru_prep_kernel,
        grid=(B, T // tm),
        in_specs=[main, prev, nxt, _const_spec((4, BR_W)), _const_spec((1, BR_W)),
                  _const_spec((BR_W, 4 * BR_W)), _const_spec((4, BR_W)), _const_spec((2, BR_W))],
        out_specs=[two, two, pl.BlockSpec((1, tm, BR_W), lambda b, i: (b, i, 0))],
        out_shape=[s2, s2, jax.ShapeDtypeStruct((B, T, BR_W), F32)],
        compiler_params=_cparams("parallel", "parallel"),
        name="lru_prep",
    )(p, p, p, conv_w, conv_b, w_blk, bias, lam)


def _affine_scan(a, b, reverse):
    tb = a.shape[0]
    row = lax.broadcasted_iota(jnp.int32, (tb, 1), 0)
    s = 1
    while s < tb:
        sh = tb - s if reverse else s
        ok = (row < tb - s) if reverse else (row >= s)
        a_s = pltpu.roll(a, sh, 0)
        b_s = pltpu.roll(b, sh, 0)
        b = jnp.where(ok, a * b_s + b, b)
        a = jnp.where(ok, a * a_s, a)
        s *= 2
    return a, b


def _lru_scan_kernel(af_ref, bf_ref, ab_ref, bb_ref, h0_ref, hf_ref, hb_ref, fin_ref):
    @pl.when(pl.program_id(1) == 0)
    def _():
        fin_ref[...] = h0_ref[...]

    tb = af_ref.shape[2]
    a, b = _affine_scan(af_ref[0, 0], bf_ref[0, 0], False)
    h = b + a * fin_ref[0, 0]
    hf_ref[0] = h
    fin_ref[0, 0] = h[tb - 1:tb, :]
    a, b = _affine_scan(ab_ref[0, 0], bb_ref[0, 0], True)
    h = b + a * fin_ref[1, 0]
    hb_ref[0] = h
    fin_ref[1, 0] = h[0:1, :]


def lru_scan(a, b, h0, tb):
    _, B, T, C = a.shape
    nb = T // tb
    fwd = pl.BlockSpec((1, 1, tb, C), lambda bi, i: (0, bi, i, 0))
    bwd = pl.BlockSpec((1, 1, tb, C), lambda bi, i: (1, bi, nb - 1 - i, 0))
    st = pl.BlockSpec((2, 1, 1, C), lambda bi, i: (0, bi, 0, 0))
    return pl.pallas_call(
        _lru_scan_kernel,
        grid=(B, nb),
        in_specs=[fwd, fwd, bwd, bwd, st],
        out_specs=[pl.BlockSpec((1, tb, C), lambda bi, i: (bi, i, 0)),
                   pl.BlockSpec((1, tb, C), lambda bi, i: (bi, nb - 1 - i, 0)), st],
        out_shape=[jax.ShapeDtypeStruct((B, T, C), F32), jax.ShapeDtypeStruct((B, T, C), F32),
                   jax.ShapeDtypeStruct((2, B, 1, C), F32)],
        compiler_params=_cparams("parallel", "arbitrary"),
        name="lru_scan",
    )(a, b, a, b, h0)


def _rope(x, cos, sin):
    q4 = RET_HEAD // 4
    lane = lax.broadcasted_iota(jnp.int32, x.shape, 1) % (2 * q4)
    partner = jnp.where(lane < q4, pltpu.roll(x, x.shape[1] - q4, 1), pltpu.roll(x, q4, 1))
    return x * cos + partner * sin


def _ret_dir(x, cos, sin, s, glane, gtile_ref, d, reverse):
    C = x.shape[0]
    q = x[:, 0:BR_W]
    k = x[:, BR_W:2 * BR_W]
    v = x[:, 2 * BR_W:3 * BR_W].astype(BF16)
    if cos is not None:
        q = _rope(q, cos, sin)
        k = _rope(k, cos, sin)
    k = k * (RET_HEAD ** -0.5)
    lg = -_softplus(-glane)
    idx = lax.broadcasted_iota(jnp.int32, (C, 1), 0).astype(F32)
    steps_in = (C - idx) if reverse else (idx + 1.0)
    steps_out = idx if reverse else (C - 1.0 - idx)
    ri = lax.broadcasted_iota(jnp.int32, (C, C), 0)
    ci = lax.broadcasted_iota(jnp.int32, (C, C), 1)
    diff = ((ci - ri) if reverse else (ri - ci)).astype(F32)
    lane_head = lax.broadcasted_iota(jnp.int32, (1, BR_W), 1) // RET_HEAD
    qb = q.astype(BF16)
    kb = k.astype(BF16)
    y = _dot(qb, s.astype(BF16)) * jnp.exp(steps_in * lg)
    for h in range(RET_HEADS):
        lg_h = -_softplus(-gtile_ref[d, h][0:1, :])
        dm = jnp.where(diff >= 0, jnp.exp(diff * lg_h), 0.0)
        mh = lane_head == h
        sc = lax.dot_general(jnp.where(mh, qb, jnp.zeros_like(qb)), kb, (((1,), (1,)), ((), ())),
                             preferred_element_type=F32)
        y = y + jnp.where(mh, _dot((sc * dm).astype(BF16), v), 0.0)
    kd = (k * jnp.exp(steps_out * lg)).astype(BF16)
    upd = lax.dot_general(kd, v, (((0,), (0,)), ((), ())), preferred_element_type=F32)
    rh = lax.broadcasted_iota(jnp.int32, (BR_W, BR_W), 0) // RET_HEAD
    ch = lax.broadcasted_iota(jnp.int32, (BR_W, BR_W), 1) // RET_HEAD
    s = s * jnp.exp(C * lg) + jnp.where(rh == ch, upd, 0.0)
    return y, s


def _ret_kernel(*refs, rope):
    if rope:
        xf_ref, xb_ref, cf_ref, sf_ref, cb_ref, sb_ref, gl_ref, gt_ref, s0_ref, yf_ref, yb_ref, s_ref = refs
    else:
        xf_ref, xb_ref, gl_ref, gt_ref, s0_ref, yf_ref, yb_ref, s_ref = refs

    @pl.when(pl.program_id(1) == 0)
    def _():
        s_ref[...] = s0_ref[...]

    y, s = _ret_dir(xf_ref[0], cf_ref[...] if rope else None, sf_ref[...] if rope else None,
                    s_ref[0, 0], gl_ref[0:1, :], gt_ref, 0, False)
    yf_ref[0] = y
    s_ref[0, 0] = s
    y, s = _ret_dir(xb_ref[0], cb_ref[...] if rope else None, sb_ref[...] if rope else None,
                    s_ref[1, 0], gl_ref[1:2, :], gt_ref, 1, True)
    yb_ref[0] = y
    s_ref[1, 0] = s


def retention(p, cos, sin, glane, gtile, s0, rope):
    B, T, _ = p.shape
    C = RET_CHUNK
    nc = T // C
    cb = OFF_RET // RET_IN
    xf = pl.BlockSpec((1, C, RET_IN), lambda b, i: (b, i, cb))
    xb = pl.BlockSpec((1, C, RET_IN), lambda b, i: (b, nc - 1 - i, cb))
    tf = pl.BlockSpec((C, BR_W), lambda b, i: (i, 0))
    tb = pl.BlockSpec((C, BR_W), lambda b, i: (nc - 1 - i, 0))
    st = pl.BlockSpec((2, 1, BR_W, BR_W), lambda b, i: (0, b, 0, 0))
    ins = [xf, xb] + ([tf, tf, tb, tb] if rope else []) + [
        _const_spec((2, BR_W)), _const_spec((2, RET_HEADS, 8, C)), st]
    args = [p, p] + ([cos, sin, cos, sin] if rope else []) + [glane, gtile, s0]
    return pl.pallas_call(
        functools.partial(_ret_kernel, rope=rope),
        grid=(B, nc),
        in_specs=ins,
        out_specs=[pl.BlockSpec((1, C, BR_W), lambda b, i: (b, i, 0)),
                   pl.BlockSpec((1, C, BR_W), lambda b, i: (b, nc - 1 - i, 0)), st],
        out_shape=[jax.ShapeDtypeStruct((B, T, BR_W), F32), jax.ShapeDtypeStruct((B, T, BR_W), F32),
                   jax.ShapeDtypeStruct((2, B, BR_W, BR_W), F32)],
        compiler_params=_cparams("parallel", "arbitrary"),
        name="retention",
    )(*args)


@functools.lru_cache(maxsize=None)
def _rope_tables(T):
    pos = np.arange(T)
    q4 = RET_HEAD // 4
    inv = ROPE_BASE ** (-np.arange(q4, dtype=np.float64) / q4)
    cos = np.zeros((T, RET_HEAD))
    sin = np.zeros((T, RET_HEAD))
    for part, coord in enumerate((pos // GRID_W, pos % GRID_W)):
        ang = coord[:, None] * inv
        base = part * 2 * q4
        cos[:, base:base + q4] = np.cos(ang)
        cos[:, base + q4:base + 2 * q4] = np.cos(ang)
        sin[:, base:base + q4] = -np.sin(ang)
        sin[:, base + q4:base + 2 * q4] = np.sin(ang)
    tile = lambda t: np.asarray(np.tile(t, (1, RET_HEADS)), np.float32)
    return tile(cos), tile(sin)


def _head_norm(y, seg_ref, eps):
    mu = _head_sum(y, seg_ref) * (1.0 / RW_HEAD)
    yc = y - mu
    var = _head_sum(yc * yc, seg_ref) * (1.0 / RW_HEAD)
    return yc * lax.rsqrt(var + eps)


def _merge_kernel(x_ref, g_ref, sh_ref, sc_ref, gt_ref, hy_ref, ryf_ref, ryb_ref, rbon_ref, rg_ref,
                  lhf_ref, lhb_ref, lgg_ref, tyf_ref, tyb_ref, tg_ref, lng_ref, seg_ref,
                  wg_ref, br_ref, wo_ref, o_ref, m_ref):
    x = x_ref[0]
    u = _norm_mod(x, g_ref[...], sh_ref[0], sc_ref[0]).astype(BF16)
    y_rw = (_head_norm(ryf_ref[0] + ryb_ref[0], seg_ref, RW_LN_EPS) * lng_ref[...] + rbon_ref[0]) * rg_ref[0]
    y_lru = (lhf_ref[0] + lhb_ref[0]) * lgg_ref[0]
    y_ret = _head_norm(tyf_ref[0] + tyb_ref[0], seg_ref, RET_LN_EPS) * _silu(tg_ref[0])
    ys = [y.astype(BF16) for y in (hy_ref[0], y_rw, y_lru, y_ret)]
    D = x.shape[1]
    cw = 256
    for c in range(D // cw):
        acc = None
        for n in range(N_BRANCH):
            gate = _sigmoid(_dot(u, wg_ref[:, n * D + c * cw:n * D + (c + 1) * cw]))
            t = gate * _dot(ys[n], br_ref[n, :, c * cw:(c + 1) * cw])
            acc = t if acc is None else acc + t
        m_ref[:, c * cw:(c + 1) * cw] = acc.astype(BF16)
    o_ref[0] = x + gt_ref[0] * _dot(m_ref[...], wo_ref[...])


def merge(x, g, sh, sc, gt, p, y_hy, rw, lru, ret, ln_g, seg, w_gate, br, w_out, tm):
    B, T, D = x.shape
    vec = pl.BlockSpec((1, 1, D), lambda b, i: (b, 0, 0))
    row = pl.BlockSpec((1, tm, D), lambda b, i: (b, i, 0))
    brn = pl.BlockSpec((1, tm, BR_W), lambda b, i: (b, i, 0))
    tg = pl.BlockSpec((1, tm, BR_W), lambda b, i: (b, i, (OFF_RET + 3 * BR_W) // BR_W))
    return pl.pallas_call(
        _merge_kernel,
        grid=(B, T // tm),
        in_specs=[row, _const_spec((1, D)), vec, vec, vec] + [brn] * 10 + [tg] + [
            _const_spec((1, BR_W)), _const_spec((BR_W, BR_W)), _const_spec((D, GATE_IN)),
            _const_spec((N_BRANCH, BR_W, D)), _const_spec((D, D))],
        out_specs=row,
        out_shape=jax.ShapeDtypeStruct((B, T, D), F32),
        scratch_shapes=[pltpu.VMEM((tm, D), BF16)],
        compiler_params=_cparams("parallel", "parallel"),
        name="merge",
    )(x, g, sh, sc, gt, y_hy, *rw, *lru, *ret, p, ln_g, seg, w_gate, br, w_out)


def _block_diag(w):
    G = w.shape[-3]
    eye = jnp.eye(G, dtype=w.dtype)
    full = w[..., :, :, None, :] * eye[:, None, :, None]
    return full.reshape(*w.shape[:-3], G * w.shape[-2], G * w.shape[-1])


def _mixers(p, lp, states, on_grid, with_output, tiles):
    B, T, _ = p.shape
    tm, tt, _ = tiles
    r, v, na, w, kt, b, bonus, g = rwkv_prep(p, *lp['rw'], on_grid, tm)
    time_major = lambda t: jnp.swapaxes(t, -3, -2)
    y_f, y_b, rw_fin = rwkv_mix(tuple(time_major(t) for t in (r, v, na, w, kt, b)), states[0], tt)
    y_f, y_b = time_major(y_f), time_major(y_b)
    a, bb, gg = lru_prep(p, *lp['lru'], tm)
    h_f, h_b, lru_fin = lru_scan(a, bb, states[1], tm)
    cos, sin = (jnp.asarray(t) for t in _rope_tables(T)) if on_grid else (None, None)
    t_f, t_b, ret_fin = retention(p, cos, sin, *lp['ret'], states[2], on_grid)
    fins = (rw_fin, lru_fin, ret_fin)
    if not with_output:
        return None, fins
    vg = hyena_prep(p, *lp['hy_conv'], min(T, 2048))
    y_hy = hyena_branch(vg[0], vg[1], vg[2], lp['hy'])
    return (y_hy, (y_f, y_b, bonus, g), (h_f, h_b, gg), (t_f, t_b)), fins


def kernel(x, c, ctx, c_ctx, w_mod, b_mod, norm1_g, norm2_g, w_in, hy_conv_w, hy_conv_b, hy_f_w1, hy_f_b1, hy_f_w2, hy_f_b2, hy_f_w3, hy_freq, hy_bias, rw_mu, rw_w0, rw_w2, rw_a0, rw_a2, rw_g2, rw_kk, rw_ka, rw_rk, rw_ln_g, lru_conv_w, lru_conv_b, lru_wa, lru_ba, lru_wx, lru_bx, lru_lam, ret_gamma, br_proj, w_out, ffn_w1, ffn_w2, final_g):
    B, T, D = x.shape
    TC = ctx.shape[1]
    L = w_in.shape[0]

    s0, s1, s2, s3 = HY_IN, HY_IN + RW_IN, HY_IN + RW_IN + LRU_IN, HY_IN + RW_IN + LRU_IN + RET_IN
    w_branch = jnp.concatenate([w_in[:, :, s0:s1], jnp.zeros((L, D, RW_PAD - RW_IN), w_in.dtype),
                                w_in[:, :, s2:s3], jnp.zeros((L, D, HY_GAP), w_in.dtype),
                                w_in[:, :, 0:s0], w_in[:, :, s1:s2]], axis=2).astype(BF16)
    w_gate = w_in[:, :, s3:].astype(BF16)
    mu = jnp.pad(rw_mu, ((0, 0), (0, RW_PAD - RW_IN)))[:, None, :]
    w_lora = jnp.zeros((L, BR_W, 5 * BR_W), F32)
    w_lora = w_lora.at[:, 0:RW_LORA, 0:BR_W].set(rw_w2[:, 0]).at[:, 0:RW_LORA, BR_W:2 * BR_W].set(rw_w2[:, 1])
    w_lora = w_lora.at[:, RW_LORA:2 * RW_LORA, 2 * BR_W:3 * BR_W].set(rw_a2[:, 0])
    w_lora = w_lora.at[:, RW_LORA:2 * RW_LORA, 3 * BR_W:4 * BR_W].set(rw_a2[:, 1])
    w_lora = w_lora.at[:, 2 * RW_LORA:3 * RW_LORA, 4 * BR_W:5 * BR_W].set(rw_g2).astype(BF16)
    kvec = jnp.stack([rw_kk, rw_ka, rw_rk], axis=1)
    seg = jnp.asarray(np.kron(np.eye(RW_HEADS), np.ones((RW_HEAD, RW_HEAD))), BF16)
    lru_w = jnp.concatenate([_block_diag(lru_wa[:, 0]), _block_diag(lru_wx[:, 0]),
                             _block_diag(lru_wa[:, 1]), _block_diag(lru_wx[:, 1])], axis=2).astype(BF16)
    lru_bias = jnp.stack([lru_ba[:, 0], lru_bx[:, 0], lru_ba[:, 1], lru_bx[:, 1]], axis=1)
    glane = jnp.repeat(ret_gamma, RET_HEAD, axis=2)
    gtile = jnp.broadcast_to(ret_gamma[:, :, :, None, None], (L, 2, RET_HEADS, 8, RET_CHUNK))
    f_w1 = jnp.pad(hy_f_w1, ((0, 0), (0, HY_FEAT_PAD - HY_FEAT), (0, 0)))
    br_b = br_proj.astype(BF16)
    w_out_b = w_out.astype(BF16)
    ffn_gate = ffn_w1[:, :, :D_FF].astype(BF16)
    ffn_up_w = ffn_w1[:, :, D_FF:].astype(BF16)
    ffn_w2_b = ffn_w2.astype(BF16)

    cc = jnp.concatenate([c, c_ctx[None, :], jnp.zeros((16 - B - 1, D), F32)], axis=0)
    mods = modulation(cc, w_mod.astype(BF16), b_mod[:, None, :])

    zero_states = (jnp.zeros((RW_VH // 8, RW_HEAD, 8, 16 * B), F32), jnp.zeros((2, B, 1, BR_W), F32),
                   jnp.zeros((2, B, BR_W, BR_W), F32))
    xc = ctx
    for l in range(L):
        last = l == L - 1
        lp = {
            'rw': (mu[l], w_lora[l], rw_w0[l], rw_a0[l], kvec[l], seg),
            'lru': (lru_conv_w[l], lru_conv_b[l][None, :], lru_w[l], lru_bias[l], lru_lam[l]),
            'ret': (glane[l], gtile[l]),
            'hy_conv': (hy_conv_w[l], hy_conv_b[l][None, :]),
            'hy': (f_w1[l], hy_f_b1[l][None, :], hy_f_w2[l], hy_f_b2[l][None, :], hy_f_w3[l], hy_freq[l],
                   hy_bias[l]),
        }
        g1 = norm1_g[l][None, :]
        g2 = norm2_g[l][None, :]
        m_lat = [m[:, None, :] for m in jnp.split(mods[l, :B], 6, axis=-1)]
        m_ctx = [jnp.broadcast_to(m[None, :, :], (B, 1, D)) for m in jnp.split(mods[l, B:B + 1], 6, axis=-1)]
        ln_g = rw_ln_g[l][None, :]

        def layer(xs, m, states, on_grid, with_output, tiles, final):
            p = in_projection(xs, g1, m[0], m[1], w_branch[l], tiles[0])
            br, fins = _mixers(p, lp, states, on_grid, with_output, tiles)
            if not with_output:
                return None, fins
            xs = merge(xs, g1, m[0], m[1], m[2], p, br[0], br[1], br[2], br[3], ln_g, seg,
                       w_gate[l], br_b[l], w_out_b[l], tiles[2])
            h = ffn_up(xs, g2, m[3], m[4], ffn_gate[l], ffn_up_w[l], tiles[0])
            xs = ffn_down(h, xs, m[5], ffn_w2_b[l], final_g[None, :], final, tiles[0])
            return xs, fins

        xc_new, ctx_states = layer(xc, m_ctx, zero_states, False, not last, (TC, 64, TC), False)
        x, _ = layer(x, m_lat, ctx_states, True, True, (512, 64, 512), last)
        if not last:
            xc = xc_new
    return x
```

```python
import functools
import math

import numpy as np
import jax
import jax.numpy as jnp
from jax import lax
from jax.experimental import pallas as pl
from jax.experimental.pallas import tpu as pltpu

F32 = jnp.float32
BF16 = jnp.bfloat16

D_MODEL = 1024
DEPTH = 4
GRID_W = 64
N_BRANCH = 4
BR_W = D_MODEL // N_BRANCH

HY_BANDS = 8
HY_FEAT = 1 + 2 * HY_BANDS
HY_FEAT_PAD = 32
HY_HID = 64
HY_TARGET = 1e-2
HY_FAST = 0.3
HY_SLOW = 1.5
HY_IN = 3 * BR_W

RW_HEAD = 64
RW_HEADS = BR_W // RW_HEAD
RW_LORA = 64
RW_IN = 3 * BR_W + 3 * RW_LORA
RW_PAD = 4 * BR_W
RW_LN_EPS = 64e-5

LRU_BLOCKS = 4
LRU_BLOCK = BR_W // LRU_BLOCKS
LRU_C = 8.0
LRU_IN = 2 * BR_W

RET_HEADS = 4
RET_HEAD = BR_W // RET_HEADS
RET_CHUNK = 128
ROPE_BASE = 10000.0
RET_IN = 4 * BR_W
RET_LN_EPS = 1e-5

GATE_IN = N_BRANCH * D_MODEL
D_FF = ((8 * D_MODEL // 3 + 255) // 256) * 256
EPS = 1e-6

OFF_RW = 0
OFF_RET = OFF_RW + RW_PAD
OFF_HY = 3 * HY_IN
OFF_LRU = OFF_HY + HY_IN
N_BR = OFF_LRU + LRU_IN
HY_GAP = OFF_HY - (OFF_RET + RET_IN)

VMEM_LIMIT = 56 * 1024 * 1024
HI = lax.Precision.HIGHEST


def _cparams(*sem):
    return pltpu.CompilerParams(dimension_semantics=sem, vmem_limit_bytes=VMEM_LIMIT)


def _const_spec(shape):
    nd = len(shape)
    return pl.BlockSpec(shape, lambda *_: (0,) * nd, pipeline_mode=pl.Buffered(1))


def _dot(a, b, **kw):
    return jnp.dot(a, b, preferred_element_type=F32, **kw)


def _norm_mod(x, g, sh, sc):
    ms = jnp.mean(x * x, axis=-1, keepdims=True)
    return x * lax.rsqrt(ms + EPS) * g * (1.0 + sc) + sh


def _sigmoid(x):
    return 1.0 / (1.0 + jnp.exp(-x))


def _silu(x):
    return x * _sigmoid(x)


def _softplus(x):
    return jnp.maximum(x, 0.0) + jnp.log(1.0 + jnp.exp(-jnp.abs(x)))


def _mod_kernel(c_ref, w_ref, b_ref, o_ref):
    c = c_ref[...]
    o_ref[0] = _dot(_silu(c).astype(BF16), w_ref[0]) + b_ref[0]


def modulation(cc, w_mod, b_mod):
    L, D, N = w_mod.shape
    tn = 1536
    return pl.pallas_call(
        _mod_kernel,
        grid=(L, N // tn),
        in_specs=[pl.BlockSpec((16, D), lambda l, j: (0, 0)),
                  pl.BlockSpec((1, D, tn), lambda l, j: (l, 0, j)),
                  pl.BlockSpec((1, 1, tn), lambda l, j: (l, 0, j))],
        out_specs=pl.BlockSpec((1, 16, tn), lambda l, j: (l, 0, j)),
        out_shape=jax.ShapeDtypeStruct((L, 16, N), F32),
        compiler_params=_cparams("parallel", "parallel"),
        name="modulation",
    )(cc, w_mod, b_mod)


def _inproj_kernel(x_ref, g_ref, sh_ref, sc_ref, w_ref, o_ref):
    u = _norm_mod(x_ref[0], g_ref[...], sh_ref[0], sc_ref[0]).astype(BF16)
    n = w_ref.shape[1]
    cw = 256
    for c in range(n // cw):
        o_ref[0, :, c * cw:(c + 1) * cw] = _dot(u, w_ref[:, c * cw:(c + 1) * cw])


def in_projection(x, g, sh, sc, w, tm):
    B, T, D = x.shape
    N = w.shape[1]
    vec = pl.BlockSpec((1, 1, D), lambda b, i: (b, 0, 0))
    return pl.pallas_call(
        _inproj_kernel,
        grid=(B, T // tm),
        in_specs=[pl.BlockSpec((1, tm, D), lambda b, i: (b, i, 0)),
                  _const_spec((1, D)), vec, vec, _const_spec((D, N))],
        out_specs=pl.BlockSpec((1, tm, N), lambda b, i: (b, i, 0)),
        out_shape=jax.ShapeDtypeStruct((B, T, N), F32),
        compiler_params=_cparams("parallel", "parallel"),
        name="in_projection",
    )(x, g, sh, sc, w)


def _ffn1_kernel(x_ref, g_ref, sh_ref, sc_ref, wg_ref, wu_ref, o_ref):
    u = _norm_mod(x_ref[0], g_ref[...], sh_ref[0], sc_ref[0]).astype(BF16)
    n = wg_ref.shape[1]
    cw = 256
    for c in range(n // cw):
        sl = slice(c * cw, (c + 1) * cw)
        gate = _dot(u, wg_ref[:, sl])
        up = _dot(u, wu_ref[:, sl])
        o_ref[0, :, sl] = (_silu(gate) * up).astype(BF16)


def ffn_up(x, g, sh, sc, w_gate, w_up, tm):
    B, T, D = x.shape
    N = w_gate.shape[1]
    vec = pl.BlockSpec((1, 1, D), lambda b, i: (b, 0, 0))
    return pl.pallas_call(
        _ffn1_kernel,
        grid=(B, T // tm),
        in_specs=[pl.BlockSpec((1, tm, D), lambda b, i: (b, i, 0)),
                  _const_spec((1, D)), vec, vec, _const_spec((D, N)), _const_spec((D, N))],
        out_specs=pl.BlockSpec((1, tm, N), lambda b, i: (b, i, 0)),
        out_shape=jax.ShapeDtypeStruct((B, T, N), BF16),
        compiler_params=_cparams("parallel", "parallel"),
        name="ffn_up",
    )(x, g, sh, sc, w_gate, w_up)


def _ffn2_kernel(h_ref, x_ref, gt_ref, w_ref, fg_ref, o_ref, *, final_norm):
    y = x_ref[0] + gt_ref[0] * _dot(h_ref[0], w_ref[...])
    if final_norm:
        ms = jnp.mean(y * y, axis=-1, keepdims=True)
        y = y * lax.rsqrt(ms + EPS) * fg_ref[...]
    o_ref[0] = y


def ffn_down(h, x, gate, w, final_g, final_norm, tm):
    B, T, D = x.shape
    N = h.shape[2]
    return pl.pallas_call(
        functools.partial(_ffn2_kernel, final_norm=final_norm),
        grid=(B, T // tm),
        in_specs=[pl.BlockSpec((1, tm, N), lambda b, i: (b, i, 0)),
                  pl.BlockSpec((1, tm, D), lambda b, i: (b, i, 0)),
                  pl.BlockSpec((1, 1, D), lambda b, i: (b, 0, 0)),
                  _const_spec((N, D)), _const_spec((1, D))],
        out_specs=pl.BlockSpec((1, tm, D), lambda b, i: (b, i, 0)),
        out_shape=jax.ShapeDtypeStruct((B, T, D), F32),
        compiler_params=_cparams("parallel", "parallel"),
        name="ffn_down",
    )(h, x, gate, w, final_g)


def _halo_specs(tm, T, width, col_fn, halo):
    r = tm // halo
    last = T // halo - 1
    main = pl.BlockSpec((1, tm, width), lambda *g: (g[0], g[1], col_fn(*g)))
    prev = pl.BlockSpec((1, halo, width), lambda *g: (g[0], jnp.maximum(g[1] * r - 1, 0), col_fn(*g)))
    nxt = pl.BlockSpec((1, halo, width), lambda *g: (g[0], jnp.minimum((g[1] + 1) * r, last), col_fn(*g)))
    return main, prev, nxt


def _shift_rows(x, prev, nxt, s, first, last):
    tm = x.shape[0]
    if s > 0:
        head = jnp.where(first, 0.0, prev[prev.shape[0] - s:, :])
        return jnp.concatenate([head, x[:tm - s, :]], axis=0)
    s = -s
    tail = jnp.where(last, 0.0, nxt[:s, :])
    return jnp.concatenate([x[s:, :], tail], axis=0)


def _hy_prep_kernel(p_ref, pp_ref, pn_ref, w_ref, b_ref, v_ref, g1_ref, g2_ref):
    i = pl.program_id(1)
    first = i == 0
    last = i == pl.num_programs(1) - 1
    x = p_ref[0]
    xm = _shift_rows(x, pp_ref[0], pn_ref[0], 1, first, last)
    xp = _shift_rows(x, pp_ref[0], pn_ref[0], -1, first, last)
    u = b_ref[...] + w_ref[0:1, :] * xm + w_ref[1:2, :] * x + w_ref[2:3, :] * xp
    v_ref[0] = u[:, 0:BR_W]
    g1_ref[0] = u[:, BR_W:2 * BR_W]
    g2_ref[0] = u[:, 2 * BR_W:3 * BR_W]


def hyena_prep(p, conv_w, conv_b, tm):
    B, T, _ = p.shape
    main, prev, nxt = _halo_specs(tm, T, HY_IN, lambda b, i: OFF_HY // HY_IN, 8)
    out = pl.BlockSpec((1, tm, BR_W), lambda b, i: (b, i, 0))
    shp = jax.ShapeDtypeStruct((B, T, BR_W), F32)
    return pl.pallas_call(
        _hy_prep_kernel,
        grid=(B, T // tm),
        in_specs=[main, prev, nxt, _const_spec((3, HY_IN)), _const_spec((1, HY_IN))],
        out_specs=[out, out, out],
        out_shape=[shp, shp, shp],
        compiler_params=_cparams("parallel", "parallel"),
        name="hyena_prep",
    )(p, p, p, conv_w, conv_b)


def _hy_filter_kernel(feat_ref, w1_ref, b1_ref, w2_ref, b2_ref, w3_ref, fq_ref, rates_ref, h_ref, ss_ref):
    i = pl.program_id(0)
    feat = feat_ref[...]
    t = feat[:, 0:1]
    h = jnp.sin(fq_ref[0:1, :] * (_dot(feat, w1_ref[...], precision=HI) + b1_ref[...]))
    h = jnp.sin(fq_ref[1:2, :] * (_dot(h, w2_ref[...], precision=HI) + b2_ref[...]))
    h = _dot(h, w3_ref[...], precision=HI) * jnp.exp(-t * rates_ref[...])
    row = lax.broadcasted_iota(jnp.int32, h.shape, 0) + i * h.shape[0]
    col = lax.broadcasted_iota(jnp.int32, h.shape, 1)
    h = jnp.where((row == 0) & ((col // BR_W) % 2 == 1), 0.0, h)
    h_ref[...] = h

    @pl.when(i == 0)
    def _():
        ss_ref[...] = jnp.zeros_like(ss_ref)

    ss_ref[...] += jnp.sum(h * h, axis=0, keepdims=True)


def hyena_filter(feat, w1, b1, w2, b2, w3, freq, rates, tl):
    L = feat.shape[0]
    C = w3.shape[1]
    return pl.pallas_call(
        _hy_filter_kernel,
        grid=(L // tl,),
        in_specs=[pl.BlockSpec((tl, HY_FEAT_PAD), lambda i: (i, 0)),
                  _const_spec((HY_FEAT_PAD, HY_HID)), _const_spec((1, HY_HID)),
                  _const_spec((HY_HID, HY_HID)), _const_spec((1, HY_HID)),
                  _const_spec((HY_HID, C)), _const_spec((2, HY_HID)), _const_spec((1, C))],
        out_specs=[pl.BlockSpec((tl, C), lambda i: (i, 0)), pl.BlockSpec((1, C), lambda i: (0, 0))],
        out_shape=[jax.ShapeDtypeStruct((L, C), F32), jax.ShapeDtypeStruct((1, C), F32)],
        compiler_params=_cparams("arbitrary"),
        name="hyena_filter",
    )(feat, w1, b1, w2, b2, w3, freq, rates)


def _filter_scale(ss_ref, o):
    e = ss_ref[:, 2 * o * BR_W:(2 * o + 1) * BR_W] + ss_ref[:, (2 * o + 1) * BR_W:(2 * o + 2) * BR_W]
    return lax.rsqrt(e + EPS)


def _combine_spectrum(x, ss_ref, o, half):
    xf = x[:, 2 * o * BR_W:(2 * o + 1) * BR_W]
    xb = x[:, (2 * o + 1) * BR_W:(2 * o + 2) * BR_W]
    sc = _filter_scale(ss_ref, o)
    hr = (xf[:half] + xb[:half]) * sc
    hi = (xf[half:] - xb[half:]) * sc
    return jnp.concatenate([hr, hi], axis=0)


def _cmul(x, h, half):
    xr, xi = x[:half], x[half:]
    hr, hi = h[:half], h[half:]
    return jnp.concatenate([xr * hr - xi * hi, xr * hi + xi * hr], axis=0)


def _dft1_kernel(z_ref, f_ref, a_ref):
    a_ref[0] = _dot(f_ref[...], z_ref[0].astype(BF16)).astype(BF16)


def dft_stage1(z, f1, tn):
    B, n1, W = z.shape
    M = f1.shape[0]
    return pl.pallas_call(
        _dft1_kernel,
        grid=(B, W // tn),
        in_specs=[pl.BlockSpec((1, n1, tn), lambda b, j: (b, 0, j)), _const_spec((M, n1))],
        out_specs=pl.BlockSpec((1, M, tn), lambda b, j: (b, 0, j)),
        out_shape=jax.ShapeDtypeStruct((B, M, W), BF16),
        compiler_params=_cparams("parallel", "parallel"),
        name="dft_stage1",
    )(z, f1)


def _spec2_kernel(a_ref, g_ref, ss_ref, h_ref):
    kb = g_ref.shape[0]
    n2 = a_ref.shape[3]
    for k in range(kb):
        a = a_ref[0, :, k].reshape(2 * n2, a_ref.shape[4])
        x = _dot(g_ref[k], a)
        for o in range(2):
            h_ref[o, k] = _combine_spectrum(x, ss_ref, o, n2)


def filter_spectrum(a, g, ss, kb):
    _, _, N1, N2, C = a.shape
    return pl.pallas_call(
        _spec2_kernel,
        grid=(N1 // kb,),
        in_specs=[pl.BlockSpec((1, 2, kb, N2, C), lambda i: (0, 0, i, 0, 0)),
                  pl.BlockSpec((kb, 2 * N2, 2 * N2), lambda i: (i, 0, 0)),
                  _const_spec((1, C))],
        out_specs=pl.BlockSpec((2, kb, 2 * N2, BR_W), lambda i: (0, i, 0, 0)),
        out_shape=jax.ShapeDtypeStruct((2, N1, 2 * N2, BR_W), F32),
        compiler_params=_cparams("parallel"),
        name="filter_spectrum",
    )(a, g, ss)


def _conv2_kernel(a_ref, g_ref, gi_ref, h_ref, o_ref):
    kb = g_ref.shape[0]
    n2 = a_ref.shape[3]
    C = a_ref.shape[4]
    for k in range(kb):
        a = a_ref[0, :, k].reshape(2 * n2, C)
        y = _cmul(_dot(g_ref[k], a), h_ref[0, k], n2).astype(BF16)
        o_ref[0, :, k] = _dot(gi_ref[k], y).astype(BF16).reshape(2, n2, C)


def spectral_multiply(a, g, gi, h, o, kb):
    B, _, N1, N2, C = a.shape
    blk = pl.BlockSpec((1, 2, kb, N2, C), lambda i, b: (b, 0, i, 0, 0))
    mat = pl.BlockSpec((kb, 2 * N2, 2 * N2), lambda i, b: (i, 0, 0))
    return pl.pallas_call(
        _conv2_kernel,
        grid=(N1 // kb, B),
        in_specs=[blk, mat, mat, pl.BlockSpec((1, kb, 2 * N2, C), lambda i, b: (o, i, 0, 0))],
        out_specs=blk,
        out_shape=jax.ShapeDtypeStruct(a.shape, BF16),
        compiler_params=_cparams("parallel", "parallel"),
        name="spectral_multiply",
    )(a, g, gi, h)


def _idft1_kernel(b_ref, f_ref, z_ref, gate_ref, bias_ref, o_ref):
    y = _dot(f_ref[...], b_ref[0])
    z = z_ref[0]
    o_ref[0] = gate_ref[0] * (y + bias_ref[...] * z)


def idft_stage1(bm, fi, z, gate, bias, tn):
    B, M, W = bm.shape
    n1 = fi.shape[0]
    blk = pl.BlockSpec((1, n1, tn), lambda b, j: (b, 0, j))
    return pl.pallas_call(
        _idft1_kernel,
        grid=(B, W // tn),
        in_specs=[pl.BlockSpec((1, M, tn), lambda b, j: (b, 0, j)), _const_spec((n1, M)),
                  blk, blk, _const_spec((1, tn))],
        out_specs=blk,
        out_shape=jax.ShapeDtypeStruct((B, n1, W), F32),
        compiler_params=_cparams("parallel", "parallel"),
        name="idft_stage1",
    )(bm, fi, z, gate, bias)


@functools.lru_cache(maxsize=None)
def _dft_tables(L):
    N = 2 * L
    N2 = 128
    N1 = N // N2
    nz = L // N2
    k1 = np.arange(N1)[:, None]
    n1 = np.arange(nz)[None, :]
    th = 2 * np.pi * ((k1 * n1) % N1) / N1
    f1 = np.concatenate([np.cos(th), -np.sin(th)], axis=0)
    fi = np.concatenate([np.cos(th).T, -np.sin(th).T], axis=1) / N
    kk1 = np.arange(N1)[:, None, None]
    k2 = np.arange(N2)[None, :, None]
    n2 = np.arange(N2)[None, None, :]
    ph = 2 * np.pi * ((n2 * k2 * N1 + n2 * kk1) % N) / N
    gr, gim = np.cos(ph), -np.sin(ph)
    g = np.concatenate([np.concatenate([gr, -gim], axis=2), np.concatenate([gim, gr], axis=2)], axis=1)
    hr, him = np.swapaxes(gr, 1, 2), -np.swapaxes(gim, 1, 2)
    gi = np.concatenate([np.concatenate([hr, -him], axis=2), np.concatenate([him, hr], axis=2)], axis=1)
    return tuple(np.asarray(t, np.float32) for t in (f1, fi, g, gi))


def _spec_direct_kernel(hf_ref, f_ref, ss_ref, h_ref):
    x = _dot(f_ref[...], hf_ref[...].astype(BF16))
    half = x.shape[0] // 2
    for o in range(2):
        h_ref[o] = _combine_spectrum(x, ss_ref, o, half)


def filter_spectrum_direct(hf, f, ss):
    L, C = hf.shape
    return pl.pallas_call(
        _spec_direct_kernel,
        grid=(1,),
        in_specs=[_const_spec((L, C)), _const_spec((4 * L, L)), _const_spec((1, C))],
        out_specs=pl.BlockSpec((2, 4 * L, BR_W), lambda i: (0, 0, 0)),
        out_shape=jax.ShapeDtypeStruct((2, 4 * L, BR_W), F32),
        compiler_params=_cparams("arbitrary"),
        name="filter_spectrum_direct",
    )(hf, f, ss)


def _conv_direct_kernel(z_ref, gate_ref, bias_ref, f_ref, fi_ref, h_ref, o_ref):
    z = z_ref[0]
    x = _dot(f_ref[...], z.astype(BF16))
    y = _cmul(x, h_ref[0], x.shape[0] // 2).astype(BF16)
    o_ref[0] = gate_ref[0] * (_dot(fi_ref[...], y) + bias_ref[...] * z)


def conv_direct(z, gate, bias, f, fi, h, o):
    B, L, C = z.shape
    blk = pl.BlockSpec((1, L, C), lambda b: (b, 0, 0))
    return pl.pallas_call(
        _conv_direct_kernel,
        grid=(B,),
        in_specs=[blk, blk, _const_spec((1, C)), _const_spec((4 * L, L)), _const_spec((L, 4 * L)),
                  pl.BlockSpec((1, 4 * L, C), lambda b: (o, 0, 0))],
        out_specs=blk,
        out_shape=jax.ShapeDtypeStruct((B, L, C), F32),
        compiler_params=_cparams("parallel"),
        name="conv_direct",
    )(z, gate, bias, f, fi, h)


@functools.lru_cache(maxsize=None)
def _dft_direct_tables(L):
    N = 2 * L
    k = np.arange(N)[:, None]
    n = np.arange(L)[None, :]
    th = 2 * np.pi * ((k * n) % N) / N
    f = np.concatenate([np.cos(th), -np.sin(th)], axis=0)
    fi = np.concatenate([np.cos(th).T, -np.sin(th).T], axis=1) / N
    return np.asarray(f, np.float32), np.asarray(fi, np.float32)


@functools.lru_cache(maxsize=None)
def _filter_features(L):
    t = np.arange(L, dtype=np.float32) / np.float32(L)
    ang = (2.0 * math.pi) * t[:, None].astype(np.float64) * np.arange(1, HY_BANDS + 1)
    feat = np.zeros((L, HY_FEAT_PAD), np.float32)
    feat[:, 0] = t
    feat[:, 1:1 + HY_BANDS] = np.sin(ang)
    feat[:, 1 + HY_BANDS:HY_FEAT] = np.cos(ang)
    rates = np.abs(np.linspace(math.log(HY_TARGET) / HY_SLOW, math.log(HY_TARGET) / HY_FAST, BR_W))
    return feat, np.tile(np.asarray(rates, np.float32), 4)[None, :]


def hyena_branch(v, g1, g2, hp):
    f_w1, f_b1, f_w2, f_b2, f_w3, freq, bias = hp
    B, L, C = v.shape
    feat, rates = _filter_features(L)
    hf, ss = hyena_filter(jnp.asarray(feat), f_w1, f_b1, f_w2, f_b2, f_w3, freq, jnp.asarray(rates),
                          min(L, 512))
    if L <= 512:
        f, fi = (jnp.asarray(t).astype(BF16) for t in _dft_direct_tables(L))
        spec = filter_spectrum_direct(hf, f, ss)
        z = v
        for o, gate in enumerate((g1, g2)):
            z = conv_direct(z, gate, bias[o:o + 1], f, fi, spec, o)
        return z
    f1, fi1, g, gi = (jnp.asarray(t).astype(BF16) for t in _dft_tables(L))
    N2 = 128
    N1 = 2 * L // N2
    nz = L // N2
    kb = 8
    a = dft_stage1(hf.reshape(1, nz, N2 * 4 * C), f1, 8192)
    spec = filter_spectrum(a.reshape(1, 2, N1, N2, 4 * C), g, ss, kb)
    W = N2 * C
    tn = 8192
    z = v.reshape(B, nz, W)
    for o, gate in enumerate((g1, g2)):
        a = dft_stage1(z, f1, tn).reshape(B, 2, N1, N2, C)
        bm = spectral_multiply(a, g, gi, spec, o, kb).reshape(B, 2 * N1, W)
        z = idft_stage1(bm, fi1, z, gate.reshape(B, nz, W), jnp.tile(bias[o:o + 1], (1, tn // C)), tn)
    return z.reshape(B, L, C)


def _head_sum(x, seg_ref):
    hi = x.astype(BF16)
    lo = (x - hi.astype(F32)).astype(BF16)
    return _dot(hi, seg_ref[...]) + _dot(lo, seg_ref[...])


def _rw_prep_kernel(p_ref, pp_ref, pn_ref, mu_ref, wl_ref, w0_ref, a0_ref, kv_ref, seg_ref,
                    r_ref, v_ref, na_ref, w_ref, kt_ref, b_ref, bonus_ref, g_ref, *, on_grid):
    i = pl.program_id(1)
    first = i == 0
    last = i == pl.num_programs(1) - 1
    x = p_ref[0]
    tm = x.shape[0]
    grp = lax.broadcasted_iota(jnp.int32, x.shape, 1) % 4
    prev, nxt = pp_ref[0], pn_ref[0]
    if on_grid:
        col = (lax.broadcasted_iota(jnp.int32, (tm, 1), 0) + i * tm) % GRID_W
        left = jnp.where(col == 0, 0.0, pltpu.roll(x, 1, 0))
        right = jnp.where(col == GRID_W - 1, 0.0, pltpu.roll(x, tm - 1, 0))
        up = _shift_rows(x, prev, nxt, GRID_W, first, last)
        down = _shift_rows(x, prev, nxt, -GRID_W, first, last)
        shifted = jnp.where(grp == 0, left, jnp.where(grp == 1, right, jnp.where(grp == 2, up, down)))
    else:
        before = _shift_rows(x, prev, nxt, 1, first, last)
        after = _shift_rows(x, prev, nxt, -1, first, last)
        shifted = jnp.where(grp % 2 == 0, before, after)
    xx = x + (shifted - x) * mu_ref[...]
    r = xx[:, 0:BR_W]
    k = xx[:, BR_W:2 * BR_W]
    v = xx[:, 2 * BR_W:3 * BR_W]
    lo = xx[:, 3 * BR_W:4 * BR_W]
    ll = lax.broadcasted_iota(jnp.int32, lo.shape, 1)
    act = jnp.where(ll < RW_LORA, jnp.tanh(lo), jnp.where(ll < 2 * RW_LORA, lo, _sigmoid(lo)))
    z = _dot(act.astype(BF16), wl_ref[...])
    kk = k * kv_ref[0:1, :]
    kk = kk * lax.rsqrt(_head_sum(kk * kk, seg_ref) + 1e-12)
    r_ref[0] = r
    v_ref[0] = v
    na_ref[0] = -kk
    bonus_ref[0] = _head_sum(r * k * kv_ref[2:3, :], seg_ref) * v
    g_ref[0] = z[:, 4 * BR_W:5 * BR_W]
    for d in range(2):
        logw = -_softplus(-(w0_ref[d:d + 1, :] + z[:, d * BR_W:(d + 1) * BR_W])) - 0.5
        w_ref[d, 0] = jnp.exp(-jnp.exp(logw))
        a = _sigmoid(a0_ref[d:d + 1, :] + z[:, (2 + d) * BR_W:(3 + d) * BR_W])
        kt_ref[d, 0] = k * (1.0 + (a - 1.0) * kv_ref[1:2, :])
        b_ref[d, 0] = kk * a


def rwkv_prep(p, mu, w_lora, w0, a0, kvec, seg, on_grid, tm):
    B, T, _ = p.shape
    halo = GRID_W if on_grid else 8
    main, prev, nxt = _halo_specs(tm, T, RW_PAD, lambda b, i: OFF_RW // RW_PAD, halo)
    one = pl.BlockSpec((1, tm, BR_W), lambda b, i: (b, i, 0))
    two = pl.BlockSpec((2, 1, tm, BR_W), lambda b, i: (0, b, i, 0))
    s1 = jax.ShapeDtypeStruct((B, T, BR_W), F32)
    s2 = jax.ShapeDtypeStruct((2, B, T, BR_W), F32)
    return pl.pallas_call(
        functools.partial(_rw_prep_kernel, on_grid=on_grid),
        grid=(B, T // tm),
        in_specs=[main, prev, nxt, _const_spec((1, RW_PAD)), _const_spec((BR_W, 5 * BR_W)),
                  _const_spec((2, BR_W)), _const_spec((2, BR_W)), _const_spec((3, BR_W)),
                  _const_spec((BR_W, BR_W))],
        out_specs=[one, one, one, two, two, two, one, one],
        out_shape=[s1, s1, s1, s2, s2, s2, s1, s1],
        compiler_params=_cparams("parallel", "parallel"),
        name="rwkv_prep",
    )(p, p, p, mu, w_lora, w0, a0, kvec, seg)


RW_VH = RW_HEAD // 2
LANE_W = 128
N_KEYED = 5
RW_UNROLL = 4
RW_PARTIALS = 2


def _chain_rows(refs, t, tb, shifts):
    blocks = []
    for d in range(2):
        full = refs[d][t if d == 0 else tb]
        for half in range(2):
            x = full[:, half * LANE_W:(half + 1) * LANE_W]
            for sh in shifts:
                blocks.append(x if sh == 0 else pltpu.roll(x, LANE_W - sh, 1))
    return blocks


def _rw_scan_kernel(*refs, B, Tt):
    nin = 2 * (N_KEYED + 1)
    ins = refs[:nin]
    s0_ref = refs[nin]
    y_refs = refs[nin + 1:nin + 3]
    s_ref = refs[nin + 3]
    per_set = N_KEYED + 2
    sets = [refs[nin + 4 + per_set * u:nin + 4 + per_set * (u + 1)] for u in range(RW_UNROLL)]
    bufs = [(st[:N_KEYED], st[N_KEYED], st[N_KEYED + 1]) for st in sets]
    sa_ref = refs[nin + 4 + per_set * RW_UNROLL]
    NL = 16 * B
    A_TILE = 3

    @pl.when(pl.program_id(0) == 0)
    def _():
        s_ref[...] = s0_ref[...]

    def keyed_job(t, slot, n):
        def issue():
            rows = _chain_rows(ins[2 * n:2 * n + 2], t, Tt - 1 - t, (0, RW_HEAD))
            return jnp.concatenate(rows + rows, axis=0).T[0:RW_HEAD, :]

        def commit(val):
            bufs[slot][0][n][...] = val
        return issue, commit

    def values_job(t, slot):
        def issue():
            rows = []
            for vs in range(2):
                rows += _chain_rows(ins[2 * N_KEYED:], t, Tt - 1 - t, (vs * RW_VH, RW_HEAD + vs * RW_VH))
            return jnp.concatenate(rows, axis=0).T[0:RW_VH, :]

        def commit(val):
            bufs[slot][1][...] = val
        return issue, commit

    def output_job(t, slot):
        def issue():
            ys = bufs[slot][2][...]
            y = jnp.concatenate([ys, jnp.zeros((LANE_W - RW_VH, NL), F32)], axis=0).T
            out = []
            for d in range(2):
                halves = []
                for half in range(2):
                    acc = None
                    for vs in range(2):
                        for h2 in range(2):
                            row0 = (((vs * 2 + d) * 2 + half) * 2 + h2) * B
                            blk = y[row0:row0 + B, :]
                            sh = h2 * RW_HEAD + vs * RW_VH
                            blk = blk if sh == 0 else pltpu.roll(blk, sh, 1)
                            acc = blk if acc is None else acc + blk
                    halves.append(acc)
                out.append(jnp.concatenate(halves, axis=1))
            return out

        def commit(val):
            y_refs[0][t] = val[0]
            y_refs[1][Tt - 1 - t] = val[1]
        return issue, commit

    groups = range(RW_VH // 8)
    others = tuple(n for n in range(N_KEYED) if n != A_TILE)

    def step_pieces(slot, sa_in, sa_out):
        tiles, vt_ref, ys_ref = bufs[slot]
        a_next = bufs[(slot + 1) % RW_UNROLL][0][A_TILE]
        row = lambda ref, k: ref[pl.ds(k, 1), :]
        ys = [[None] * RW_PARTIALS for _ in groups]
        san = [[None] * RW_PARTIALS for _ in groups]
        vt = {}
        acc = lambda lst, i, v: lst.__setitem__(i, v if lst[i] is None else lst[i] + v)

        def update(k0, k1):
            if not vt:
                for g in groups:
                    vt[g] = vt_ref[g * 8:(g + 1) * 8, :]
            for k in range(k0, k1):
                r_k, w_k, kt_k, b_k = (row(tiles[n], k) for n in (0, 1, 2, 4))
                an_k = row(a_next, k)
                for g in groups:
                    s = s_ref[g, k] * w_k + sa_in[g] * b_k + vt[g] * kt_k
                    s_ref[g, k] = s
                    acc(ys[g], k % RW_PARTIALS, s * r_k)
                    acc(san[g], k % RW_PARTIALS, s * an_k)
            if k1 == RW_HEAD:
                for g in groups:
                    ys_ref[g * 8:(g + 1) * 8, :] = functools.reduce(lambda a, b: a + b, ys[g])
                    sa_out.append(functools.reduce(lambda a, b: a + b, san[g]))

        cuts = [0, 9, 18, 27, 36, 45, 54, RW_HEAD]
        return [functools.partial(update, cuts[i], cuts[i + 1]) for i in range(len(cuts) - 1)]

    def run(pieces, jobs):
        for i, piece in enumerate(pieces):
            val = jobs[i][0]() if i < len(jobs) else None
            piece()
            if i < len(jobs):
                jobs[i][1](val)

    for issue, commit in ([keyed_job(0, 0, n) for n in range(N_KEYED)] + [values_job(0, 0)]
                          + [keyed_job(1, 1, A_TILE)]):
        commit(issue())
    first = [[None, None] for _ in groups]
    for k in range(RW_HEAD):
        a_k = bufs[0][0][A_TILE][pl.ds(k, 1), :]
        for g in groups:
            p = s_ref[g, k] * a_k
            first[g][k % 2] = p if first[g][k % 2] is None else first[g][k % 2] + p
    for g in groups:
        sa_ref[g] = first[g][0] + first[g][1]

    def body(j, carry):
        t0 = RW_UNROLL * j
        sa = [sa_ref[g] for g in groups]
        for u in range(RW_UNROLL):
            t = t0 + u
            jobs = [keyed_job(jnp.minimum(t + 2, Tt - 1), (u + 2) % RW_UNROLL, A_TILE)]
            t1 = jnp.minimum(t + 1, Tt - 1)
            jobs += [keyed_job(t1, (u + 1) % RW_UNROLL, n) for n in others] + [values_job(t1, (u + 1) % RW_UNROLL)]
            if u > 0:
                jobs.append(output_job(t - 1, u - 1))
            sa_next = []
            run(step_pieces(u, sa, sa_next), jobs)
            sa = sa_next
        for g in groups:
            sa_ref[g] = sa[g]
        issue, commit = output_job(t0 + RW_UNROLL - 1, RW_UNROLL - 1)
        commit(issue())
        return carry

    lax.fori_loop(0, Tt // RW_UNROLL, body, 0)


def rwkv_mix(prep, s0, tt):
    r, v, na, w, kt, b = prep
    T, B, _ = r.shape
    nb = T // tt
    tblk = lambda i, d: i if d == 0 else nb - 1 - i

    in_specs, args = [], []
    for x, per_dir in ((r, False), (w, True), (kt, True), (na, False), (b, True), (v, False)):
        for d in range(2):
            if per_dir:
                in_specs.append(pl.BlockSpec((None, tt, B, BR_W), lambda i, d=d: (d, tblk(i, d), 0, 0)))
            else:
                in_specs.append(pl.BlockSpec((tt, B, BR_W), lambda i, d=d: (tblk(i, d), 0, 0)))
            args.append(x)
    NL = 16 * B
    sblk = pl.BlockSpec((RW_VH // 8, RW_HEAD, 8, NL), lambda i: (0, 0, 0, 0))
    yspecs = [pl.BlockSpec((tt, B, BR_W), lambda i, d=d: (tblk(i, d), 0, 0)) for d in range(2)]
    ysh = jax.ShapeDtypeStruct((T, B, BR_W), F32)
    out = pl.pallas_call(
        functools.partial(_rw_scan_kernel, B=B, Tt=tt),
        grid=(nb,),
        in_specs=in_specs + [sblk],
        out_specs=yspecs + [sblk],
        out_shape=[ysh] * 2 + [jax.ShapeDtypeStruct((RW_VH // 8, RW_HEAD, 8, NL), F32)],
        scratch_shapes=([pltpu.VMEM((RW_HEAD, NL), F32)] * N_KEYED + [pltpu.VMEM((RW_VH, NL), F32)] * 2) * RW_UNROLL
        + [pltpu.VMEM((RW_VH // 8, 8, NL), F32)],
        compiler_params=_cparams("arbitrary"),
        name="rwkv_scan",
    )(*args, s0)
    return out[0], out[1], out[2]


def _gelu_tanh(x):
    return 0.5 * x * (1.0 + jnp.tanh(math.sqrt(2.0 / math.pi) * (x + 0.044715 * (x * x * x))))


def _lru_prep_kernel(p_ref, pp_ref, pn_ref, cw_ref, cb_ref, w_ref, bias_ref, lam_ref, a_ref, b_ref, gg_ref):
    i = pl.program_id(1)
    first = i == 0
    last = i == pl.num_programs(1) - 1
    x = p_ref[0][:, 0:BR_W]
    prev = pp_ref[0][:, 0:BR_W]
    nxt = pn_ref[0][:, 0:BR_W]
    xc = cb_ref[...] + cw_ref[0:1, :] * _shift_rows(x, prev, nxt, 2, first, last)
    xc = xc + cw_ref[1:2, :] * _shift_rows(x, prev, nxt, 1, first, last)
    xc = xc + cw_ref[2:3, :] * x
    xc = xc + cw_ref[3:4, :] * _shift_rows(x, prev, nxt, -1, first, last)
    z = _dot(xc.astype(BF16), w_ref[...])
    for d in range(2):
        r = _sigmoid(z[:, 2 * d * BR_W:(2 * d + 1) * BR_W] + bias_ref[2 * d:2 * d + 1, :])
        gi = _sigmoid(z[:, (2 * d + 1) * BR_W:(2 * d + 2) * BR_W] + bias_ref[2 * d + 1:2 * d + 2, :])
        log_a = -LRU_C * r * _softplus(-lam_ref[d:d + 1, :])
        a_ref[d, 0] = jnp.exp(log_a)
        b_ref[d, 0] = jnp.sqrt(1.0 - jnp.exp(2.0 * log_a)) * (gi * xc)
    gg_ref[0] = _gelu_tanh(p_ref[0][:, BR_W:2 * BR_W])


def lru_prep(p, conv_w, conv_b, w_blk, bias, lam, tm):
    B, T, _ = p.shape
    main, prev, nxt = _halo_specs(tm, T, LRU_IN, lambda b, i: OFF_LRU // LRU_IN, 8)
    two = pl.BlockSpec((2, 1, tm, BR_W), lambda b, i: (0, b, i, 0))
    s2 = jax.ShapeDtypeStruct((2, B, T, BR_W), F32)
    return pl.pallas_call(
        _lru_prep_kernel,
        grid=(B, T // tm),
        in_specs=[main, prev, nxt, _const_spec((4, BR_W)), _const_spec((1, BR_W)),
                  _const_spec((BR_W, 4 * BR_W)), _const_spec((4, BR_W)), _const_spec((2, BR_W))],
        out_specs=[two, two, pl.BlockSpec((1, tm, BR_W), lambda b, i: (b, i, 0))],
        out_shape=[s2, s2, jax.ShapeDtypeStruct((B, T, BR_W), F32)],
        compiler_params=_cparams("parallel", "parallel"),
        name="lru_prep",
    )(p, p, p, conv_w, conv_b, w_blk, bias, lam)


def _affine_scan(a, b, reverse):
    tb = a.shape[0]
    row = lax.broadcasted_iota(jnp.int32, (tb, 1), 0)
    s = 1
    while s < tb:
        sh = tb - s if reverse else s
        ok = (row < tb - s) if reverse else (row >= s)
        a_s = pltpu.roll(a, sh, 0)
        b_s = pltpu.roll(b, sh, 0)
        b = jnp.where(ok, a * b_s + b, b)
        a = jnp.where(ok, a * a_s, a)
        s *= 2
    return a, b


def _lru_scan_kernel(af_ref, bf_ref, ab_ref, bb_ref, h0_ref, hf_ref, hb_ref, fin_ref):
    @pl.when(pl.program_id(1) == 0)
    def _():
        fin_ref[...] = h0_ref[...]

    tb = af_ref.shape[2]
    a, b = _affine_scan(af_ref[0, 0], bf_ref[0, 0], False)
    h = b + a * fin_ref[0, 0]
    hf_ref[0] = h
    fin_ref[0, 0] = h[tb - 1:tb, :]
    a, b = _affine_scan(ab_ref[0, 0], bb_ref[0, 0], True)
    h = b + a * fin_ref[1, 0]
    hb_ref[0] = h
    fin_ref[1, 0] = h[0:1, :]


def lru_scan(a, b, h0, tb):
    _, B, T, C = a.shape
    nb = T // tb
    fwd = pl.BlockSpec((1, 1, tb, C), lambda bi, i: (0, bi, i, 0))
    bwd = pl.BlockSpec((1, 1, tb, C), lambda bi, i: (1, bi, nb - 1 - i, 0))
    st = pl.BlockSpec((2, 1, 1, C), lambda bi, i: (0, bi, 0, 0))
    return pl.pallas_call(
        _lru_scan_kernel,
        grid=(B, nb),
        in_specs=[fwd, fwd, bwd, bwd, st],
        out_specs=[pl.BlockSpec((1, tb, C), lambda bi, i: (bi, i, 0)),
                   pl.BlockSpec((1, tb, C), lambda bi, i: (bi, nb - 1 - i, 0)), st],
        out_shape=[jax.ShapeDtypeStruct((B, T, C), F32), jax.ShapeDtypeStruct((B, T, C), F32),
                   jax.ShapeDtypeStruct((2, B, 1, C), F32)],
        compiler_params=_cparams("parallel", "arbitrary"),
        name="lru_scan",
    )(a, b, a, b, h0)


def _rope(x, cos, sin):
    q4 = RET_HEAD // 4
    lane = lax.broadcasted_iota(jnp.int32, x.shape, 1) % (2 * q4)
    partner = jnp.where(lane < q4, pltpu.roll(x, x.shape[1] - q4, 1), pltpu.roll(x, q4, 1))
    return x * cos + partner * sin


def _ret_dir(x, cos, sin, s, glane, gtile_ref, d, reverse):
    C = x.shape[0]
    q = x[:, 0:BR_W]
    k = x[:, BR_W:2 * BR_W]
    v = x[:, 2 * BR_W:3 * BR_W].astype(BF16)
    if cos is not None:
        q = _rope(q, cos, sin)
        k = _rope(k, cos, sin)
    k = k * (RET_HEAD ** -0.5)
    lg = -_softplus(-glane)
    idx = lax.broadcasted_iota(jnp.int32, (C, 1), 0).astype(F32)
    steps_in = (C - idx) if reverse else (idx + 1.0)
    steps_out = idx if reverse else (C - 1.0 - idx)
    ri = lax.broadcasted_iota(jnp.int32, (C, C), 0)
    ci = lax.broadcasted_iota(jnp.int32, (C, C), 1)
    diff = ((ci - ri) if reverse else (ri - ci)).astype(F32)
    lane_head = lax.broadcasted_iota(jnp.int32, (1, BR_W), 1) // RET_HEAD
    qb = q.astype(BF16)
    kb = k.astype(BF16)
    y = _dot(qb, s.astype(BF16)) * jnp.exp(steps_in * lg)
    for h in range(RET_HEADS):
        lg_h = -_softplus(-gtile_ref[d, h][0:1, :])
        dm = jnp.where(diff >= 0, jnp.exp(diff * lg_h), 0.0)
        mh = lane_head == h
        sc = lax.dot_general(jnp.where(mh, qb, jnp.zeros_like(qb)), kb, (((1,), (1,)), ((), ())),
                             preferred_element_type=F32)
        y = y + jnp.where(mh, _dot((sc * dm).astype(BF16), v), 0.0)
    kd = (k * jnp.exp(steps_out * lg)).astype(BF16)
    upd = lax.dot_general(kd, v, (((0,), (0,)), ((), ())), preferred_element_type=F32)
    rh = lax.broadcasted_iota(jnp.int32, (BR_W, BR_W), 0) // RET_HEAD
    ch = lax.broadcasted_iota(jnp.int32, (BR_W, BR_W), 1) // RET_HEAD
    s = s * jnp.exp(C * lg) + jnp.where(rh == ch, upd, 0.0)
    return y, s


def _ret_kernel(*refs, rope):
    if rope:
        xf_ref, xb_ref, cf_ref, sf_ref, cb_ref, sb_ref, gl_ref, gt_ref, s0_ref, yf_ref, yb_ref, s_ref = refs
    else:
        xf_ref, xb_ref, gl_ref, gt_ref, s0_ref, yf_ref, yb_ref, s_ref = refs

    @pl.when(pl.program_id(1) == 0)
    def _():
        s_ref[...] = s0_ref[...]

    y, s = _ret_dir(xf_ref[0], cf_ref[...] if rope else None, sf_ref[...] if rope else None,
                    s_ref[0, 0], gl_ref[0:1, :], gt_ref, 0, False)
    yf_ref[0] = y
    s_ref[0, 0] = s
    y, s = _ret_dir(xb_ref[0], cb_ref[...] if rope else None, sb_ref[...] if rope else None,
                    s_ref[1, 0], gl_ref[1:2, :], gt_ref, 1, True)
    yb_ref[0] = y
    s_ref[1, 0] = s


def retention(p, cos, sin, glane, gtile, s0, rope):
    B, T, _ = p.shape
    C = RET_CHUNK
    nc = T // C
    cb = OFF_RET // RET_IN
    xf = pl.BlockSpec((1, C, RET_IN), lambda b, i: (b, i, cb))
    xb = pl.BlockSpec((1, C, RET_IN), lambda b, i: (b, nc - 1 - i, cb))
    tf = pl.BlockSpec((C, BR_W), lambda b, i: (i, 0))
    tb = pl.BlockSpec((C, BR_W), lambda b, i: (nc - 1 - i, 0))
    st = pl.BlockSpec((2, 1, BR_W, BR_W), lambda b, i: (0, b, 0, 0))
    ins = [xf, xb] + ([tf, tf, tb, tb] if rope else []) + [
        _const_spec((2, BR_W)), _const_spec((2, RET_HEADS, 8, C)), st]
    args = [p, p] + ([cos, sin, cos, sin] if rope else []) + [glane, gtile, s0]
    return pl.pallas_call(
        functools.partial(_ret_kernel, rope=rope),
        grid=(B, nc),
        in_specs=ins,
        out_specs=[pl.BlockSpec((1, C, BR_W), lambda b, i: (b, i, 0)),
                   pl.BlockSpec((1, C, BR_W), lambda b, i: (b, nc - 1 - i, 0)), st],
        out_shape=[jax.ShapeDtypeStruct((B, T, BR_W), F32), jax.ShapeDtypeStruct((B, T, BR_W), F32),
                   jax.ShapeDtypeStruct((2, B, BR_W, BR_W), F32)],
        compiler_params=_cparams("parallel", "arbitrary"),
        name="retention",
    )(*args)


@functools.lru_cache(maxsize=None)
def _rope_tables(T):
    pos = np.arange(T)
    q4 = RET_HEAD // 4
    inv = ROPE_BASE ** (-np.arange(q4, dtype=np.float64) / q4)
    cos = np.zeros((T, RET_HEAD))
    sin = np.zeros((T, RET_HEAD))
    for part, coord in enumerate((pos // GRID_W, pos % GRID_W)):
        ang = coord[:, None] * inv
        base = part * 2 * q4
        cos[:, base:base + q4] = np.cos(ang)
        cos[:, base + q4:base + 2 * q4] = np.cos(ang)
        sin[:, base:base + q4] = -np.sin(ang)
        sin[:, base + q4:base + 2 * q4] = np.sin(ang)
    tile = lambda t: np.asarray(np.tile(t, (1, RET_HEADS)), np.float32)
    return tile(cos), tile(sin)


def _head_norm(y, seg_ref, eps):
    mu = _head_sum(y, seg_ref) * (1.0 / RW_HEAD)
    yc = y - mu
    var = _head_sum(yc * yc, seg_ref) * (1.0 / RW_HEAD)
    return yc * lax.rsqrt(var + eps)


def _merge_kernel(x_ref, g_ref, sh_ref, sc_ref, gt_ref, hy_ref, ryf_ref, ryb_ref, rbon_ref, rg_ref,
                  lhf_ref, lhb_ref, lgg_ref, tyf_ref, tyb_ref, tg_ref, lng_ref, seg_ref,
                  wg_ref, br_ref, wo_ref, o_ref, m_ref):
    x = x_ref[0]
    u = _norm_mod(x, g_ref[...], sh_ref[0], sc_ref[0]).astype(BF16)
    y_rw = (_head_norm(ryf_ref[0] + ryb_ref[0], seg_ref, RW_LN_EPS) * lng_ref[...] + rbon_ref[0]) * rg_ref[0]
    y_lru = (lhf_ref[0] + lhb_ref[0]) * lgg_ref[0]
    y_ret = _head_norm(tyf_ref[0] + tyb_ref[0], seg_ref, RET_LN_EPS) * _silu(tg_ref[0])
    ys = [y.astype(BF16) for y in (hy_ref[0], y_rw, y_lru, y_ret)]
    D = x.shape[1]
    cw = 256
    for c in range(D // cw):
        acc = None
        for n in range(N_BRANCH):
            gate = _sigmoid(_dot(u, wg_ref[:, n * D + c * cw:n * D + (c + 1) * cw]))
            t = gate * _dot(ys[n], br_ref[n, :, c * cw:(c + 1) * cw])
            acc = t if acc is None else acc + t
        m_ref[:, c * cw:(c + 1) * cw] = acc.astype(BF16)
    o_ref[0] = x + gt_ref[0] * _dot(m_ref[...], wo_ref[...])


def merge(x, g, sh, sc, gt, p, y_hy, rw, lru, ret, ln_g, seg, w_gate, br, w_out, tm):
    B, T, D = x.shape
    vec = pl.BlockSpec((1, 1, D), lambda b, i: (b, 0, 0))
    row = pl.BlockSpec((1, tm, D), lambda b, i: (b, i, 0))
    brn = pl.BlockSpec((1, tm, BR_W), lambda b, i: (b, i, 0))
    tg = pl.BlockSpec((1, tm, BR_W), lambda b, i: (b, i, (OFF_RET + 3 * BR_W) // BR_W))
    return pl.pallas_call(
        _merge_kernel,
        grid=(B, T // tm),
        in_specs=[row, _const_spec((1, D)), vec, vec, vec] + [brn] * 10 + [tg] + [
            _const_spec((1, BR_W)), _const_spec((BR_W, BR_W)), _const_spec((D, GATE_IN)),
            _const_spec((N_BRANCH, BR_W, D)), _const_spec((D, D))],
        out_specs=row,
        out_shape=jax.ShapeDtypeStruct((B, T, D), F32),
        scratch_shapes=[pltpu.VMEM((tm, D), BF16)],
        compiler_params=_cparams("parallel", "parallel"),
        name="merge",
    )(x, g, sh, sc, gt, y_hy, *rw, *lru, *ret, p, ln_g, seg, w_gate, br, w_out)


def _block_diag(w):
    G = w.shape[-3]
    eye = jnp.eye(G, dtype=w.dtype)
    full = w[..., :, :, None, :] * eye[:, None, :, None]
    return full.reshape(*w.shape[:-3], G * w.shape[-2], G * w.shape[-1])


def _mixers(p, lp, states, on_grid, with_output, tiles):
    B, T, _ = p.shape
    tm, tt, _ = tiles
    r, v, na, w, kt, b, bonus, g = rwkv_prep(p, *lp['rw'], on_grid, tm)
    time_major = lambda t: jnp.swapaxes(t, -3, -2)
    y_f, y_b, rw_fin = rwkv_mix(tuple(time_major(t) for t in (r, v, na, w, kt, b)), states[0], tt)
    y_f, y_b = time_major(y_f), time_major(y_b)
    a, bb, gg = lru_prep(p, *lp['lru'], tm)
    h_f, h_b, lru_fin = lru_scan(a, bb, states[1], tm)
    cos, sin = (jnp.asarray(t) for t in _rope_tables(T)) if on_grid else (None, None)
    t_f, t_b, ret_fin = retention(p, cos, sin, *lp['ret'], states[2], on_grid)
    fins = (rw_fin, lru_fin, ret_fin)
    if not with_output:
        return None, fins
    vg = hyena_prep(p, *lp['hy_conv'], min(T, 2048))
    y_hy = hyena_branch(vg[0], vg[1], vg[2], lp['hy'])
    return (y_hy, (y_f, y_b, bonus, g), (h_f, h_b, gg), (t_f, t_b)), fins


def kernel(x, c, ctx, c_ctx, w_mod, b_mod, norm1_g, norm2_g, w_in, hy_conv_w, hy_conv_b, hy_f_w1, hy_f_b1, hy_f_w2, hy_f_b2, hy_f_w3, hy_freq, hy_bias, rw_mu, rw_w0, rw_w2, rw_a0, rw_a2, rw_g2, rw_kk, rw_ka, rw_rk, rw_ln_g, lru_conv_w, lru_conv_b, lru_wa, lru_ba, lru_wx, lru_bx, lru_lam, ret_gamma, br_proj, w_out, ffn_w1, ffn_w2, final_g):
    B, T, D = x.shape
    TC = ctx.shape[1]
    L = w_in.shape[0]

    s0, s1, s2, s3 = HY_IN, HY_IN + RW_IN, HY_IN + RW_IN + LRU_IN, HY_IN + RW_IN + LRU_IN + RET_IN
    w_branch = jnp.concatenate([w_in[:, :, s0:s1], jnp.zeros((L, D, RW_PAD - RW_IN), w_in.dtype),
                                w_in[:, :, s2:s3], jnp.zeros((L, D, HY_GAP), w_in.dtype),
                                w_in[:, :, 0:s0], w_in[:, :, s1:s2]], axis=2).astype(BF16)
    w_gate = w_in[:, :, s3:].astype(BF16)
    mu = jnp.pad(rw_mu, ((0, 0), (0, RW_PAD - RW_IN)))[:, None, :]
    w_lora = jnp.zeros((L, BR_W, 5 * BR_W), F32)
    w_lora = w_lora.at[:, 0:RW_LORA, 0:BR_W].set(rw_w2[:, 0]).at[:, 0:RW_LORA, BR_W:2 * BR_W].set(rw_w2[:, 1])
    w_lora = w_lora.at[:, RW_LORA:2 * RW_LORA, 2 * BR_W:3 * BR_W].set(rw_a2[:, 0])
    w_lora = w_lora.at[:, RW_LORA:2 * RW_LORA, 3 * BR_W:4 * BR_W].set(rw_a2[:, 1])
    w_lora = w_lora.at[:, 2 * RW_LORA:3 * RW_LORA, 4 * BR_W:5 * BR_W].set(rw_g2).astype(BF16)
    kvec = jnp.stack([rw_kk, rw_ka, rw_rk], axis=1)
    seg = jnp.asarray(np.kron(np.eye(RW_HEADS), np.ones((RW_HEAD, RW_HEAD))), BF16)
    lru_w = jnp.concatenate([_block_diag(lru_wa[:, 0]), _block_diag(lru_wx[:, 0]),
                             _block_diag(lru_wa[:, 1]), _block_diag(lru_wx[:, 1])], axis=2).astype(BF16)
    lru_bias = jnp.stack([lru_ba[:, 0], lru_bx[:, 0], lru_ba[:, 1], lru_bx[:, 1]], axis=1)
    glane = jnp.repeat(ret_gamma, RET_HEAD, axis=2)
    gtile = jnp.broadcast_to(ret_gamma[:, :, :, None, None], (L, 2, RET_HEADS, 8, RET_CHUNK))
    f_w1 = jnp.pad(hy_f_w1, ((0, 0), (0, HY_FEAT_PAD - HY_FEAT), (0, 0)))
    br_b = br_proj.astype(BF16)
    w_out_b = w_out.astype(BF16)
    ffn_gate = ffn_w1[:, :, :D_FF].astype(BF16)
    ffn_up_w = ffn_w1[:, :, D_FF:].astype(BF16)
    ffn_w2_b = ffn_w2.astype(BF16)

    cc = jnp.concatenate([c, c_ctx[None, :], jnp.zeros((16 - B - 1, D), F32)], axis=0)
    mods = modulation(cc, w_mod.astype(BF16), b_mod[:, None, :])

    zero_states = (jnp.zeros((RW_VH // 8, RW_HEAD, 8, 16 * B), F32), jnp.zeros((2, B, 1, BR_W), F32),
                   jnp.zeros((2, B, BR_W, BR_W), F32))
    xc = ctx
    for l in range(L):
        last = l == L - 1
        lp = {
            'rw': (mu[l], w_lora[l], rw_w0[l], rw_a0[l], kvec[l], seg),
            'lru': (lru_conv_w[l], lru_conv_b[l][None, :], lru_w[l], lru_bias[l], lru_lam[l]),
            'ret': (glane[l], gtile[l]),
            'hy_conv': (hy_conv_w[l], hy_conv_b[l][None, :]),
            'hy': (f_w1[l], hy_f_b1[l][None, :], hy_f_w2[l], hy_f_b2[l][None, :], hy_f_w3[l], hy_freq[l],
                   hy_bias[l]),
        }
        g1 = norm1_g[l][None, :]
        g2 = norm2_g[l][None, :]
        m_lat = [m[:, None, :] for m in jnp.split(mods[l, :B], 6, axis=-1)]
        m_ctx = [jnp.broadcast_to(m[None, :, :], (B, 1, D)) for m in jnp.split(mods[l, B:B + 1], 6, axis=-1)]
        ln_g = rw_ln_g[l][None, :]

        def layer(xs, m, states, on_grid, with_output, tiles, final):
            p = in_projection(xs, g1, m[0], m[1], w_branch[l], tiles[0])
            br, fins = _mixers(p, lp, states, on_grid, with_output, tiles)
            if not with_output:
                return None, fins
            xs = merge(xs, g1, m[0], m[1], m[2], p, br[0], br[1], br[2], br[3], ln_g, seg,
                       w_gate[l], br_b[l], w_out_b[l], tiles[2])
            h = ffn_up(xs, g2, m[3], m[4], ffn_gate[l], ffn_up_w[l], tiles[0])
            xs = ffn_down(h, xs, m[5], ffn_w2_b[l], final_g[None, :], final, tiles[0])
            return xs, fins

        xc_new, ctx_states = layer(xc, m_ctx, zero_states, False, not last, (TC, 64, TC), False)
        x, _ = layer(x, m_lat, ctx_states, True, True, (512, 64, 512), last)
        if not last:
            xc = xc_new
    return x
```

```python
import functools
import math

import numpy as np
import jax
import jax.numpy as jnp
from jax import lax
from jax.experimental import pallas as pl
from jax.experimental.pallas import tpu as pltpu

F32 = jnp.float32
BF16 = jnp.bfloat16

D_MODEL = 1024
DEPTH = 4
GRID_W = 64
N_BRANCH = 4
BR_W = D_MODEL // N_BRANCH

HY_BANDS = 8
HY_FEAT = 1 + 2 * HY_BANDS
HY_FEAT_PAD = 32
HY_HID = 64
HY_TARGET = 1e-2
HY_FAST = 0.3
HY_SLOW = 1.5
HY_IN = 3 * BR_W

RW_HEAD = 64
RW_HEADS = BR_W // RW_HEAD
RW_LORA = 64
RW_IN = 3 * BR_W + 3 * RW_LORA
RW_PAD = 4 * BR_W
RW_LN_EPS = 64e-5

LRU_BLOCKS = 4
LRU_BLOCK = BR_W // LRU_BLOCKS
LRU_C = 8.0
LRU_IN = 2 * BR_W

RET_HEADS = 4
RET_HEAD = BR_W // RET_HEADS
RET_CHUNK = 128
ROPE_BASE = 10000.0
RET_IN = 4 * BR_W
RET_LN_EPS = 1e-5

GATE_IN = N_BRANCH * D_MODEL
D_FF = ((8 * D_MODEL // 3 + 255) // 256) * 256
EPS = 1e-6

OFF_RW = 0
OFF_RET = OFF_RW + RW_PAD
OFF_HY = 3 * HY_IN
OFF_LRU = OFF_HY + HY_IN
N_BR = OFF_LRU + LRU_IN
HY_GAP = OFF_HY - (OFF_RET + RET_IN)

VMEM_LIMIT = 56 * 1024 * 1024
HI = lax.Precision.HIGHEST


def _cparams(*sem):
    return pltpu.CompilerParams(dimension_semantics=sem, vmem_limit_bytes=VMEM_LIMIT)


def _const_spec(shape):
    nd = len(shape)
    return pl.BlockSpec(shape, lambda *_: (0,) * nd, pipeline_mode=pl.Buffered(1))


def _dot(a, b, **kw):
    return jnp.dot(a, b, preferred_element_type=F32, **kw)


def _norm_mod(x, g, sh, sc):
    ms = jnp.mean(x * x, axis=-1, keepdims=True)
    return x * lax.rsqrt(ms + EPS) * g * (1.0 + sc) + sh


def _sigmoid(x):
    return 1.0 / (1.0 + jnp.exp(-x))


def _silu(x):
    return x * _sigmoid(x)


def _softplus(x):
    return jnp.maximum(x, 0.0) + jnp.log(1.0 + jnp.exp(-jnp.abs(x)))


def _mod_kernel(c_ref, w_ref, b_ref, o_ref):
    c = c_ref[...]
    o_ref[0] = _dot(_silu(c).astype(BF16), w_ref[0]) + b_ref[0]


def modulation(cc, w_mod, b_mod):
    L, D, N = w_mod.shape
    tn = 1536
    return pl.pallas_call(
        _mod_kernel,
        grid=(L, N // tn),
        in_specs=[pl.BlockSpec((16, D), lambda l, j: (0, 0)),
                  pl.BlockSpec((1, D, tn), lambda l, j: (l, 0, j)),
                  pl.BlockSpec((1, 1, tn), lambda l, j: (l, 0, j))],
        out_specs=pl.BlockSpec((1, 16, tn), lambda l, j: (l, 0, j)),
        out_shape=jax.ShapeDtypeStruct((L, 16, N), F32),
        compiler_params=_cparams("parallel", "parallel"),
        name="modulation",
    )(cc, w_mod, b_mod)


def _inproj_kernel(x_ref, g_ref, sh_ref, sc_ref, w_ref, o_ref):
    u = _norm_mod(x_ref[0], g_ref[...], sh_ref[0], sc_ref[0]).astype(BF16)
    n = w_ref.shape[1]
    cw = 256
    for c in range(n // cw):
        o_ref[0, :, c * cw:(c + 1) * cw] = _dot(u, w_ref[:, c * cw:(c + 1) * cw])


def in_projection(x, g, sh, sc, w, tm):
    B, T, D = x.shape
    N = w.shape[1]
    vec = pl.BlockSpec((1, 1, D), lambda b, i: (b, 0, 0))
    return pl.pallas_call(
        _inproj_kernel,
        grid=(B, T // tm),
        in_specs=[pl.BlockSpec((1, tm, D), lambda b, i: (b, i, 0)),
                  _const_spec((1, D)), vec, vec, _const_spec((D, N))],
        out_specs=pl.BlockSpec((1, tm, N), lambda b, i: (b, i, 0)),
        out_shape=jax.ShapeDtypeStruct((B, T, N), F32),
        compiler_params=_cparams("parallel", "parallel"),
        name="in_projection",
    )(x, g, sh, sc, w)


def _ffn1_kernel(x_ref, g_ref, sh_ref, sc_ref, wg_ref, wu_ref, o_ref):
    u = _norm_mod(x_ref[0], g_ref[...], sh_ref[0], sc_ref[0]).astype(BF16)
    n = wg_ref.shape[1]
    cw = 256
    for c in range(n // cw):
        sl = slice(c * cw, (c + 1) * cw)
        gate = _dot(u, wg_ref[:, sl])
        up = _dot(u, wu_ref[:, sl])
        o_ref[0, :, sl] = (_silu(gate) * up).astype(BF16)


def ffn_up(x, g, sh, sc, w_gate, w_up, tm):
    B, T, D = x.shape
    N = w_gate.shape[1]
    vec = pl.BlockSpec((1, 1, D), lambda b, i: (b, 0, 0))
    return pl.pallas_call(
        _ffn1_kernel,
        grid=(B, T // tm),
        in_specs=[pl.BlockSpec((1, tm, D), lambda b, i: (b, i, 0)),
                  _const_spec((1, D)), vec, vec, _const_spec((D, N)), _const_spec((D, N))],
        out_specs=pl.BlockSpec((1, tm, N), lambda b, i: (b, i, 0)),
        out_shape=jax.ShapeDtypeStruct((B, T, N), BF16),
        compiler_params=_cparams("parallel", "parallel"),
        name="ffn_up",
    )(x, g, sh, sc, w_gate, w_up)


def _ffn2_kernel(h_ref, x_ref, gt_ref, w_ref, fg_ref, o_ref, *, final_norm):
    y = x_ref[0] + gt_ref[0] * _dot(h_ref[0], w_ref[...])
    if final_norm:
        ms = jnp.mean(y * y, axis=-1, keepdims=True)
        y = y * lax.rsqrt(ms + EPS) * fg_ref[...]
    o_ref[0] = y


def ffn_down(h, x, gate, w, final_g, final_norm, tm):
    B, T, D = x.shape
    N = h.shape[2]
    return pl.pallas_call(
        functools.partial(_ffn2_kernel, final_norm=final_norm),
        grid=(B, T // tm),
        in_specs=[pl.BlockSpec((1, tm, N), lambda b, i: (b, i, 0)),
                  pl.BlockSpec((1, tm, D), lambda b, i: (b, i, 0)),
                  pl.BlockSpec((1, 1, D), lambda b, i: (b, 0, 0)),
                  _const_spec((N, D)), _const_spec((1, D))],
        out_specs=pl.BlockSpec((1, tm, D), lambda b, i: (b, i, 0)),
        out_shape=jax.ShapeDtypeStruct((B, T, D), F32),
        compiler_params=_cparams("parallel", "parallel"),
        name="ffn_down",
    )(h, x, gate, w, final_g)


def _halo_specs(tm, T, width, col_fn, halo):
    r = tm // halo
    last = T // halo - 1
    main = pl.BlockSpec((1, tm, width), lambda *g: (g[0], g[1], col_fn(*g)))
    prev = pl.BlockSpec((1, halo, width), lambda *g: (g[0], jnp.maximum(g[1] * r - 1, 0), col_fn(*g)))
    nxt = pl.BlockSpec((1, halo, width), lambda *g: (g[0], jnp.minimum((g[1] + 1) * r, last), col_fn(*g)))
    return main, prev, nxt


def _shift_rows(x, prev, nxt, s, first, last):
    tm = x.shape[0]
    if s > 0:
        head = jnp.where(first, 0.0, prev[prev.shape[0] - s:, :])
        return jnp.concatenate([head, x[:tm - s, :]], axis=0)
    s = -s
    tail = jnp.where(last, 0.0, nxt[:s, :])
    return jnp.concatenate([x[s:, :], tail], axis=0)


def _hy_prep_kernel(p_ref, pp_ref, pn_ref, w_ref, b_ref, v_ref, g1_ref, g2_ref):
    i = pl.program_id(1)
    first = i == 0
    last = i == pl.num_programs(1) - 1
    x = p_ref[0]
    xm = _shift_rows(x, pp_ref[0], pn_ref[0], 1, first, last)
    xp = _shift_rows(x, pp_ref[0], pn_ref[0], -1, first, last)
    u = b_ref[...] + w_ref[0:1, :] * xm + w_ref[1:2, :] * x + w_ref[2:3, :] * xp
    v_ref[0] = u[:, 0:BR_W]
    g1_ref[0] = u[:, BR_W:2 * BR_W]
    g2_ref[0] = u[:, 2 * BR_W:3 * BR_W]


def hyena_prep(p, conv_w, conv_b, tm):
    B, T, _ = p.shape
    main, prev, nxt = _halo_specs(tm, T, HY_IN, lambda b, i: OFF_HY // HY_IN, 8)
    out = pl.BlockSpec((1, tm, BR_W), lambda b, i: (b, i, 0))
    shp = jax.ShapeDtypeStruct((B, T, BR_W), F32)
    return pl.pallas_call(
        _hy_prep_kernel,
        grid=(B, T // tm),
        in_specs=[main, prev, nxt, _const_spec((3, HY_IN)), _const_spec((1, HY_IN))],
        out_specs=[out, out, out],
        out_shape=[shp, shp, shp],
        compiler_params=_cparams("parallel", "parallel"),
        name="hyena_prep",
    )(p, p, p, conv_w, conv_b)


def _hy_filter_kernel(feat_ref, w1_ref, b1_ref, w2_ref, b2_ref, w3_ref, fq_ref, rates_ref, h_ref, ss_ref):
    i = pl.program_id(0)
    feat = feat_ref[...]
    t = feat[:, 0:1]
    h = jnp.sin(fq_ref[0:1, :] * (_dot(feat, w1_ref[...], precision=HI) + b1_ref[...]))
    h = jnp.sin(fq_ref[1:2, :] * (_dot(h, w2_ref[...], precision=HI) + b2_ref[...]))
    h = _dot(h, w3_ref[...], precision=HI) * jnp.exp(-t * rates_ref[...])
    row = lax.broadcasted_iota(jnp.int32, h.shape, 0) + i * h.shape[0]
    col = lax.broadcasted_iota(jnp.int32, h.shape, 1)
    h = jnp.where((row == 0) & ((col // BR_W) % 2 == 1), 0.0, h)
    h_ref[...] = h

    @pl.when(i == 0)
    def _():
        ss_ref[...] = jnp.zeros_like(ss_ref)

    ss_ref[...] += jnp.sum(h * h, axis=0, keepdims=True)


def hyena_filter(feat, w1, b1, w2, b2, w3, freq, rates, tl):
    L = feat.shape[0]
    C = w3.shape[1]
    return pl.pallas_call(
        _hy_filter_kernel,
        grid=(L // tl,),
        in_specs=[pl.BlockSpec((tl, HY_FEAT_PAD), lambda i: (i, 0)),
                  _const_spec((HY_FEAT_PAD, HY_HID)), _const_spec((1, HY_HID)),
                  _const_spec((HY_HID, HY_HID)), _const_spec((1, HY_HID)),
                  _const_spec((HY_HID, C)), _const_spec((2, HY_HID)), _const_spec((1, C))],
        out_specs=[pl.BlockSpec((tl, C), lambda i: (i, 0)), pl.BlockSpec((1, C), lambda i: (0, 0))],
        out_shape=[jax.ShapeDtypeStruct((L, C), F32), jax.ShapeDtypeStruct((1, C), F32)],
        compiler_params=_cparams("arbitrary"),
        name="hyena_filter",
    )(feat, w1, b1, w2, b2, w3, freq, rates)


def _filter_scale(ss_ref, o):
    e = ss_ref[:, 2 * o * BR_W:(2 * o + 1) * BR_W] + ss_ref[:, (2 * o + 1) * BR_W:(2 * o + 2) * BR_W]
    return lax.rsqrt(e + EPS)


def _combine_spectrum(x, ss_ref, o, half):
    xf = x[:, 2 * o * BR_W:(2 * o + 1) * BR_W]
    xb = x[:, (2 * o + 1) * BR_W:(2 * o + 2) * BR_W]
    sc = _filter_scale(ss_ref, o)
    hr = (xf[:half] + xb[:half]) * sc
    hi = (xf[half:] - xb[half:]) * sc
    return jnp.concatenate([hr, hi], axis=0)


def _cmul(x, h, half):
    xr, xi = x[:half], x[half:]
    hr, hi = h[:half], h[half:]
    return jnp.concatenate([xr * hr - xi * hi, xr * hi + xi * hr], axis=0)


def _dft1_kernel(z_ref, f_ref, a_ref):
    a_ref[0] = _dot(f_ref[...], z_ref[0].astype(BF16)).astype(BF16)


def dft_stage1(z, f1, tn):
    B, n1, W = z.shape
    M = f1.shape[0]
    return pl.pallas_call(
        _dft1_kernel,
        grid=(B, W // tn),
        in_specs=[pl.BlockSpec((1, n1, tn), lambda b, j: (b, 0, j)), _const_spec((M, n1))],
        out_specs=pl.BlockSpec((1, M, tn), lambda b, j: (b, 0, j)),
        out_shape=jax.ShapeDtypeStruct((B, M, W), BF16),
        compiler_params=_cparams("parallel", "parallel"),
        name="dft_stage1",
    )(z, f1)


def _spec2_kernel(a_ref, g_ref, ss_ref, h_ref):
    kb = g_ref.shape[0]
    n2 = a_ref.shape[3]
    for k in range(kb):
        a = a_ref[0, :, k].reshape(2 * n2, a_ref.shape[4])
        x = _dot(g_ref[k], a)
        for o in range(2):
            h_ref[o, k] = _combine_spectrum(x, ss_ref, o, n2)


def filter_spectrum(a, g, ss, kb):
    _, _, N1, N2, C = a.shape
    return pl.pallas_call(
        _spec2_kernel,
        grid=(N1 // kb,),
        in_specs=[pl.BlockSpec((1, 2, kb, N2, C), lambda i: (0, 0, i, 0, 0)),
                  pl.BlockSpec((kb, 2 * N2, 2 * N2), lambda i: (i, 0, 0)),
                  _const_spec((1, C))],
        out_specs=pl.BlockSpec((2, kb, 2 * N2, BR_W), lambda i: (0, i, 0, 0)),
        out_shape=jax.ShapeDtypeStruct((2, N1, 2 * N2, BR_W), F32),
        compiler_params=_cparams("parallel"),
        name="filter_spectrum",
    )(a, g, ss)


def _conv2_kernel(a_ref, g_ref, gi_ref, h_ref, o_ref):
    kb = g_ref.shape[0]
    n2 = a_ref.shape[3]
    C = a_ref.shape[4]
    for k in range(kb):
        a = a_ref[0, :, k].reshape(2 * n2, C)
        y = _cmul(_dot(g_ref[k], a), h_ref[0, k], n2).astype(BF16)
        o_ref[0, :, k] = _dot(gi_ref[k], y).astype(BF16).reshape(2, n2, C)


def spectral_multiply(a, g, gi, h, o, kb):
    B, _, N1, N2, C = a.shape
    blk = pl.BlockSpec((1, 2, kb, N2, C), lambda i, b: (b, 0, i, 0, 0))
    mat = pl.BlockSpec((kb, 2 * N2, 2 * N2), lambda i, b: (i, 0, 0))
    return pl.pallas_call(
        _conv2_kernel,
        grid=(N1 // kb, B),
        in_specs=[blk, mat, mat, pl.BlockSpec((1, kb, 2 * N2, C), lambda i, b: (o, i, 0, 0))],
        out_specs=blk,
        out_shape=jax.ShapeDtypeStruct(a.shape, BF16),
        compiler_params=_cparams("parallel", "parallel"),
        name="spectral_multiply",
    )(a, g, gi, h)


def _idft1_kernel(b_ref, f_ref, z_ref, gate_ref, bias_ref, o_ref):
    y = _dot(f_ref[...], b_ref[0])
    z = z_ref[0]
    o_ref[0] = gate_ref[0] * (y + bias_ref[...] * z)


def idft_stage1(bm, fi, z, gate, bias, tn):
    B, M, W = bm.shape
    n1 = fi.shape[0]
    blk = pl.BlockSpec((1, n1, tn), lambda b, j: (b, 0, j))
    return pl.pallas_call(
        _idft1_kernel,
        grid=(B, W // tn),
        in_specs=[pl.BlockSpec((1, M, tn), lambda b, j: (b, 0, j)), _const_spec((n1, M)),
                  blk, blk, _const_spec((1, tn))],
        out_specs=blk,
        out_shape=jax.ShapeDtypeStruct((B, n1, W), F32),
        compiler_params=_cparams("parallel", "parallel"),
        name="idft_stage1",
    )(bm, fi, z, gate, bias)


@functools.lru_cache(maxsize=None)
def _dft_tables(L):
    N = 2 * L
    N2 = 128
    N1 = N // N2
    nz = L // N2
    k1 = np.arange(N1)[:, None]
    n1 = np.arange(nz)[None, :]
    th = 2 * np.pi * ((k1 * n1) % N1) / N1
    f1 = np.concatenate([np.cos(th), -np.sin(th)], axis=0)
    fi = np.concatenate([np.cos(th).T, -np.sin(th).T], axis=1) / N
    kk1 = np.arange(N1)[:, None, None]
    k2 = np.arange(N2)[None, :, None]
    n2 = np.arange(N2)[None, None, :]
    ph = 2 * np.pi * ((n2 * k2 * N1 + n2 * kk1) % N) / N
    gr, gim = np.cos(ph), -np.sin(ph)
    g = np.concatenate([np.concatenate([gr, -gim], axis=2), np.concatenate([gim, gr], axis=2)], axis=1)
    hr, him = np.swapaxes(gr, 1, 2), -np.swapaxes(gim, 1, 2)
    gi = np.concatenate([np.concatenate([hr, -him], axis=2), np.concatenate([him, hr], axis=2)], axis=1)
    return tuple(np.asarray(t, np.float32) for t in (f1, fi, g, gi))


def _spec_direct_kernel(hf_ref, f_ref, ss_ref, h_ref):
    x = _dot(f_ref[...], hf_ref[...].astype(BF16))
    half = x.shape[0] // 2
    for o in range(2):
        h_ref[o] = _combine_spectrum(x, ss_ref, o, half)


def filter_spectrum_direct(hf, f, ss):
    L, C = hf.shape
    return pl.pallas_call(
        _spec_direct_kernel,
        grid=(1,),
        in_specs=[_const_spec((L, C)), _const_spec((4 * L, L)), _const_spec((1, C))],
        out_specs=pl.BlockSpec((2, 4 * L, BR_W), lambda i: (0, 0, 0)),
        out_shape=jax.ShapeDtypeStruct((2, 4 * L, BR_W), F32),
        compiler_params=_cparams("arbitrary"),
        name="filter_spectrum_direct",
    )(hf, f, ss)


def _conv_direct_kernel(z_ref, gate_ref, bias_ref, f_ref, fi_ref, h_ref, o_ref):
    z = z_ref[0]
    x = _dot(f_ref[...], z.astype(BF16))
    y = _cmul(x, h_ref[0], x.shape[0] // 2).astype(BF16)
    o_ref[0] = gate_ref[0] * (_dot(fi_ref[...], y) + bias_ref[...] * z)


def conv_direct(z, gate, bias, f, fi, h, o):
    B, L, C = z.shape
    blk = pl.BlockSpec((1, L, C), lambda b: (b, 0, 0))
    return pl.pallas_call(
        _conv_direct_kernel,
        grid=(B,),
        in_specs=[blk, blk, _const_spec((1, C)), _const_spec((4 * L, L)), _const_spec((L, 4 * L)),
                  pl.BlockSpec((1, 4 * L, C), lambda b: (o, 0, 0))],
        out_specs=blk,
        out_shape=jax.ShapeDtypeStruct((B, L, C), F32),
        compiler_params=_cparams("parallel"),
        name="conv_direct",
    )(z, gate, bias, f, fi, h)


@functools.lru_cache(maxsize=None)
def _dft_direct_tables(L):
    N = 2 * L
    k = np.arange(N)[:, None]
    n = np.arange(L)[None, :]
    th = 2 * np.pi * ((k * n) % N) / N
    f = np.concatenate([np.cos(th), -np.sin(th)], axis=0)
    fi = np.concatenate([np.cos(th).T, -np.sin(th).T], axis=1) / N
    return np.asarray(f, np.float32), np.asarray(fi, np.float32)


@functools.lru_cache(maxsize=None)
def _filter_features(L):
    t = np.arange(L, dtype=np.float32) / np.float32(L)
    ang = (2.0 * math.pi) * t[:, None].astype(np.float64) * np.arange(1, HY_BANDS + 1)
    feat = np.zeros((L, HY_FEAT_PAD), np.float32)
    feat[:, 0] = t
    feat[:, 1:1 + HY_BANDS] = np.sin(ang)
    feat[:, 1 + HY_BANDS:HY_FEAT] = np.cos(ang)
    rates = np.abs(np.linspace(math.log(HY_TARGET) / HY_SLOW, math.log(HY_TARGET) / HY_FAST, BR_W))
    return feat, np.tile(np.asarray(rates, np.float32), 4)[None, :]


def hyena_branch(v, g1, g2, hp):
    f_w1, f_b1, f_w2, f_b2, f_w3, freq, bias = hp
    B, L, C = v.shape
    feat, rates = _filter_features(L)
    hf, ss = hyena_filter(jnp.asarray(feat), f_w1, f_b1, f_w2, f_b2, f_w3, freq, jnp.asarray(rates),
                          min(L, 512))
    if L <= 512:
        f, fi = (jnp.asarray(t).astype(BF16) for t in _dft_direct_tables(L))
        spec = filter_spectrum_direct(hf, f, ss)
        z = v
        for o, gate in enumerate((g1, g2)):
            z = conv_direct(z, gate, bias[o:o + 1], f, fi, spec, o)
        return z
    f1, fi1, g, gi = (jnp.asarray(t).astype(BF16) for t in _dft_tables(L))
    N2 = 128
    N1 = 2 * L // N2
    nz = L // N2
    kb = 8
    a = dft_stage1(hf.reshape(1, nz, N2 * 4 * C), f1, 8192)
    spec = filter_spectrum(a.reshape(1, 2, N1, N2, 4 * C), g, ss, kb)
    W = N2 * C
    tn = 8192
    z = v.reshape(B, nz, W)
    for o, gate in enumerate((g1, g2)):
        a = dft_stage1(z, f1, tn).reshape(B, 2, N1, N2, C)
        bm = spectral_multiply(a, g, gi, spec, o, kb).reshape(B, 2 * N1, W)
        z = idft_stage1(bm, fi1, z, gate.reshape(B, nz, W), jnp.tile(bias[o:o + 1], (1, tn // C)), tn)
    return z.reshape(B, L, C)


def _head_sum(x, seg_ref):
    hi = x.astype(BF16)
    lo = (x - hi.astype(F32)).astype(BF16)
    return _dot(hi, seg_ref[...]) + _dot(lo, seg_ref[...])


def _rw_prep_kernel(p_ref, pp_ref, pn_ref, mu_ref, wl_ref, w0_ref, a0_ref, kv_ref, seg_ref,
                    r_ref, v_ref, na_ref, w_ref, kt_ref, b_ref, bonus_ref, g_ref, *, on_grid):
    i = pl.program_id(1)
    first = i == 0
    last = i == pl.num_programs(1) - 1
    x = p_ref[0]
    tm = x.shape[0]
    grp = lax.broadcasted_iota(jnp.int32, x.shape, 1) % 4
    prev, nxt = pp_ref[0], pn_ref[0]
    if on_grid:
        col = (lax.broadcasted_iota(jnp.int32, (tm, 1), 0) + i * tm) % GRID_W
        left = jnp.where(col == 0, 0.0, pltpu.roll(x, 1, 0))
        right = jnp.where(col == GRID_W - 1, 0.0, pltpu.roll(x, tm - 1, 0))
        up = _shift_rows(x, prev, nxt, GRID_W, first, last)
        down = _shift_rows(x, prev, nxt, -GRID_W, first, last)
        shifted = jnp.where(grp == 0, left, jnp.where(grp == 1, right, jnp.where(grp == 2, up, down)))
    else:
        before = _shift_rows(x, prev, nxt, 1, first, last)
        after = _shift_rows(x, prev, nxt, -1, first, last)
        shifted = jnp.where(grp % 2 == 0, before, after)
    xx = x + (shifted - x) * mu_ref[...]
    r = xx[:, 0:BR_W]
    k = xx[:, BR_W:2 * BR_W]
    v = xx[:, 2 * BR_W:3 * BR_W]
    lo = xx[:, 3 * BR_W:4 * BR_W]
    ll = lax.broadcasted_iota(jnp.int32, lo.shape, 1)
    act = jnp.where(ll < RW_LORA, jnp.tanh(lo), jnp.where(ll < 2 * RW_LORA, lo, _sigmoid(lo)))
    z = _dot(act.astype(BF16), wl_ref[...])
    kk = k * kv_ref[0:1, :]
    kk = kk * lax.rsqrt(_head_sum(kk * kk, seg_ref) + 1e-12)
    r_ref[0] = r
    v_ref[0] = v
    na_ref[0] = -kk
    bonus_ref[0] = _head_sum(r * k * kv_ref[2:3, :], seg_ref) * v
    g_ref[0] = z[:, 4 * BR_W:5 * BR_W]
    for d in range(2):
        logw = -_softplus(-(w0_ref[d:d + 1, :] + z[:, d * BR_W:(d + 1) * BR_W])) - 0.5
        w_ref[d, 0] = jnp.exp(-jnp.exp(logw))
        a = _sigmoid(a0_ref[d:d + 1, :] + z[:, (2 + d) * BR_W:(3 + d) * BR_W])
        kt_ref[d, 0] = k * (1.0 + (a - 1.0) * kv_ref[1:2, :])
        b_ref[d, 0] = kk * a


def rwkv_prep(p, mu, w_lora, w0, a0, kvec, seg, on_grid, tm):
    B, T, _ = p.shape
    halo = GRID_W if on_grid else 8
    main, prev, nxt = _halo_specs(tm, T, RW_PAD, lambda b, i: OFF_RW // RW_PAD, halo)
    one = pl.BlockSpec((1, tm, BR_W), lambda b, i: (b, i, 0))
    two = pl.BlockSpec((2, 1, tm, BR_W), lambda b, i: (0, b, i, 0))
    s1 = jax.ShapeDtypeStruct((B, T, BR_W), F32)
    s2 = jax.ShapeDtypeStruct((2, B, T, BR_W), F32)
    return pl.pallas_call(
        functools.partial(_rw_prep_kernel, on_grid=on_grid),
        grid=(B, T // tm),
        in_specs=[main, prev, nxt, _const_spec((1, RW_PAD)), _const_spec((BR_W, 5 * BR_W)),
                  _const_spec((2, BR_W)), _const_spec((2, BR_W)), _const_spec((3, BR_W)),
                  _const_spec((BR_W, BR_W))],
        out_specs=[one, one, one, two, two, two, one, one],
        out_shape=[s1, s1, s1, s2, s2, s2, s1, s1],
        compiler_params=_cparams("parallel", "parallel"),
        name="rwkv_prep",
    )(p, p, p, mu, w_lora, w0, a0, kvec, seg)


RW_VQ = RW_HEAD // 4
RW_VH = 2 * RW_VQ
LANE_W = 128
N_KEYED = 5
RW_UNROLL = 2
RW_PARTIALS = 1


def _chain_rows(refs, t, tb, shift):
    blocks = []
    for d in range(2):
        full = refs[d][t if d == 0 else tb]
        for half in range(2):
            x = full[:, half * LANE_W:(half + 1) * LANE_W]
            blocks.append(x if shift == 0 else pltpu.roll(x, LANE_W - shift, 1))
    return blocks


def _rw_scan_kernel(*refs, B, Tt):
    nin = 2 * (N_KEYED + 1)
    ins = refs[:nin]
    s0_ref = refs[nin]
    y_refs = refs[nin + 1:nin + 3]
    s_ref = refs[nin + 3]
    per_set = N_KEYED + 2
    sets = [refs[nin + 4 + per_set * u:nin + 4 + per_set * (u + 1)] for u in range(RW_UNROLL)]
    bufs = [(st[:N_KEYED], st[N_KEYED], st[N_KEYED + 1]) for st in sets]
    sa_ref = refs[nin + 4 + per_set * RW_UNROLL]
    NL = 16 * B
    A_TILE = 3

    @pl.when(pl.program_id(0) == 0)
    def _():
        s_ref[...] = s0_ref[...]
        bufs[RW_UNROLL - 1][2][...] = jnp.zeros_like(bufs[RW_UNROLL - 1][2])

    def keyed_job(t, slot, n):
        def issue():
            rows = _chain_rows(ins[2 * n:2 * n + 2], t, Tt - 1 - t, 0)
            return jnp.concatenate(rows * 4, axis=0).T

        def commit(val):
            bufs[slot][0][n][...] = val
        return issue, commit

    def values_job(t, slot):
        def issue():
            rows = []
            for vq in range(4):
                rows += _chain_rows(ins[2 * N_KEYED:], t, Tt - 1 - t, vq * RW_VQ)
            vt = jnp.concatenate(rows, axis=0).T
            return jnp.concatenate([vt[0:RW_VQ], vt[RW_HEAD:RW_HEAD + RW_VQ]], axis=0)

        def commit(val):
            bufs[slot][1][...] = val
        return issue, commit

    def output_job(t, slot):
        def issue():
            ys = bufs[slot][2][...]
            pad = jnp.zeros((RW_HEAD - RW_VQ, NL), F32)
            y = jnp.concatenate([ys[0:RW_VQ], pad, ys[RW_VQ:RW_VH], pad], axis=0).T
            out = []
            for d in range(2):
                halves = []
                for half in range(2):
                    acc = None
                    for vq in range(4):
                        row0 = ((vq * 2 + d) * 2 + half) * B
                        blk = y[row0:row0 + B, :]
                        blk = blk if vq == 0 else pltpu.roll(blk, vq * RW_VQ, 1)
                        acc = blk if acc is None else acc + blk
                    halves.append(acc)
                out.append(jnp.concatenate(halves, axis=1))
            return out

        def commit(val):
            y_refs[0][t] = val[0]
            y_refs[1][Tt - 1 - t] = val[1]
        return issue, commit

    groups = range(RW_VH // 8)
    others = tuple(n for n in range(N_KEYED) if n != A_TILE)
    never = pl.program_id(0) < 0

    def step_pieces(slot, sa_in, sa_out):
        tiles, vt_ref, ys_ref = bufs[slot]
        a_next = bufs[(slot + 1) % RW_UNROLL][0][A_TILE]
        row = lambda ref, k: ref[pl.ds(k, 1), :]
        ys = [[None] * RW_PARTIALS for _ in groups]
        san = [[None] * RW_PARTIALS for _ in groups]
        vt = {}
        acc = lambda lst, i, v: lst.__setitem__(i, v if lst[i] is None else lst[i] + v)

        def update(k0, k1):
            if not vt:
                for g in groups:
                    vt[g] = vt_ref[g * 8:(g + 1) * 8, :]
            for k in range(k0, k1):
                for h2 in range(2):
                    kk = h2 * RW_HEAD + k
                    r_k, w_k, kt_k, b_k = (row(tiles[n], kk) for n in (0, 1, 2, 4))
                    an_k = row(a_next, kk)
                    for g in (2 * h2, 2 * h2 + 1):
                        s = s_ref[g, k] * w_k + sa_in[g] * b_k + vt[g] * kt_k
                        s_ref[g, k] = s
                        acc(ys[g], k % RW_PARTIALS, s * r_k)
                        acc(san[g], k % RW_PARTIALS, s * an_k)
            if k1 == RW_HEAD:
                for g in groups:
                    ys_ref[g * 8:(g + 1) * 8, :] = functools.reduce(lambda a, b: a + b, ys[g])
                    sa_out.append(functools.reduce(lambda a, b: a + b, san[g]))

        def anchor(i, val):
            g = i % len(groups)
            if val is not None and ys[g][0] is not None:
                ys[g][0] = jnp.where(never, val, ys[g][0])

        cuts = [0, 9, 18, 27, 36, 45, 54, RW_HEAD]
        return [functools.partial(update, cuts[i], cuts[i + 1]) for i in range(len(cuts) - 1)], anchor

    def run(step, jobs):
        pieces, anchor = step
        for i, piece in enumerate(pieces):
            val = jobs[i][0]() if i < len(jobs) else None
            piece()
            if i < len(jobs):
                jobs[i][1](val)
                if i + 1 < len(pieces):
                    if not isinstance(val, list):
                        anchor(i, val[val.shape[0] - 8:, :])
                    elif val[1].shape[0] == 8 and NL == LANE_W:
                        anchor(i, val[1][:, LANE_W:])

    for issue, commit in ([keyed_job(0, 0, n) for n in range(N_KEYED)] + [values_job(0, 0)]
                          + [keyed_job(1, 1, A_TILE)]):
        commit(issue())
    first = [[None, None] for _ in groups]
    for k in range(RW_HEAD):
        for g in groups:
            p = s_ref[g, k] * bufs[0][0][A_TILE][pl.ds((g // 2) * RW_HEAD + k, 1), :]
            first[g][k % 2] = p if first[g][k % 2] is None else first[g][k % 2] + p
    for g in groups:
        sa_ref[g] = first[g][0] + first[g][1]

    def body(j, carry):
        t0 = RW_UNROLL * j
        sa = [sa_ref[g] for g in groups]
        for u in range(RW_UNROLL):
            t = t0 + u
            jobs = [output_job(jnp.maximum(t - 1, 0), (u - 1) % RW_UNROLL),
                    keyed_job(jnp.minimum(t + 2, Tt - 1), (u + 2) % RW_UNROLL, A_TILE)]
            t1 = jnp.minimum(t + 1, Tt - 1)
            jobs += [keyed_job(t1, (u + 1) % RW_UNROLL, n) for n in others] + [values_job(t1, (u + 1) % RW_UNROLL)]
            sa_next = []
            run(step_pieces(u, sa, sa_next), jobs)
            sa = sa_next
        for g in groups:
            sa_ref[g] = sa[g]
        return carry

    lax.fori_loop(0, Tt // RW_UNROLL, body, 0)
    issue, commit = output_job(Tt - 1, RW_UNROLL - 1)
    commit(issue())


def rwkv_mix(prep, s0, tt):
    r, v, na, w, kt, b = prep
    T, B, _ = r.shape
    nb = T // tt
    tblk = lambda i, d: i if d == 0 else nb - 1 - i

    in_specs, args = [], []
    for x, per_dir in ((r, False), (w, True), (kt, True), (na, False), (b, True), (v, False)):
        for d in range(2):
            if per_dir:
                in_specs.append(pl.BlockSpec((None, tt, B, BR_W), lambda i, d=d: (d, tblk(i, d), 0, 0)))
            else:
                in_specs.append(pl.BlockSpec((tt, B, BR_W), lambda i, d=d: (tblk(i, d), 0, 0)))
            args.append(x)
    NL = 16 * B
    sblk = pl.BlockSpec((RW_VH // 8, RW_HEAD, 8, NL), lambda i: (0, 0, 0, 0))
    yspecs = [pl.BlockSpec((tt, B, BR_W), lambda i, d=d: (tblk(i, d), 0, 0)) for d in range(2)]
    ysh = jax.ShapeDtypeStruct((T, B, BR_W), F32)
    out = pl.pallas_call(
        functools.partial(_rw_scan_kernel, B=B, Tt=tt),
        grid=(nb,),
        in_specs=in_specs + [sblk],
        out_specs=yspecs + [sblk],
        out_shape=[ysh] * 2 + [jax.ShapeDtypeStruct((RW_VH // 8, RW_HEAD, 8, NL), F32)],
        scratch_shapes=([pltpu.VMEM((LANE_W, NL), F32)] * N_KEYED + [pltpu.VMEM((RW_VH, NL), F32)] * 2) * RW_UNROLL
        + [pltpu.VMEM((RW_VH // 8, 8, NL), F32)],
        compiler_params=_cparams("arbitrary"),
        name="rwkv_scan",
    )(*args, s0)
    return out[0], out[1], out[2]


def _gelu_tanh(x):
    return 0.5 * x * (1.0 + jnp.tanh(math.sqrt(2.0 / math.pi) * (x + 0.044715 * (x * x * x))))


def _lru_prep_kernel(p_ref, pp_ref, pn_ref, cw_ref, cb_ref, w_ref, bias_ref, lam_ref, a_ref, b_ref, gg_ref):
    i = pl.program_id(1)
    first = i == 0
    last = i == pl.num_programs(1) - 1
    x = p_ref[0][:, 0:BR_W]
    prev = pp_ref[0][:, 0:BR_W]
    nxt = pn_ref[0][:, 0:BR_W]
    xc = cb_ref[...] + cw_ref[0:1, :] * _shift_rows(x, prev, nxt, 2, first, last)
    xc = xc + cw_ref[1:2, :] * _shift_rows(x, prev, nxt, 1, first, last)
    xc = xc + cw_ref[2:3, :] * x
    xc = xc + cw_ref[3:4, :] * _shift_rows(x, prev, nxt, -1, first, last)
    z = _dot(xc.astype(BF16), w_ref[...])
    for d in range(2):
        r = _sigmoid(z[:, 2 * d * BR_W:(2 * d + 1) * BR_W] + bias_ref[2 * d:2 * d + 1, :])
        gi = _sigmoid(z[:, (2 * d + 1) * BR_W:(2 * d + 2) * BR_W] + bias_ref[2 * d + 1:2 * d + 2, :])
        log_a = -LRU_C * r * _softplus(-lam_ref[d:d + 1, :])
        a_ref[d, 0] = jnp.exp(log_a)
        b_ref[d, 0] = jnp.sqrt(1.0 - jnp.exp(2.0 * log_a)) * (gi * xc)
    gg_ref[0] = _gelu_tanh(p_ref[0][:, BR_W:2 * BR_W])


def lru_prep(p, conv_w, conv_b, w_blk, bias, lam, tm):
    B, T, _ = p.shape
    main, prev, nxt = _halo_specs(tm, T, LRU_IN, lambda b, i: OFF_LRU // LRU_IN, 8)
    two = pl.BlockSpec((2, 1, tm, BR_W), lambda b, i: (0, b, i, 0))
    s2 = jax.ShapeDtypeStruct((2, B, T, BR_W), F32)
    return pl.pallas_call(
        _lru_prep_kernel,
        grid=(B, T // tm),
        in_specs=[main, prev, nxt, _const_spec((4, BR_W)), _const_spec((1, BR_W)),
                  _const_spec((BR_W, 4 * BR_W)), _const_spec((4, BR_W)), _const_spec((2, BR_W))],
        out_specs=[two, two, pl.BlockSpec((1, tm, BR_W), lambda b, i: (b, i, 0))],
        out_shape=[s2, s2, jax.ShapeDtypeStruct((B, T, BR_W), F32)],
        compiler_params=_cparams("parallel", "parallel"),
        name="lru_prep",
    )(p, p, p, conv_w, conv_b, w_blk, bias, lam)


def _affine_scan(a, b, reverse):
    tb = a.shape[0]
    row = lax.broadcasted_iota(jnp.int32, (tb, 1), 0)
    s = 1
    while s < tb:
        sh = tb - s if reverse else s
        ok = (row < tb - s) if reverse else (row >= s)
        a_s = pltpu.roll(a, sh, 0)
        b_s = pltpu.roll(b, sh, 0)
        b = jnp.where(ok, a * b_s + b, b)
        a = jnp.where(ok, a * a_s, a)
        s *= 2
    return a, b


def _lru_scan_kernel(af_ref, bf_ref, ab_ref, bb_ref, h0_ref, hf_ref, hb_ref, fin_ref):
    @pl.when(pl.program_id(1) == 0)
    def _():
        fin_ref[...] = h0_ref[...]

    tb = af_ref.shape[2]
    a, b = _affine_scan(af_ref[0, 0], bf_ref[0, 0], False)
    h = b + a * fin_ref[0, 0]
    hf_ref[0] = h
    fin_ref[0, 0] = h[tb - 1:tb, :]
    a, b = _affine_scan(ab_ref[0, 0], bb_ref[0, 0], True)
    h = b + a * fin_ref[1, 0]
    hb_ref[0] = h
    fin_ref[1, 0] = h[0:1, :]


def lru_scan(a, b, h0, tb):
    _, B, T, C = a.shape
    nb = T // tb
    fwd = pl.BlockSpec((1, 1, tb, C), lambda bi, i: (0, bi, i, 0))
    bwd = pl.BlockSpec((1, 1, tb, C), lambda bi, i: (1, bi, nb - 1 - i, 0))
    st = pl.BlockSpec((2, 1, 1, C), lambda bi, i: (0, bi, 0, 0))
    return pl.pallas_call(
        _lru_scan_kernel,
        grid=(B, nb),
        in_specs=[fwd, fwd, bwd, bwd, st],
        out_specs=[pl.BlockSpec((1, tb, C), lambda bi, i: (bi, i, 0)),
                   pl.BlockSpec((1, tb, C), lambda bi, i: (bi, nb - 1 - i, 0)), st],
        out_shape=[jax.ShapeDtypeStruct((B, T, C), F32), jax.ShapeDtypeStruct((B, T, C), F32),
                   jax.ShapeDtypeStruct((2, B, 1, C), F32)],
        compiler_params=_cparams("parallel", "arbitrary"),
        name="lru_scan",
    )(a, b, a, b, h0)


def _rope(x, cos, sin):
    q4 = RET_HEAD // 4
    lane = lax.broadcasted_iota(jnp.int32, x.shape, 1) % (2 * q4)
    partner = jnp.where(lane < q4, pltpu.roll(x, x.shape[1] - q4, 1), pltpu.roll(x, q4, 1))
    return x * cos + partner * sin


def _ret_dir(x, cos, sin, s, glane, gtile_ref, d, reverse):
    C = x.shape[0]
    q = x[:, 0:BR_W]
    k = x[:, BR_W:2 * BR_W]
    v = x[:, 2 * BR_W:3 * BR_W].astype(BF16)
    if cos is not None:
        q = _rope(q, cos, sin)
        k = _rope(k, cos, sin)
    k = k * (RET_HEAD ** -0.5)
    lg = -_softplus(-glane)
    idx = lax.broadcasted_iota(jnp.int32, (C, 1), 0).astype(F32)
    steps_in = (C - idx) if reverse else (idx + 1.0)
    steps_out = idx if reverse else (C - 1.0 - idx)
    ri = lax.broadcasted_iota(jnp.int32, (C, C), 0)
    ci = lax.broadcasted_iota(jnp.int32, (C, C), 1)
    diff = ((ci - ri) if reverse else (ri - ci)).astype(F32)
    lane_head = lax.broadcasted_iota(jnp.int32, (1, BR_W), 1) // RET_HEAD
    qb = q.astype(BF16)
    kb = k.astype(BF16)
    y = _dot(qb, s.astype(BF16)) * jnp.exp(steps_in * lg)
    for h in range(RET_HEADS):
        lg_h = -_softplus(-gtile_ref[d, h][0:1, :])
        dm = jnp.where(diff >= 0, jnp.exp(diff * lg_h), 0.0)
        mh = lane_head == h
        sc = lax.dot_general(jnp.where(mh, qb, jnp.zeros_like(qb)), kb, (((1,), (1,)), ((), ())),
                             preferred_element_type=F32)
        y = y + jnp.where(mh, _dot((sc * dm).astype(BF16), v), 0.0)
    kd = (k * jnp.exp(steps_out * lg)).astype(BF16)
    upd = lax.dot_general(kd, v, (((0,), (0,)), ((), ())), preferred_element_type=F32)
    rh = lax.broadcasted_iota(jnp.int32, (BR_W, BR_W), 0) // RET_HEAD
    ch = lax.broadcasted_iota(jnp.int32, (BR_W, BR_W), 1) // RET_HEAD
    s = s * jnp.exp(C * lg) + jnp.where(rh == ch, upd, 0.0)
    return y, s


def _ret_kernel(*refs, rope):
    if rope:
        xf_ref, xb_ref, cf_ref, sf_ref, cb_ref, sb_ref, gl_ref, gt_ref, s0_ref, yf_ref, yb_ref, s_ref = refs
    else:
        xf_ref, xb_ref, gl_ref, gt_ref, s0_ref, yf_ref, yb_ref, s_ref = refs

    @pl.when(pl.program_id(1) == 0)
    def _():
        s_ref[...] = s0_ref[...]

    y, s = _ret_dir(xf_ref[0], cf_ref[...] if rope else None, sf_ref[...] if rope else None,
                    s_ref[0, 0], gl_ref[0:1, :], gt_ref, 0, False)
    yf_ref[0] = y
    s_ref[0, 0] = s
    y, s = _ret_dir(xb_ref[0], cb_ref[...] if rope else None, sb_ref[...] if rope else None,
                    s_ref[1, 0], gl_ref[1:2, :], gt_ref, 1, True)
    yb_ref[0] = y
    s_ref[1, 0] = s


def retention(p, cos, sin, glane, gtile, s0, rope):
    B, T, _ = p.shape
    C = RET_CHUNK
    nc = T // C
    cb = OFF_RET // RET_IN
    xf = pl.BlockSpec((1, C, RET_IN), lambda b, i: (b, i, cb))
    xb = pl.BlockSpec((1, C, RET_IN), lambda b, i: (b, nc - 1 - i, cb))
    tf = pl.BlockSpec((C, BR_W), lambda b, i: (i, 0))
    tb = pl.BlockSpec((C, BR_W), lambda b, i: (nc - 1 - i, 0))
    st = pl.BlockSpec((2, 1, BR_W, BR_W), lambda b, i: (0, b, 0, 0))
    ins = [xf, xb] + ([tf, tf, tb, tb] if rope else []) + [
        _const_spec((2, BR_W)), _const_spec((2, RET_HEADS, 8, C)), st]
    args = [p, p] + ([cos, sin, cos, sin] if rope else []) + [glane, gtile, s0]
    return pl.pallas_call(
        functools.partial(_ret_kernel, rope=rope),
        grid=(B, nc),
        in_specs=ins,
        out_specs=[pl.BlockSpec((1, C, BR_W), lambda b, i: (b, i, 0)),
                   pl.BlockSpec((1, C, BR_W), lambda b, i: (b, nc - 1 - i, 0)), st],
        out_shape=[jax.ShapeDtypeStruct((B, T, BR_W), F32), jax.ShapeDtypeStruct((B, T, BR_W), F32),
                   jax.ShapeDtypeStruct((2, B, BR_W, BR_W), F32)],
        compiler_params=_cparams("parallel", "arbitrary"),
        name="retention",
    )(*args)


@functools.lru_cache(maxsize=None)
def _rope_tables(T):
    pos = np.arange(T)
    q4 = RET_HEAD // 4
    inv = ROPE_BASE ** (-np.arange(q4, dtype=np.float64) / q4)
    cos = np.zeros((T, RET_HEAD))
    sin = np.zeros((T, RET_HEAD))
    for part, coord in enumerate((pos // GRID_W, pos % GRID_W)):
        ang = coord[:, None] * inv
        base = part * 2 * q4
        cos[:, base:base + q4] = np.cos(ang)
        cos[:, base + q4:base + 2 * q4] = np.cos(ang)
        sin[:, base:base + q4] = -np.sin(ang)
        sin[:, base + q4:base + 2 * q4] = np.sin(ang)
    tile = lambda t: np.asarray(np.tile(t, (1, RET_HEADS)), np.float32)
    return tile(cos), tile(sin)


def _head_norm(y, seg_ref, eps):
    mu = _head_sum(y, seg_ref) * (1.0 / RW_HEAD)
    yc = y - mu
    var = _head_sum(yc * yc, seg_ref) * (1.0 / RW_HEAD)
    return yc * lax.rsqrt(var + eps)


def _merge_kernel(x_ref, g_ref, sh_ref, sc_ref, gt_ref, hy_ref, ryf_ref, ryb_ref, rbon_ref, rg_ref,
                  lhf_ref, lhb_ref, lgg_ref, tyf_ref, tyb_ref, tg_ref, lng_ref, seg_ref,
                  wg_ref, br_ref, wo_ref, o_ref, m_ref):
    x = x_ref[0]
    u = _norm_mod(x, g_ref[...], sh_ref[0], sc_ref[0]).astype(BF16)
    y_rw = (_head_norm(ryf_ref[0] + ryb_ref[0], seg_ref, RW_LN_EPS) * lng_ref[...] + rbon_ref[0]) * rg_ref[0]
    y_lru = (lhf_ref[0] + lhb_ref[0]) * lgg_ref[0]
    y_ret = _head_norm(tyf_ref[0] + tyb_ref[0], seg_ref, RET_LN_EPS) * _silu(tg_ref[0])
    ys = [y.astype(BF16) for y in (hy_ref[0], y_rw, y_lru, y_ret)]
    D = x.shape[1]
    cw = 256
    for c in range(D // cw):
        acc = None
        for n in range(N_BRANCH):
            gate = _sigmoid(_dot(u, wg_ref[:, n * D + c * cw:n * D + (c + 1) * cw]))
            t = gate * _dot(ys[n], br_ref[n, :, c * cw:(c + 1) * cw])
            acc = t if acc is None else acc + t
        m_ref[:, c * cw:(c + 1) * cw] = acc.astype(BF16)
    o_ref[0] = x + gt_ref[0] * _dot(m_ref[...], wo_ref[...])


def merge(x, g, sh, sc, gt, p, y_hy, rw, lru, ret, ln_g, seg, w_gate, br, w_out, tm):
    B, T, D = x.shape
    vec = pl.BlockSpec((1, 1, D), lambda b, i: (b, 0, 0))
    row = pl.BlockSpec((1, tm, D), lambda b, i: (b, i, 0))
    brn = pl.BlockSpec((1, tm, BR_W), lambda b, i: (b, i, 0))
    tg = pl.BlockSpec((1, tm, BR_W), lambda b, i: (b, i, (OFF_RET + 3 * BR_W) // BR_W))
    return pl.pallas_call(
        _merge_kernel,
        grid=(B, T // tm),
        in_specs=[row, _const_spec((1, D)), vec, vec, vec] + [brn] * 10 + [tg] + [
            _const_spec((1, BR_W)), _const_spec((BR_W, BR_W)), _const_spec((D, GATE_IN)),
            _const_spec((N_BRANCH, BR_W, D)), _const_spec((D, D))],
        out_specs=row,
        out_shape=jax.ShapeDtypeStruct((B, T, D), F32),
        scratch_shapes=[pltpu.VMEM((tm, D), BF16)],
        compiler_params=_cparams("parallel", "parallel"),
        name="merge",
    )(x, g, sh, sc, gt, y_hy, *rw, *lru, *ret, p, ln_g, seg, w_gate, br, w_out)


def _block_diag(w):
    G = w.shape[-3]
    eye = jnp.eye(G, dtype=w.dtype)
    full = w[..., :, :, None, :] * eye[:, None, :, None]
    return full.reshape(*w.shape[:-3], G * w.shape[-2], G * w.shape[-1])


def _mixers(p, lp, states, on_grid, with_output, tiles):
    B, T, _ = p.shape
    tm, tt, _ = tiles
    r, v, na, w, kt, b, bonus, g = rwkv_prep(p, *lp['rw'], on_grid, tm)
    time_major = lambda t: jnp.swapaxes(t, -3, -2)
    y_f, y_b, rw_fin = rwkv_mix(tuple(time_major(t) for t in (r, v, na, w, kt, b)), states[0], tt)
    y_f, y_b = time_major(y_f), time_major(y_b)
    a, bb, gg = lru_prep(p, *lp['lru'], tm)
    h_f, h_b, lru_fin = lru_scan(a, bb, states[1], tm)
    cos, sin = (jnp.asarray(t) for t in _rope_tables(T)) if on_grid else (None, None)
    t_f, t_b, ret_fin = retention(p, cos, sin, *lp['ret'], states[2], on_grid)
    fins = (rw_fin, lru_fin, ret_fin)
    if not with_output:
        return None, fins
    vg = hyena_prep(p, *lp['hy_conv'], min(T, 2048))
    y_hy = hyena_branch(vg[0], vg[1], vg[2], lp['hy'])
    return (y_hy, (y_f, y_b, bonus, g), (h_f, h_b, gg), (t_f, t_b)), fins


def kernel(x, c, ctx, c_ctx, w_mod, b_mod, norm1_g, norm2_g, w_in, hy_conv_w, hy_conv_b, hy_f_w1, hy_f_b1, hy_f_w2, hy_f_b2, hy_f_w3, hy_freq, hy_bias, rw_mu, rw_w0, rw_w2, rw_a0, rw_a2, rw_g2, rw_kk, rw_ka, rw_rk, rw_ln_g, lru_conv_w, lru_conv_b, lru_wa, lru_ba, lru_wx, lru_bx, lru_lam, ret_gamma, br_proj, w_out, ffn_w1, ffn_w2, final_g):
    B, T, D = x.shape
    TC = ctx.shape[1]
    L = w_in.shape[0]

    s0, s1, s2, s3 = HY_IN, HY_IN + RW_IN, HY_IN + RW_IN + LRU_IN, HY_IN + RW_IN + LRU_IN + RET_IN
    w_branch = jnp.concatenate([w_in[:, :, s0:s1], jnp.zeros((L, D, RW_PAD - RW_IN), w_in.dtype),
                                w_in[:, :, s2:s3], jnp.zeros((L, D, HY_GAP), w_in.dtype),
                                w_in[:, :, 0:s0], w_in[:, :, s1:s2]], axis=2).astype(BF16)
    w_gate = w_in[:, :, s3:].astype(BF16)
    mu = jnp.pad(rw_mu, ((0, 0), (0, RW_PAD - RW_IN)))[:, None, :]
    w_lora = jnp.zeros((L, BR_W, 5 * BR_W), F32)
    w_lora = w_lora.at[:, 0:RW_LORA, 0:BR_W].set(rw_w2[:, 0]).at[:, 0:RW_LORA, BR_W:2 * BR_W].set(rw_w2[:, 1])
    w_lora = w_lora.at[:, RW_LORA:2 * RW_LORA, 2 * BR_W:3 * BR_W].set(rw_a2[:, 0])
    w_lora = w_lora.at[:, RW_LORA:2 * RW_LORA, 3 * BR_W:4 * BR_W].set(rw_a2[:, 1])
    w_lora = w_lora.at[:, 2 * RW_LORA:3 * RW_LORA, 4 * BR_W:5 * BR_W].set(rw_g2).astype(BF16)
    kvec = jnp.stack([rw_kk, rw_ka, rw_rk], axis=1)
    seg = jnp.asarray(np.kron(np.eye(RW_HEADS), np.ones((RW_HEAD, RW_HEAD))), BF16)
    lru_w = jnp.concatenate([_block_diag(lru_wa[:, 0]), _block_diag(lru_wx[:, 0]),
                             _block_diag(lru_wa[:, 1]), _block_diag(lru_wx[:, 1])], axis=2).astype(BF16)
    lru_bias = jnp.stack([lru_ba[:, 0], lru_bx[:, 0], lru_ba[:, 1], lru_bx[:, 1]], axis=1)
    glane = jnp.repeat(ret_gamma, RET_HEAD, axis=2)
    gtile = jnp.broadcast_to(ret_gamma[:, :, :, None, None], (L, 2, RET_HEADS, 8, RET_CHUNK))
    f_w1 = jnp.pad(hy_f_w1, ((0, 0), (0, HY_FEAT_PAD - HY_FEAT), (0, 0)))
    br_b = br_proj.astype(BF16)
    w_out_b = w_out.astype(BF16)
    ffn_gate = ffn_w1[:, :, :D_FF].astype(BF16)
    ffn_up_w = ffn_w1[:, :, D_FF:].astype(BF16)
    ffn_w2_b = ffn_w2.astype(BF16)

    cc = jnp.concatenate([c, c_ctx[None, :], jnp.zeros((16 - B - 1, D), F32)], axis=0)
    mods = modulation(cc, w_mod.astype(BF16), b_mod[:, None, :])

    zero_states = (jnp.zeros((RW_VH // 8, RW_HEAD, 8, 16 * B), F32), jnp.zeros((2, B, 1, BR_W), F32),
                   jnp.zeros((2, B, BR_W, BR_W), F32))
    xc = ctx
    for l in range(L):
        last = l == L - 1
        lp = {
            'rw': (mu[l], w_lora[l], rw_w0[l], rw_a0[l], kvec[l], seg),
            'lru': (lru_conv_w[l], lru_conv_b[l][None, :], lru_w[l], lru_bias[l], lru_lam[l]),
            'ret': (glane[l], gtile[l]),
            'hy_conv': (hy_conv_w[l], hy_conv_b[l][None, :]),
            'hy': (f_w1[l], hy_f_b1[l][None, :], hy_f_w2[l], hy_f_b2[l][None, :], hy_f_w3[l], hy_freq[l],
                   hy_bias[l]),
        }
        g1 = norm1_g[l][None, :]
        g2 = norm2_g[l][None, :]
        m_lat = [m[:, None, :] for m in jnp.split(mods[l, :B], 6, axis=-1)]
        m_ctx = [jnp.broadcast_to(m[None, :, :], (B, 1, D)) for m in jnp.split(mods[l, B:B + 1], 6, axis=-1)]
        ln_g = rw_ln_g[l][None, :]

        def layer(xs, m, states, on_grid, with_output, tiles, final):
            p = in_projection(xs, g1, m[0], m[1], w_branch[l], tiles[0])
            br, fins = _mixers(p, lp, states, on_grid, with_output, tiles)
            if not with_output:
                return None, fins
            xs = merge(xs, g1, m[0], m[1], m[2], p, br[0], br[1], br[2], br[3], ln_g, seg,
                       w_gate[l], br_b[l], w_out_b[l], tiles[2])
            h = ffn_up(xs, g2, m[3], m[4], ffn_gate[l], ffn_up_w[l], tiles[0])
            xs = ffn_down(h, xs, m[5], ffn_w2_b[l], final_g[None, :], final, tiles[0])
            return xs, fins

        xc_new, ctx_states = layer(xc, m_ctx, zero_states, False, not last, (TC, 64, TC), False)
        x, _ = layer(x, m_lat, ctx_states, True, True, (512, 64, 512), last)
        if not last:
            xc = xc_new
    return x
```

```python
import functools
import math

import numpy as np
import jax
import jax.numpy as jnp
from jax import lax
from jax.experimental import pallas as pl
from jax.experimental.pallas import tpu as pltpu

F32 = jnp.float32
BF16 = jnp.bfloat16

D_MODEL = 1024
DEPTH = 4
GRID_W = 64
N_BRANCH = 4
BR_W = D_MODEL // N_BRANCH

HY_BANDS = 8
HY_FEAT = 1 + 2 * HY_BANDS
HY_FEAT_PAD = 32
HY_HID = 64
HY_TARGET = 1e-2
HY_FAST = 0.3
HY_SLOW = 1.5
HY_IN = 3 * BR_W

RW_HEAD = 64
RW_HEADS = BR_W // RW_HEAD
RW_LORA = 64
RW_IN = 3 * BR_W + 3 * RW_LORA
RW_PAD = 4 * BR_W
RW_LN_EPS = 64e-5

LRU_BLOCKS = 4
LRU_BLOCK = BR_W // LRU_BLOCKS
LRU_C = 8.0
LRU_IN = 2 * BR_W

RET_HEADS = 4
RET_HEAD = BR_W // RET_HEADS
RET_CHUNK = 128
ROPE_BASE = 10000.0
RET_IN = 4 * BR_W
RET_LN_EPS = 1e-5

GATE_IN = N_BRANCH * D_MODEL
D_FF = ((8 * D_MODEL // 3 + 255) // 256) * 256
EPS = 1e-6

OFF_RW = 0
OFF_RET = OFF_RW + RW_PAD
OFF_HY = 3 * HY_IN
OFF_LRU = OFF_HY + HY_IN
N_BR = OFF_LRU + LRU_IN
HY_GAP = OFF_HY - (OFF_RET + RET_IN)

VMEM_LIMIT = 56 * 1024 * 1024
HI = lax.Precision.HIGHEST


def _cparams(*sem):
    return pltpu.CompilerParams(dimension_semantics=sem, vmem_limit_bytes=VMEM_LIMIT)


def _const_spec(shape):
    nd = len(shape)
    return pl.BlockSpec(shape, lambda *_: (0,) * nd, pipeline_mode=pl.Buffered(1))


def _dot(a, b, **kw):
    return jnp.dot(a, b, preferred_element_type=F32, **kw)


def _norm_mod(x, g, sh, sc):
    ms = jnp.mean(x * x, axis=-1, keepdims=True)
    return x * lax.rsqrt(ms + EPS) * g * (1.0 + sc) + sh


def _sigmoid(x):
    return 1.0 / (1.0 + jnp.exp(-x))


def _silu(x):
    return x * _sigmoid(x)


def _softplus(x):
    return jnp.maximum(x, 0.0) + jnp.log(1.0 + jnp.exp(-jnp.abs(x)))


def _mod_kernel(c_ref, w_ref, b_ref, o_ref):
    c = c_ref[...]
    o_ref[0] = _dot(_silu(c).astype(BF16), w_ref[0]) + b_ref[0]


def modulation(cc, w_mod, b_mod):
    L, D, N = w_mod.shape
    tn = 1536
    return pl.pallas_call(
        _mod_kernel,
        grid=(L, N // tn),
        in_specs=[pl.BlockSpec((16, D), lambda l, j: (0, 0)),
                  pl.BlockSpec((1, D, tn), lambda l, j: (l, 0, j)),
                  pl.BlockSpec((1, 1, tn), lambda l, j: (l, 0, j))],
        out_specs=pl.BlockSpec((1, 16, tn), lambda l, j: (l, 0, j)),
        out_shape=jax.ShapeDtypeStruct((L, 16, N), F32),
        compiler_params=_cparams("parallel", "parallel"),
        name="modulation",
    )(cc, w_mod, b_mod)


def _inproj_kernel(x_ref, g_ref, sh_ref, sc_ref, w_ref, o_ref):
    u = _norm_mod(x_ref[0], g_ref[...], sh_ref[0], sc_ref[0]).astype(BF16)
    n = w_ref.shape[1]
    cw = 256
    for c in range(n // cw):
        o_ref[0, :, c * cw:(c + 1) * cw] = _dot(u, w_ref[:, c * cw:(c + 1) * cw])


def in_projection(x, g, sh, sc, w, tm):
    B, T, D = x.shape
    N = w.shape[1]
    vec = pl.BlockSpec((1, 1, D), lambda b, i: (b, 0, 0))
    return pl.pallas_call(
        _inproj_kernel,
        grid=(B, T // tm),
        in_specs=[pl.BlockSpec((1, tm, D), lambda b, i: (b, i, 0)),
                  _const_spec((1, D)), vec, vec, _const_spec((D, N))],
        out_specs=pl.BlockSpec((1, tm, N), lambda b, i: (b, i, 0)),
        out_shape=jax.ShapeDtypeStruct((B, T, N), F32),
        compiler_params=_cparams("parallel", "parallel"),
        name="in_projection",
    )(x, g, sh, sc, w)


def _ffn1_kernel(x_ref, g_ref, sh_ref, sc_ref, wg_ref, wu_ref, o_ref):
    u = _norm_mod(x_ref[0], g_ref[...], sh_ref[0], sc_ref[0]).astype(BF16)
    n = wg_ref.shape[1]
    cw = 256
    for c in range(n // cw):
        sl = slice(c * cw, (c + 1) * cw)
        gate = _dot(u, wg_ref[:, sl])
        up = _dot(u, wu_ref[:, sl])
        o_ref[0, :, sl] = (_silu(gate) * up).astype(BF16)


def ffn_up(x, g, sh, sc, w_gate, w_up, tm):
    B, T, D = x.shape
    N = w_gate.shape[1]
    vec = pl.BlockSpec((1, 1, D), lambda b, i: (b, 0, 0))
    return pl.pallas_call(
        _ffn1_kernel,
        grid=(B, T // tm),
        in_specs=[pl.BlockSpec((1, tm, D), lambda b, i: (b, i, 0)),
                  _const_spec((1, D)), vec, vec, _const_spec((D, N)), _const_spec((D, N))],
        out_specs=pl.BlockSpec((1, tm, N), lambda b, i: (b, i, 0)),
        out_shape=jax.ShapeDtypeStruct((B, T, N), BF16),
        compiler_params=_cparams("parallel", "parallel"),
        name="ffn_up",
    )(x, g, sh, sc, w_gate, w_up)


def _ffn2_kernel(h_ref, x_ref, gt_ref, w_ref, fg_ref, o_ref, *, final_norm):
    y = x_ref[0] + gt_ref[0] * _dot(h_ref[0], w_ref[...])
    if final_norm:
        ms = jnp.mean(y * y, axis=-1, keepdims=True)
        y = y * lax.rsqrt(ms + EPS) * fg_ref[...]
    o_ref[0] = y


def ffn_down(h, x, gate, w, final_g, final_norm, tm):
    B, T, D = x.shape
    N = h.shape[2]
    return pl.pallas_call(
        functools.partial(_ffn2_kernel, final_norm=final_norm),
        grid=(B, T // tm),
        in_specs=[pl.BlockSpec((1, tm, N), lambda b, i: (b, i, 0)),
                  pl.BlockSpec((1, tm, D), lambda b, i: (b, i, 0)),
                  pl.BlockSpec((1, 1, D), lambda b, i: (b, 0, 0)),
                  _const_spec((N, D)), _const_spec((1, D))],
        out_specs=pl.BlockSpec((1, tm, D), lambda b, i: (b, i, 0)),
        out_shape=jax.ShapeDtypeStruct((B, T, D), F32),
        compiler_params=_cparams("parallel", "parallel"),
        name="ffn_down",
    )(h, x, gate, w, final_g)


def _halo_specs(tm, T, width, col_fn, halo):
    r = tm // halo
    last = T // halo - 1
    main = pl.BlockSpec((1, tm, width), lambda *g: (g[0], g[1], col_fn(*g)))
    prev = pl.BlockSpec((1, halo, width), lambda *g: (g[0], jnp.maximum(g[1] * r - 1, 0), col_fn(*g)))
    nxt = pl.BlockSpec((1, halo, width), lambda *g: (g[0], jnp.minimum((g[1] + 1) * r, last), col_fn(*g)))
    return main, prev, nxt


def _shift_rows(x, prev, nxt, s, first, last):
    tm = x.shape[0]
    if s > 0:
        head = jnp.where(first, 0.0, prev[prev.shape[0] - s:, :])
        return jnp.concatenate([head, x[:tm - s, :]], axis=0)
    s = -s
    tail = jnp.where(last, 0.0, nxt[:s, :])
    return jnp.concatenate([x[s:, :], tail], axis=0)


def _hy_prep_kernel(p_ref, pp_ref, pn_ref, w_ref, b_ref, v_ref, g1_ref, g2_ref):
    i = pl.program_id(1)
    first = i == 0
    last = i == pl.num_programs(1) - 1
    x = p_ref[0]
    xm = _shift_rows(x, pp_ref[0], pn_ref[0], 1, first, last)
    xp = _shift_rows(x, pp_ref[0], pn_ref[0], -1, first, last)
    u = b_ref[...] + w_ref[0:1, :] * xm + w_ref[1:2, :] * x + w_ref[2:3, :] * xp
    v_ref[0] = u[:, 0:BR_W]
    g1_ref[0] = u[:, BR_W:2 * BR_W]
    g2_ref[0] = u[:, 2 * BR_W:3 * BR_W]


def hyena_prep(p, conv_w, conv_b, tm):
    B, T, _ = p.shape
    main, prev, nxt = _halo_specs(tm, T, HY_IN, lambda b, i: OFF_HY // HY_IN, 8)
    out = pl.BlockSpec((1, tm, BR_W), lambda b, i: (b, i, 0))
    shp = jax.ShapeDtypeStruct((B, T, BR_W), F32)
    return pl.pallas_call(
        _hy_prep_kernel,
        grid=(B, T // tm),
        in_specs=[main, prev, nxt, _const_spec((3, HY_IN)), _const_spec((1, HY_IN))],
        out_specs=[out, out, out],
        out_shape=[shp, shp, shp],
        compiler_params=_cparams("parallel", "parallel"),
        name="hyena_prep",
    )(p, p, p, conv_w, conv_b)


def _hy_filter_kernel(feat_ref, w1_ref, b1_ref, w2_ref, b2_ref, w3_ref, fq_ref, rates_ref, h_ref, ss_ref):
    i = pl.program_id(0)
    feat = feat_ref[...]
    t = feat[:, 0:1]
    h = jnp.sin(fq_ref[0:1, :] * (_dot(feat, w1_ref[...], precision=HI) + b1_ref[...]))
    h = jnp.sin(fq_ref[1:2, :] * (_dot(h, w2_ref[...], precision=HI) + b2_ref[...]))
    h = _dot(h, w3_ref[...], precision=HI) * jnp.exp(-t * rates_ref[...])
    row = lax.broadcasted_iota(jnp.int32, h.shape, 0) + i * h.shape[0]
    col = lax.broadcasted_iota(jnp.int32, h.shape, 1)
    h = jnp.where((row == 0) & ((col // BR_W) % 2 == 1), 0.0, h)
    h_ref[...] = h

    @pl.when(i == 0)
    def _():
        ss_ref[...] = jnp.zeros_like(ss_ref)

    ss_ref[...] += jnp.sum(h * h, axis=0, keepdims=True)


def hyena_filter(feat, w1, b1, w2, b2, w3, freq, rates, tl):
    L = feat.shape[0]
    C = w3.shape[1]
    return pl.pallas_call(
        _hy_filter_kernel,
        grid=(L // tl,),
        in_specs=[pl.BlockSpec((tl, HY_FEAT_PAD), lambda i: (i, 0)),
                  _const_spec((HY_FEAT_PAD, HY_HID)), _const_spec((1, HY_HID)),
                  _const_spec((HY_HID, HY_HID)), _const_spec((1, HY_HID)),
                  _const_spec((HY_HID, C)), _const_spec((2, HY_HID)), _const_spec((1, C))],
        out_specs=[pl.BlockSpec((tl, C), lambda i: (i, 0)), pl.BlockSpec((1, C), lambda i: (0, 0))],
        out_shape=[jax.ShapeDtypeStruct((L, C), F32), jax.ShapeDtypeStruct((1, C), F32)],
        compiler_params=_cparams("arbitrary"),
        name="hyena_filter",
    )(feat, w1, b1, w2, b2, w3, freq, rates)


def _filter_scale(ss_ref, o):
    e = ss_ref[:, 2 * o * BR_W:(2 * o + 1) * BR_W] + ss_ref[:, (2 * o + 1) * BR_W:(2 * o + 2) * BR_W]
    return lax.rsqrt(e + EPS)


def _combine_spectrum(x, ss_ref, o, half):
    xf = x[:, 2 * o * BR_W:(2 * o + 1) * BR_W]
    xb = x[:, (2 * o + 1) * BR_W:(2 * o + 2) * BR_W]
    sc = _filter_scale(ss_ref, o)
    hr = (xf[:half] + xb[:half]) * sc
    hi = (xf[half:] - xb[half:]) * sc
    return jnp.concatenate([hr, hi], axis=0)


def _cmul(x, h, half):
    xr, xi = x[:half], x[half:]
    hr, hi = h[:half], h[half:]
    return jnp.concatenate([xr * hr - xi * hi, xr * hi + xi * hr], axis=0)


def _dft1_kernel(z_ref, f_ref, a_ref):
    a_ref[0] = _dot(f_ref[...], z_ref[0].astype(BF16)).astype(BF16)


def dft_stage1(z, f1, tn):
    B, n1, W = z.shape
    M = f1.shape[0]
    return pl.pallas_call(
        _dft1_kernel,
        grid=(B, W // tn),
        in_specs=[pl.BlockSpec((1, n1, tn), lambda b, j: (b, 0, j)), _const_spec((M, n1))],
        out_specs=pl.BlockSpec((1, M, tn), lambda b, j: (b, 0, j)),
        out_shape=jax.ShapeDtypeStruct((B, M, W), BF16),
        compiler_params=_cparams("parallel", "parallel"),
        name="dft_stage1",
    )(z, f1)


def _spec2_kernel(a_ref, g_ref, ss_ref, h_ref):
    kb = g_ref.shape[0]
    n2 = a_ref.shape[3]
    for k in range(kb):
        a = a_ref[0, :, k].reshape(2 * n2, a_ref.shape[4])
        x = _dot(g_ref[k], a)
        for o in range(2):
            h_ref[o, k] = _combine_spectrum(x, ss_ref, o, n2)


def filter_spectrum(a, g, ss, kb):
    _, _, N1, N2, C = a.shape
    return pl.pallas_call(
        _spec2_kernel,
        grid=(N1 // kb,),
        in_specs=[pl.BlockSpec((1, 2, kb, N2, C), lambda i: (0, 0, i, 0, 0)),
                  pl.BlockSpec((kb, 2 * N2, 2 * N2), lambda i: (i, 0, 0)),
                  _const_spec((1, C))],
        out_specs=pl.BlockSpec((2, kb, 2 * N2, BR_W), lambda i: (0, i, 0, 0)),
        out_shape=jax.ShapeDtypeStruct((2, N1, 2 * N2, BR_W), F32),
        compiler_params=_cparams("parallel"),
        name="filter_spectrum",
    )(a, g, ss)


def _conv2_kernel(a_ref, g_ref, gi_ref, h_ref, o_ref):
    kb = g_ref.shape[0]
    n2 = a_ref.shape[3]
    C = a_ref.shape[4]
    for k in range(kb):
        a = a_ref[0, :, k].reshape(2 * n2, C)
        y = _cmul(_dot(g_ref[k], a), h_ref[0, k], n2).astype(BF16)
        o_ref[0, :, k] = _dot(gi_ref[k], y).astype(BF16).reshape(2, n2, C)


def spectral_multiply(a, g, gi, h, o, kb):
    B, _, N1, N2, C = a.shape
    blk = pl.BlockSpec((1, 2, kb, N2, C), lambda i, b: (b, 0, i, 0, 0))
    mat = pl.BlockSpec((kb, 2 * N2, 2 * N2), lambda i, b: (i, 0, 0))
    return pl.pallas_call(
        _conv2_kernel,
        grid=(N1 // kb, B),
        in_specs=[blk, mat, mat, pl.BlockSpec((1, kb, 2 * N2, C), lambda i, b: (o, i, 0, 0))],
        out_specs=blk,
        out_shape=jax.ShapeDtypeStruct(a.shape, BF16),
        compiler_params=_cparams("parallel", "parallel"),
        name="spectral_multiply",
    )(a, g, gi, h)


def _idft1_kernel(b_ref, f_ref, z_ref, gate_ref, bias_ref, f1_ref, o_ref, *a_ref):
    y = _dot(f_ref[...], b_ref[0])
    z = z_ref[0]
    out = gate_ref[0] * (y + bias_ref[...] * z)
    o_ref[0] = out
    if a_ref:
        a_ref[0][0] = _dot(f1_ref[...], out.astype(BF16)).astype(BF16)


def idft_stage1(bm, fi, z, gate, bias, f1, tn, with_next):
    B, M, W = bm.shape
    n1 = fi.shape[0]
    blk = pl.BlockSpec((1, n1, tn), lambda b, j: (b, 0, j))
    wide = pl.BlockSpec((1, M, tn), lambda b, j: (b, 0, j))
    out_specs, out_shape = [blk], [jax.ShapeDtypeStruct((B, n1, W), F32)]
    if with_next:
        out_specs.append(wide)
        out_shape.append(jax.ShapeDtypeStruct((B, M, W), BF16))
    return pl.pallas_call(
        _idft1_kernel,
        grid=(B, W // tn),
        in_specs=[wide, _const_spec((n1, M)), blk, blk, _const_spec((1, tn)), _const_spec((M, n1))],
        out_specs=out_specs,
        out_shape=out_shape,
        compiler_params=_cparams("parallel", "parallel"),
        name="idft_stage1",
    )(bm, fi, z, gate, bias, f1)


@functools.lru_cache(maxsize=None)
def _dft_tables(L):
    N = 2 * L
    N2 = 128
    N1 = N // N2
    nz = L // N2
    k1 = np.arange(N1)[:, None]
    n1 = np.arange(nz)[None, :]
    th = 2 * np.pi * ((k1 * n1) % N1) / N1
    f1 = np.concatenate([np.cos(th), -np.sin(th)], axis=0)
    fi = np.concatenate([np.cos(th).T, -np.sin(th).T], axis=1) / N
    kk1 = np.arange(N1)[:, None, None]
    k2 = np.arange(N2)[None, :, None]
    n2 = np.arange(N2)[None, None, :]
    ph = 2 * np.pi * ((n2 * k2 * N1 + n2 * kk1) % N) / N
    gr, gim = np.cos(ph), -np.sin(ph)
    g = np.concatenate([np.concatenate([gr, -gim], axis=2), np.concatenate([gim, gr], axis=2)], axis=1)
    hr, him = np.swapaxes(gr, 1, 2), -np.swapaxes(gim, 1, 2)
    gi = np.concatenate([np.concatenate([hr, -him], axis=2), np.concatenate([him, hr], axis=2)], axis=1)
    return tuple(np.asarray(t, np.float32) for t in (f1, fi, g, gi))


def _spec_direct_kernel(hf_ref, f_ref, ss_ref, h_ref):
    x = _dot(f_ref[...], hf_ref[...].astype(BF16))
    half = x.shape[0] // 2
    for o in range(2):
        h_ref[o] = _combine_spectrum(x, ss_ref, o, half)


def filter_spectrum_direct(hf, f, ss):
    L, C = hf.shape
    return pl.pallas_call(
        _spec_direct_kernel,
        grid=(1,),
        in_specs=[_const_spec((L, C)), _const_spec((4 * L, L)), _const_spec((1, C))],
        out_specs=pl.BlockSpec((2, 4 * L, BR_W), lambda i: (0, 0, 0)),
        out_shape=jax.ShapeDtypeStruct((2, 4 * L, BR_W), F32),
        compiler_params=_cparams("arbitrary"),
        name="filter_spectrum_direct",
    )(hf, f, ss)


def _conv_direct_kernel(z_ref, gate_ref, bias_ref, f_ref, fi_ref, h_ref, o_ref):
    z = z_ref[0]
    x = _dot(f_ref[...], z.astype(BF16))
    y = _cmul(x, h_ref[0], x.shape[0] // 2).astype(BF16)
    o_ref[0] = gate_ref[0] * (_dot(fi_ref[...], y) + bias_ref[...] * z)


def conv_direct(z, gate, bias, f, fi, h, o):
    B, L, C = z.shape
    blk = pl.BlockSpec((1, L, C), lambda b: (b, 0, 0))
    return pl.pallas_call(
        _conv_direct_kernel,
        grid=(B,),
        in_specs=[blk, blk, _const_spec((1, C)), _const_spec((4 * L, L)), _const_spec((L, 4 * L)),
                  pl.BlockSpec((1, 4 * L, C), lambda b: (o, 0, 0))],
        out_specs=blk,
        out_shape=jax.ShapeDtypeStruct((B, L, C), F32),
        compiler_params=_cparams("parallel"),
        name="conv_direct",
    )(z, gate, bias, f, fi, h)


@functools.lru_cache(maxsize=None)
def _dft_direct_tables(L):
    N = 2 * L
    k = np.arange(N)[:, None]
    n = np.arange(L)[None, :]
    th = 2 * np.pi * ((k * n) % N) / N
    f = np.concatenate([np.cos(th), -np.sin(th)], axis=0)
    fi = np.concatenate([np.cos(th).T, -np.sin(th).T], axis=1) / N
    return np.asarray(f, np.float32), np.asarray(fi, np.float32)


@functools.lru_cache(maxsize=None)
def _filter_features(L):
    t = np.arange(L, dtype=np.float32) / np.float32(L)
    ang = (2.0 * math.pi) * t[:, None].astype(np.float64) * np.arange(1, HY_BANDS + 1)
    feat = np.zeros((L, HY_FEAT_PAD), np.float32)
    feat[:, 0] = t
    feat[:, 1:1 + HY_BANDS] = np.sin(ang)
    feat[:, 1 + HY_BANDS:HY_FEAT] = np.cos(ang)
    rates = np.abs(np.linspace(math.log(HY_TARGET) / HY_SLOW, math.log(HY_TARGET) / HY_FAST, BR_W))
    return feat, np.tile(np.asarray(rates, np.float32), 4)[None, :]


def hyena_branch(v, g1, g2, hp):
    f_w1, f_b1, f_w2, f_b2, f_w3, freq, bias = hp
    B, L, C = v.shape
    feat, rates = _filter_features(L)
    hf, ss = hyena_filter(jnp.asarray(feat), f_w1, f_b1, f_w2, f_b2, f_w3, freq, jnp.asarray(rates),
                          min(L, 512))
    if L <= 512:
        f, fi = (jnp.asarray(t).astype(BF16) for t in _dft_direct_tables(L))
        spec = filter_spectrum_direct(hf, f, ss)
        z = v
        for o, gate in enumerate((g1, g2)):
            z = conv_direct(z, gate, bias[o:o + 1], f, fi, spec, o)
        return z
    f1, fi1, g, gi = (jnp.asarray(t).astype(BF16) for t in _dft_tables(L))
    N2 = 128
    N1 = 2 * L // N2
    nz = L // N2
    kb = 8
    a = dft_stage1(hf.reshape(1, nz, N2 * 4 * C), f1, 8192)
    spec = filter_spectrum(a.reshape(1, 2, N1, N2, 4 * C), g, ss, kb)
    W = N2 * C
    tn = 8192
    z = v.reshape(B, nz, W)
    a = dft_stage1(z, f1, tn)
    for o, gate in enumerate((g1, g2)):
        bm = spectral_multiply(a.reshape(B, 2, N1, N2, C), g, gi, spec, o, kb).reshape(B, 2 * N1, W)
        out = idft_stage1(bm, fi1, z, gate.reshape(B, nz, W), jnp.tile(bias[o:o + 1], (1, tn // C)), f1, tn, o == 0)
        z = out[0]
        if o == 0:
            a = out[1]
    return z.reshape(B, L, C)


def _head_sum(x, seg_ref):
    hi = x.astype(BF16)
    lo = (x - hi.astype(F32)).astype(BF16)
    return _dot(hi, seg_ref[...]) + _dot(lo, seg_ref[...])


def _rw_prep_kernel(p_ref, pp_ref, pn_ref, mu_ref, wl_ref, w0_ref, a0_ref, kv_ref, seg_ref,
                    r_ref, v_ref, na_ref, w_ref, kt_ref, b_ref, bonus_ref, g_ref, *, on_grid):
    i = pl.program_id(1)
    first = i == 0
    last = i == pl.num_programs(1) - 1
    x = p_ref[0]
    tm = x.shape[0]
    grp = lax.broadcasted_iota(jnp.int32, x.shape, 1) % 4
    prev, nxt = pp_ref[0], pn_ref[0]
    if on_grid:
        col = (lax.broadcasted_iota(jnp.int32, (tm, 1), 0) + i * tm) % GRID_W
        left = jnp.where(col == 0, 0.0, pltpu.roll(x, 1, 0))
        right = jnp.where(col == GRID_W - 1, 0.0, pltpu.roll(x, tm - 1, 0))
        up = _shift_rows(x, prev, nxt, GRID_W, first, last)
        down = _shift_rows(x, prev, nxt, -GRID_W, first, last)
        shifted = jnp.where(grp == 0, left, jnp.where(grp == 1, right, jnp.where(grp == 2, up, down)))
    else:
        before = _shift_rows(x, prev, nxt, 1, first, last)
        after = _shift_rows(x, prev, nxt, -1, first, last)
        shifted = jnp.where(grp % 2 == 0, before, after)
    xx = x + (shifted - x) * mu_ref[...]
    r = xx[:, 0:BR_W]
    k = xx[:, BR_W:2 * BR_W]
    v = xx[:, 2 * BR_W:3 * BR_W]
    lo = xx[:, 3 * BR_W:4 * BR_W]
    ll = lax.broadcasted_iota(jnp.int32, lo.shape, 1)
    act = jnp.where(ll < RW_LORA, jnp.tanh(lo), jnp.where(ll < 2 * RW_LORA, lo, _sigmoid(lo)))
    z = _dot(act.astype(BF16), wl_ref[...])
    kk = k * kv_ref[0:1, :]
    kk = kk * lax.rsqrt(_head_sum(kk * kk, seg_ref) + 1e-12)
    r_ref[0] = r
    v_ref[0] = v
    na_ref[0] = -kk
    bonus_ref[0] = _head_sum(r * k * kv_ref[2:3, :], seg_ref) * v
    g_ref[0] = z[:, 4 * BR_W:5 * BR_W]
    for d in range(2):
        logw = -_softplus(-(w0_ref[d:d + 1, :] + z[:, d * BR_W:(d + 1) * BR_W])) - 0.5
        w_ref[d, 0] = jnp.exp(-jnp.exp(logw))
        a = _sigmoid(a0_ref[d:d + 1, :] + z[:, (2 + d) * BR_W:(3 + d) * BR_W])
        kt_ref[d, 0] = k * (1.0 + (a - 1.0) * kv_ref[1:2, :])
        b_ref[d, 0] = kk * a


def rwkv_prep(p, mu, w_lora, w0, a0, kvec, seg, on_grid, tm):
    B, T, _ = p.shape
    halo = GRID_W if on_grid else 8
    main, prev, nxt = _halo_specs(tm, T, RW_PAD, lambda b, i: OFF_RW // RW_PAD, halo)
    one = pl.BlockSpec((1, tm, BR_W), lambda b, i: (b, i, 0))
    two = pl.BlockSpec((2, 1, tm, BR_W), lambda b, i: (0, b, i, 0))
    s1 = jax.ShapeDtypeStruct((B, T, BR_W), F32)
    s2 = jax.ShapeDtypeStruct((2, B, T, BR_W), F32)
    return pl.pallas_call(
        functools.partial(_rw_prep_kernel, on_grid=on_grid),
        grid=(B, T // tm),
        in_specs=[main, prev, nxt, _const_spec((1, RW_PAD)), _const_spec((BR_W, 5 * BR_W)),
                  _const_spec((2, BR_W)), _const_spec((2, BR_W)), _const_spec((3, BR_W)),
                  _const_spec((BR_W, BR_W))],
        out_specs=[one, one, one, two, two, two, one, one],
        out_shape=[s1, s1, s1, s2, s2, s2, s1, s1],
        compiler_params=_cparams("parallel", "parallel"),
        name="rwkv_prep",
    )(p, p, p, mu, w_lora, w0, a0, kvec, seg)


RW_VQ = RW_HEAD // 4
RW_VH = 2 * RW_VQ
LANE_W = 128
N_KEYED = 5
RW_UNROLL = 2
RW_PARTIALS = 1


def _chain_rows(refs, t, tb, shift):
    blocks = []
    for d in range(2):
        full = refs[d][t if d == 0 else tb]
        for half in range(2):
            x = full[:, half * LANE_W:(half + 1) * LANE_W]
            blocks.append(x if shift == 0 else pltpu.roll(x, LANE_W - shift, 1))
    return blocks


def _rw_scan_kernel(*refs, B, Tt):
    nin = 2 * (N_KEYED + 1)
    ins = refs[:nin]
    s0_ref = refs[nin]
    y_refs = refs[nin + 1:nin + 3]
    s_ref = refs[nin + 3]
    per_set = N_KEYED + 2
    sets = [refs[nin + 4 + per_set * u:nin + 4 + per_set * (u + 1)] for u in range(RW_UNROLL)]
    bufs = [(st[:N_KEYED], st[N_KEYED], st[N_KEYED + 1]) for st in sets]
    sa_ref = refs[nin + 4 + per_set * RW_UNROLL]
    NL = 16 * B
    A_TILE = 3

    @pl.when(pl.program_id(0) == 0)
    def _():
        s_ref[...] = s0_ref[...]
        bufs[RW_UNROLL - 1][2][...] = jnp.zeros_like(bufs[RW_UNROLL - 1][2])

    def keyed_job(t, slot, n):
        def issue():
            rows = _chain_rows(ins[2 * n:2 * n + 2], t, Tt - 1 - t, 0)
            return jnp.concatenate(rows * 4, axis=0).T

        def commit(val):
            bufs[slot][0][n][...] = val
        return issue, commit

    def values_job(t, slot):
        def issue():
            rows = []
            for vq in range(4):
                rows += _chain_rows(ins[2 * N_KEYED:], t, Tt - 1 - t, vq * RW_VQ)
            vt = jnp.concatenate(rows, axis=0).T
            return jnp.concatenate([vt[0:RW_VQ], vt[RW_HEAD:RW_HEAD + RW_VQ]], axis=0)

        def commit(val):
            bufs[slot][1][...] = val
        return issue, commit

    def output_job(t, slot):
        def issue():
            ys = bufs[slot][2][...]
            pad = jnp.zeros((RW_HEAD - RW_VQ, NL), F32)
            y = jnp.concatenate([ys[0:RW_VQ], pad, ys[RW_VQ:RW_VH], pad], axis=0).T
            out = []
            for d in range(2):
                halves = []
                for half in range(2):
                    acc = None
                    for vq in range(4):
                        row0 = ((vq * 2 + d) * 2 + half) * B
                        blk = y[row0:row0 + B, :]
                        blk = blk if vq == 0 else pltpu.roll(blk, vq * RW_VQ, 1)
                        acc = blk if acc is None else acc + blk
                    halves.append(acc)
                out.append(jnp.concatenate(halves, axis=1))
            return out

        def commit(val):
            y_refs[0][t] = val[0]
            y_refs[1][Tt - 1 - t] = val[1]
        return issue, commit

    groups = range(RW_VH // 8)
    others = tuple(n for n in range(N_KEYED) if n != A_TILE)
    never = pl.program_id(0) < 0

    def step_pieces(slot, sa_in, sa_out):
        tiles, vt_ref, ys_ref = bufs[slot]
        a_next = bufs[(slot + 1) % RW_UNROLL][0][A_TILE]
        row = lambda ref, k: ref[pl.ds(k, 1), :]
        ys = [[None] * RW_PARTIALS for _ in groups]
        san = [[None] * RW_PARTIALS for _ in groups]
        vt = {}
        acc = lambda lst, i, v: lst.__setitem__(i, v if lst[i] is None else lst[i] + v)

        def update(k0, k1):
            if not vt:
                for g in groups:
                    vt[g] = vt_ref[g * 8:(g + 1) * 8, :]
            for k in range(k0, k1):
                for h2 in range(2):
                    kk = h2 * RW_HEAD + k
                    r_k, w_k, kt_k, b_k = (row(tiles[n], kk) for n in (0, 1, 2, 4))
                    an_k = row(a_next, kk)
                    for g in (2 * h2, 2 * h2 + 1):
                        s = s_ref[g, k] * w_k + sa_in[g] * b_k + vt[g] * kt_k
                        s_ref[g, k] = s
                        acc(ys[g], k % RW_PARTIALS, s * r_k)
                        acc(san[g], k % RW_PARTIALS, s * an_k)
            if k1 == RW_HEAD:
                for g in groups:
                    ys_ref[g * 8:(g + 1) * 8, :] = functools.reduce(lambda a, b: a + b, ys[g])
                    sa_out.append(functools.reduce(lambda a, b: a + b, san[g]))

        def anchor(i, val):
            g = i % len(groups)
            if val is not None and ys[g][0] is not None:
                ys[g][0] = jnp.where(never, val, ys[g][0])

        cuts = [0, 9, 18, 27, 36, 45, 54, RW_HEAD]
        return [functools.partial(update, cuts[i], cuts[i + 1]) for i in range(len(cuts) - 1)], anchor

    def run(step, jobs):
        pieces, anchor = step
        pending = None
        for i, piece in enumerate(pieces):
            val = jobs[i][0]() if i < len(jobs) else None
            piece()
            if pending is not None and i + 1 < len(pieces):
                anchor(*pending)
            pending = None
            if i < len(jobs):
                jobs[i][1](val)
                if not isinstance(val, list):
                    pending = (i, val[val.shape[0] - 8:, :])
                elif val[1].shape[0] == 8 and NL == LANE_W:
                    pending = (i, val[1][:, LANE_W:])

    for issue, commit in ([keyed_job(0, 0, n) for n in range(N_KEYED)] + [values_job(0, 0)]
                          + [keyed_job(1, 1, A_TILE)]):
        commit(issue())
    first = [[None, None] for _ in groups]
    for k in range(RW_HEAD):
        for g in groups:
            p = s_ref[g, k] * bufs[0][0][A_TILE][pl.ds((g // 2) * RW_HEAD + k, 1), :]
            first[g][k % 2] = p if first[g][k % 2] is None else first[g][k % 2] + p
    for g in groups:
        sa_ref[g] = first[g][0] + first[g][1]

    def body(j, carry):
        t0 = RW_UNROLL * j
        sa = [sa_ref[g] for g in groups]
        for u in range(RW_UNROLL):
            t = t0 + u
            t1 = jnp.minimum(t + 1, Tt - 1)
            jobs = [output_job(jnp.maximum(t - 1, 0), (u - 1) % RW_UNROLL),
                    keyed_job(jnp.minimum(t + 2, Tt - 1), (u + 2) % RW_UNROLL, A_TILE)]
            jobs += [keyed_job(t1, (u + 1) % RW_UNROLL, n) for n in others] + [values_job(t1, (u + 1) % RW_UNROLL)]
            sa_next = []
            run(step_pieces(u, sa, sa_next), jobs)
            sa = sa_next
        for g in groups:
            sa_ref[g] = sa[g]
        return carry

    lax.fori_loop(0, Tt // RW_UNROLL, body, 0)
    issue, commit = output_job(Tt - 1, RW_UNROLL - 1)
    commit(issue())


def rwkv_mix(prep, s0, tt):
    r, v, na, w, kt, b = prep
    T, B, _ = r.shape
    nb = T // tt
    tblk = lambda i, d: i if d == 0 else nb - 1 - i

    in_specs, args = [], []
    for x, per_dir in ((r, False), (w, True), (kt, True), (na, False), (b, True), (v, False)):
        for d in range(2):
            if per_dir:
                in_specs.append(pl.BlockSpec((None, tt, B, BR_W), lambda i, d=d: (d, tblk(i, d), 0, 0)))
            else:
                in_specs.append(pl.BlockSpec((tt, B, BR_W), lambda i, d=d: (tblk(i, d), 0, 0)))
            args.append(x)
    NL = 16 * B
    sblk = pl.BlockSpec((RW_VH // 8, RW_HEAD, 8, NL), lambda i: (0, 0, 0, 0))
    yspecs = [pl.BlockSpec((tt, B, BR_W), lambda i, d=d: (tblk(i, d), 0, 0)) for d in range(2)]
    ysh = jax.ShapeDtypeStruct((T, B, BR_W), F32)
    out = pl.pallas_call(
        functools.partial(_rw_scan_kernel, B=B, Tt=tt),
        grid=(nb,),
        in_specs=in_specs + [sblk],
        out_specs=yspecs + [sblk],
        out_shape=[ysh] * 2 + [jax.ShapeDtypeStruct((RW_VH // 8, RW_HEAD, 8, NL), F32)],
        scratch_shapes=([pltpu.VMEM((LANE_W, NL), F32)] * N_KEYED + [pltpu.VMEM((RW_VH, NL), F32)] * 2) * RW_UNROLL
        + [pltpu.VMEM((RW_VH // 8, 8, NL), F32)],
        compiler_params=_cparams("arbitrary"),
        name="rwkv_scan",
    )(*args, s0)
    return out[0], out[1], out[2]


def _gelu_tanh(x):
    return 0.5 * x * (1.0 + jnp.tanh(math.sqrt(2.0 / math.pi) * (x + 0.044715 * (x * x * x))))


def _lru_prep_kernel(p_ref, pp_ref, pn_ref, cw_ref, cb_ref, w_ref, bias_ref, lam_ref, a_ref, b_ref, gg_ref):
    i = pl.program_id(1)
    first = i == 0
    last = i == pl.num_programs(1) - 1
    x = p_ref[0][:, 0:BR_W]
    prev = pp_ref[0][:, 0:BR_W]
    nxt = pn_ref[0][:, 0:BR_W]
    xc = cb_ref[...] + cw_ref[0:1, :] * _shift_rows(x, prev, nxt, 2, first, last)
    xc = xc + cw_ref[1:2, :] * _shift_rows(x, prev, nxt, 1, first, last)
    xc = xc + cw_ref[2:3, :] * x
    xc = xc + cw_ref[3:4, :] * _shift_rows(x, prev, nxt, -1, first, last)
    z = _dot(xc.astype(BF16), w_ref[...])
    for d in range(2):
        r = _sigmoid(z[:, 2 * d * BR_W:(2 * d + 1) * BR_W] + bias_ref[2 * d:2 * d + 1, :])
        gi = _sigmoid(z[:, (2 * d + 1) * BR_W:(2 * d + 2) * BR_W] + bias_ref[2 * d + 1:2 * d + 2, :])
        log_a = -LRU_C * r * _softplus(-lam_ref[d:d + 1, :])
        a_ref[d, 0] = jnp.exp(log_a)
        b_ref[d, 0] = jnp.sqrt(1.0 - jnp.exp(2.0 * log_a)) * (gi * xc)
    gg_ref[0] = _gelu_tanh(p_ref[0][:, BR_W:2 * BR_W])


def lru_prep(p, conv_w, conv_b, w_blk, bias, lam, tm):
    B, T, _ = p.shape
    main, prev, nxt = _halo_specs(tm, T, LRU_IN, lambda b, i: OFF_LRU // LRU_IN, 8)
    two = pl.BlockSpec((2, 1, tm, BR_W), lambda b, i: (0, b, i, 0))
    s2 = jax.ShapeDtypeStruct((2, B, T, BR_W), F32)
    return pl.pallas_call(
        _lru_prep_kernel,
        grid=(B, T // tm),
        in_specs=[main, prev, nxt, _const_spec((4, BR_W)), _const_spec((1, BR_W)),
                  _const_spec((BR_W, 4 * BR_W)), _const_spec((4, BR_W)), _const_spec((2, BR_W))],
        out_specs=[two, two, pl.BlockSpec((1, tm, BR_W), lambda b, i: (b, i, 0))],
        out_shape=[s2, s2, jax.ShapeDtypeStruct((B, T, BR_W), F32)],
        compiler_params=_cparams("parallel", "parallel"),
        name="lru_prep",
    )(p, p, p, conv_w, conv_b, w_blk, bias, lam)


def _affine_scan(a, b, reverse):
    tb = a.shape[0]
    row = lax.broadcasted_iota(jnp.int32, (tb, 1), 0)
    s = 1
    while s < tb:
        sh = tb - s if reverse else s
        ok = (row < tb - s) if reverse else (row >= s)
        a_s = pltpu.roll(a, sh, 0)
        b_s = pltpu.roll(b, sh, 0)
        b = jnp.where(ok, a * b_s + b, b)
        a = jnp.where(ok, a * a_s, a)
        s *= 2
    return a, b


def _lru_scan_kernel(af_ref, bf_ref, ab_ref, bb_ref, h0_ref, hf_ref, hb_ref, fin_ref):
    @pl.when(pl.program_id(1) == 0)
    def _():
        fin_ref[...] = h0_ref[...]

    tb = af_ref.shape[2]
    a, b = _affine_scan(af_ref[0, 0], bf_ref[0, 0], False)
    h = b + a * fin_ref[0, 0]
    hf_ref[0] = h
    fin_ref[0, 0] = h[tb - 1:tb, :]
    a, b = _affine_scan(ab_ref[0, 0], bb_ref[0, 0], True)
    h = b + a * fin_ref[1, 0]
    hb_ref[0] = h
    fin_ref[1, 0] = h[0:1, :]


def lru_scan(a, b, h0, tb):
    _, B, T, C = a.shape
    nb = T // tb
    fwd = pl.BlockSpec((1, 1, tb, C), lambda bi, i: (0, bi, i, 0))
    bwd = pl.BlockSpec((1, 1, tb, C), lambda bi, i: (1, bi, nb - 1 - i, 0))
    st = pl.BlockSpec((2, 1, 1, C), lambda bi, i: (0, bi, 0, 0))
    return pl.pallas_call(
        _lru_scan_kernel,
        grid=(B, nb),
        in_specs=[fwd, fwd, bwd, bwd, st],
        out_specs=[pl.BlockSpec((1, tb, C), lambda bi, i: (bi, i, 0)),
                   pl.BlockSpec((1, tb, C), lambda bi, i: (bi, nb - 1 - i, 0)), st],
        out_shape=[jax.ShapeDtypeStruct((B, T, C), F32), jax.ShapeDtypeStruct((B, T, C), F32),
                   jax.ShapeDtypeStruct((2, B, 1, C), F32)],
        compiler_params=_cparams("parallel", "arbitrary"),
        name="lru_scan",
    )(a, b, a, b, h0)


def _rope(x, cos, sin):
    q4 = RET_HEAD // 4
    lane = lax.broadcasted_iota(jnp.int32, x.shape, 1) % (2 * q4)
    partner = jnp.where(lane < q4, pltpu.roll(x, x.shape[1] - q4, 1), pltpu.roll(x, q4, 1))
    return x * cos + partner * sin


def _ret_dir(x, cos, sin, s, glane, gtile_ref, d, reverse):
    C = x.shape[0]
    q = x[:, 0:BR_W]
    k = x[:, BR_W:2 * BR_W]
    v = x[:, 2 * BR_W:3 * BR_W].astype(BF16)
    if cos is not None:
        q = _rope(q, cos, sin)
        k = _rope(k, cos, sin)
    k = k * (RET_HEAD ** -0.5)
    lg = -_softplus(-glane)
    idx = lax.broadcasted_iota(jnp.int32, (C, 1), 0).astype(F32)
    steps_in = (C - idx) if reverse else (idx + 1.0)
    steps_out = idx if reverse else (C - 1.0 - idx)
    ri = lax.broadcasted_iota(jnp.int32, (C, C), 0)
    ci = lax.broadcasted_iota(jnp.int32, (C, C), 1)
    diff = ((ci - ri) if reverse else (ri - ci)).astype(F32)
    lane_head = lax.broadcasted_iota(jnp.int32, (1, BR_W), 1) // RET_HEAD
    qb = q.astype(BF16)
    kb = k.astype(BF16)
    y = _dot(qb, s.astype(BF16)) * jnp.exp(steps_in * lg)
    for h in range(RET_HEADS):
        lg_h = -_softplus(-gtile_ref[d, h][0:1, :])
        dm = jnp.where(diff >= 0, jnp.exp(diff * lg_h), 0.0)
        mh = lane_head == h
        sc = lax.dot_general(jnp.where(mh, qb, jnp.zeros_like(qb)), kb, (((1,), (1,)), ((), ())),
                             preferred_element_type=F32)
        y = y + jnp.where(mh, _dot((sc * dm).astype(BF16), v), 0.0)
    kd = (k * jnp.exp(steps_out * lg)).astype(BF16)
    upd = lax.dot_general(kd, v, (((0,), (0,)), ((), ())), preferred_element_type=F32)
    rh = lax.broadcasted_iota(jnp.int32, (BR_W, BR_W), 0) // RET_HEAD
    ch = lax.broadcasted_iota(jnp.int32, (BR_W, BR_W), 1) // RET_HEAD
    s = s * jnp.exp(C * lg) + jnp.where(rh == ch, upd, 0.0)
    return y, s


def _ret_kernel(*refs, rope):
    if rope:
        xf_ref, xb_ref, cf_ref, sf_ref, cb_ref, sb_ref, gl_ref, gt_ref, s0_ref, yf_ref, yb_ref, s_ref = refs
    else:
        xf_ref, xb_ref, gl_ref, gt_ref, s0_ref, yf_ref, yb_ref, s_ref = refs

    @pl.when(pl.program_id(1) == 0)
    def _():
        s_ref[...] = s0_ref[...]

    y, s = _ret_dir(xf_ref[0], cf_ref[...] if rope else None, sf_ref[...] if rope else None,
                    s_ref[0, 0], gl_ref[0:1, :], gt_ref, 0, False)
    yf_ref[0] = y
    s_ref[0, 0] = s
    y, s = _ret_dir(xb_ref[0], cb_ref[...] if rope else None, sb_ref[...] if rope else None,
                    s_ref[1, 0], gl_ref[1:2, :], gt_ref, 1, True)
    yb_ref[0] = y
    s_ref[1, 0] = s


def retention(p, cos, sin, glane, gtile, s0, rope):
    B, T, _ = p.shape
    C = RET_CHUNK
    nc = T // C
    cb = OFF_RET // RET_IN
    xf = pl.BlockSpec((1, C, RET_IN), lambda b, i: (b, i, cb))
    xb = pl.BlockSpec((1, C, RET_IN), lambda b, i: (b, nc - 1 - i, cb))
    tf = pl.BlockSpec((C, BR_W), lambda b, i: (i, 0))
    tb = pl.BlockSpec((C, BR_W), lambda b, i: (nc - 1 - i, 0))
    st = pl.BlockSpec((2, 1, BR_W, BR_W), lambda b, i: (0, b, 0, 0))
    ins = [xf, xb] + ([tf, tf, tb, tb] if rope else []) + [
        _const_spec((2, BR_W)), _const_spec((2, RET_HEADS, 8, C)), st]
    args = [p, p] + ([cos, sin, cos, sin] if rope else []) + [glane, gtile, s0]
    return pl.pallas_call(
        functools.partial(_ret_kernel, rope=rope),
        grid=(B, nc),
        in_specs=ins,
        out_specs=[pl.BlockSpec((1, C, BR_W), lambda b, i: (b, i, 0)),
                   pl.BlockSpec((1, C, BR_W), lambda b, i: (b, nc - 1 - i, 0)), st],
        out_shape=[jax.ShapeDtypeStruct((B, T, BR_W), F32), jax.ShapeDtypeStruct((B, T, BR_W), F32),
                   jax.ShapeDtypeStruct((2, B, BR_W, BR_W), F32)],
        compiler_params=_cparams("parallel", "arbitrary"),
        name="retention",
    )(*args)


@functools.lru_cache(maxsize=None)
def _rope_tables(T):
    pos = np.arange(T)
    q4 = RET_HEAD // 4
    inv = ROPE_BASE ** (-np.arange(q4, dtype=np.float64) / q4)
    cos = np.zeros((T, RET_HEAD))
    sin = np.zeros((T, RET_HEAD))
    for part, coord in enumerate((pos // GRID_W, pos % GRID_W)):
        ang = coord[:, None] * inv
        base = part * 2 * q4
        cos[:, base:base + q4] = np.cos(ang)
        cos[:, base + q4:base + 2 * q4] = np.cos(ang)
        sin[:, base:base + q4] = -np.sin(ang)
        sin[:, base + q4:base + 2 * q4] = np.sin(ang)
    tile = lambda t: np.asarray(np.tile(t, (1, RET_HEADS)), np.float32)
    return tile(cos), tile(sin)


def _head_norm(y, seg_ref, eps):
    mu = _head_sum(y, seg_ref) * (1.0 / RW_HEAD)
    yc = y - mu
    var = _head_sum(yc * yc, seg_ref) * (1.0 / RW_HEAD)
    return yc * lax.rsqrt(var + eps)


def _merge_kernel(x_ref, g_ref, sh_ref, sc_ref, gt_ref, hy_ref, ryf_ref, ryb_ref, rbon_ref, rg_ref,
                  lhf_ref, lhb_ref, lgg_ref, tyf_ref, tyb_ref, tg_ref, lng_ref, seg_ref,
                  wg_ref, br_ref, wo_ref, o_ref, m_ref):
    x = x_ref[0]
    u = _norm_mod(x, g_ref[...], sh_ref[0], sc_ref[0]).astype(BF16)
    y_rw = (_head_norm(ryf_ref[0] + ryb_ref[0], seg_ref, RW_LN_EPS) * lng_ref[...] + rbon_ref[0]) * rg_ref[0]
    y_lru = (lhf_ref[0] + lhb_ref[0]) * lgg_ref[0]
    y_ret = _head_norm(tyf_ref[0] + tyb_ref[0], seg_ref, RET_LN_EPS) * _silu(tg_ref[0])
    ys = [y.astype(BF16) for y in (hy_ref[0], y_rw, y_lru, y_ret)]
    D = x.shape[1]
    cw = 256
    for c in range(D // cw):
        acc = None
        for n in range(N_BRANCH):
            gate = _sigmoid(_dot(u, wg_ref[:, n * D + c * cw:n * D + (c + 1) * cw]))
            t = gate * _dot(ys[n], br_ref[n, :, c * cw:(c + 1) * cw])
            acc = t if acc is None else acc + t
        m_ref[:, c * cw:(c + 1) * cw] = acc.astype(BF16)
    o_ref[0] = x + gt_ref[0] * _dot(m_ref[...], wo_ref[...])


def merge(x, g, sh, sc, gt, p, y_hy, rw, lru, ret, ln_g, seg, w_gate, br, w_out, tm):
    B, T, D = x.shape
    vec = pl.BlockSpec((1, 1, D), lambda b, i: (b, 0, 0))
    row = pl.BlockSpec((1, tm, D), lambda b, i: (b, i, 0))
    brn = pl.BlockSpec((1, tm, BR_W), lambda b, i: (b, i, 0))
    tg = pl.BlockSpec((1, tm, BR_W), lambda b, i: (b, i, (OFF_RET + 3 * BR_W) // BR_W))
    return pl.pallas_call(
        _merge_kernel,
        grid=(B, T // tm),
        in_specs=[row, _const_spec((1, D)), vec, vec, vec] + [brn] * 10 + [tg] + [
            _const_spec((1, BR_W)), _const_spec((BR_W, BR_W)), _const_spec((D, GATE_IN)),
            _const_spec((N_BRANCH, BR_W, D)), _const_spec((D, D))],
        out_specs=row,
        out_shape=jax.ShapeDtypeStruct((B, T, D), F32),
        scratch_shapes=[pltpu.VMEM((tm, D), BF16)],
        compiler_params=_cparams("parallel", "parallel"),
        name="merge",
    )(x, g, sh, sc, gt, y_hy, *rw, *lru, *ret, p, ln_g, seg, w_gate, br, w_out)


def _block_diag(w):
    G = w.shape[-3]
    eye = jnp.eye(G, dtype=w.dtype)
    full = w[..., :, :, None, :] * eye[:, None, :, None]
    return full.reshape(*w.shape[:-3], G * w.shape[-2], G * w.shape[-1])


def _mixers(p, lp, states, on_grid, with_output, tiles):
    B, T, _ = p.shape
    tm, tt, _ = tiles
    r, v, na, w, kt, b, bonus, g = rwkv_prep(p, *lp['rw'], on_grid, tm)
    time_major = lambda t: jnp.swapaxes(t, -3, -2)
    y_f, y_b, rw_fin = rwkv_mix(tuple(time_major(t) for t in (r, v, na, w, kt, b)), states[0], tt)
    y_f, y_b = time_major(y_f), time_major(y_b)
    a, bb, gg = lru_prep(p, *lp['lru'], tm)
    h_f, h_b, lru_fin = lru_scan(a, bb, states[1], tm)
    cos, sin = (jnp.asarray(t) for t in _rope_tables(T)) if on_grid else (None, None)
    t_f, t_b, ret_fin = retention(p, cos, sin, *lp['ret'], states[2], on_grid)
    fins = (rw_fin, lru_fin, ret_fin)
    if not with_output:
        return None, fins
    vg = hyena_prep(p, *lp['hy_conv'], min(T, 2048))
    y_hy = hyena_branch(vg[0], vg[1], vg[2], lp['hy'])
    return (y_hy, (y_f, y_b, bonus, g), (h_f, h_b, gg), (t_f, t_b)), fins


def kernel(x, c, ctx, c_ctx, w_mod, b_mod, norm1_g, norm2_g, w_in, hy_conv_w, hy_conv_b, hy_f_w1, hy_f_b1, hy_f_w2, hy_f_b2, hy_f_w3, hy_freq, hy_bias, rw_mu, rw_w0, rw_w2, rw_a0, rw_a2, rw_g2, rw_kk, rw_ka, rw_rk, rw_ln_g, lru_conv_w, lru_conv_b, lru_wa, lru_ba, lru_wx, lru_bx, lru_lam, ret_gamma, br_proj, w_out, ffn_w1, ffn_w2, final_g):
    B, T, D = x.shape
    TC = ctx.shape[1]
    L = w_in.shape[0]

    s0, s1, s2, s3 = HY_IN, HY_IN + RW_IN, HY_IN + RW_IN + LRU_IN, HY_IN + RW_IN + LRU_IN + RET_IN
    w_branch = jnp.concatenate([w_in[:, :, s0:s1], jnp.zeros((L, D, RW_PAD - RW_IN), w_in.dtype),
                                w_in[:, :, s2:s3], jnp.zeros((L, D, HY_GAP), w_in.dtype),
                                w_in[:, :, 0:s0], w_in[:, :, s1:s2]], axis=2).astype(BF16)
    w_gate = w_in[:, :, s3:].astype(BF16)
    mu = jnp.pad(rw_mu, ((0, 0), (0, RW_PAD - RW_IN)))[:, None, :]
    w_lora = jnp.zeros((L, BR_W, 5 * BR_W), F32)
    w_lora = w_lora.at[:, 0:RW_LORA, 0:BR_W].set(rw_w2[:, 0]).at[:, 0:RW_LORA, BR_W:2 * BR_W].set(rw_w2[:, 1])
    w_lora = w_lora.at[:, RW_LORA:2 * RW_LORA, 2 * BR_W:3 * BR_W].set(rw_a2[:, 0])
    w_lora = w_lora.at[:, RW_LORA:2 * RW_LORA, 3 * BR_W:4 * BR_W].set(rw_a2[:, 1])
    w_lora = w_lora.at[:, 2 * RW_LORA:3 * RW_LORA, 4 * BR_W:5 * BR_W].set(rw_g2).astype(BF16)
    kvec = jnp.stack([rw_kk, rw_ka, rw_rk], axis=1)
    seg = jnp.asarray(np.kron(np.eye(RW_HEADS), np.ones((RW_HEAD, RW_HEAD))), BF16)
    lru_w = jnp.concatenate([_block_diag(lru_wa[:, 0]), _block_diag(lru_wx[:, 0]),
                             _block_diag(lru_wa[:, 1]), _block_diag(lru_wx[:, 1])], axis=2).astype(BF16)
    lru_bias = jnp.stack([lru_ba[:, 0], lru_bx[:, 0], lru_ba[:, 1], lru_bx[:, 1]], axis=1)
    glane = jnp.repeat(ret_gamma, RET_HEAD, axis=2)
    gtile = jnp.broadcast_to(ret_gamma[:, :, :, None, None], (L, 2, RET_HEADS, 8, RET_CHUNK))
    f_w1 = jnp.pad(hy_f_w1, ((0, 0), (0, HY_FEAT_PAD - HY_FEAT), (0, 0)))
    br_b = br_proj.astype(BF16)
    w_out_b = w_out.astype(BF16)
    ffn_gate = ffn_w1[:, :, :D_FF].astype(BF16)
    ffn_up_w = ffn_w1[:, :, D_FF:].astype(BF16)
    ffn_w2_b = ffn_w2.astype(BF16)

    cc = jnp.concatenate([c, c_ctx[None, :], jnp.zeros((16 - B - 1, D), F32)], axis=0)
    mods = modulation(cc, w_mod.astype(BF16), b_mod[:, None, :])

    zero_states = (jnp.zeros((RW_VH // 8, RW_HEAD, 8, 16 * B), F32), jnp.zeros((2, B, 1, BR_W), F32),
                   jnp.zeros((2, B, BR_W, BR_W), F32))
    xc = ctx
    for l in range(L):
        last = l == L - 1
        lp = {
            'rw': (mu[l], w_lora[l], rw_w0[l], rw_a0[l], kvec[l], seg),
            'lru': (lru_conv_w[l], lru_conv_b[l][None, :], lru_w[l], lru_bias[l], lru_lam[l]),
            'ret': (glane[l], gtile[l]),
            'hy_conv': (hy_conv_w[l], hy_conv_b[l][None, :]),
            'hy': (f_w1[l], hy_f_b1[l][None, :], hy_f_w2[l], hy_f_b2[l][None, :], hy_f_w3[l], hy_freq[l],
                   hy_bias[l]),
        }
        g1 = norm1_g[l][None, :]
        g2 = norm2_g[l][None, :]
        m_lat = [m[:, None, :] for m in jnp.split(mods[l, :B], 6, axis=-1)]
        m_ctx = [jnp.broadcast_to(m[None, :, :], (B, 1, D)) for m in jnp.split(mods[l, B:B + 1], 6, axis=-1)]
        ln_g = rw_ln_g[l][None, :]

        def layer(xs, m, states, on_grid, with_output, tiles, final):
            p = in_projection(xs, g1, m[0], m[1], w_branch[l], tiles[0])
            br, fins = _mixers(p, lp, states, on_grid, with_output, tiles)
            if not with_output:
                return None, fins
            xs = merge(xs, g1, m[0], m[1], m[2], p, br[0], br[1], br[2], br[3], ln_g, seg,
                       w_gate[l], br_b[l], w_out_b[l], tiles[2])
            h = ffn_up(xs, g2, m[3], m[4], ffn_gate[l], ffn_up_w[l], tiles[0])
            xs = ffn_down(h, xs, m[5], ffn_w2_b[l], final_g[None, :], final, tiles[0])
            return xs, fins

        xc_new, ctx_states = layer(xc, m_ctx, zero_states, False, not last, (TC, 64, TC), False)
        x, _ = layer(x, m_lat, ctx_states, True, True, (512, 128, 512), last)
        if not last:
            xc = xc_new
    return x
```

```python
import functools
import math

import numpy as np
import jax
import jax.numpy as jnp
from jax import lax
from jax.experimental import pallas as pl
from jax.experimental.pallas import tpu as pltpu

F32 = jnp.float32
BF16 = jnp.bfloat16

D_MODEL = 1024
DEPTH = 4
GRID_W = 64
N_BRANCH = 4
BR_W = D_MODEL // N_BRANCH

HY_BANDS = 8
HY_FEAT = 1 + 2 * HY_BANDS
HY_FEAT_PAD = 32
HY_HID = 64
HY_TARGET = 1e-2
HY_FAST = 0.3
HY_SLOW = 1.5
HY_IN = 3 * BR_W

RW_HEAD = 64
RW_HEADS = BR_W // RW_HEAD
RW_LORA = 64
RW_IN = 3 * BR_W + 3 * RW_LORA
RW_PAD = 4 * BR_W
RW_LN_EPS = 64e-5

LRU_BLOCKS = 4
LRU_BLOCK = BR_W // LRU_BLOCKS
LRU_C = 8.0
LRU_IN = 2 * BR_W

RET_HEADS = 4
RET_HEAD = BR_W // RET_HEADS
RET_CHUNK = 128
ROPE_BASE = 10000.0
RET_IN = 4 * BR_W
RET_LN_EPS = 1e-5

GATE_IN = N_BRANCH * D_MODEL
D_FF = ((8 * D_MODEL // 3 + 255) // 256) * 256
EPS = 1e-6

OFF_RW = 0
OFF_RET = OFF_RW + RW_PAD
OFF_HY = 3 * HY_IN
OFF_LRU = OFF_HY + HY_IN
N_BR = OFF_LRU + LRU_IN
HY_GAP = OFF_HY - (OFF_RET + RET_IN)

VMEM_LIMIT = 56 * 1024 * 1024
HI = lax.Precision.HIGHEST


def _cparams(*sem):
    return pltpu.CompilerParams(dimension_semantics=sem, vmem_limit_bytes=VMEM_LIMIT)


def _const_spec(shape):
    nd = len(shape)
    return pl.BlockSpec(shape, lambda *_: (0,) * nd, pipeline_mode=pl.Buffered(1))


def _dot(a, b, **kw):
    return jnp.dot(a, b, preferred_element_type=F32, **kw)


def _norm_mod(x, g, sh, sc):
    ms = jnp.mean(x * x, axis=-1, keepdims=True)
    return x * lax.rsqrt(ms + EPS) * g * (1.0 + sc) + sh


def _sigmoid(x):
    return 1.0 / (1.0 + jnp.exp(-x))


def _silu(x):
    return x * _sigmoid(x)


def _softplus(x):
    return jnp.maximum(x, 0.0) + jnp.log(1.0 + jnp.exp(-jnp.abs(x)))


def _mod_kernel(c_ref, w_ref, b_ref, o_ref):
    c = c_ref[...]
    o_ref[0] = _dot(_silu(c).astype(BF16), w_ref[0]) + b_ref[0]


def modulation(cc, w_mod, b_mod):
    L, D, N = w_mod.shape
    tn = 1536
    return pl.pallas_call(
        _mod_kernel,
        grid=(L, N // tn),
        in_specs=[pl.BlockSpec((16, D), lambda l, j: (0, 0)),
                  pl.BlockSpec((1, D, tn), lambda l, j: (l, 0, j)),
                  pl.BlockSpec((1, 1, tn), lambda l, j: (l, 0, j))],
        out_specs=pl.BlockSpec((1, 16, tn), lambda l, j: (l, 0, j)),
        out_shape=jax.ShapeDtypeStruct((L, 16, N), F32),
        compiler_params=_cparams("parallel", "parallel"),
        name="modulation",
    )(cc, w_mod, b_mod)


def _inproj_kernel(x_ref, g_ref, sh_ref, sc_ref, w_ref, o_ref):
    u = _norm_mod(x_ref[0], g_ref[...], sh_ref[0], sc_ref[0]).astype(BF16)
    n = w_ref.shape[1]
    cw = 256
    for c in range(n // cw):
        o_ref[0, :, c * cw:(c + 1) * cw] = _dot(u, w_ref[:, c * cw:(c + 1) * cw])


def in_projection(x, g, sh, sc, w, tm):
    B, T, D = x.shape
    N = w.shape[1]
    vec = pl.BlockSpec((1, 1, D), lambda b, i: (b, 0, 0))
    return pl.pallas_call(
        _inproj_kernel,
        grid=(B, T // tm),
        in_specs=[pl.BlockSpec((1, tm, D), lambda b, i: (b, i, 0)),
                  _const_spec((1, D)), vec, vec, _const_spec((D, N))],
        out_specs=pl.BlockSpec((1, tm, N), lambda b, i: (b, i, 0)),
        out_shape=jax.ShapeDtypeStruct((B, T, N), F32),
        compiler_params=_cparams("parallel", "parallel"),
        name="in_projection",
    )(x, g, sh, sc, w)


def _ffn1_kernel(x_ref, g_ref, sh_ref, sc_ref, wg_ref, wu_ref, o_ref):
    u = _norm_mod(x_ref[0], g_ref[...], sh_ref[0], sc_ref[0]).astype(BF16)
    n = wg_ref.shape[1]
    cw = 256
    for c in range(n // cw):
        sl = slice(c * cw, (c + 1) * cw)
        gate = _dot(u, wg_ref[:, sl])
        up = _dot(u, wu_ref[:, sl])
        o_ref[0, :, sl] = (_silu(gate) * up).astype(BF16)


def ffn_up(x, g, sh, sc, w_gate, w_up, tm):
    B, T, D = x.shape
    N = w_gate.shape[1]
    vec = pl.BlockSpec((1, 1, D), lambda b, i: (b, 0, 0))
    return pl.pallas_call(
        _ffn1_kernel,
        grid=(B, T // tm),
        in_specs=[pl.BlockSpec((1, tm, D), lambda b, i: (b, i, 0)),
                  _const_spec((1, D)), vec, vec, _const_spec((D, N)), _const_spec((D, N))],
        out_specs=pl.BlockSpec((1, tm, N), lambda b, i: (b, i, 0)),
        out_shape=jax.ShapeDtypeStruct((B, T, N), BF16),
        compiler_params=_cparams("parallel", "parallel"),
        name="ffn_up",
    )(x, g, sh, sc, w_gate, w_up)


def _ffn2_kernel(h_ref, x_ref, gt_ref, w_ref, fg_ref, o_ref, *, final_norm):
    y = x_ref[0] + gt_ref[0] * _dot(h_ref[0], w_ref[...])
    if final_norm:
        ms = jnp.mean(y * y, axis=-1, keepdims=True)
        y = y * lax.rsqrt(ms + EPS) * fg_ref[...]
    o_ref[0] = y


def ffn_down(h, x, gate, w, final_g, final_norm, tm):
    B, T, D = x.shape
    N = h.shape[2]
    return pl.pallas_call(
        functools.partial(_ffn2_kernel, final_norm=final_norm),
        grid=(B, T // tm),
        in_specs=[pl.BlockSpec((1, tm, N), lambda b, i: (b, i, 0)),
                  pl.BlockSpec((1, tm, D), lambda b, i: (b, i, 0)),
                  pl.BlockSpec((1, 1, D), lambda b, i: (b, 0, 0)),
                  _const_spec((N, D)), _const_spec((1, D))],
        out_specs=pl.BlockSpec((1, tm, D), lambda b, i: (b, i, 0)),
        out_shape=jax.ShapeDtypeStruct((B, T, D), F32),
        compiler_params=_cparams("parallel", "parallel"),
        name="ffn_down",
    )(h, x, gate, w, final_g)


def _halo_specs(tm, T, width, col_fn, halo):
    r = tm // halo
    last = T // halo - 1
    main = pl.BlockSpec((1, tm, width), lambda *g: (g[0], g[1], col_fn(*g)))
    prev = pl.BlockSpec((1, halo, width), lambda *g: (g[0], jnp.maximum(g[1] * r - 1, 0), col_fn(*g)))
    nxt = pl.BlockSpec((1, halo, width), lambda *g: (g[0], jnp.minimum((g[1] + 1) * r, last), col_fn(*g)))
    return main, prev, nxt


def _shift_rows(x, prev, nxt, s, first, last):
    tm = x.shape[0]
    if s > 0:
        head = jnp.where(first, 0.0, prev[prev.shape[0] - s:, :])
        return jnp.concatenate([head, x[:tm - s, :]], axis=0)
    s = -s
    tail = jnp.where(last, 0.0, nxt[:s, :])
    return jnp.concatenate([x[s:, :], tail], axis=0)


def _hy_prep_kernel(p_ref, pp_ref, pn_ref, w_ref, b_ref, v_ref, g1_ref, g2_ref):
    i = pl.program_id(1)
    first = i == 0
    last = i == pl.num_programs(1) - 1
    x = p_ref[0]
    xm = _shift_rows(x, pp_ref[0], pn_ref[0], 1, first, last)
    xp = _shift_rows(x, pp_ref[0], pn_ref[0], -1, first, last)
    u = b_ref[...] + w_ref[0:1, :] * xm + w_ref[1:2, :] * x + w_ref[2:3, :] * xp
    v_ref[0] = u[:, 0:BR_W]
    g1_ref[0] = u[:, BR_W:2 * BR_W]
    g2_ref[0] = u[:, 2 * BR_W:3 * BR_W]


def hyena_prep(p, conv_w, conv_b, tm):
    B, T, _ = p.shape
    main, prev, nxt = _halo_specs(tm, T, HY_IN, lambda b, i: OFF_HY // HY_IN, 8)
    out = pl.BlockSpec((1, tm, BR_W), lambda b, i: (b, i, 0))
    shp = jax.ShapeDtypeStruct((B, T, BR_W), F32)
    return pl.pallas_call(
        _hy_prep_kernel,
        grid=(B, T // tm),
        in_specs=[main, prev, nxt, _const_spec((3, HY_IN)), _const_spec((1, HY_IN))],
        out_specs=[out, out, out],
        out_shape=[shp, shp, shp],
        compiler_params=_cparams("parallel", "parallel"),
        name="hyena_prep",
    )(p, p, p, conv_w, conv_b)


def _hy_filter_kernel(feat_ref, w1_ref, b1_ref, w2_ref, b2_ref, w3_ref, fq_ref, rates_ref, h_ref, ss_ref):
    i = pl.program_id(0)
    feat = feat_ref[...]
    t = feat[:, 0:1]
    h = jnp.sin(fq_ref[0:1, :] * (_dot(feat, w1_ref[...], precision=HI) + b1_ref[...]))
    h = jnp.sin(fq_ref[1:2, :] * (_dot(h, w2_ref[...], precision=HI) + b2_ref[...]))
    h = _dot(h, w3_ref[...], precision=HI) * jnp.exp(-t * rates_ref[...])
    row = lax.broadcasted_iota(jnp.int32, h.shape, 0) + i * h.shape[0]
    col = lax.broadcasted_iota(jnp.int32, h.shape, 1)
    h = jnp.where((row == 0) & ((col // BR_W) % 2 == 1), 0.0, h)
    h_ref[...] = h

    @pl.when(i == 0)
    def _():
        ss_ref[...] = jnp.zeros_like(ss_ref)

    ss_ref[...] += jnp.sum(h * h, axis=0, keepdims=True)


def hyena_filter(feat, w1, b1, w2, b2, w3, freq, rates, tl):
    L = feat.shape[0]
    C = w3.shape[1]
    return pl.pallas_call(
        _hy_filter_kernel,
        grid=(L // tl,),
        in_specs=[pl.BlockSpec((tl, HY_FEAT_PAD), lambda i: (i, 0)),
                  _const_spec((HY_FEAT_PAD, HY_HID)), _const_spec((1, HY_HID)),
                  _const_spec((HY_HID, HY_HID)), _const_spec((1, HY_HID)),
                  _const_spec((HY_HID, C)), _const_spec((2, HY_HID)), _const_spec((1, C))],
        out_specs=[pl.BlockSpec((tl, C), lambda i: (i, 0)), pl.BlockSpec((1, C), lambda i: (0, 0))],
        out_shape=[jax.ShapeDtypeStruct((L, C), F32), jax.ShapeDtypeStruct((1, C), F32)],
        compiler_params=_cparams("arbitrary"),
        name="hyena_filter",
    )(feat, w1, b1, w2, b2, w3, freq, rates)


def _filter_scale(ss_ref, o):
    e = ss_ref[:, 2 * o * BR_W:(2 * o + 1) * BR_W] + ss_ref[:, (2 * o + 1) * BR_W:(2 * o + 2) * BR_W]
    return lax.rsqrt(e + EPS)


def _combine_spectrum(x, ss_ref, o, half):
    xf = x[:, 2 * o * BR_W:(2 * o + 1) * BR_W]
    xb = x[:, (2 * o + 1) * BR_W:(2 * o + 2) * BR_W]
    sc = _filter_scale(ss_ref, o)
    hr = (xf[:half] + xb[:half]) * sc
    hi = (xf[half:] - xb[half:]) * sc
    return jnp.concatenate([hr, hi], axis=0)


def _cmul(x, h, half):
    xr, xi = x[:half], x[half:]
    hr, hi = h[:half], h[half:]
    return jnp.concatenate([xr * hr - xi * hi, xr * hi + xi * hr], axis=0)


def _dft1_kernel(z_ref, f_ref, a_ref):
    a_ref[0] = _dot(f_ref[...], z_ref[0].astype(BF16)).astype(BF16)


def dft_stage1(z, f1, tn):
    B, n1, W = z.shape
    M = f1.shape[0]
    return pl.pallas_call(
        _dft1_kernel,
        grid=(B, W // tn),
        in_specs=[pl.BlockSpec((1, n1, tn), lambda b, j: (b, 0, j)), _const_spec((M, n1))],
        out_specs=pl.BlockSpec((1, M, tn), lambda b, j: (b, 0, j)),
        out_shape=jax.ShapeDtypeStruct((B, M, W), BF16),
        compiler_params=_cparams("parallel", "parallel"),
        name="dft_stage1",
    )(z, f1)


def _spec2_kernel(a_ref, g_ref, ss_ref, h_ref):
    kb = g_ref.shape[0]
    n2 = a_ref.shape[3]
    for k in range(kb):
        a = a_ref[0, :, k].reshape(2 * n2, a_ref.shape[4])
        x = _dot(g_ref[k], a)
        for o in range(2):
            h_ref[o, k] = _combine_spectrum(x, ss_ref, o, n2)


def filter_spectrum(a, g, ss, kb):
    _, _, N1, N2, C = a.shape
    return pl.pallas_call(
        _spec2_kernel,
        grid=(N1 // kb,),
        in_specs=[pl.BlockSpec((1, 2, kb, N2, C), lambda i: (0, 0, i, 0, 0)),
                  pl.BlockSpec((kb, 2 * N2, 2 * N2), lambda i: (i, 0, 0)),
                  _const_spec((1, C))],
        out_specs=pl.BlockSpec((2, kb, 2 * N2, BR_W), lambda i: (0, i, 0, 0)),
        out_shape=jax.ShapeDtypeStruct((2, N1, 2 * N2, BR_W), F32),
        compiler_params=_cparams("parallel"),
        name="filter_spectrum",
    )(a, g, ss)


def _conv2_kernel(a_ref, g_ref, gi_ref, h_ref, o_ref):
    kb = g_ref.shape[0]
    n2 = a_ref.shape[3]
    C = a_ref.shape[4]
    for k in range(kb):
        a = a_ref[0, :, k].reshape(2 * n2, C)
        y = _cmul(_dot(g_ref[k], a), h_ref[0, k], n2).astype(BF16)
        o_ref[0, :, k] = _dot(gi_ref[k], y).astype(BF16).reshape(2, n2, C)


def spectral_multiply(a, g, gi, h, o, kb):
    B, _, N1, N2, C = a.shape
    blk = pl.BlockSpec((1, 2, kb, N2, C), lambda i, b: (b, 0, i, 0, 0))
    mat = pl.BlockSpec((kb, 2 * N2, 2 * N2), lambda i, b: (i, 0, 0))
    return pl.pallas_call(
        _conv2_kernel,
        grid=(N1 // kb, B),
        in_specs=[blk, mat, mat, pl.BlockSpec((1, kb, 2 * N2, C), lambda i, b: (o, i, 0, 0))],
        out_specs=blk,
        out_shape=jax.ShapeDtypeStruct(a.shape, BF16),
        compiler_params=_cparams("parallel", "parallel"),
        name="spectral_multiply",
    )(a, g, gi, h)


def _idft1_kernel(b_ref, f_ref, z_ref, gate_ref, bias_ref, f1_ref, o_ref, *a_ref):
    y = _dot(f_ref[...], b_ref[0])
    z = z_ref[0]
    out = gate_ref[0] * (y + bias_ref[...] * z)
    o_ref[0] = out
    if a_ref:
        a_ref[0][0] = _dot(f1_ref[...], out.astype(BF16)).astype(BF16)


def idft_stage1(bm, fi, z, gate, bias, f1, tn, with_next):
    B, M, W = bm.shape
    n1 = fi.shape[0]
    blk = pl.BlockSpec((1, n1, tn), lambda b, j: (b, 0, j))
    wide = pl.BlockSpec((1, M, tn), lambda b, j: (b, 0, j))
    out_specs, out_shape = [blk], [jax.ShapeDtypeStruct((B, n1, W), F32)]
    if with_next:
        out_specs.append(wide)
        out_shape.append(jax.ShapeDtypeStruct((B, M, W), BF16))
    return pl.pallas_call(
        _idft1_kernel,
        grid=(B, W // tn),
        in_specs=[wide, _const_spec((n1, M)), blk, blk, _const_spec((1, tn)), _const_spec((M, n1))],
        out_specs=out_specs,
        out_shape=out_shape,
        compiler_params=_cparams("parallel", "parallel"),
        name="idft_stage1",
    )(bm, fi, z, gate, bias, f1)


@functools.lru_cache(maxsize=None)
def _dft_tables(L):
    N = 2 * L
    N2 = 128
    N1 = N // N2
    nz = L // N2
    k1 = np.arange(N1)[:, None]
    n1 = np.arange(nz)[None, :]
    th = 2 * np.pi * ((k1 * n1) % N1) / N1
    f1 = np.concatenate([np.cos(th), -np.sin(th)], axis=0)
    fi = np.concatenate([np.cos(th).T, -np.sin(th).T], axis=1) / N
    kk1 = np.arange(N1)[:, None, None]
    k2 = np.arange(N2)[None, :, None]
    n2 = np.arange(N2)[None, None, :]
    ph = 2 * np.pi * ((n2 * k2 * N1 + n2 * kk1) % N) / N
    gr, gim = np.cos(ph), -np.sin(ph)
    g = np.concatenate([np.concatenate([gr, -gim], axis=2), np.concatenate([gim, gr], axis=2)], axis=1)
    hr, him = np.swapaxes(gr, 1, 2), -np.swapaxes(gim, 1, 2)
    gi = np.concatenate([np.concatenate([hr, -him], axis=2), np.concatenate([him, hr], axis=2)], axis=1)
    return tuple(np.asarray(t, np.float32) for t in (f1, fi, g, gi))


def _spec_direct_kernel(hf_ref, f_ref, ss_ref, h_ref):
    x = _dot(f_ref[...], hf_ref[...].astype(BF16))
    half = x.shape[0] // 2
    for o in range(2):
        h_ref[o] = _combine_spectrum(x, ss_ref, o, half)


def filter_spectrum_direct(hf, f, ss):
    L, C = hf.shape
    return pl.pallas_call(
        _spec_direct_kernel,
        grid=(1,),
        in_specs=[_const_spec((L, C)), _const_spec((4 * L, L)), _const_spec((1, C))],
        out_specs=pl.BlockSpec((2, 4 * L, BR_W), lambda i: (0, 0, 0)),
        out_shape=jax.ShapeDtypeStruct((2, 4 * L, BR_W), F32),
        compiler_params=_cparams("arbitrary"),
        name="filter_spectrum_direct",
    )(hf, f, ss)


def _conv_direct_kernel(z_ref, gate_ref, bias_ref, f_ref, fi_ref, h_ref, o_ref):
    z = z_ref[0]
    x = _dot(f_ref[...], z.astype(BF16))
    y = _cmul(x, h_ref[0], x.shape[0] // 2).astype(BF16)
    o_ref[0] = gate_ref[0] * (_dot(fi_ref[...], y) + bias_ref[...] * z)


def conv_direct(z, gate, bias, f, fi, h, o):
    B, L, C = z.shape
    blk = pl.BlockSpec((1, L, C), lambda b: (b, 0, 0))
    return pl.pallas_call(
        _conv_direct_kernel,
        grid=(B,),
        in_specs=[blk, blk, _const_spec((1, C)), _const_spec((4 * L, L)), _const_spec((L, 4 * L)),
                  pl.BlockSpec((1, 4 * L, C), lambda b: (o, 0, 0))],
        out_specs=blk,
        out_shape=jax.ShapeDtypeStruct((B, L, C), F32),
        compiler_params=_cparams("parallel"),
        name="conv_direct",
    )(z, gate, bias, f, fi, h)


@functools.lru_cache(maxsize=None)
def _dft_direct_tables(L):
    N = 2 * L
    k = np.arange(N)[:, None]
    n = np.arange(L)[None, :]
    th = 2 * np.pi * ((k * n) % N) / N
    f = np.concatenate([np.cos(th), -np.sin(th)], axis=0)
    fi = np.concatenate([np.cos(th).T, -np.sin(th).T], axis=1) / N
    return np.asarray(f, np.float32), np.asarray(fi, np.float32)


@functools.lru_cache(maxsize=None)
def _filter_features(L):
    t = np.arange(L, dtype=np.float32) / np.float32(L)
    ang = (2.0 * math.pi) * t[:, None].astype(np.float64) * np.arange(1, HY_BANDS + 1)
    feat = np.zeros((L, HY_FEAT_PAD), np.float32)
    feat[:, 0] = t
    feat[:, 1:1 + HY_BANDS] = np.sin(ang)
    feat[:, 1 + HY_BANDS:HY_FEAT] = np.cos(ang)
    rates = np.abs(np.linspace(math.log(HY_TARGET) / HY_SLOW, math.log(HY_TARGET) / HY_FAST, BR_W))
    return feat, np.tile(np.asarray(rates, np.float32), 4)[None, :]


def hyena_branch(v, g1, g2, hp):
    f_w1, f_b1, f_w2, f_b2, f_w3, freq, bias = hp
    B, L, C = v.shape
    feat, rates = _filter_features(L)
    hf, ss = hyena_filter(jnp.asarray(feat), f_w1, f_b1, f_w2, f_b2, f_w3, freq, jnp.asarray(rates),
                          min(L, 512))
    if L <= 512:
        f, fi = (jnp.asarray(t).astype(BF16) for t in _dft_direct_tables(L))
        spec = filter_spectrum_direct(hf, f, ss)
        z = v
        for o, gate in enumerate((g1, g2)):
            z = conv_direct(z, gate, bias[o:o + 1], f, fi, spec, o)
        return z
    f1, fi1, g, gi = (jnp.asarray(t).astype(BF16) for t in _dft_tables(L))
    N2 = 128
    N1 = 2 * L // N2
    nz = L // N2
    kb = 8
    a = dft_stage1(hf.reshape(1, nz, N2 * 4 * C), f1, 8192)
    spec = filter_spectrum(a.reshape(1, 2, N1, N2, 4 * C), g, ss, kb)
    W = N2 * C
    tn = 8192
    z = v.reshape(B, nz, W)
    a = dft_stage1(z, f1, tn)
    for o, gate in enumerate((g1, g2)):
        bm = spectral_multiply(a.reshape(B, 2, N1, N2, C), g, gi, spec, o, kb).reshape(B, 2 * N1, W)
        out = idft_stage1(bm, fi1, z, gate.reshape(B, nz, W), jnp.tile(bias[o:o + 1], (1, tn // C)), f1, tn, o == 0)
        z = out[0]
        if o == 0:
            a = out[1]
    return z.reshape(B, L, C)


def _head_sum(x, seg_ref):
    hi = x.astype(BF16)
    lo = (x - hi.astype(F32)).astype(BF16)
    return _dot(hi, seg_ref[...]) + _dot(lo, seg_ref[...])


def _rw_prep_kernel(p_ref, pp_ref, pn_ref, mu_ref, wl_ref, w0_ref, a0_ref, kv_ref, seg_ref,
                    r_ref, v_ref, na_ref, w_ref, kt_ref, b_ref, bonus_ref, g_ref, *, on_grid):
    i = pl.program_id(1)
    first = i == 0
    last = i == pl.num_programs(1) - 1
    x = p_ref[0]
    tm = x.shape[0]
    grp = lax.broadcasted_iota(jnp.int32, x.shape, 1) % 4
    prev, nxt = pp_ref[0], pn_ref[0]
    if on_grid:
        col = (lax.broadcasted_iota(jnp.int32, (tm, 1), 0) + i * tm) % GRID_W
        left = jnp.where(col == 0, 0.0, pltpu.roll(x, 1, 0))
        right = jnp.where(col == GRID_W - 1, 0.0, pltpu.roll(x, tm - 1, 0))
        up = _shift_rows(x, prev, nxt, GRID_W, first, last)
        down = _shift_rows(x, prev, nxt, -GRID_W, first, last)
        shifted = jnp.where(grp == 0, left, jnp.where(grp == 1, right, jnp.where(grp == 2, up, down)))
    else:
        before = _shift_rows(x, prev, nxt, 1, first, last)
        after = _shift_rows(x, prev, nxt, -1, first, last)
        shifted = jnp.where(grp % 2 == 0, before, after)
    xx = x + (shifted - x) * mu_ref[...]
    r = xx[:, 0:BR_W]
    k = xx[:, BR_W:2 * BR_W]
    v = xx[:, 2 * BR_W:3 * BR_W]
    lo = xx[:, 3 * BR_W:4 * BR_W]
    ll = lax.broadcasted_iota(jnp.int32, lo.shape, 1)
    act = jnp.where(ll < RW_LORA, jnp.tanh(lo), jnp.where(ll < 2 * RW_LORA, lo, _sigmoid(lo)))
    z = _dot(act.astype(BF16), wl_ref[...])
    kk = k * kv_ref[0:1, :]
    kk = kk * lax.rsqrt(_head_sum(kk * kk, seg_ref) + 1e-12)
    r_ref[0] = r
    v_ref[0] = v
    na_ref[0] = -kk
    bonus_ref[0] = _head_sum(r * k * kv_ref[2:3, :], seg_ref) * v
    g_ref[0] = z[:, 4 * BR_W:5 * BR_W]
    for d in range(2):
        logw = -_softplus(-(w0_ref[d:d + 1, :] + z[:, d * BR_W:(d + 1) * BR_W])) - 0.5
        w_ref[d, 0] = jnp.exp(-jnp.exp(logw))
        a = _sigmoid(a0_ref[d:d + 1, :] + z[:, (2 + d) * BR_W:(3 + d) * BR_W])
        kt_ref[d, 0] = k * (1.0 + (a - 1.0) * kv_ref[1:2, :])
        b_ref[d, 0] = kk * a


def rwkv_prep(p, mu, w_lora, w0, a0, kvec, seg, on_grid, tm):
    B, T, _ = p.shape
    halo = GRID_W if on_grid else 8
    main, prev, nxt = _halo_specs(tm, T, RW_PAD, lambda b, i: OFF_RW // RW_PAD, halo)
    one = pl.BlockSpec((1, tm, BR_W), lambda b, i: (b, i, 0))
    two = pl.BlockSpec((2, 1, tm, BR_W), lambda b, i: (0, b, i, 0))
    s1 = jax.ShapeDtypeStruct((B, T, BR_W), F32)
    s2 = jax.ShapeDtypeStruct((2, B, T, BR_W), F32)
    return pl.pallas_call(
        functools.partial(_rw_prep_kernel, on_grid=on_grid),
        grid=(B, T // tm),
        in_specs=[main, prev, nxt, _const_spec((1, RW_PAD)), _const_spec((BR_W, 5 * BR_W)),
                  _const_spec((2, BR_W)), _const_spec((2, BR_W)), _const_spec((3, BR_W)),
                  _const_spec((BR_W, BR_W))],
        out_specs=[one, one, one, two, two, two, one, one],
        out_shape=[s1, s1, s1, s2, s2, s2, s1, s1],
        compiler_params=_cparams("parallel", "parallel"),
        name="rwkv_prep",
    )(p, p, p, mu, w_lora, w0, a0, kvec, seg)


RW_VQ = RW_HEAD // 4
RW_VH = 2 * RW_VQ
LANE_W = 128
N_KEYED = 5
RW_UNROLL = 2
RW_PARTIALS = 1


def _chain_rows(refs, t, tb, shift):
    blocks = []
    for d in range(2):
        full = refs[d][t if d == 0 else tb]
        for half in range(2):
            x = full[:, half * LANE_W:(half + 1) * LANE_W]
            blocks.append(x if shift == 0 else pltpu.roll(x, LANE_W - shift, 1))
    return blocks


def _rw_scan_kernel(*refs, B, Tt):
    nin = 2 * (N_KEYED + 1)
    ins = refs[:nin]
    s0_ref = refs[nin]
    y_refs = refs[nin + 1:nin + 5]
    s_ref = refs[nin + 5]
    first_scratch = nin + 6
    per_set = N_KEYED + 2
    sets = [refs[first_scratch + per_set * u:first_scratch + per_set * (u + 1)] for u in range(RW_UNROLL)]
    bufs = [(st[:N_KEYED], st[N_KEYED], st[N_KEYED + 1]) for st in sets]
    sa_ref = refs[first_scratch + per_set * RW_UNROLL]
    NL = 16 * B
    A_TILE = 3

    @pl.when(pl.program_id(0) == 0)
    def _():
        s_ref[...] = s0_ref[...]
        bufs[RW_UNROLL - 1][2][...] = jnp.zeros_like(bufs[RW_UNROLL - 1][2])

    def keyed_job(t, slot, n):
        def issue():
            rows = _chain_rows(ins[2 * n:2 * n + 2], t, Tt - 1 - t, 0)
            return jnp.concatenate(rows * 4, axis=0).T

        def commit(val):
            bufs[slot][0][n][...] = val
        return issue, commit

    def values_job(t, slot):
        def issue():
            rows = []
            for vq in range(4):
                rows += _chain_rows(ins[2 * N_KEYED:], t, Tt - 1 - t, vq * RW_VQ)
            vt = jnp.concatenate(rows, axis=0).T
            return jnp.concatenate([vt[0:RW_VQ], vt[RW_HEAD:RW_HEAD + RW_VQ]], axis=0)

        def commit(val):
            bufs[slot][1][...] = val
        return issue, commit

    def output_job(t, slot):
        def issue():
            ys = bufs[slot][2][...]
            pad = jnp.zeros((RW_HEAD - RW_VQ, NL), F32)
            y = jnp.concatenate([ys[0:RW_VQ], pad, ys[RW_VQ:RW_VH], pad], axis=0).T
            out = []
            for d in range(2):
                for half in range(2):
                    acc = None
                    for vq in range(4):
                        row0 = ((vq * 2 + d) * 2 + half) * B
                        blk = y[row0:row0 + B, :]
                        blk = blk if vq == 0 else pltpu.roll(blk, vq * RW_VQ, 1)
                        acc = blk if acc is None else acc + blk
                    out.append(acc)
            return out

        def commit(val):
            for d in range(2):
                for half in range(2):
                    y_refs[2 * d + half][:, t if d == 0 else Tt - 1 - t, :] = val[2 * d + half]
        return issue, commit

    groups = range(RW_VH // 8)
    others = tuple(n for n in range(N_KEYED) if n != A_TILE)
    never = pl.program_id(0) < 0

    def step_pieces(slot, sa_in, sa_out):
        tiles, vt_ref, ys_ref = bufs[slot]
        a_next = bufs[(slot + 1) % RW_UNROLL][0][A_TILE]
        row = lambda ref, k: ref[pl.ds(k, 1), :]
        ys = [[None] * RW_PARTIALS for _ in groups]
        san = [[None] * RW_PARTIALS for _ in groups]
        vt = {}
        acc = lambda lst, i, v: lst.__setitem__(i, v if lst[i] is None else lst[i] + v)

        def update(k0, k1):
            if not vt:
                for g in groups:
                    vt[g] = vt_ref[g * 8:(g + 1) * 8, :]
            for k in range(k0, k1):
                for h2 in range(2):
                    kk = h2 * RW_HEAD + k
                    r_k, w_k, kt_k, b_k = (row(tiles[n], kk) for n in (0, 1, 2, 4))
                    an_k = row(a_next, kk)
                    for g in (2 * h2, 2 * h2 + 1):
                        s = s_ref[g, k] * w_k + sa_in[g] * b_k + vt[g] * kt_k
                        s_ref[g, k] = s
                        acc(ys[g], k % RW_PARTIALS, s * r_k)
                        acc(san[g], k % RW_PARTIALS, s * an_k)
            if k1 == RW_HEAD:
                for g in groups:
                    ys_ref[g * 8:(g + 1) * 8, :] = functools.reduce(lambda a, b: a + b, ys[g])
                    sa_out.append(functools.reduce(lambda a, b: a + b, san[g]))

        def anchor(i, val):
            g = i % len(groups)
            if val is not None and ys[g][0] is not None:
                ys[g][0] = jnp.where(never, val, ys[g][0])

        cuts = [0, 9, 18, 27, 36, 45, 54, RW_HEAD]
        return [functools.partial(update, cuts[i], cuts[i + 1]) for i in range(len(cuts) - 1)], anchor

    def run(step, jobs):
        pieces, anchor = step
        pending = None
        for i, piece in enumerate(pieces):
            val = jobs[i][0]() if i < len(jobs) else None
            piece()
            if pending is not None and i + 1 < len(pieces):
                anchor(*pending)
            pending = None
            if i < len(jobs):
                jobs[i][1](val)
                if not isinstance(val, list):
                    pending = (i, val[val.shape[0] - 8:, :])
                elif val[3].shape == (8, NL):
                    pending = (i, val[3])

    for issue, commit in ([keyed_job(0, 0, n) for n in range(N_KEYED)] + [values_job(0, 0)]
                          + [keyed_job(1, 1, A_TILE)]):
        commit(issue())
    first = [[None, None] for _ in groups]
    for k in range(RW_HEAD):
        for g in groups:
            p = s_ref[g, k] * bufs[0][0][A_TILE][pl.ds((g // 2) * RW_HEAD + k, 1), :]
            first[g][k % 2] = p if first[g][k % 2] is None else first[g][k % 2] + p
    for g in groups:
        sa_ref[g] = first[g][0] + first[g][1]

    def body(j, carry):
        t0 = RW_UNROLL * j
        sa = [sa_ref[g] for g in groups]
        for u in range(RW_UNROLL):
            t = t0 + u
            t1 = jnp.minimum(t + 1, Tt - 1)
            jobs = [output_job(jnp.maximum(t - 1, 0), (u - 1) % RW_UNROLL),
                    keyed_job(jnp.minimum(t + 2, Tt - 1), (u + 2) % RW_UNROLL, A_TILE)]
            jobs += [keyed_job(t1, (u + 1) % RW_UNROLL, n) for n in others] + [values_job(t1, (u + 1) % RW_UNROLL)]
            sa_next = []
            run(step_pieces(u, sa, sa_next), jobs)
            sa = sa_next
        for g in groups:
            sa_ref[g] = sa[g]
        return carry

    lax.fori_loop(0, Tt // RW_UNROLL, body, 0)
    issue, commit = output_job(Tt - 1, RW_UNROLL - 1)
    commit(issue())


def _time_major_kernel(x_ref, o_ref):
    for b in range(x_ref.shape[0]):
        o_ref[:, b, :] = x_ref[b]


def time_major(x, tm):
    lead = x.shape[:-3]
    B, T, C = x.shape[-3:]
    x4 = x.reshape((-1, B, T, C))
    out = pl.pallas_call(
        _time_major_kernel,
        grid=(x4.shape[0], T // tm),
        in_specs=[pl.BlockSpec((None, B, tm, C), lambda n, i: (n, 0, i, 0))],
        out_specs=pl.BlockSpec((None, tm, B, C), lambda n, i: (n, i, 0, 0)),
        out_shape=jax.ShapeDtypeStruct((x4.shape[0], T, B, C), x.dtype),
        compiler_params=_cparams("parallel", "parallel"),
        name="time_major",
    )(x4)
    return out.reshape(lead + (T, B, C))


def rwkv_mix(prep, s0, tt):
    r, v, na, w, kt, b = prep
    T, B, _ = r.shape
    nb = T // tt
    tblk = lambda i, d: i if d == 0 else nb - 1 - i

    in_specs, args = [], []
    for x, per_dir in ((r, False), (w, True), (kt, True), (na, False), (b, True), (v, False)):
        for d in range(2):
            if per_dir:
                in_specs.append(pl.BlockSpec((None, tt, B, BR_W), lambda i, d=d: (d, tblk(i, d), 0, 0)))
            else:
                in_specs.append(pl.BlockSpec((tt, B, BR_W), lambda i, d=d: (tblk(i, d), 0, 0)))
            args.append(x)
    NL = 16 * B
    sblk = pl.BlockSpec((RW_VH // 8, RW_HEAD, 8, NL), lambda i: (0, 0, 0, 0))
    yspecs = [pl.BlockSpec((B, tt, LANE_W), lambda i, d=d: (0, tblk(i, d), 0)) for d in range(2) for _ in range(2)]
    ysh = jax.ShapeDtypeStruct((B, T, LANE_W), F32)
    out = pl.pallas_call(
        functools.partial(_rw_scan_kernel, B=B, Tt=tt),
        grid=(nb,),
        in_specs=in_specs + [sblk],
        out_specs=yspecs + [sblk],
        out_shape=[ysh] * 4 + [jax.ShapeDtypeStruct((RW_VH // 8, RW_HEAD, 8, NL), F32)],
        scratch_shapes=([pltpu.VMEM((LANE_W, NL), F32)] * N_KEYED + [pltpu.VMEM((RW_VH, NL), F32)] * 2) * RW_UNROLL
        + [pltpu.VMEM((RW_VH // 8, 8, NL), F32)],
        compiler_params=_cparams("arbitrary"),
        name="rwkv_scan",
    )(*args, s0)
    return (out[0], out[1]), (out[2], out[3]), out[4]


def _gelu_tanh(x):
    return 0.5 * x * (1.0 + jnp.tanh(math.sqrt(2.0 / math.pi) * (x + 0.044715 * (x * x * x))))


def _lru_prep_kernel(p_ref, pp_ref, pn_ref, cw_ref, cb_ref, w_ref, bias_ref, lam_ref, a_ref, b_ref, gg_ref):
    i = pl.program_id(1)
    first = i == 0
    last = i == pl.num_programs(1) - 1
    x = p_ref[0][:, 0:BR_W]
    prev = pp_ref[0][:, 0:BR_W]
    nxt = pn_ref[0][:, 0:BR_W]
    xc = cb_ref[...] + cw_ref[0:1, :] * _shift_rows(x, prev, nxt, 2, first, last)
    xc = xc + cw_ref[1:2, :] * _shift_rows(x, prev, nxt, 1, first, last)
    xc = xc + cw_ref[2:3, :] * x
    xc = xc + cw_ref[3:4, :] * _shift_rows(x, prev, nxt, -1, first, last)
    z = _dot(xc.astype(BF16), w_ref[...])
    for d in range(2):
        r = _sigmoid(z[:, 2 * d * BR_W:(2 * d + 1) * BR_W] + bias_ref[2 * d:2 * d + 1, :])
        gi = _sigmoid(z[:, (2 * d + 1) * BR_W:(2 * d + 2) * BR_W] + bias_ref[2 * d + 1:2 * d + 2, :])
        log_a = -LRU_C * r * _softplus(-lam_ref[d:d + 1, :])
        a_ref[d, 0] = jnp.exp(log_a)
        b_ref[d, 0] = jnp.sqrt(1.0 - jnp.exp(2.0 * log_a)) * (gi * xc)
    gg_ref[0] = _gelu_tanh(p_ref[0][:, BR_W:2 * BR_W])


def lru_prep(p, conv_w, conv_b, w_blk, bias, lam, tm):
    B, T, _ = p.shape
    main, prev, nxt = _halo_specs(tm, T, LRU_IN, lambda b, i: OFF_LRU // LRU_IN, 8)
    two = pl.BlockSpec((2, 1, tm, BR_W), lambda b, i: (0, b, i, 0))
    s2 = jax.ShapeDtypeStruct((2, B, T, BR_W), F32)
    return pl.pallas_call(
        _lru_prep_kernel,
        grid=(B, T // tm),
        in_specs=[main, prev, nxt, _const_spec((4, BR_W)), _const_spec((1, BR_W)),
                  _const_spec((BR_W, 4 * BR_W)), _const_spec((4, BR_W)), _const_spec((2, BR_W))],
        out_specs=[two, two, pl.BlockSpec((1, tm, BR_W), lambda b, i: (b, i, 0))],
        out_shape=[s2, s2, jax.ShapeDtypeStruct((B, T, BR_W), F32)],
        compiler_params=_cparams("parallel", "parallel"),
        name="lru_prep",
    )(p, p, p, conv_w, conv_b, w_blk, bias, lam)


def _affine_scan(a, b, reverse):
    tb = a.shape[0]
    row = lax.broadcasted_iota(jnp.int32, (tb, 1), 0)
    s = 1
    while s < tb:
        sh = tb - s if reverse else s
        ok = (row < tb - s) if reverse else (row >= s)
        a_s = pltpu.roll(a, sh, 0)
        b_s = pltpu.roll(b, sh, 0)
        b = jnp.where(ok, a * b_s + b, b)
        a = jnp.where(ok, a * a_s, a)
        s *= 2
    return a, b


def _lru_scan_kernel(af_ref, bf_ref, ab_ref, bb_ref, h0_ref, hf_ref, hb_ref, fin_ref):
    @pl.when(pl.program_id(1) == 0)
    def _():
        fin_ref[...] = h0_ref[...]

    tb = af_ref.shape[2]
    a, b = _affine_scan(af_ref[0, 0], bf_ref[0, 0], False)
    h = b + a * fin_ref[0, 0]
    hf_ref[0] = h
    fin_ref[0, 0] = h[tb - 1:tb, :]
    a, b = _affine_scan(ab_ref[0, 0], bb_ref[0, 0], True)
    h = b + a * fin_ref[1, 0]
    hb_ref[0] = h
    fin_ref[1, 0] = h[0:1, :]


def lru_scan(a, b, h0, tb):
    _, B, T, C = a.shape
    nb = T // tb
    fwd = pl.BlockSpec((1, 1, tb, C), lambda bi, i: (0, bi, i, 0))
    bwd = pl.BlockSpec((1, 1, tb, C), lambda bi, i: (1, bi, nb - 1 - i, 0))
    st = pl.BlockSpec((2, 1, 1, C), lambda bi, i: (0, bi, 0, 0))
    return pl.pallas_call(
        _lru_scan_kernel,
        grid=(B, nb),
        in_specs=[fwd, fwd, bwd, bwd, st],
        out_specs=[pl.BlockSpec((1, tb, C), lambda bi, i: (bi, i, 0)),
                   pl.BlockSpec((1, tb, C), lambda bi, i: (bi, nb - 1 - i, 0)), st],
        out_shape=[jax.ShapeDtypeStruct((B, T, C), F32), jax.ShapeDtypeStruct((B, T, C), F32),
                   jax.ShapeDtypeStruct((2, B, 1, C), F32)],
        compiler_params=_cparams("parallel", "arbitrary"),
        name="lru_scan",
    )(a, b, a, b, h0)


def _rope(x, cos, sin):
    q4 = RET_HEAD // 4
    lane = lax.broadcasted_iota(jnp.int32, x.shape, 1) % (2 * q4)
    partner = jnp.where(lane < q4, pltpu.roll(x, x.shape[1] - q4, 1), pltpu.roll(x, q4, 1))
    return x * cos + partner * sin


def _ret_dir(x, cos, sin, s, glane, gtile_ref, d, reverse):
    C = x.shape[0]
    q = x[:, 0:BR_W]
    k = x[:, BR_W:2 * BR_W]
    v = x[:, 2 * BR_W:3 * BR_W].astype(BF16)
    if cos is not None:
        q = _rope(q, cos, sin)
        k = _rope(k, cos, sin)
    k = k * (RET_HEAD ** -0.5)
    lg = -_softplus(-glane)
    idx = lax.broadcasted_iota(jnp.int32, (C, 1), 0).astype(F32)
    steps_in = (C - idx) if reverse else (idx + 1.0)
    steps_out = idx if reverse else (C - 1.0 - idx)
    ri = lax.broadcasted_iota(jnp.int32, (C, C), 0)
    ci = lax.broadcasted_iota(jnp.int32, (C, C), 1)
    diff = ((ci - ri) if reverse else (ri - ci)).astype(F32)
    lane_head = lax.broadcasted_iota(jnp.int32, (1, BR_W), 1) // RET_HEAD
    qb = q.astype(BF16)
    kb = k.astype(BF16)
    y = _dot(qb, s.astype(BF16)) * jnp.exp(steps_in * lg)
    for h in range(RET_HEADS):
        lg_h = -_softplus(-gtile_ref[d, h][0:1, :])
        dm = jnp.where(diff >= 0, jnp.exp(diff * lg_h), 0.0)
        mh = lane_head == h
        sc = lax.dot_general(jnp.where(mh, qb, jnp.zeros_like(qb)), kb, (((1,), (1,)), ((), ())),
                             preferred_element_type=F32)
        y = y + jnp.where(mh, _dot((sc * dm).astype(BF16), v), 0.0)
    kd = (k * jnp.exp(steps_out * lg)).astype(BF16)
    upd = lax.dot_general(kd, v, (((0,), (0,)), ((), ())), preferred_element_type=F32)
    rh = lax.broadcasted_iota(jnp.int32, (BR_W, BR_W), 0) // RET_HEAD
    ch = lax.broadcasted_iota(jnp.int32, (BR_W, BR_W), 1) // RET_HEAD
    s = s * jnp.exp(C * lg) + jnp.where(rh == ch, upd, 0.0)
    return y, s


def _ret_kernel(*refs, rope):
    if rope:
        xf_ref, xb_ref, cf_ref, sf_ref, cb_ref, sb_ref, gl_ref, gt_ref, s0_ref, yf_ref, yb_ref, s_ref = refs
    else:
        xf_ref, xb_ref, gl_ref, gt_ref, s0_ref, yf_ref, yb_ref, s_ref = refs

    @pl.when(pl.program_id(1) == 0)
    def _():
        s_ref[...] = s0_ref[...]

    y, s = _ret_dir(xf_ref[0], cf_ref[...] if rope else None, sf_ref[...] if rope else None,
                    s_ref[0, 0], gl_ref[0:1, :], gt_ref, 0, False)
    yf_ref[0] = y
    s_ref[0, 0] = s
    y, s = _ret_dir(xb_ref[0], cb_ref[...] if rope else None, sb_ref[...] if rope else None,
                    s_ref[1, 0], gl_ref[1:2, :], gt_ref, 1, True)
    yb_ref[0] = y
    s_ref[1, 0] = s


def retention(p, cos, sin, glane, gtile, s0, rope):
    B, T, _ = p.shape
    C = RET_CHUNK
    nc = T // C
    cb = OFF_RET // RET_IN
    xf = pl.BlockSpec((1, C, RET_IN), lambda b, i: (b, i, cb))
    xb = pl.BlockSpec((1, C, RET_IN), lambda b, i: (b, nc - 1 - i, cb))
    tf = pl.BlockSpec((C, BR_W), lambda b, i: (i, 0))
    tb = pl.BlockSpec((C, BR_W), lambda b, i: (nc - 1 - i, 0))
    st = pl.BlockSpec((2, 1, BR_W, BR_W), lambda b, i: (0, b, 0, 0))
    ins = [xf, xb] + ([tf, tf, tb, tb] if rope else []) + [
        _const_spec((2, BR_W)), _const_spec((2, RET_HEADS, 8, C)), st]
    args = [p, p] + ([cos, sin, cos, sin] if rope else []) + [glane, gtile, s0]
    return pl.pallas_call(
        functools.partial(_ret_kernel, rope=rope),
        grid=(B, nc),
        in_specs=ins,
        out_specs=[pl.BlockSpec((1, C, BR_W), lambda b, i: (b, i, 0)),
                   pl.BlockSpec((1, C, BR_W), lambda b, i: (b, nc - 1 - i, 0)), st],
        out_shape=[jax.ShapeDtypeStruct((B, T, BR_W), F32), jax.ShapeDtypeStruct((B, T, BR_W), F32),
                   jax.ShapeDtypeStruct((2, B, BR_W, BR_W), F32)],
        compiler_params=_cparams("parallel", "arbitrary"),
        name="retention",
    )(*args)


@functools.lru_cache(maxsize=None)
def _rope_tables(T):
    pos = np.arange(T)
    q4 = RET_HEAD // 4
    inv = ROPE_BASE ** (-np.arange(q4, dtype=np.float64) / q4)
    cos = np.zeros((T, RET_HEAD))
    sin = np.zeros((T, RET_HEAD))
    for part, coord in enumerate((pos // GRID_W, pos % GRID_W)):
        ang = coord[:, None] * inv
        base = part * 2 * q4
        cos[:, base:base + q4] = np.cos(ang)
        cos[:, base + q4:base + 2 * q4] = np.cos(ang)
        sin[:, base:base + q4] = -np.sin(ang)
        sin[:, base + q4:base + 2 * q4] = np.sin(ang)
    tile = lambda t: np.asarray(np.tile(t, (1, RET_HEADS)), np.float32)
    return tile(cos), tile(sin)


def _head_norm(y, seg_ref, eps):
    mu = _head_sum(y, seg_ref) * (1.0 / RW_HEAD)
    yc = y - mu
    var = _head_sum(yc * yc, seg_ref) * (1.0 / RW_HEAD)
    return yc * lax.rsqrt(var + eps)


def _merge_kernel(x_ref, g_ref, sh_ref, sc_ref, gt_ref, hy_ref, rfl_ref, rfh_ref, rbl_ref, rbh_ref,
                  rbon_ref, rg_ref, lhf_ref, lhb_ref, lgg_ref, tyf_ref, tyb_ref, tg_ref, lng_ref, seg_ref,
                  wg_ref, br_ref, wo_ref, o_ref, m_ref):
    x = x_ref[0]
    u = _norm_mod(x, g_ref[...], sh_ref[0], sc_ref[0]).astype(BF16)
    wkv = jnp.concatenate([rfl_ref[0] + rbl_ref[0], rfh_ref[0] + rbh_ref[0]], axis=1)
    y_rw = (_head_norm(wkv, seg_ref, RW_LN_EPS) * lng_ref[...] + rbon_ref[0]) * rg_ref[0]
    y_lru = (lhf_ref[0] + lhb_ref[0]) * lgg_ref[0]
    y_ret = _head_norm(tyf_ref[0] + tyb_ref[0], seg_ref, RET_LN_EPS) * _silu(tg_ref[0])
    ys = [y.astype(BF16) for y in (hy_ref[0], y_rw, y_lru, y_ret)]
    D = x.shape[1]
    cw = 256
    for c in range(D // cw):
        acc = None
        for n in range(N_BRANCH):
            gate = _sigmoid(_dot(u, wg_ref[:, n * D + c * cw:n * D + (c + 1) * cw]))
            t = gate * _dot(ys[n], br_ref[n, :, c * cw:(c + 1) * cw])
            acc = t if acc is None else acc + t
        m_ref[:, c * cw:(c + 1) * cw] = acc.astype(BF16)
    o_ref[0] = x + gt_ref[0] * _dot(m_ref[...], wo_ref[...])


def merge(x, g, sh, sc, gt, p, y_hy, rw, lru, ret, ln_g, seg, w_gate, br, w_out, tm):
    B, T, D = x.shape
    vec = pl.BlockSpec((1, 1, D), lambda b, i: (b, 0, 0))
    row = pl.BlockSpec((1, tm, D), lambda b, i: (b, i, 0))
    brn = pl.BlockSpec((1, tm, BR_W), lambda b, i: (b, i, 0))
    half = pl.BlockSpec((1, tm, LANE_W), lambda b, i: (b, i, 0))
    tg = pl.BlockSpec((1, tm, BR_W), lambda b, i: (b, i, (OFF_RET + 3 * BR_W) // BR_W))
    return pl.pallas_call(
        _merge_kernel,
        grid=(B, T // tm),
        in_specs=[row, _const_spec((1, D)), vec, vec, vec, brn] + [half] * 4 + [brn] * 7 + [tg] + [
            _const_spec((1, BR_W)), _const_spec((BR_W, BR_W)), _const_spec((D, GATE_IN)),
            _const_spec((N_BRANCH, BR_W, D)), _const_spec((D, D))],
        out_specs=row,
        out_shape=jax.ShapeDtypeStruct((B, T, D), F32),
        scratch_shapes=[pltpu.VMEM((tm, D), BF16)],
        compiler_params=_cparams("parallel", "parallel"),
        name="merge",
    )(x, g, sh, sc, gt, y_hy, *rw, *lru, *ret, p, ln_g, seg, w_gate, br, w_out)


def _block_diag(w):
    G = w.shape[-3]
    eye = jnp.eye(G, dtype=w.dtype)
    full = w[..., :, :, None, :] * eye[:, None, :, None]
    return full.reshape(*w.shape[:-3], G * w.shape[-2], G * w.shape[-1])


def _mixers(p, lp, states, on_grid, with_output, tiles):
    B, T, _ = p.shape
    tm, tt, _ = tiles
    r, v, na, w, kt, b, bonus, g = rwkv_prep(p, *lp['rw'], on_grid, tm)
    y_f, y_b, rw_fin = rwkv_mix(tuple(time_major(t, min(T, 256)) for t in (r, v, na, w, kt, b)), states[0], tt)
    a, bb, gg = lru_prep(p, *lp['lru'], tm)
    h_f, h_b, lru_fin = lru_scan(a, bb, states[1], tm)
    cos, sin = (jnp.asarray(t) for t in _rope_tables(T)) if on_grid else (None, None)
    t_f, t_b, ret_fin = retention(p, cos, sin, *lp['ret'], states[2], on_grid)
    fins = (rw_fin, lru_fin, ret_fin)
    if not with_output:
        return None, fins
    vg = hyena_prep(p, *lp['hy_conv'], min(T, 2048))
    y_hy = hyena_branch(vg[0], vg[1], vg[2], lp['hy'])
    return (y_hy, (y_f[0], y_f[1], y_b[0], y_b[1], bonus, g), (h_f, h_b, gg), (t_f, t_b)), fins


def kernel(x, c, ctx, c_ctx, w_mod, b_mod, norm1_g, norm2_g, w_in, hy_conv_w, hy_conv_b, hy_f_w1, hy_f_b1, hy_f_w2, hy_f_b2, hy_f_w3, hy_freq, hy_bias, rw_mu, rw_w0, rw_w2, rw_a0, rw_a2, rw_g2, rw_kk, rw_ka, rw_rk, rw_ln_g, lru_conv_w, lru_conv_b, lru_wa, lru_ba, lru_wx, lru_bx, lru_lam, ret_gamma, br_proj, w_out, ffn_w1, ffn_w2, final_g):
    B, T, D = x.shape
    TC = ctx.shape[1]
    L = w_in.shape[0]

    s0, s1, s2, s3 = HY_IN, HY_IN + RW_IN, HY_IN + RW_IN + LRU_IN, HY_IN + RW_IN + LRU_IN + RET_IN
    w_branch = jnp.concatenate([w_in[:, :, s0:s1], jnp.zeros((L, D, RW_PAD - RW_IN), w_in.dtype),
                                w_in[:, :, s2:s3], jnp.zeros((L, D, HY_GAP), w_in.dtype),
                                w_in[:, :, 0:s0], w_in[:, :, s1:s2]], axis=2).astype(BF16)
    w_gate = w_in[:, :, s3:].astype(BF16)
    mu = jnp.pad(rw_mu, ((0, 0), (0, RW_PAD - RW_IN)))[:, None, :]
    w_lora = jnp.zeros((L, BR_W, 5 * BR_W), F32)
    w_lora = w_lora.at[:, 0:RW_LORA, 0:BR_W].set(rw_w2[:, 0]).at[:, 0:RW_LORA, BR_W:2 * BR_W].set(rw_w2[:, 1])
    w_lora = w_lora.at[:, RW_LORA:2 * RW_LORA, 2 * BR_W:3 * BR_W].set(rw_a2[:, 0])
    w_lora = w_lora.at[:, RW_LORA:2 * RW_LORA, 3 * BR_W:4 * BR_W].set(rw_a2[:, 1])
    w_lora = w_lora.at[:, 2 * RW_LORA:3 * RW_LORA, 4 * BR_W:5 * BR_W].set(rw_g2).astype(BF16)
    kvec = jnp.stack([rw_kk, rw_ka, rw_rk], axis=1)
    seg = jnp.asarray(np.kron(np.eye(RW_HEADS), np.ones((RW_HEAD, RW_HEAD))), BF16)
    lru_w = jnp.concatenate([_block_diag(lru_wa[:, 0]), _block_diag(lru_wx[:, 0]),
                             _block_diag(lru_wa[:, 1]), _block_diag(lru_wx[:, 1])], axis=2).astype(BF16)
    lru_bias = jnp.stack([lru_ba[:, 0], lru_bx[:, 0], lru_ba[:, 1], lru_bx[:, 1]], axis=1)
    glane = jnp.repeat(ret_gamma, RET_HEAD, axis=2)
    gtile = jnp.broadcast_to(ret_gamma[:, :, :, None, None], (L, 2, RET_HEADS, 8, RET_CHUNK))
    f_w1 = jnp.pad(hy_f_w1, ((0, 0), (0, HY_FEAT_PAD - HY_FEAT), (0, 0)))
    br_b = br_proj.astype(BF16)
    w_out_b = w_out.astype(BF16)
    ffn_gate = ffn_w1[:, :, :D_FF].astype(BF16)
    ffn_up_w = ffn_w1[:, :, D_FF:].astype(BF16)
    ffn_w2_b = ffn_w2.astype(BF16)

    cc = jnp.concatenate([c, c_ctx[None, :], jnp.zeros((16 - B - 1, D), F32)], axis=0)
    mods = modulation(cc, w_mod.astype(BF16), b_mod[:, None, :])

    zero_states = (jnp.zeros((RW_VH // 8, RW_HEAD, 8, 16 * B), F32), jnp.zeros((2, B, 1, BR_W), F32),
                   jnp.zeros((2, B, BR_W, BR_W), F32))
    xc = ctx
    for l in range(L):
        last = l == L - 1
        lp = {
            'rw': (mu[l], w_lora[l], rw_w0[l], rw_a0[l], kvec[l], seg),
            'lru': (lru_conv_w[l], lru_conv_b[l][None, :], lru_w[l], lru_bias[l], lru_lam[l]),
            'ret': (glane[l], gtile[l]),
            'hy_conv': (hy_conv_w[l], hy_conv_b[l][None, :]),
            'hy': (f_w1[l], hy_f_b1[l][None, :], hy_f_w2[l], hy_f_b2[l][None, :], hy_f_w3[l], hy_freq[l],
                   hy_bias[l]),
        }
        g1 = norm1_g[l][None, :]
        g2 = norm2_g[l][None, :]
        m_lat = [m[:, None, :] for m in jnp.split(mods[l, :B], 6, axis=-1)]
        m_ctx = [jnp.broadcast_to(m[None, :, :], (B, 1, D)) for m in jnp.split(mods[l, B:B + 1], 6, axis=-1)]
        ln_g = rw_ln_g[l][None, :]

        def layer(xs, m, states, on_grid, with_output, tiles, final):
            p = in_projection(xs, g1, m[0], m[1], w_branch[l], tiles[0])
            br, fins = _mixers(p, lp, states, on_grid, with_output, tiles)
            if not with_output:
                return None, fins
            xs = merge(xs, g1, m[0], m[1], m[2], p, br[0], br[1], br[2], br[3], ln_g, seg,
                       w_gate[l], br_b[l], w_out_b[l], tiles[2])
            h = ffn_up(xs, g2, m[3], m[4], ffn_gate[l], ffn_up_w[l], tiles[0])
            xs = ffn_down(h, xs, m[5], ffn_w2_b[l], final_g[None, :], final, tiles[0])
            return xs, fins

        xc_new, ctx_states = layer(xc, m_ctx, zero_states, False, not last, (TC, 64, TC), False)
        x, _ = layer(x, m_lat, ctx_states, True, True, (512, 128, 512), last)
        if not last:
            xc = xc_new
    return x
```

```python
import functools
import math

import numpy as np
import jax
import jax.numpy as jnp
from jax import lax
from jax.experimental import pallas as pl
from jax.experimental.pallas import tpu as pltpu

F32 = jnp.float32
BF16 = jnp.bfloat16

D_MODEL = 1024
DEPTH = 4
GRID_W = 64
N_BRANCH = 4
BR_W = D_MODEL // N_BRANCH

HY_BANDS = 8
HY_FEAT = 1 + 2 * HY_BANDS
HY_FEAT_PAD = 32
HY_HID = 64
HY_TARGET = 1e-2
HY_FAST = 0.3
HY_SLOW = 1.5
HY_IN = 3 * BR_W

RW_HEAD = 64
RW_HEADS = BR_W // RW_HEAD
RW_LORA = 64
RW_IN = 3 * BR_W + 3 * RW_LORA
RW_PAD = 4 * BR_W
RW_LN_EPS = 64e-5

LRU_BLOCKS = 4
LRU_BLOCK = BR_W // LRU_BLOCKS
LRU_C = 8.0
LRU_IN = 2 * BR_W

RET_HEADS = 4
RET_HEAD = BR_W // RET_HEADS
RET_CHUNK = 128
ROPE_BASE = 10000.0
RET_IN = 4 * BR_W
RET_LN_EPS = 1e-5

GATE_IN = N_BRANCH * D_MODEL
D_FF = ((8 * D_MODEL // 3 + 255) // 256) * 256
EPS = 1e-6

OFF_RW = 0
OFF_RET = OFF_RW + RW_PAD
OFF_HY = 3 * HY_IN
OFF_LRU = OFF_HY + HY_IN
N_BR = OFF_LRU + LRU_IN
HY_GAP = OFF_HY - (OFF_RET + RET_IN)

VMEM_LIMIT = 56 * 1024 * 1024
HI = lax.Precision.HIGHEST


def _cparams(*sem):
    return pltpu.CompilerParams(dimension_semantics=sem, vmem_limit_bytes=VMEM_LIMIT)


def _const_spec(shape):
    nd = len(shape)
    return pl.BlockSpec(shape, lambda *_: (0,) * nd, pipeline_mode=pl.Buffered(1))


def _dot(a, b, **kw):
    return jnp.dot(a, b, preferred_element_type=F32, **kw)


def _norm_mod(x, g, sh, sc):
    ms = jnp.mean(x * x, axis=-1, keepdims=True)
    return x * lax.rsqrt(ms + EPS) * g * (1.0 + sc) + sh


def _sigmoid(x):
    return 1.0 / (1.0 + jnp.exp(-x))


def _silu(x):
    return x * _sigmoid(x)


def _softplus(x):
    return jnp.maximum(x, 0.0) + jnp.log(1.0 + jnp.exp(-jnp.abs(x)))


def _mod_kernel(c_ref, w_ref, b_ref, o_ref):
    c = c_ref[...]
    o_ref[0] = _dot(_silu(c).astype(BF16), w_ref[0]) + b_ref[0]


def modulation(cc, w_mod, b_mod):
    L, D, N = w_mod.shape
    tn = 1536
    return pl.pallas_call(
        _mod_kernel,
        grid=(L, N // tn),
        in_specs=[pl.BlockSpec((16, D), lambda l, j: (0, 0)),
                  pl.BlockSpec((1, D, tn), lambda l, j: (l, 0, j)),
                  pl.BlockSpec((1, 1, tn), lambda l, j: (l, 0, j))],
        out_specs=pl.BlockSpec((1, 16, tn), lambda l, j: (l, 0, j)),
        out_shape=jax.ShapeDtypeStruct((L, 16, N), F32),
        compiler_params=_cparams("parallel", "parallel"),
        name="modulation",
    )(cc, w_mod, b_mod)


def _inproj_kernel(x_ref, g_ref, sh_ref, sc_ref, w_ref, o_ref):
    u = _norm_mod(x_ref[0], g_ref[...], sh_ref[0], sc_ref[0]).astype(BF16)
    n = w_ref.shape[1]
    cw = 256
    for c in range(n // cw):
        o_ref[0, :, c * cw:(c + 1) * cw] = _dot(u, w_ref[:, c * cw:(c + 1) * cw])


def in_projection(x, g, sh, sc, w, tm):
    B, T, D = x.shape
    N = w.shape[1]
    vec = pl.BlockSpec((1, 1, D), lambda b, i: (b, 0, 0))
    return pl.pallas_call(
        _inproj_kernel,
        grid=(B, T // tm),
        in_specs=[pl.BlockSpec((1, tm, D), lambda b, i: (b, i, 0)),
                  _const_spec((1, D)), vec, vec, _const_spec((D, N))],
        out_specs=pl.BlockSpec((1, tm, N), lambda b, i: (b, i, 0)),
        out_shape=jax.ShapeDtypeStruct((B, T, N), F32),
        compiler_params=_cparams("parallel", "parallel"),
        name="in_projection",
    )(x, g, sh, sc, w)


def _ffn1_kernel(x_ref, g_ref, sh_ref, sc_ref, wg_ref, wu_ref, o_ref):
    u = _norm_mod(x_ref[0], g_ref[...], sh_ref[0], sc_ref[0]).astype(BF16)
    n = wg_ref.shape[1]
    cw = 256
    for c in range(n // cw):
        sl = slice(c * cw, (c + 1) * cw)
        gate = _dot(u, wg_ref[:, sl])
        up = _dot(u, wu_ref[:, sl])
        o_ref[0, :, sl] = (_silu(gate) * up).astype(BF16)


def ffn_up(x, g, sh, sc, w_gate, w_up, tm):
    B, T, D = x.shape
    N = w_gate.shape[1]
    vec = pl.BlockSpec((1, 1, D), lambda b, i: (b, 0, 0))
    return pl.pallas_call(
        _ffn1_kernel,
        grid=(B, T // tm),
        in_specs=[pl.BlockSpec((1, tm, D), lambda b, i: (b, i, 0)),
                  _const_spec((1, D)), vec, vec, _const_spec((D, N)), _const_spec((D, N))],
        out_specs=pl.BlockSpec((1, tm, N), lambda b, i: (b, i, 0)),
        out_shape=jax.ShapeDtypeStruct((B, T, N), BF16),
        compiler_params=_cparams("parallel", "parallel"),
        name="ffn_up",
    )(x, g, sh, sc, w_gate, w_up)


def _ffn2_kernel(h_ref, x_ref, gt_ref, w_ref, fg_ref, o_ref, *, final_norm):
    y = x_ref[0] + gt_ref[0] * _dot(h_ref[0], w_ref[...])
    if final_norm:
        ms = jnp.mean(y * y, axis=-1, keepdims=True)
        y = y * lax.rsqrt(ms + EPS) * fg_ref[...]
    o_ref[0] = y


def ffn_down(h, x, gate, w, final_g, final_norm, tm):
    B, T, D = x.shape
    N = h.shape[2]
    return pl.pallas_call(
        functools.partial(_ffn2_kernel, final_norm=final_norm),
        grid=(B, T // tm),
        in_specs=[pl.BlockSpec((1, tm, N), lambda b, i: (b, i, 0)),
                  pl.BlockSpec((1, tm, D), lambda b, i: (b, i, 0)),
                  pl.BlockSpec((1, 1, D), lambda b, i: (b, 0, 0)),
                  _const_spec((N, D)), _const_spec((1, D))],
        out_specs=pl.BlockSpec((1, tm, D), lambda b, i: (b, i, 0)),
        out_shape=jax.ShapeDtypeStruct((B, T, D), F32),
        compiler_params=_cparams("parallel", "parallel"),
        name="ffn_down",
    )(h, x, gate, w, final_g)


def _halo_specs(tm, T, width, col_fn, halo):
    r = tm // halo
    last = T // halo - 1
    main = pl.BlockSpec((1, tm, width), lambda *g: (g[0], g[1], col_fn(*g)))
    prev = pl.BlockSpec((1, halo, width), lambda *g: (g[0], jnp.maximum(g[1] * r - 1, 0), col_fn(*g)))
    nxt = pl.BlockSpec((1, halo, width), lambda *g: (g[0], jnp.minimum((g[1] + 1) * r, last), col_fn(*g)))
    return main, prev, nxt


def _shift_rows(x, prev, nxt, s, first, last):
    tm = x.shape[0]
    if s > 0:
        head = jnp.where(first, 0.0, prev[prev.shape[0] - s:, :])
        return head if s == tm else jnp.concatenate([head, x[:tm - s, :]], axis=0)
    s = -s
    tail = jnp.where(last, 0.0, nxt[:s, :])
    return tail if s == tm else jnp.concatenate([x[s:, :], tail], axis=0)


def _hy_prep_kernel(p_ref, pp_ref, pn_ref, w_ref, b_ref, v_ref, g1_ref, g2_ref):
    i = pl.program_id(1)
    first = i == 0
    last = i == pl.num_programs(1) - 1
    x = p_ref[0]
    xm = _shift_rows(x, pp_ref[0], pn_ref[0], 1, first, last)
    xp = _shift_rows(x, pp_ref[0], pn_ref[0], -1, first, last)
    u = b_ref[...] + w_ref[0:1, :] * xm + w_ref[1:2, :] * x + w_ref[2:3, :] * xp
    v_ref[0] = u[:, 0:BR_W]
    g1_ref[0] = u[:, BR_W:2 * BR_W]
    g2_ref[0] = u[:, 2 * BR_W:3 * BR_W]


def hyena_prep(p, conv_w, conv_b, tm):
    B, T, _ = p.shape
    main, prev, nxt = _halo_specs(tm, T, HY_IN, lambda b, i: OFF_HY // HY_IN, 8)
    out = pl.BlockSpec((1, tm, BR_W), lambda b, i: (b, i, 0))
    shp = jax.ShapeDtypeStruct((B, T, BR_W), F32)
    return pl.pallas_call(
        _hy_prep_kernel,
        grid=(B, T // tm),
        in_specs=[main, prev, nxt, _const_spec((3, HY_IN)), _const_spec((1, HY_IN))],
        out_specs=[out, out, out],
        out_shape=[shp, shp, shp],
        compiler_params=_cparams("parallel", "parallel"),
        name="hyena_prep",
    )(p, p, p, conv_w, conv_b)


def _hy_filter_kernel(feat_ref, w1_ref, b1_ref, w2_ref, b2_ref, w3_ref, fq_ref, rates_ref, h_ref, ss_ref):
    i = pl.program_id(0)
    feat = feat_ref[...]
    t = feat[:, 0:1]
    h = jnp.sin(fq_ref[0:1, :] * (_dot(feat, w1_ref[...], precision=HI) + b1_ref[...]))
    h = jnp.sin(fq_ref[1:2, :] * (_dot(h, w2_ref[...], precision=HI) + b2_ref[...]))
    h = _dot(h, w3_ref[...], precision=HI) * jnp.exp(-t * rates_ref[...])
    row = lax.broadcasted_iota(jnp.int32, h.shape, 0) + i * h.shape[0]
    col = lax.broadcasted_iota(jnp.int32, h.shape, 1)
    h = jnp.where((row == 0) & ((col // BR_W) % 2 == 1), 0.0, h)
    h_ref[...] = h

    @pl.when(i == 0)
    def _():
        ss_ref[...] = jnp.zeros_like(ss_ref)

    ss_ref[...] += jnp.sum(h * h, axis=0, keepdims=True)


def hyena_filter(feat, w1, b1, w2, b2, w3, freq, rates, tl):
    L = feat.shape[0]
    C = w3.shape[1]
    return pl.pallas_call(
        _hy_filter_kernel,
        grid=(L // tl,),
        in_specs=[pl.BlockSpec((tl, HY_FEAT_PAD), lambda i: (i, 0)),
                  _const_spec((HY_FEAT_PAD, HY_HID)), _const_spec((1, HY_HID)),
                  _const_spec((HY_HID, HY_HID)), _const_spec((1, HY_HID)),
                  _const_spec((HY_HID, C)), _const_spec((2, HY_HID)), _const_spec((1, C))],
        out_specs=[pl.BlockSpec((tl, C), lambda i: (i, 0)), pl.BlockSpec((1, C), lambda i: (0, 0))],
        out_shape=[jax.ShapeDtypeStruct((L, C), F32), jax.ShapeDtypeStruct((1, C), F32)],
        compiler_params=_cparams("arbitrary"),
        name="hyena_filter",
    )(feat, w1, b1, w2, b2, w3, freq, rates)


def _filter_scale(ss_ref, o):
    e = ss_ref[:, 2 * o * BR_W:(2 * o + 1) * BR_W] + ss_ref[:, (2 * o + 1) * BR_W:(2 * o + 2) * BR_W]
    return lax.rsqrt(e + EPS)


def _combine_spectrum(x, ss_ref, o, half):
    xf = x[:, 2 * o * BR_W:(2 * o + 1) * BR_W]
    xb = x[:, (2 * o + 1) * BR_W:(2 * o + 2) * BR_W]
    sc = _filter_scale(ss_ref, o)
    hr = (xf[:half] + xb[:half]) * sc
    hi = (xf[half:] - xb[half:]) * sc
    return jnp.concatenate([hr, hi], axis=0)


def _cmul(x, h, half):
    xr, xi = x[:half], x[half:]
    hr, hi = h[:half], h[half:]
    return jnp.concatenate([xr * hr - xi * hi, xr * hi + xi * hr], axis=0)


def _dft1_kernel(z_ref, f_ref, a_ref):
    a_ref[0] = _dot(f_ref[...], z_ref[0].astype(BF16)).astype(BF16)


def dft_stage1(z, f1, tn):
    B, n1, W = z.shape
    M = f1.shape[0]
    return pl.pallas_call(
        _dft1_kernel,
        grid=(B, W // tn),
        in_specs=[pl.BlockSpec((1, n1, tn), lambda b, j: (b, 0, j)), _const_spec((M, n1))],
        out_specs=pl.BlockSpec((1, M, tn), lambda b, j: (b, 0, j)),
        out_shape=jax.ShapeDtypeStruct((B, M, W), BF16),
        compiler_params=_cparams("parallel", "parallel"),
        name="dft_stage1",
    )(z, f1)


def _spec2_kernel(a_ref, g_ref, ss_ref, h_ref):
    kb = g_ref.shape[0]
    n2 = a_ref.shape[3]
    for k in range(kb):
        a = a_ref[0, :, k].reshape(2 * n2, a_ref.shape[4])
        x = _dot(g_ref[k], a)
        for o in range(2):
            h_ref[o, k] = _combine_spectrum(x, ss_ref, o, n2)


def filter_spectrum(a, g, ss, kb):
    _, _, N1, N2, C = a.shape
    return pl.pallas_call(
        _spec2_kernel,
        grid=(N1 // kb,),
        in_specs=[pl.BlockSpec((1, 2, kb, N2, C), lambda i: (0, 0, i, 0, 0)),
                  pl.BlockSpec((kb, 2 * N2, 2 * N2), lambda i: (i, 0, 0)),
                  _const_spec((1, C))],
        out_specs=pl.BlockSpec((2, kb, 2 * N2, BR_W), lambda i: (0, i, 0, 0)),
        out_shape=jax.ShapeDtypeStruct((2, N1, 2 * N2, BR_W), F32),
        compiler_params=_cparams("parallel"),
        name="filter_spectrum",
    )(a, g, ss)


def _conv2_kernel(a_ref, g_ref, gi_ref, h_ref, o_ref):
    kb = g_ref.shape[0]
    n2 = a_ref.shape[3]
    C = a_ref.shape[4]
    for k in range(kb):
        a = a_ref[0, :, k].reshape(2 * n2, C)
        y = _cmul(_dot(g_ref[k], a), h_ref[0, k], n2).astype(BF16)
        o_ref[0, :, k] = _dot(gi_ref[k], y).astype(BF16).reshape(2, n2, C)


def spectral_multiply(a, g, gi, h, o, kb):
    B, _, N1, N2, C = a.shape
    blk = pl.BlockSpec((1, 2, kb, N2, C), lambda i, b: (b, 0, i, 0, 0))
    mat = pl.BlockSpec((kb, 2 * N2, 2 * N2), lambda i, b: (i, 0, 0))
    return pl.pallas_call(
        _conv2_kernel,
        grid=(N1 // kb, B),
        in_specs=[blk, mat, mat, pl.BlockSpec((1, kb, 2 * N2, C), lambda i, b: (o, i, 0, 0))],
        out_specs=blk,
        out_shape=jax.ShapeDtypeStruct(a.shape, BF16),
        compiler_params=_cparams("parallel", "parallel"),
        name="spectral_multiply",
    )(a, g, gi, h)


def _idft1_kernel(b_ref, f_ref, z_ref, gate_ref, bias_ref, f1_ref, o_ref, *a_ref):
    y = _dot(f_ref[...], b_ref[0])
    z = z_ref[0]
    out = gate_ref[0] * (y + bias_ref[...] * z)
    o_ref[0] = out
    if a_ref:
        a_ref[0][0] = _dot(f1_ref[...], out.astype(BF16)).astype(BF16)


def idft_stage1(bm, fi, z, gate, bias, f1, tn, with_next):
    B, M, W = bm.shape
    n1 = fi.shape[0]
    blk = pl.BlockSpec((1, n1, tn), lambda b, j: (b, 0, j))
    wide = pl.BlockSpec((1, M, tn), lambda b, j: (b, 0, j))
    out_specs, out_shape = [blk], [jax.ShapeDtypeStruct((B, n1, W), F32)]
    if with_next:
        out_specs.append(wide)
        out_shape.append(jax.ShapeDtypeStruct((B, M, W), BF16))
    return pl.pallas_call(
        _idft1_kernel,
        grid=(B, W // tn),
        in_specs=[wide, _const_spec((n1, M)), blk, blk, _const_spec((1, tn)), _const_spec((M, n1))],
        out_specs=out_specs,
        out_shape=out_shape,
        compiler_params=_cparams("parallel", "parallel"),
        name="idft_stage1",
    )(bm, fi, z, gate, bias, f1)


@functools.lru_cache(maxsize=None)
def _dft_tables(L):
    N = 2 * L
    N2 = 128
    N1 = N // N2
    nz = L // N2
    k1 = np.arange(N1)[:, None]
    n1 = np.arange(nz)[None, :]
    th = 2 * np.pi * ((k1 * n1) % N1) / N1
    f1 = np.concatenate([np.cos(th), -np.sin(th)], axis=0)
    fi = np.concatenate([np.cos(th).T, -np.sin(th).T], axis=1) / N
    kk1 = np.arange(N1)[:, None, None]
    k2 = np.arange(N2)[None, :, None]
    n2 = np.arange(N2)[None, None, :]
    ph = 2 * np.pi * ((n2 * k2 * N1 + n2 * kk1) % N) / N
    gr, gim = np.cos(ph), -np.sin(ph)
    g = np.concatenate([np.concatenate([gr, -gim], axis=2), np.concatenate([gim, gr], axis=2)], axis=1)
    hr, him = np.swapaxes(gr, 1, 2), -np.swapaxes(gim, 1, 2)
    gi = np.concatenate([np.concatenate([hr, -him], axis=2), np.concatenate([him, hr], axis=2)], axis=1)
    return tuple(np.asarray(t, np.float32) for t in (f1, fi, g, gi))


def _spec_direct_kernel(hf_ref, f_ref, ss_ref, h_ref):
    x = _dot(f_ref[...], hf_ref[...].astype(BF16))
    half = x.shape[0] // 2
    for o in range(2):
        h_ref[o] = _combine_spectrum(x, ss_ref, o, half)


def filter_spectrum_direct(hf, f, ss):
    L, C = hf.shape
    return pl.pallas_call(
        _spec_direct_kernel,
        grid=(1,),
        in_specs=[_const_spec((L, C)), _const_spec((4 * L, L)), _const_spec((1, C))],
        out_specs=pl.BlockSpec((2, 4 * L, BR_W), lambda i: (0, 0, 0)),
        out_shape=jax.ShapeDtypeStruct((2, 4 * L, BR_W), F32),
        compiler_params=_cparams("arbitrary"),
        name="filter_spectrum_direct",
    )(hf, f, ss)


def _conv_direct_kernel(z_ref, gate_ref, bias_ref, f_ref, fi_ref, h_ref, o_ref):
    z = z_ref[0]
    x = _dot(f_ref[...], z.astype(BF16))
    y = _cmul(x, h_ref[0], x.shape[0] // 2).astype(BF16)
    o_ref[0] = gate_ref[0] * (_dot(fi_ref[...], y) + bias_ref[...] * z)


def conv_direct(z, gate, bias, f, fi, h, o):
    B, L, C = z.shape
    blk = pl.BlockSpec((1, L, C), lambda b: (b, 0, 0))
    return pl.pallas_call(
        _conv_direct_kernel,
        grid=(B,),
        in_specs=[blk, blk, _const_spec((1, C)), _const_spec((4 * L, L)), _const_spec((L, 4 * L)),
                  pl.BlockSpec((1, 4 * L, C), lambda b: (o, 0, 0))],
        out_specs=blk,
        out_shape=jax.ShapeDtypeStruct((B, L, C), F32),
        compiler_params=_cparams("parallel"),
        name="conv_direct",
    )(z, gate, bias, f, fi, h)


@functools.lru_cache(maxsize=None)
def _dft_direct_tables(L):
    N = 2 * L
    k = np.arange(N)[:, None]
    n = np.arange(L)[None, :]
    th = 2 * np.pi * ((k * n) % N) / N
    f = np.concatenate([np.cos(th), -np.sin(th)], axis=0)
    fi = np.concatenate([np.cos(th).T, -np.sin(th).T], axis=1) / N
    return np.asarray(f, np.float32), np.asarray(fi, np.float32)


@functools.lru_cache(maxsize=None)
def _filter_features(L):
    t = np.arange(L, dtype=np.float32) / np.float32(L)
    ang = (2.0 * math.pi) * t[:, None].astype(np.float64) * np.arange(1, HY_BANDS + 1)
    feat = np.zeros((L, HY_FEAT_PAD), np.float32)
    feat[:, 0] = t
    feat[:, 1:1 + HY_BANDS] = np.sin(ang)
    feat[:, 1 + HY_BANDS:HY_FEAT] = np.cos(ang)
    rates = np.abs(np.linspace(math.log(HY_TARGET) / HY_SLOW, math.log(HY_TARGET) / HY_FAST, BR_W))
    return feat, np.tile(np.asarray(rates, np.float32), 4)[None, :]


def hyena_branch(v, g1, g2, hp):
    f_w1, f_b1, f_w2, f_b2, f_w3, freq, bias = hp
    B, L, C = v.shape
    feat, rates = _filter_features(L)
    hf, ss = hyena_filter(jnp.asarray(feat), f_w1, f_b1, f_w2, f_b2, f_w3, freq, jnp.asarray(rates),
                          min(L, 512))
    if L <= 512:
        f, fi = (jnp.asarray(t).astype(BF16) for t in _dft_direct_tables(L))
        spec = filter_spectrum_direct(hf, f, ss)
        z = v
        for o, gate in enumerate((g1, g2)):
            z = conv_direct(z, gate, bias[o:o + 1], f, fi, spec, o)
        return z
    f1, fi1, g, gi = (jnp.asarray(t).astype(BF16) for t in _dft_tables(L))
    N2 = 128
    N1 = 2 * L // N2
    nz = L // N2
    kb = 8
    a = dft_stage1(hf.reshape(1, nz, N2 * 4 * C), f1, 8192)
    spec = filter_spectrum(a.reshape(1, 2, N1, N2, 4 * C), g, ss, kb)
    W = N2 * C
    tn = 8192
    z = v.reshape(B, nz, W)
    a = dft_stage1(z, f1, tn)
    for o, gate in enumerate((g1, g2)):
        bm = spectral_multiply(a.reshape(B, 2, N1, N2, C), g, gi, spec, o, kb).reshape(B, 2 * N1, W)
        out = idft_stage1(bm, fi1, z, gate.reshape(B, nz, W), jnp.tile(bias[o:o + 1], (1, tn // C)), f1, tn, o == 0)
        z = out[0]
        if o == 0:
            a = out[1]
    return z.reshape(B, L, C)


def _head_sum(x, seg_ref):
    hi = x.astype(BF16)
    lo = (x - hi.astype(F32)).astype(BF16)
    return _dot(hi, seg_ref[...]) + _dot(lo, seg_ref[...])


def _rw_prep_kernel(p_ref, pp_ref, pn_ref, mu_ref, wl_ref, w0_ref, a0_ref, kv_ref, seg_ref, *outs, on_grid):
    for bi in range(p_ref.shape[0]):
        _rw_prep_one(bi, p_ref, pp_ref, pn_ref, mu_ref, wl_ref, w0_ref, a0_ref, kv_ref, seg_ref, outs, on_grid)


def _rw_prep_one(bi, p_ref, pp_ref, pn_ref, mu_ref, wl_ref, w0_ref, a0_ref, kv_ref, seg_ref, outs, on_grid):
    def put(pair, val):
        pair[0][:, bi, :] = val[:, 0:LANE_W]
        pair[1][:, bi, :] = val[:, LANE_W:2 * LANE_W]

    r_out, v_out, na_out = outs[0:2], outs[2:4], outs[4:6]
    w_out, kt_out, b_out = outs[6:10], outs[10:14], outs[14:18]
    bonus_ref, g_ref = outs[18], outs[19]
    i = pl.program_id(0)
    first = i == 0
    last = i == pl.num_programs(0) - 1
    x = p_ref[bi]
    tm = x.shape[0]
    grp = lax.broadcasted_iota(jnp.int32, x.shape, 1) % 4
    prev, nxt = pp_ref[bi], pn_ref[bi]
    if on_grid:
        col = (lax.broadcasted_iota(jnp.int32, (tm, 1), 0) + i * tm) % GRID_W
        left = jnp.where(col == 0, 0.0, pltpu.roll(x, 1, 0))
        right = jnp.where(col == GRID_W - 1, 0.0, pltpu.roll(x, tm - 1, 0))
        up = _shift_rows(x, prev, nxt, GRID_W, first, last)
        down = _shift_rows(x, prev, nxt, -GRID_W, first, last)
        shifted = jnp.where(grp == 0, left, jnp.where(grp == 1, right, jnp.where(grp == 2, up, down)))
    else:
        before = _shift_rows(x, prev, nxt, 1, first, last)
        after = _shift_rows(x, prev, nxt, -1, first, last)
        shifted = jnp.where(grp % 2 == 0, before, after)
    xx = x + (shifted - x) * mu_ref[...]
    r = xx[:, 0:BR_W]
    k = xx[:, BR_W:2 * BR_W]
    v = xx[:, 2 * BR_W:3 * BR_W]
    lo = xx[:, 3 * BR_W:4 * BR_W]
    ll = lax.broadcasted_iota(jnp.int32, lo.shape, 1)
    act = jnp.where(ll < RW_LORA, jnp.tanh(lo), jnp.where(ll < 2 * RW_LORA, lo, _sigmoid(lo)))
    z = _dot(act.astype(BF16), wl_ref[...])
    kk = k * kv_ref[0:1, :]
    kk = kk * lax.rsqrt(_head_sum(kk * kk, seg_ref) + 1e-12)
    put(r_out, r)
    put(v_out, v)
    put(na_out, -kk)
    bonus_ref[bi] = _head_sum(r * k * kv_ref[2:3, :], seg_ref) * v
    g_ref[bi] = z[:, 4 * BR_W:5 * BR_W]
    for d in range(2):
        logw = -_softplus(-(w0_ref[d:d + 1, :] + z[:, d * BR_W:(d + 1) * BR_W])) - 0.5
        put(w_out[2 * d:2 * d + 2], jnp.exp(-jnp.exp(logw)))
        a = _sigmoid(a0_ref[d:d + 1, :] + z[:, (2 + d) * BR_W:(3 + d) * BR_W])
        put(kt_out[2 * d:2 * d + 2], k * (1.0 + (a - 1.0) * kv_ref[1:2, :]))
        put(b_out[2 * d:2 * d + 2], kk * a)


def rwkv_prep(p, mu, w_lora, w0, a0, kvec, seg, on_grid, tm):
    B, T, _ = p.shape
    halo = GRID_W if on_grid else 8
    r = tm // halo
    last = T // halo - 1
    main = pl.BlockSpec((B, tm, RW_PAD), lambda i: (0, i, OFF_RW // RW_PAD))
    prev = pl.BlockSpec((B, halo, RW_PAD), lambda i: (0, jnp.maximum(i * r - 1, 0), OFF_RW // RW_PAD))
    nxt = pl.BlockSpec((B, halo, RW_PAD), lambda i: (0, jnp.minimum((i + 1) * r, last), OFF_RW // RW_PAD))
    half = pl.BlockSpec((tm, B, LANE_W), lambda i: (i, 0, 0))
    full = pl.BlockSpec((B, tm, BR_W), lambda i: (0, i, 0))
    sh = jax.ShapeDtypeStruct((T, B, LANE_W), F32)
    sf = jax.ShapeDtypeStruct((B, T, BR_W), F32)
    return pl.pallas_call(
        functools.partial(_rw_prep_kernel, on_grid=on_grid),
        grid=(T // tm,),
        in_specs=[main, prev, nxt, _const_spec((1, RW_PAD)), _const_spec((BR_W, 5 * BR_W)),
                  _const_spec((2, BR_W)), _const_spec((2, BR_W)), _const_spec((3, BR_W)),
                  _const_spec((BR_W, BR_W))],
        out_specs=[half] * 18 + [full, full],
        out_shape=[sh] * 18 + [sf, sf],
        compiler_params=_cparams("parallel"),
        name="rwkv_prep",
    )(p, p, p, mu, w_lora, w0, a0, kvec, seg)


RW_VQ = RW_HEAD // 4
RW_VH = 2 * RW_VQ
LANE_W = 128
N_KEYED = 5
RW_UNROLL = 2
RW_PARTIALS = 1


def _chain_rows(refs, t, tb, shift):
    blocks = []
    for d in range(2):
        for half in range(2):
            x = refs[2 * d + half][t if d == 0 else tb]
            blocks.append(x if shift == 0 else pltpu.roll(x, LANE_W - shift, 1))
    return blocks


def _rw_scan_kernel(*refs, B, Tt):
    nin = 4 * (N_KEYED + 1)
    ins = refs[:nin]
    s0_ref = refs[nin]
    y_refs = refs[nin + 1:nin + 5]
    s_ref = refs[nin + 5]
    first_scratch = nin + 6
    per_set = N_KEYED + 2
    sets = [refs[first_scratch + per_set * u:first_scratch + per_set * (u + 1)] for u in range(RW_UNROLL)]
    bufs = [(st[:N_KEYED], st[N_KEYED], st[N_KEYED + 1]) for st in sets]
    sa_ref = refs[first_scratch + per_set * RW_UNROLL]
    NL = 16 * B
    A_TILE = 3

    @pl.when(pl.program_id(0) == 0)
    def _():
        s_ref[...] = s0_ref[...]
        bufs[RW_UNROLL - 1][2][...] = jnp.zeros_like(bufs[RW_UNROLL - 1][2])

    def keyed_job(t, slot, n):
        def issue():
            rows = _chain_rows(ins[4 * n:4 * n + 4], t, Tt - 1 - t, 0)
            return jnp.concatenate(rows * 4, axis=0).T

        def commit(val):
            bufs[slot][0][n][...] = val
        return issue, commit

    def values_job(t, slot):
        def issue():
            rows = []
            for vq in range(4):
                rows += _chain_rows(ins[4 * N_KEYED:], t, Tt - 1 - t, vq * RW_VQ)
            vt = jnp.concatenate(rows, axis=0).T
            return jnp.concatenate([vt[0:RW_VQ], vt[RW_HEAD:RW_HEAD + RW_VQ]], axis=0)

        def commit(val):
            bufs[slot][1][...] = val
        return issue, commit

    def output_job(t, slot):
        def issue():
            ys = bufs[slot][2][...]
            pad = jnp.zeros((RW_HEAD - RW_VQ, NL), F32)
            y = jnp.concatenate([ys[0:RW_VQ], pad, ys[RW_VQ:RW_VH], pad], axis=0).T
            out = []
            for d in range(2):
                for half in range(2):
                    acc = None
                    for vq in range(4):
                        row0 = ((vq * 2 + d) * 2 + half) * B
                        blk = y[row0:row0 + B, :]
                        blk = blk if vq == 0 else pltpu.roll(blk, vq * RW_VQ, 1)
                        acc = blk if acc is None else acc + blk
                    out.append(acc)
            return out

        def commit(val):
            for d in range(2):
                for half in range(2):
                    y_refs[2 * d + half][:, t if d == 0 else Tt - 1 - t, :] = val[2 * d + half]
        return issue, commit

    groups = range(RW_VH // 8)
    others = tuple(n for n in range(N_KEYED) if n != A_TILE)
    never = pl.program_id(0) < 0

    def step_pieces(slot, sa_in, sa_out):
        tiles, vt_ref, ys_ref = bufs[slot]
        a_next = bufs[(slot + 1) % RW_UNROLL][0][A_TILE]
        row = lambda ref, k: ref[pl.ds(k, 1), :]
        ys = [[None] * RW_PARTIALS for _ in groups]
        san = [[None] * RW_PARTIALS for _ in groups]
        vt = {}
        acc = lambda lst, i, v: lst.__setitem__(i, v if lst[i] is None else lst[i] + v)

        def update(k0, k1):
            if not vt:
                for g in groups:
                    vt[g] = vt_ref[g * 8:(g + 1) * 8, :]
            for k in range(k0, k1):
                for h2 in range(2):
                    kk = h2 * RW_HEAD + k
                    r_k, w_k, kt_k, b_k = (row(tiles[n], kk) for n in (0, 1, 2, 4))
                    an_k = row(a_next, kk)
                    for g in (2 * h2, 2 * h2 + 1):
                        s = s_ref[g, k] * w_k + sa_in[g] * b_k + vt[g] * kt_k
                        s_ref[g, k] = s
                        acc(ys[g], k % RW_PARTIALS, s * r_k)
                        acc(san[g], k % RW_PARTIALS, s * an_k)
            if k1 == RW_HEAD:
                for g in groups:
                    ys_ref[g * 8:(g + 1) * 8, :] = functools.reduce(lambda a, b: a + b, ys[g])
                    sa_out.append(functools.reduce(lambda a, b: a + b, san[g]))

        def anchor(i, val):
            g = i % len(groups)
            if val is not None and ys[g][0] is not None:
                ys[g][0] = jnp.where(never, val, ys[g][0])

        cuts = [0, 9, 18, 27, 36, 45, 54, RW_HEAD]
        return [functools.partial(update, cuts[i], cuts[i + 1]) for i in range(len(cuts) - 1)], anchor

    def run(step, jobs):
        pieces, anchor = step
        pending = None
        for i, piece in enumerate(pieces):
            val = jobs[i][0]() if i < len(jobs) else None
            piece()
            if pending is not None and i + 1 < len(pieces):
                anchor(*pending)
            pending = None
            if i < len(jobs):
                jobs[i][1](val)
                if not isinstance(val, list):
                    pending = (i, val[val.shape[0] - 8:, :])
                elif val[3].shape == (8, NL):
                    pending = (i, val[3])

    for issue, commit in ([keyed_job(0, 0, n) for n in range(N_KEYED)] + [values_job(0, 0)]
                          + [keyed_job(1, 1, A_TILE)]):
        commit(issue())
    first = [[None, None] for _ in groups]
    for k in range(RW_HEAD):
        for g in groups:
            p = s_ref[g, k] * bufs[0][0][A_TILE][pl.ds((g // 2) * RW_HEAD + k, 1), :]
            first[g][k % 2] = p if first[g][k % 2] is None else first[g][k % 2] + p
    for g in groups:
        sa_ref[g] = first[g][0] + first[g][1]

    def body(j, carry):
        t0 = RW_UNROLL * j
        sa = [sa_ref[g] for g in groups]
        for u in range(RW_UNROLL):
            t = t0 + u
            t1 = jnp.minimum(t + 1, Tt - 1)
            jobs = [output_job(jnp.maximum(t - 1, 0), (u - 1) % RW_UNROLL),
                    keyed_job(jnp.minimum(t + 2, Tt - 1), (u + 2) % RW_UNROLL, A_TILE)]
            jobs += [keyed_job(t1, (u + 1) % RW_UNROLL, n) for n in others] + [values_job(t1, (u + 1) % RW_UNROLL)]
            sa_next = []
            run(step_pieces(u, sa, sa_next), jobs)
            sa = sa_next
        for g in groups:
            sa_ref[g] = sa[g]
        return carry

    lax.fori_loop(0, Tt // RW_UNROLL, body, 0)
    issue, commit = output_job(Tt - 1, RW_UNROLL - 1)
    commit(issue())


def rwkv_mix(prep, s0, tt):
    T, B, _ = prep[0][0].shape
    nb = T // tt
    tblk = lambda i, d: i if d == 0 else nb - 1 - i

    in_specs, args = [], []
    for quad in prep:
        for j, x in enumerate(quad):
            in_specs.append(pl.BlockSpec((tt, B, LANE_W), lambda i, d=j // 2: (tblk(i, d), 0, 0)))
            args.append(x)
    NL = 16 * B
    sblk = pl.BlockSpec((RW_VH // 8, RW_HEAD, 8, NL), lambda i: (0, 0, 0, 0))
    yspecs = [pl.BlockSpec((B, tt, LANE_W), lambda i, d=d: (0, tblk(i, d), 0)) for d in range(2) for _ in range(2)]
    ysh = jax.ShapeDtypeStruct((B, T, LANE_W), F32)
    out = pl.pallas_call(
        functools.partial(_rw_scan_kernel, B=B, Tt=tt),
        grid=(nb,),
        in_specs=in_specs + [sblk],
        out_specs=yspecs + [sblk],
        out_shape=[ysh] * 4 + [jax.ShapeDtypeStruct((RW_VH // 8, RW_HEAD, 8, NL), F32)],
        scratch_shapes=([pltpu.VMEM((LANE_W, NL), F32)] * N_KEYED + [pltpu.VMEM((RW_VH, NL), F32)] * 2) * RW_UNROLL
        + [pltpu.VMEM((RW_VH // 8, 8, NL), F32)],
        compiler_params=_cparams("arbitrary"),
        name="rwkv_scan",
    )(*args, s0)
    return (out[0], out[1]), (out[2], out[3]), out[4]


def _gelu_tanh(x):
    return 0.5 * x * (1.0 + jnp.tanh(math.sqrt(2.0 / math.pi) * (x + 0.044715 * (x * x * x))))


def _lru_prep_kernel(p_ref, pp_ref, pn_ref, cw_ref, cb_ref, w_ref, bias_ref, lam_ref, a_ref, b_ref, gg_ref):
    i = pl.program_id(1)
    first = i == 0
    last = i == pl.num_programs(1) - 1
    x = p_ref[0][:, 0:BR_W]
    prev = pp_ref[0][:, 0:BR_W]
    nxt = pn_ref[0][:, 0:BR_W]
    xc = cb_ref[...] + cw_ref[0:1, :] * _shift_rows(x, prev, nxt, 2, first, last)
    xc = xc + cw_ref[1:2, :] * _shift_rows(x, prev, nxt, 1, first, last)
    xc = xc + cw_ref[2:3, :] * x
    xc = xc + cw_ref[3:4, :] * _shift_rows(x, prev, nxt, -1, first, last)
    z = _dot(xc.astype(BF16), w_ref[...])
    for d in range(2):
        r = _sigmoid(z[:, 2 * d * BR_W:(2 * d + 1) * BR_W] + bias_ref[2 * d:2 * d + 1, :])
        gi = _sigmoid(z[:, (2 * d + 1) * BR_W:(2 * d + 2) * BR_W] + bias_ref[2 * d + 1:2 * d + 2, :])
        log_a = -LRU_C * r * _softplus(-lam_ref[d:d + 1, :])
        a_ref[d, 0] = jnp.exp(log_a)
        b_ref[d, 0] = jnp.sqrt(1.0 - jnp.exp(2.0 * log_a)) * (gi * xc)
    gg_ref[0] = _gelu_tanh(p_ref[0][:, BR_W:2 * BR_W])


def lru_prep(p, conv_w, conv_b, w_blk, bias, lam, tm):
    B, T, _ = p.shape
    main, prev, nxt = _halo_specs(tm, T, LRU_IN, lambda b, i: OFF_LRU // LRU_IN, 8)
    two = pl.BlockSpec((2, 1, tm, BR_W), lambda b, i: (0, b, i, 0))
    s2 = jax.ShapeDtypeStruct((2, B, T, BR_W), F32)
    return pl.pallas_call(
        _lru_prep_kernel,
        grid=(B, T // tm),
        in_specs=[main, prev, nxt, _const_spec((4, BR_W)), _const_spec((1, BR_W)),
                  _const_spec((BR_W, 4 * BR_W)), _const_spec((4, BR_W)), _const_spec((2, BR_W))],
        out_specs=[two, two, pl.BlockSpec((1, tm, BR_W), lambda b, i: (b, i, 0))],
        out_shape=[s2, s2, jax.ShapeDtypeStruct((B, T, BR_W), F32)],
        compiler_params=_cparams("parallel", "parallel"),
        name="lru_prep",
    )(p, p, p, conv_w, conv_b, w_blk, bias, lam)


def _affine_scan(a, b, reverse):
    tb = a.shape[0]
    row = lax.broadcasted_iota(jnp.int32, (tb, 1), 0)
    s = 1
    while s < tb:
        sh = tb - s if reverse else s
        ok = (row < tb - s) if reverse else (row >= s)
        a_s = pltpu.roll(a, sh, 0)
        b_s = pltpu.roll(b, sh, 0)
        b = jnp.where(ok, a * b_s + b, b)
        a = jnp.where(ok, a * a_s, a)
        s *= 2
    return a, b


def _lru_scan_kernel(af_ref, bf_ref, ab_ref, bb_ref, h0_ref, hf_ref, hb_ref, fin_ref):
    @pl.when(pl.program_id(1) == 0)
    def _():
        fin_ref[...] = h0_ref[...]

    tb = af_ref.shape[2]
    a, b = _affine_scan(af_ref[0, 0], bf_ref[0, 0], False)
    h = b + a * fin_ref[0, 0]
    hf_ref[0] = h
    fin_ref[0, 0] = h[tb - 1:tb, :]
    a, b = _affine_scan(ab_ref[0, 0], bb_ref[0, 0], True)
    h = b + a * fin_ref[1, 0]
    hb_ref[0] = h
    fin_ref[1, 0] = h[0:1, :]


def lru_scan(a, b, h0, tb):
    _, B, T, C = a.shape
    nb = T // tb
    fwd = pl.BlockSpec((1, 1, tb, C), lambda bi, i: (0, bi, i, 0))
    bwd = pl.BlockSpec((1, 1, tb, C), lambda bi, i: (1, bi, nb - 1 - i, 0))
    st = pl.BlockSpec((2, 1, 1, C), lambda bi, i: (0, bi, 0, 0))
    return pl.pallas_call(
        _lru_scan_kernel,
        grid=(B, nb),
        in_specs=[fwd, fwd, bwd, bwd, st],
        out_specs=[pl.BlockSpec((1, tb, C), lambda bi, i: (bi, i, 0)),
                   pl.BlockSpec((1, tb, C), lambda bi, i: (bi, nb - 1 - i, 0)), st],
        out_shape=[jax.ShapeDtypeStruct((B, T, C), F32), jax.ShapeDtypeStruct((B, T, C), F32),
                   jax.ShapeDtypeStruct((2, B, 1, C), F32)],
        compiler_params=_cparams("parallel", "arbitrary"),
        name="lru_scan",
    )(a, b, a, b, h0)


def _rope(x, cos, sin):
    q4 = RET_HEAD // 4
    lane = lax.broadcasted_iota(jnp.int32, x.shape, 1) % (2 * q4)
    partner = jnp.where(lane < q4, pltpu.roll(x, x.shape[1] - q4, 1), pltpu.roll(x, q4, 1))
    return x * cos + partner * sin


def _ret_dir(x, cos, sin, s, glane, gtile_ref, d, reverse):
    C = x.shape[0]
    q = x[:, 0:BR_W]
    k = x[:, BR_W:2 * BR_W]
    v = x[:, 2 * BR_W:3 * BR_W].astype(BF16)
    if cos is not None:
        q = _rope(q, cos, sin)
        k = _rope(k, cos, sin)
    k = k * (RET_HEAD ** -0.5)
    lg = -_softplus(-glane)
    idx = lax.broadcasted_iota(jnp.int32, (C, 1), 0).astype(F32)
    steps_in = (C - idx) if reverse else (idx + 1.0)
    steps_out = idx if reverse else (C - 1.0 - idx)
    ri = lax.broadcasted_iota(jnp.int32, (C, C), 0)
    ci = lax.broadcasted_iota(jnp.int32, (C, C), 1)
    diff = ((ci - ri) if reverse else (ri - ci)).astype(F32)
    lane_head = lax.broadcasted_iota(jnp.int32, (1, BR_W), 1) // RET_HEAD
    qb = q.astype(BF16)
    kb = k.astype(BF16)
    y = _dot(qb, s.astype(BF16)) * jnp.exp(steps_in * lg)
    for h in range(RET_HEADS):
        lg_h = -_softplus(-gtile_ref[d, h][0:1, :])
        dm = jnp.where(diff >= 0, jnp.exp(diff * lg_h), 0.0)
        mh = lane_head == h
        sc = lax.dot_general(jnp.where(mh, qb, jnp.zeros_like(qb)), kb, (((1,), (1,)), ((), ())),
                             preferred_element_type=F32)
        y = y + jnp.where(mh, _dot((sc * dm).astype(BF16), v), 0.0)
    kd = (k * jnp.exp(steps_out * lg)).astype(BF16)
    upd = lax.dot_general(kd, v, (((0,), (0,)), ((), ())), preferred_element_type=F32)
    rh = lax.broadcasted_iota(jnp.int32, (BR_W, BR_W), 0) // RET_HEAD
    ch = lax.broadcasted_iota(jnp.int32, (BR_W, BR_W), 1) // RET_HEAD
    s = s * jnp.exp(C * lg) + jnp.where(rh == ch, upd, 0.0)
    return y, s


def _ret_kernel(*refs, rope):
    if rope:
        xf_ref, xb_ref, cf_ref, sf_ref, cb_ref, sb_ref, gl_ref, gt_ref, s0_ref, yf_ref, yb_ref, s_ref = refs
    else:
        xf_ref, xb_ref, gl_ref, gt_ref, s0_ref, yf_ref, yb_ref, s_ref = refs

    @pl.when(pl.program_id(1) == 0)
    def _():
        s_ref[...] = s0_ref[...]

    y, s = _ret_dir(xf_ref[0], cf_ref[...] if rope else None, sf_ref[...] if rope else None,
                    s_ref[0, 0], gl_ref[0:1, :], gt_ref, 0, False)
    yf_ref[0] = y
    s_ref[0, 0] = s
    y, s = _ret_dir(xb_ref[0], cb_ref[...] if rope else None, sb_ref[...] if rope else None,
                    s_ref[1, 0], gl_ref[1:2, :], gt_ref, 1, True)
    yb_ref[0] = y
    s_ref[1, 0] = s


def retention(p, cos, sin, glane, gtile, s0, rope):
    B, T, _ = p.shape
    C = RET_CHUNK
    nc = T // C
    cb = OFF_RET // RET_IN
    xf = pl.BlockSpec((1, C, RET_IN), lambda b, i: (b, i, cb))
    xb = pl.BlockSpec((1, C, RET_IN), lambda b, i: (b, nc - 1 - i, cb))
    tf = pl.BlockSpec((C, BR_W), lambda b, i: (i, 0))
    tb = pl.BlockSpec((C, BR_W), lambda b, i: (nc - 1 - i, 0))
    st = pl.BlockSpec((2, 1, BR_W, BR_W), lambda b, i: (0, b, 0, 0))
    ins = [xf, xb] + ([tf, tf, tb, tb] if rope else []) + [
        _const_spec((2, BR_W)), _const_spec((2, RET_HEADS, 8, C)), st]
    args = [p, p] + ([cos, sin, cos, sin] if rope else []) + [glane, gtile, s0]
    return pl.pallas_call(
        functools.partial(_ret_kernel, rope=rope),
        grid=(B, nc),
        in_specs=ins,
        out_specs=[pl.BlockSpec((1, C, BR_W), lambda b, i: (b, i, 0)),
                   pl.BlockSpec((1, C, BR_W), lambda b, i: (b, nc - 1 - i, 0)), st],
        out_shape=[jax.ShapeDtypeStruct((B, T, BR_W), F32), jax.ShapeDtypeStruct((B, T, BR_W), F32),
                   jax.ShapeDtypeStruct((2, B, BR_W, BR_W), F32)],
        compiler_params=_cparams("parallel", "arbitrary"),
        name="retention",
    )(*args)


@functools.lru_cache(maxsize=None)
def _rope_tables(T):
    pos = np.arange(T)
    q4 = RET_HEAD // 4
    inv = ROPE_BASE ** (-np.arange(q4, dtype=np.float64) / q4)
    cos = np.zeros((T, RET_HEAD))
    sin = np.zeros((T, RET_HEAD))
    for part, coord in enumerate((pos // GRID_W, pos % GRID_W)):
        ang = coord[:, None] * inv
        base = part * 2 * q4
        cos[:, base:base + q4] = np.cos(ang)
        cos[:, base + q4:base + 2 * q4] = np.cos(ang)
        sin[:, base:base + q4] = -np.sin(ang)
        sin[:, base + q4:base + 2 * q4] = np.sin(ang)
    tile = lambda t: np.asarray(np.tile(t, (1, RET_HEADS)), np.float32)
    return tile(cos), tile(sin)


def _head_norm(y, seg_ref, eps):
    mu = _head_sum(y, seg_ref) * (1.0 / RW_HEAD)
    yc = y - mu
    var = _head_sum(yc * yc, seg_ref) * (1.0 / RW_HEAD)
    return yc * lax.rsqrt(var + eps)


def _merge_kernel(x_ref, g_ref, sh_ref, sc_ref, gt_ref, hy_ref, rfl_ref, rfh_ref, rbl_ref, rbh_ref,
                  rbon_ref, rg_ref, lhf_ref, lhb_ref, lgg_ref, tyf_ref, tyb_ref, tg_ref, lng_ref, seg_ref,
                  wg_ref, br_ref, wo_ref, o_ref, m_ref):
    x = x_ref[0]
    u = _norm_mod(x, g_ref[...], sh_ref[0], sc_ref[0]).astype(BF16)
    wkv = jnp.concatenate([rfl_ref[0] + rbl_ref[0], rfh_ref[0] + rbh_ref[0]], axis=1)
    y_rw = (_head_norm(wkv, seg_ref, RW_LN_EPS) * lng_ref[...] + rbon_ref[0]) * rg_ref[0]
    y_lru = (lhf_ref[0] + lhb_ref[0]) * lgg_ref[0]
    y_ret = _head_norm(tyf_ref[0] + tyb_ref[0], seg_ref, RET_LN_EPS) * _silu(tg_ref[0])
    ys = [y.astype(BF16) for y in (hy_ref[0], y_rw, y_lru, y_ret)]
    D = x.shape[1]
    cw = 256
    for c in range(D // cw):
        acc = None
        for n in range(N_BRANCH):
            gate = _sigmoid(_dot(u, wg_ref[:, n * D + c * cw:n * D + (c + 1) * cw]))
            t = gate * _dot(ys[n], br_ref[n, :, c * cw:(c + 1) * cw])
            acc = t if acc is None else acc + t
        m_ref[:, c * cw:(c + 1) * cw] = acc.astype(BF16)
    o_ref[0] = x + gt_ref[0] * _dot(m_ref[...], wo_ref[...])


def merge(x, g, sh, sc, gt, p, y_hy, rw, lru, ret, ln_g, seg, w_gate, br, w_out, tm):
    B, T, D = x.shape
    vec = pl.BlockSpec((1, 1, D), lambda b, i: (b, 0, 0))
    row = pl.BlockSpec((1, tm, D), lambda b, i: (b, i, 0))
    brn = pl.BlockSpec((1, tm, BR_W), lambda b, i: (b, i, 0))
    half = pl.BlockSpec((1, tm, LANE_W), lambda b, i: (b, i, 0))
    tg = pl.BlockSpec((1, tm, BR_W), lambda b, i: (b, i, (OFF_RET + 3 * BR_W) // BR_W))
    return pl.pallas_call(
        _merge_kernel,
        grid=(B, T // tm),
        in_specs=[row, _const_spec((1, D)), vec, vec, vec, brn] + [half] * 4 + [brn] * 7 + [tg] + [
            _const_spec((1, BR_W)), _const_spec((BR_W, BR_W)), _const_spec((D, GATE_IN)),
            _const_spec((N_BRANCH, BR_W, D)), _const_spec((D, D))],
        out_specs=row,
        out_shape=jax.ShapeDtypeStruct((B, T, D), F32),
        scratch_shapes=[pltpu.VMEM((tm, D), BF16)],
        compiler_params=_cparams("parallel", "parallel"),
        name="merge",
    )(x, g, sh, sc, gt, y_hy, *rw, *lru, *ret, p, ln_g, seg, w_gate, br, w_out)


def _block_diag(w):
    G = w.shape[-3]
    eye = jnp.eye(G, dtype=w.dtype)
    full = w[..., :, :, None, :] * eye[:, None, :, None]
    return full.reshape(*w.shape[:-3], G * w.shape[-2], G * w.shape[-1])


def _mixers(p, lp, states, on_grid, with_output, tiles):
    B, T, _ = p.shape
    tm, tt, _ = tiles
    pre = rwkv_prep(p, *lp['rw'], on_grid, GRID_W)
    r, v, na, w, kt, b = pre[0:2] * 2, pre[2:4] * 2, pre[4:6] * 2, pre[6:10], pre[10:14], pre[14:18]
    bonus, g = pre[18], pre[19]
    y_f, y_b, rw_fin = rwkv_mix((r, w, kt, na, b, v), states[0], tt)
    a, bb, gg = lru_prep(p, *lp['lru'], tm)
    h_f, h_b, lru_fin = lru_scan(a, bb, states[1], tm)
    cos, sin = (jnp.asarray(t) for t in _rope_tables(T)) if on_grid else (None, None)
    t_f, t_b, ret_fin = retention(p, cos, sin, *lp['ret'], states[2], on_grid)
    fins = (rw_fin, lru_fin, ret_fin)
    if not with_output:
        return None, fins
    vg = hyena_prep(p, *lp['hy_conv'], min(T, 2048))
    y_hy = hyena_branch(vg[0], vg[1], vg[2], lp['hy'])
    return (y_hy, (y_f[0], y_f[1], y_b[0], y_b[1], bonus, g), (h_f, h_b, gg), (t_f, t_b)), fins


def kernel(x, c, ctx, c_ctx, w_mod, b_mod, norm1_g, norm2_g, w_in, hy_conv_w, hy_conv_b, hy_f_w1, hy_f_b1, hy_f_w2, hy_f_b2, hy_f_w3, hy_freq, hy_bias, rw_mu, rw_w0, rw_w2, rw_a0, rw_a2, rw_g2, rw_kk, rw_ka, rw_rk, rw_ln_g, lru_conv_w, lru_conv_b, lru_wa, lru_ba, lru_wx, lru_bx, lru_lam, ret_gamma, br_proj, w_out, ffn_w1, ffn_w2, final_g):
    B, T, D = x.shape
    TC = ctx.shape[1]
    L = w_in.shape[0]

    s0, s1, s2, s3 = HY_IN, HY_IN + RW_IN, HY_IN + RW_IN + LRU_IN, HY_IN + RW_IN + LRU_IN + RET_IN
    w_branch = jnp.concatenate([w_in[:, :, s0:s1], jnp.zeros((L, D, RW_PAD - RW_IN), w_in.dtype),
                                w_in[:, :, s2:s3], jnp.zeros((L, D, HY_GAP), w_in.dtype),
                                w_in[:, :, 0:s0], w_in[:, :, s1:s2]], axis=2).astype(BF16)
    w_gate = w_in[:, :, s3:].astype(BF16)
    mu = jnp.pad(rw_mu, ((0, 0), (0, RW_PAD - RW_IN)))[:, None, :]
    w_lora = jnp.zeros((L, BR_W, 5 * BR_W), F32)
    w_lora = w_lora.at[:, 0:RW_LORA, 0:BR_W].set(rw_w2[:, 0]).at[:, 0:RW_LORA, BR_W:2 * BR_W].set(rw_w2[:, 1])
    w_lora = w_lora.at[:, RW_LORA:2 * RW_LORA, 2 * BR_W:3 * BR_W].set(rw_a2[:, 0])
    w_lora = w_lora.at[:, RW_LORA:2 * RW_LORA, 3 * BR_W:4 * BR_W].set(rw_a2[:, 1])
    w_lora = w_lora.at[:, 2 * RW_LORA:3 * RW_LORA, 4 * BR_W:5 * BR_W].set(rw_g2).astype(BF16)
    kvec = jnp.stack([rw_kk, rw_ka, rw_rk], axis=1)
    seg = jnp.asarray(np.kron(np.eye(RW_HEADS), np.ones((RW_HEAD, RW_HEAD))), BF16)
    lru_w = jnp.concatenate([_block_diag(lru_wa[:, 0]), _block_diag(lru_wx[:, 0]),
                             _block_diag(lru_wa[:, 1]), _block_diag(lru_wx[:, 1])], axis=2).astype(BF16)
    lru_bias = jnp.stack([lru_ba[:, 0], lru_bx[:, 0], lru_ba[:, 1], lru_bx[:, 1]], axis=1)
    glane = jnp.repeat(ret_gamma, RET_HEAD, axis=2)
    gtile = jnp.broadcast_to(ret_gamma[:, :, :, None, None], (L, 2, RET_HEADS, 8, RET_CHUNK))
    f_w1 = jnp.pad(hy_f_w1, ((0, 0), (0, HY_FEAT_PAD - HY_FEAT), (0, 0)))
    br_b = br_proj.astype(BF16)
    w_out_b = w_out.astype(BF16)
    ffn_gate = ffn_w1[:, :, :D_FF].astype(BF16)
    ffn_up_w = ffn_w1[:, :, D_FF:].astype(BF16)
    ffn_w2_b = ffn_w2.astype(BF16)

    cc = jnp.concatenate([c, c_ctx[None, :], jnp.zeros((16 - B - 1, D), F32)], axis=0)
    mods = modulation(cc, w_mod.astype(BF16), b_mod[:, None, :])

    zero_states = (jnp.zeros((RW_VH // 8, RW_HEAD, 8, 16 * B), F32), jnp.zeros((2, B, 1, BR_W), F32),
                   jnp.zeros((2, B, BR_W, BR_W), F32))
    xc = ctx
    for l in range(L):
        last = l == L - 1
        lp = {
            'rw': (mu[l], w_lora[l], rw_w0[l], rw_a0[l], kvec[l], seg),
            'lru': (lru_conv_w[l], lru_conv_b[l][None, :], lru_w[l], lru_bias[l], lru_lam[l]),
            'ret': (glane[l], gtile[l]),
            'hy_conv': (hy_conv_w[l], hy_conv_b[l][None, :]),
            'hy': (f_w1[l], hy_f_b1[l][None, :], hy_f_w2[l], hy_f_b2[l][None, :], hy_f_w3[l], hy_freq[l],
                   hy_bias[l]),
        }
        g1 = norm1_g[l][None, :]
        g2 = norm2_g[l][None, :]
        m_lat = [m[:, None, :] for m in jnp.split(mods[l, :B], 6, axis=-1)]
        m_ctx = [jnp.broadcast_to(m[None, :, :], (B, 1, D)) for m in jnp.split(mods[l, B:B + 1], 6, axis=-1)]
        ln_g = rw_ln_g[l][None, :]

        def layer(xs, m, states, on_grid, with_output, tiles, final):
            p = in_projection(xs, g1, m[0], m[1], w_branch[l], tiles[0])
            br, fins = _mixers(p, lp, states, on_grid, with_output, tiles)
            if not with_output:
                return None, fins
            xs = merge(xs, g1, m[0], m[1], m[2], p, br[0], br[1], br[2], br[3], ln_g, seg,
                       w_gate[l], br_b[l], w_out_b[l], tiles[2])
            h = ffn_up(xs, g2, m[3], m[4], ffn_gate[l], ffn_up_w[l], tiles[0])
            xs = ffn_down(h, xs, m[5], ffn_w2_b[l], final_g[None, :], final, tiles[0])
            return xs, fins

        xc_new, ctx_states = layer(xc, m_ctx, zero_states, False, not last, (TC, 64, TC), False)
        x, _ = layer(x, m_lat, ctx_states, True, True, (512, 128, 512), last)
        if not last:
            xc = xc_new
    return x
```

```python
import functools
import math

import numpy as np
import jax
import jax.numpy as jnp
from jax import lax
from jax.experimental import pallas as pl
from jax.experimental.pallas import tpu as pltpu

F32 = jnp.float32
BF16 = jnp.bfloat16

D_MODEL = 1024
DEPTH = 4
GRID_W = 64
N_BRANCH = 4
BR_W = D_MODEL // N_BRANCH

HY_BANDS = 8
HY_FEAT = 1 + 2 * HY_BANDS
HY_FEAT_PAD = 32
HY_HID = 64
HY_TARGET = 1e-2
HY_FAST = 0.3
HY_SLOW = 1.5
HY_IN = 3 * BR_W

RW_HEAD = 64
RW_HEADS = BR_W // RW_HEAD
RW_LORA = 64
RW_IN = 3 * BR_W + 3 * RW_LORA
RW_PAD = 4 * BR_W
RW_LN_EPS = 64e-5

LRU_BLOCKS = 4
LRU_BLOCK = BR_W // LRU_BLOCKS
LRU_C = 8.0
LRU_IN = 2 * BR_W

RET_HEADS = 4
RET_HEAD = BR_W // RET_HEADS
RET_CHUNK = 128
ROPE_BASE = 10000.0
RET_IN = 4 * BR_W
RET_LN_EPS = 1e-5

GATE_IN = N_BRANCH * D_MODEL
D_FF = ((8 * D_MODEL // 3 + 255) // 256) * 256
EPS = 1e-6

OFF_RW = 0
OFF_RET = OFF_RW + RW_PAD
OFF_HY = 3 * HY_IN
OFF_LRU = OFF_HY + HY_IN
N_BR = OFF_LRU + LRU_IN
HY_GAP = OFF_HY - (OFF_RET + RET_IN)

VMEM_LIMIT = 56 * 1024 * 1024
HI = lax.Precision.HIGHEST


def _cparams(*sem):
    return pltpu.CompilerParams(dimension_semantics=sem, vmem_limit_bytes=VMEM_LIMIT)


def _const_spec(shape):
    nd = len(shape)
    return pl.BlockSpec(shape, lambda *_: (0,) * nd, pipeline_mode=pl.Buffered(1))


def _dot(a, b, **kw):
    return jnp.dot(a, b, preferred_element_type=F32, **kw)


def _norm_mod(x, g, sh, sc):
    ms = jnp.mean(x * x, axis=-1, keepdims=True)
    return x * lax.rsqrt(ms + EPS) * g * (1.0 + sc) + sh


def _sigmoid(x):
    return 1.0 / (1.0 + jnp.exp(-x))


def _silu(x):
    return x * _sigmoid(x)


def _softplus(x):
    return jnp.maximum(x, 0.0) + jnp.log(1.0 + jnp.exp(-jnp.abs(x)))


def _mod_kernel(c_ref, w_ref, b_ref, o_ref):
    c = c_ref[...]
    o_ref[0] = _dot(_silu(c).astype(BF16), w_ref[0]) + b_ref[0]


def modulation(cc, w_mod, b_mod):
    L, D, N = w_mod.shape
    tn = 1536
    return pl.pallas_call(
        _mod_kernel,
        grid=(L, N // tn),
        in_specs=[pl.BlockSpec((16, D), lambda l, j: (0, 0)),
                  pl.BlockSpec((1, D, tn), lambda l, j: (l, 0, j)),
                  pl.BlockSpec((1, 1, tn), lambda l, j: (l, 0, j))],
        out_specs=pl.BlockSpec((1, 16, tn), lambda l, j: (l, 0, j)),
        out_shape=jax.ShapeDtypeStruct((L, 16, N), F32),
        compiler_params=_cparams("parallel", "parallel"),
        name="modulation",
    )(cc, w_mod, b_mod)


def _inproj_kernel(x_ref, g_ref, sh_ref, sc_ref, w_ref, o_ref):
    u = _norm_mod(x_ref[0], g_ref[...], sh_ref[0], sc_ref[0]).astype(BF16)
    n = w_ref.shape[1]
    cw = 256
    for c in range(n // cw):
        o_ref[0, :, c * cw:(c + 1) * cw] = _dot(u, w_ref[:, c * cw:(c + 1) * cw])


def in_projection(x, g, sh, sc, w, tm):
    B, T, D = x.shape
    N = w.shape[1]
    vec = pl.BlockSpec((1, 1, D), lambda b, i: (b, 0, 0))
    return pl.pallas_call(
        _inproj_kernel,
        grid=(B, T // tm),
        in_specs=[pl.BlockSpec((1, tm, D), lambda b, i: (b, i, 0)),
                  _const_spec((1, D)), vec, vec, _const_spec((D, N))],
        out_specs=pl.BlockSpec((1, tm, N), lambda b, i: (b, i, 0)),
        out_shape=jax.ShapeDtypeStruct((B, T, N), F32),
        compiler_params=_cparams("parallel", "parallel"),
        name="in_projection",
    )(x, g, sh, sc, w)


def _ffn1_kernel(x_ref, g_ref, sh_ref, sc_ref, wg_ref, wu_ref, o_ref):
    u = _norm_mod(x_ref[0], g_ref[...], sh_ref[0], sc_ref[0]).astype(BF16)
    n = wg_ref.shape[1]
    cw = 256
    for c in range(n // cw):
        sl = slice(c * cw, (c + 1) * cw)
        gate = _dot(u, wg_ref[:, sl])
        up = _dot(u, wu_ref[:, sl])
        o_ref[0, :, sl] = (_silu(gate) * up).astype(BF16)


def ffn_up(x, g, sh, sc, w_gate, w_up, tm):
    B, T, D = x.shape
    N = w_gate.shape[1]
    vec = pl.BlockSpec((1, 1, D), lambda b, i: (b, 0, 0))
    return pl.pallas_call(
        _ffn1_kernel,
        grid=(B, T // tm),
        in_specs=[pl.BlockSpec((1, tm, D), lambda b, i: (b, i, 0)),
                  _const_spec((1, D)), vec, vec, _const_spec((D, N)), _const_spec((D, N))],
        out_specs=pl.BlockSpec((1, tm, N), lambda b, i: (b, i, 0)),
        out_shape=jax.ShapeDtypeStruct((B, T, N), BF16),
        compiler_params=_cparams("parallel", "parallel"),
        name="ffn_up",
    )(x, g, sh, sc, w_gate, w_up)


def _ffn2_kernel(h_ref, x_ref, gt_ref, w_ref, fg_ref, o_ref, *, final_norm):
    y = x_ref[0] + gt_ref[0] * _dot(h_ref[0], w_ref[...])
    if final_norm:
        ms = jnp.mean(y * y, axis=-1, keepdims=True)
        y = y * lax.rsqrt(ms + EPS) * fg_ref[...]
    o_ref[0] = y


def ffn_down(h, x, gate, w, final_g, final_norm, tm):
    B, T, D = x.shape
    N = h.shape[2]
    return pl.pallas_call(
        functools.partial(_ffn2_kernel, final_norm=final_norm),
        grid=(B, T // tm),
        in_specs=[pl.BlockSpec((1, tm, N), lambda b, i: (b, i, 0)),
                  pl.BlockSpec((1, tm, D), lambda b, i: (b, i, 0)),
                  pl.BlockSpec((1, 1, D), lambda b, i: (b, 0, 0)),
                  _const_spec((N, D)), _const_spec((1, D))],
        out_specs=pl.BlockSpec((1, tm, D), lambda b, i: (b, i, 0)),
        out_shape=jax.ShapeDtypeStruct((B, T, D), F32),
        compiler_params=_cparams("parallel", "parallel"),
        name="ffn_down",
    )(h, x, gate, w, final_g)


def _halo_specs(tm, T, width, col_fn, halo):
    r = tm // halo
    last = T // halo - 1
    main = pl.BlockSpec((1, tm, width), lambda *g: (g[0], g[1], col_fn(*g)))
    prev = pl.BlockSpec((1, halo, width), lambda *g: (g[0], jnp.maximum(g[1] * r - 1, 0), col_fn(*g)))
    nxt = pl.BlockSpec((1, halo, width), lambda *g: (g[0], jnp.minimum((g[1] + 1) * r, last), col_fn(*g)))
    return main, prev, nxt


def _shift_rows(x, prev, nxt, s, first, last):
    tm = x.shape[0]
    if s > 0:
        head = jnp.where(first, 0.0, prev[prev.shape[0] - s:, :])
        return head if s == tm else jnp.concatenate([head, x[:tm - s, :]], axis=0)
    s = -s
    tail = jnp.where(last, 0.0, nxt[:s, :])
    return tail if s == tm else jnp.concatenate([x[s:, :], tail], axis=0)


def _hy_prep_kernel(p_ref, pp_ref, pn_ref, w_ref, b_ref, v_ref, g1_ref, g2_ref):
    i = pl.program_id(1)
    first = i == 0
    last = i == pl.num_programs(1) - 1
    x = p_ref[0]
    xm = _shift_rows(x, pp_ref[0], pn_ref[0], 1, first, last)
    xp = _shift_rows(x, pp_ref[0], pn_ref[0], -1, first, last)
    u = b_ref[...] + w_ref[0:1, :] * xm + w_ref[1:2, :] * x + w_ref[2:3, :] * xp
    v_ref[0] = u[:, 0:BR_W]
    g1_ref[0] = u[:, BR_W:2 * BR_W]
    g2_ref[0] = u[:, 2 * BR_W:3 * BR_W]


def hyena_prep(p, conv_w, conv_b, tm):
    B, T, _ = p.shape
    main, prev, nxt = _halo_specs(tm, T, HY_IN, lambda b, i: OFF_HY // HY_IN, 8)
    out = pl.BlockSpec((1, tm, BR_W), lambda b, i: (b, i, 0))
    shp = jax.ShapeDtypeStruct((B, T, BR_W), F32)
    return pl.pallas_call(
        _hy_prep_kernel,
        grid=(B, T // tm),
        in_specs=[main, prev, nxt, _const_spec((3, HY_IN)), _const_spec((1, HY_IN))],
        out_specs=[out, out, out],
        out_shape=[shp, shp, shp],
        compiler_params=_cparams("parallel", "parallel"),
        name="hyena_prep",
    )(p, p, p, conv_w, conv_b)


def _hy_filter_kernel(feat_ref, w1_ref, b1_ref, w2_ref, b2_ref, w3_ref, fq_ref, rates_ref, h_ref, ss_ref):
    i = pl.program_id(0)
    feat = feat_ref[...]
    t = feat[:, 0:1]
    h = jnp.sin(fq_ref[0:1, :] * (_dot(feat, w1_ref[...], precision=HI) + b1_ref[...]))
    h = jnp.sin(fq_ref[1:2, :] * (_dot(h, w2_ref[...], precision=HI) + b2_ref[...]))
    h = _dot(h, w3_ref[...], precision=HI) * jnp.exp(-t * rates_ref[...])
    row = lax.broadcasted_iota(jnp.int32, h.shape, 0) + i * h.shape[0]
    col = lax.broadcasted_iota(jnp.int32, h.shape, 1)
    h = jnp.where((row == 0) & ((col // BR_W) % 2 == 1), 0.0, h)
    h_ref[...] = h

    @pl.when(i == 0)
    def _():
        ss_ref[...] = jnp.zeros_like(ss_ref)

    ss_ref[...] += jnp.sum(h * h, axis=0, keepdims=True)


def hyena_filter(feat, w1, b1, w2, b2, w3, freq, rates, tl):
    L = feat.shape[0]
    C = w3.shape[1]
    return pl.pallas_call(
        _hy_filter_kernel,
        grid=(L // tl,),
        in_specs=[pl.BlockSpec((tl, HY_FEAT_PAD), lambda i: (i, 0)),
                  _const_spec((HY_FEAT_PAD, HY_HID)), _const_spec((1, HY_HID)),
                  _const_spec((HY_HID, HY_HID)), _const_spec((1, HY_HID)),
                  _const_spec((HY_HID, C)), _const_spec((2, HY_HID)), _const_spec((1, C))],
        out_specs=[pl.BlockSpec((tl, C), lambda i: (i, 0)), pl.BlockSpec((1, C), lambda i: (0, 0))],
        out_shape=[jax.ShapeDtypeStruct((L, C), F32), jax.ShapeDtypeStruct((1, C), F32)],
        compiler_params=_cparams("arbitrary"),
        name="hyena_filter",
    )(feat, w1, b1, w2, b2, w3, freq, rates)


def _filter_scale(ss_ref, o):
    e = ss_ref[:, 2 * o * BR_W:(2 * o + 1) * BR_W] + ss_ref[:, (2 * o + 1) * BR_W:(2 * o + 2) * BR_W]
    return lax.rsqrt(e + EPS)


def _combine_spectrum(x, ss_ref, o, half):
    xf = x[:, 2 * o * BR_W:(2 * o + 1) * BR_W]
    xb = x[:, (2 * o + 1) * BR_W:(2 * o + 2) * BR_W]
    sc = _filter_scale(ss_ref, o)
    hr = (xf[:half] + xb[:half]) * sc
    hi = (xf[half:] - xb[half:]) * sc
    return jnp.concatenate([hr, hi], axis=0)


def _cmul(x, h, half):
    xr, xi = x[:half], x[half:]
    hr, hi = h[:half], h[half:]
    return jnp.concatenate([xr * hr - xi * hi, xr * hi + xi * hr], axis=0)


def _dft1_kernel(z_ref, f_ref, a_ref):
    a_ref[0] = _dot(f_ref[...], z_ref[0].astype(BF16)).astype(BF16)


def dft_stage1(z, f1, tn):
    B, n1, W = z.shape
    M = f1.shape[0]
    return pl.pallas_call(
        _dft1_kernel,
        grid=(B, W // tn),
        in_specs=[pl.BlockSpec((1, n1, tn), lambda b, j: (b, 0, j)), _const_spec((M, n1))],
        out_specs=pl.BlockSpec((1, M, tn), lambda b, j: (b, 0, j)),
        out_shape=jax.ShapeDtypeStruct((B, M, W), BF16),
        compiler_params=_cparams("parallel", "parallel"),
        name="dft_stage1",
    )(z, f1)


def _spec2_kernel(a_ref, g_ref, ss_ref, h_ref):
    kb = g_ref.shape[0]
    n2 = a_ref.shape[3]
    for k in range(kb):
        a = a_ref[0, :, k].reshape(2 * n2, a_ref.shape[4])
        x = _dot(g_ref[k], a)
        for o in range(2):
            h_ref[o, k] = _combine_spectrum(x, ss_ref, o, n2)


def filter_spectrum(a, g, ss, kb):
    _, _, N1, N2, C = a.shape
    return pl.pallas_call(
        _spec2_kernel,
        grid=(N1 // kb,),
        in_specs=[pl.BlockSpec((1, 2, kb, N2, C), lambda i: (0, 0, i, 0, 0)),
                  pl.BlockSpec((kb, 2 * N2, 2 * N2), lambda i: (i, 0, 0)),
                  _const_spec((1, C))],
        out_specs=pl.BlockSpec((2, kb, 2 * N2, BR_W), lambda i: (0, i, 0, 0)),
        out_shape=jax.ShapeDtypeStruct((2, N1, 2 * N2, BR_W), F32),
        compiler_params=_cparams("parallel"),
        name="filter_spectrum",
    )(a, g, ss)


def _conv2_kernel(a_ref, g_ref, gi_ref, h_ref, o_ref):
    kb = g_ref.shape[0]
    n2 = a_ref.shape[3]
    C = a_ref.shape[4]
    for k in range(kb):
        a = a_ref[0, :, k].reshape(2 * n2, C)
        y = _cmul(_dot(g_ref[k], a), h_ref[0, k], n2).astype(BF16)
        o_ref[0, :, k] = _dot(gi_ref[k], y).astype(BF16).reshape(2, n2, C)


def spectral_multiply(a, g, gi, h, o, kb):
    B, _, N1, N2, C = a.shape
    blk = pl.BlockSpec((1, 2, kb, N2, C), lambda i, b: (b, 0, i, 0, 0))
    mat = pl.BlockSpec((kb, 2 * N2, 2 * N2), lambda i, b: (i, 0, 0))
    return pl.pallas_call(
        _conv2_kernel,
        grid=(N1 // kb, B),
        in_specs=[blk, mat, mat, pl.BlockSpec((1, kb, 2 * N2, C), lambda i, b: (o, i, 0, 0))],
        out_specs=blk,
        out_shape=jax.ShapeDtypeStruct(a.shape, BF16),
        compiler_params=_cparams("parallel", "parallel"),
        name="spectral_multiply",
    )(a, g, gi, h)


def _idft1_kernel(b_ref, f_ref, z_ref, gate_ref, bias_ref, f1_ref, o_ref, *a_ref):
    y = _dot(f_ref[...], b_ref[0])
    z = z_ref[0]
    out = gate_ref[0] * (y + bias_ref[...] * z)
    o_ref[0] = out
    if a_ref:
        a_ref[0][0] = _dot(f1_ref[...], out.astype(BF16)).astype(BF16)


def idft_stage1(bm, fi, z, gate, bias, f1, tn, with_next):
    B, M, W = bm.shape
    n1 = fi.shape[0]
    blk = pl.BlockSpec((1, n1, tn), lambda b, j: (b, 0, j))
    wide = pl.BlockSpec((1, M, tn), lambda b, j: (b, 0, j))
    out_specs, out_shape = [blk], [jax.ShapeDtypeStruct((B, n1, W), F32)]
    if with_next:
        out_specs.append(wide)
        out_shape.append(jax.ShapeDtypeStruct((B, M, W), BF16))
    return pl.pallas_call(
        _idft1_kernel,
        grid=(B, W // tn),
        in_specs=[wide, _const_spec((n1, M)), blk, blk, _const_spec((1, tn)), _const_spec((M, n1))],
        out_specs=out_specs,
        out_shape=out_shape,
        compiler_params=_cparams("parallel", "parallel"),
        name="idft_stage1",
    )(bm, fi, z, gate, bias, f1)


@functools.lru_cache(maxsize=None)
def _dft_tables(L):
    N = 2 * L
    N2 = 128
    N1 = N // N2
    nz = L // N2
    k1 = np.arange(N1)[:, None]
    n1 = np.arange(nz)[None, :]
    th = 2 * np.pi * ((k1 * n1) % N1) / N1
    f1 = np.concatenate([np.cos(th), -np.sin(th)], axis=0)
    fi = np.concatenate([np.cos(th).T, -np.sin(th).T], axis=1) / N
    kk1 = np.arange(N1)[:, None, None]
    k2 = np.arange(N2)[None, :, None]
    n2 = np.arange(N2)[None, None, :]
    ph = 2 * np.pi * ((n2 * k2 * N1 + n2 * kk1) % N) / N
    gr, gim = np.cos(ph), -np.sin(ph)
    g = np.concatenate([np.concatenate([gr, -gim], axis=2), np.concatenate([gim, gr], axis=2)], axis=1)
    hr, him = np.swapaxes(gr, 1, 2), -np.swapaxes(gim, 1, 2)
    gi = np.concatenate([np.concatenate([hr, -him], axis=2), np.concatenate([him, hr], axis=2)], axis=1)
    return tuple(np.asarray(t, np.float32) for t in (f1, fi, g, gi))


def _spec_direct_kernel(hf_ref, f_ref, ss_ref, h_ref):
    x = _dot(f_ref[...], hf_ref[...].astype(BF16))
    half = x.shape[0] // 2
    for o in range(2):
        h_ref[o] = _combine_spectrum(x, ss_ref, o, half)


def filter_spectrum_direct(hf, f, ss):
    L, C = hf.shape
    return pl.pallas_call(
        _spec_direct_kernel,
        grid=(1,),
        in_specs=[_const_spec((L, C)), _const_spec((4 * L, L)), _const_spec((1, C))],
        out_specs=pl.BlockSpec((2, 4 * L, BR_W), lambda i: (0, 0, 0)),
        out_shape=jax.ShapeDtypeStruct((2, 4 * L, BR_W), F32),
        compiler_params=_cparams("arbitrary"),
        name="filter_spectrum_direct",
    )(hf, f, ss)


def _conv_direct_kernel(z_ref, gate_ref, bias_ref, f_ref, fi_ref, h_ref, o_ref):
    z = z_ref[0]
    x = _dot(f_ref[...], z.astype(BF16))
    y = _cmul(x, h_ref[0], x.shape[0] // 2).astype(BF16)
    o_ref[0] = gate_ref[0] * (_dot(fi_ref[...], y) + bias_ref[...] * z)


def conv_direct(z, gate, bias, f, fi, h, o):
    B, L, C = z.shape
    blk = pl.BlockSpec((1, L, C), lambda b: (b, 0, 0))
    return pl.pallas_call(
        _conv_direct_kernel,
        grid=(B,),
        in_specs=[blk, blk, _const_spec((1, C)), _const_spec((4 * L, L)), _const_spec((L, 4 * L)),
                  pl.BlockSpec((1, 4 * L, C), lambda b: (o, 0, 0))],
        out_specs=blk,
        out_shape=jax.ShapeDtypeStruct((B, L, C), F32),
        compiler_params=_cparams("parallel"),
        name="conv_direct",
    )(z, gate, bias, f, fi, h)


@functools.lru_cache(maxsize=None)
def _dft_direct_tables(L):
    N = 2 * L
    k = np.arange(N)[:, None]
    n = np.arange(L)[None, :]
    th = 2 * np.pi * ((k * n) % N) / N
    f = np.concatenate([np.cos(th), -np.sin(th)], axis=0)
    fi = np.concatenate([np.cos(th).T, -np.sin(th).T], axis=1) / N
    return np.asarray(f, np.float32), np.asarray(fi, np.float32)


@functools.lru_cache(maxsize=None)
def _filter_features(L):
    t = np.arange(L, dtype=np.float32) / np.float32(L)
    ang = (2.0 * math.pi) * t[:, None].astype(np.float64) * np.arange(1, HY_BANDS + 1)
    feat = np.zeros((L, HY_FEAT_PAD), np.float32)
    feat[:, 0] = t
    feat[:, 1:1 + HY_BANDS] = np.sin(ang)
    feat[:, 1 + HY_BANDS:HY_FEAT] = np.cos(ang)
    rates = np.abs(np.linspace(math.log(HY_TARGET) / HY_SLOW, math.log(HY_TARGET) / HY_FAST, BR_W))
    return feat, np.tile(np.asarray(rates, np.float32), 4)[None, :]


def hyena_branch(v, g1, g2, hp):
    f_w1, f_b1, f_w2, f_b2, f_w3, freq, bias = hp
    B, L, C = v.shape
    feat, rates = _filter_features(L)
    hf, ss = hyena_filter(jnp.asarray(feat), f_w1, f_b1, f_w2, f_b2, f_w3, freq, jnp.asarray(rates),
                          min(L, 512))
    if L <= 512:
        f, fi = (jnp.asarray(t).astype(BF16) for t in _dft_direct_tables(L))
        spec = filter_spectrum_direct(hf, f, ss)
        z = v
        for o, gate in enumerate((g1, g2)):
            z = conv_direct(z, gate, bias[o:o + 1], f, fi, spec, o)
        return z
    f1, fi1, g, gi = (jnp.asarray(t).astype(BF16) for t in _dft_tables(L))
    N2 = 128
    N1 = 2 * L // N2
    nz = L // N2
    kb = 8
    a = dft_stage1(hf.reshape(1, nz, N2 * 4 * C), f1, 8192)
    spec = filter_spectrum(a.reshape(1, 2, N1, N2, 4 * C), g, ss, kb)
    W = N2 * C
    tn = 8192
    z = v.reshape(B, nz, W)
    a = dft_stage1(z, f1, tn)
    for o, gate in enumerate((g1, g2)):
        bm = spectral_multiply(a.reshape(B, 2, N1, N2, C), g, gi, spec, o, kb).reshape(B, 2 * N1, W)
        out = idft_stage1(bm, fi1, z, gate.reshape(B, nz, W), jnp.tile(bias[o:o + 1], (1, tn // C)), f1, tn, o == 0)
        z = out[0]
        if o == 0:
            a = out[1]
    return z.reshape(B, L, C)


def _head_sum(x, seg_ref):
    hi = x.astype(BF16)
    lo = (x - hi.astype(F32)).astype(BF16)
    return _dot(hi, seg_ref[...]) + _dot(lo, seg_ref[...])


def _rw_prep_kernel(p_ref, pp_ref, pn_ref, mu_ref, wl_ref, w0_ref, a0_ref, kv_ref, seg_ref, *outs, on_grid):
    for bi in range(p_ref.shape[0]):
        _rw_prep_one(bi, p_ref, pp_ref, pn_ref, mu_ref, wl_ref, w0_ref, a0_ref, kv_ref, seg_ref, outs, on_grid)


def _rw_prep_one(bi, p_ref, pp_ref, pn_ref, mu_ref, wl_ref, w0_ref, a0_ref, kv_ref, seg_ref, outs, on_grid):
    def put(pair, val):
        pair[0][:, bi, :] = val[:, 0:LANE_W]
        pair[1][:, bi, :] = val[:, LANE_W:2 * LANE_W]

    r_out, v_out, na_out = outs[0:2], outs[2:4], outs[4:6]
    w_out, kt_out, b_out = outs[6:10], outs[10:14], outs[14:18]
    bonus_ref, g_ref = outs[18], outs[19]
    i = pl.program_id(0)
    first = i == 0
    last = i == pl.num_programs(0) - 1
    x = p_ref[bi]
    tm = x.shape[0]
    grp = lax.broadcasted_iota(jnp.int32, x.shape, 1) % 4
    prev, nxt = pp_ref[bi], pn_ref[bi]
    if on_grid:
        col = (lax.broadcasted_iota(jnp.int32, (tm, 1), 0) + i * tm) % GRID_W
        left = jnp.where(col == 0, 0.0, pltpu.roll(x, 1, 0))
        right = jnp.where(col == GRID_W - 1, 0.0, pltpu.roll(x, tm - 1, 0))
        up = _shift_rows(x, prev, nxt, GRID_W, first, last)
        down = _shift_rows(x, prev, nxt, -GRID_W, first, last)
        shifted = jnp.where(grp == 0, left, jnp.where(grp == 1, right, jnp.where(grp == 2, up, down)))
    else:
        before = _shift_rows(x, prev, nxt, 1, first, last)
        after = _shift_rows(x, prev, nxt, -1, first, last)
        shifted = jnp.where(grp % 2 == 0, before, after)
    xx = x + (shifted - x) * mu_ref[...]
    r = xx[:, 0:BR_W]
    k = xx[:, BR_W:2 * BR_W]
    v = xx[:, 2 * BR_W:3 * BR_W]
    lo = xx[:, 3 * BR_W:4 * BR_W]
    ll = lax.broadcasted_iota(jnp.int32, lo.shape, 1)
    act = jnp.where(ll < RW_LORA, jnp.tanh(lo), jnp.where(ll < 2 * RW_LORA, lo, _sigmoid(lo)))
    z = _dot(act.astype(BF16), wl_ref[...])
    kk = k * kv_ref[0:1, :]
    kk = kk * lax.rsqrt(_head_sum(kk * kk, seg_ref) + 1e-12)
    put(r_out, r)
    put(v_out, v)
    put(na_out, -kk)
    bonus_ref[bi] = _head_sum(r * k * kv_ref[2:3, :], seg_ref) * v
    g_ref[bi] = z[:, 4 * BR_W:5 * BR_W]
    for d in range(2):
        sig = _sigmoid(w0_ref[d:d + 1, :] + z[:, d * BR_W:(d + 1) * BR_W])
        put(w_out[2 * d:2 * d + 2], jnp.exp(-math.exp(-0.5) * sig))
        a = _sigmoid(a0_ref[d:d + 1, :] + z[:, (2 + d) * BR_W:(3 + d) * BR_W])
        put(kt_out[2 * d:2 * d + 2], k * (1.0 + (a - 1.0) * kv_ref[1:2, :]))
        put(b_out[2 * d:2 * d + 2], kk * a)


def rwkv_prep(p, mu, w_lora, w0, a0, kvec, seg, on_grid, tm):
    B, T, _ = p.shape
    halo = GRID_W if on_grid else 8
    r = tm // halo
    last = T // halo - 1
    main = pl.BlockSpec((B, tm, RW_PAD), lambda i: (0, i, OFF_RW // RW_PAD))
    prev = pl.BlockSpec((B, halo, RW_PAD), lambda i: (0, jnp.maximum(i * r - 1, 0), OFF_RW // RW_PAD))
    nxt = pl.BlockSpec((B, halo, RW_PAD), lambda i: (0, jnp.minimum((i + 1) * r, last), OFF_RW // RW_PAD))
    half = pl.BlockSpec((tm, B, LANE_W), lambda i: (i, 0, 0))
    full = pl.BlockSpec((B, tm, BR_W), lambda i: (0, i, 0))
    sh = jax.ShapeDtypeStruct((T, B, LANE_W), F32)
    sf = jax.ShapeDtypeStruct((B, T, BR_W), F32)
    return pl.pallas_call(
        functools.partial(_rw_prep_kernel, on_grid=on_grid),
        grid=(T // tm,),
        in_specs=[main, prev, nxt, _const_spec((1, RW_PAD)), _const_spec((BR_W, 5 * BR_W)),
                  _const_spec((2, BR_W)), _const_spec((2, BR_W)), _const_spec((3, BR_W)),
                  _const_spec((BR_W, BR_W))],
        out_specs=[half] * 18 + [full, full],
        out_shape=[sh] * 18 + [sf, sf],
        compiler_params=_cparams("parallel"),
        name="rwkv_prep",
    )(p, p, p, mu, w_lora, w0, a0, kvec, seg)


RW_VQ = RW_HEAD // 4
RW_VH = 2 * RW_VQ
LANE_W = 128
N_KEYED = 5
RW_UNROLL = 2
RW_PARTIALS = 1


def _chain_rows(refs, t, tb, shift):
    blocks = []
    for d in range(2):
        for half in range(2):
            x = refs[2 * d + half][t if d == 0 else tb]
            blocks.append(x if shift == 0 else pltpu.roll(x, LANE_W - shift, 1))
    return blocks


def _rw_scan_kernel(*refs, B, Tt):
    nin = 4 * (N_KEYED + 1)
    ins = refs[:nin]
    s0_ref = refs[nin]
    y_refs = refs[nin + 1:nin + 5]
    s_ref = refs[nin + 5]
    first_scratch = nin + 6
    per_set = N_KEYED + 2
    sets = [refs[first_scratch + per_set * u:first_scratch + per_set * (u + 1)] for u in range(RW_UNROLL)]
    bufs = [(st[:N_KEYED], st[N_KEYED], st[N_KEYED + 1]) for st in sets]
    sa_ref = refs[first_scratch + per_set * RW_UNROLL]
    NL = 16 * B
    A_TILE = 3

    @pl.when(pl.program_id(0) == 0)
    def _():
        s_ref[...] = s0_ref[...]
        bufs[RW_UNROLL - 1][2][...] = jnp.zeros_like(bufs[RW_UNROLL - 1][2])

    def keyed_job(t, slot, n):
        def issue():
            rows = _chain_rows(ins[4 * n:4 * n + 4], t, Tt - 1 - t, 0)
            return jnp.concatenate(rows * 4, axis=0).T

        def commit(val):
            bufs[slot][0][n][...] = val
        return issue, commit

    def values_job(t, slot):
        def issue():
            rows = []
            for vq in range(4):
                rows += _chain_rows(ins[4 * N_KEYED:], t, Tt - 1 - t, vq * RW_VQ)
            vt = jnp.concatenate(rows, axis=0).T
            return jnp.concatenate([vt[0:RW_VQ], vt[RW_HEAD:RW_HEAD + RW_VQ]], axis=0)

        def commit(val):
            bufs[slot][1][...] = val
        return issue, commit

    def output_job(t, slot):
        def issue():
            ys = bufs[slot][2][...]
            pad = jnp.zeros((RW_HEAD - RW_VQ, NL), F32)
            y = jnp.concatenate([ys[0:RW_VQ], pad, ys[RW_VQ:RW_VH], pad], axis=0).T
            out = []
            for d in range(2):
                for half in range(2):
                    acc = None
                    for vq in range(4):
                        row0 = ((vq * 2 + d) * 2 + half) * B
                        blk = y[row0:row0 + B, :]
                        blk = blk if vq == 0 else pltpu.roll(blk, vq * RW_VQ, 1)
                        acc = blk if acc is None else acc + blk
                    out.append(acc)
            return out

        def commit(val):
            for d in range(2):
                for half in range(2):
                    y_refs[2 * d + half][:, t if d == 0 else Tt - 1 - t, :] = val[2 * d + half]
        return issue, commit

    groups = range(RW_VH // 8)
    others = tuple(n for n in range(N_KEYED) if n != A_TILE)
    never = pl.program_id(0) < 0

    def step_pieces(slot, sa_in, sa_out):
        tiles, vt_ref, ys_ref = bufs[slot]
        a_next = bufs[(slot + 1) % RW_UNROLL][0][A_TILE]
        row = lambda ref, k: ref[pl.ds(k, 1), :]
        ys = [[None] * RW_PARTIALS for _ in groups]
        san = [[None] * RW_PARTIALS for _ in groups]
        vt = {}
        acc = lambda lst, i, v: lst.__setitem__(i, v if lst[i] is None else lst[i] + v)

        def update(k0, k1):
            if not vt:
                for g in groups:
                    vt[g] = vt_ref[g * 8:(g + 1) * 8, :]
            for k in range(k0, k1):
                for h2 in range(2):
                    kk = h2 * RW_HEAD + k
                    r_k, w_k, kt_k, b_k = (row(tiles[n], kk) for n in (0, 1, 2, 4))
                    an_k = row(a_next, kk)
                    for g in (2 * h2, 2 * h2 + 1):
                        s = s_ref[g, k] * w_k + sa_in[g] * b_k + vt[g] * kt_k
                        s_ref[g, k] = s
                        acc(ys[g], k % RW_PARTIALS, s * r_k)
                        acc(san[g], k % RW_PARTIALS, s * an_k)
            if k1 == RW_HEAD:
                for g in groups:
                    ys_ref[g * 8:(g + 1) * 8, :] = functools.reduce(lambda a, b: a + b, ys[g])
                    sa_out.append(functools.reduce(lambda a, b: a + b, san[g]))

        def anchor(i, val):
            g = i % len(groups)
            if val is not None and ys[g][0] is not None:
                ys[g][0] = jnp.where(never, val, ys[g][0])

        cuts = [0, 9, 18, 27, 36, 45, 54, RW_HEAD]
        return [functools.partial(update, cuts[i], cuts[i + 1]) for i in range(len(cuts) - 1)], anchor

    def run(step, jobs):
        pieces, anchor = step
        pending = None
        for i, piece in enumerate(pieces):
            val = jobs[i][0]() if i < len(jobs) else None
            piece()
            if pending is not None and i + 1 < len(pieces):
                anchor(*pending)
            pending = None
            if i < len(jobs):
                jobs[i][1](val)
                if not isinstance(val, list):
                    pending = (i, val[val.shape[0] - 8:, :])
                elif val[3].shape == (8, NL):
                    pending = (i, val[3])

    for issue, commit in ([keyed_job(0, 0, n) for n in range(N_KEYED)] + [values_job(0, 0)]
                          + [keyed_job(1, 1, A_TILE)]):
        commit(issue())
    first = [[None, None] for _ in groups]
    for k in range(RW_HEAD):
        for g in groups:
            p = s_ref[g, k] * bufs[0][0][A_TILE][pl.ds((g // 2) * RW_HEAD + k, 1), :]
            first[g][k % 2] = p if first[g][k % 2] is None else first[g][k % 2] + p
    for g in groups:
        sa_ref[g] = first[g][0] + first[g][1]

    def body(j, carry):
        t0 = RW_UNROLL * j
        sa = [sa_ref[g] for g in groups]
        for u in range(RW_UNROLL):
            t = t0 + u
            t1 = jnp.minimum(t + 1, Tt - 1)
            jobs = [output_job(jnp.maximum(t - 1, 0), (u - 1) % RW_UNROLL),
                    keyed_job(jnp.minimum(t + 2, Tt - 1), (u + 2) % RW_UNROLL, A_TILE)]
            jobs += [keyed_job(t1, (u + 1) % RW_UNROLL, n) for n in others] + [values_job(t1, (u + 1) % RW_UNROLL)]
            sa_next = []
            run(step_pieces(u, sa, sa_next), jobs)
            sa = sa_next
        for g in groups:
            sa_ref[g] = sa[g]
        return carry

    lax.fori_loop(0, Tt // RW_UNROLL, body, 0)
    issue, commit = output_job(Tt - 1, RW_UNROLL - 1)
    commit(issue())


def rwkv_mix(prep, s0, tt):
    T, B, _ = prep[0][0].shape
    nb = T // tt
    tblk = lambda i, d: i if d == 0 else nb - 1 - i

    in_specs, args = [], []
    for quad in prep:
        for j, x in enumerate(quad):
            in_specs.append(pl.BlockSpec((tt, B, LANE_W), lambda i, d=j // 2: (tblk(i, d), 0, 0)))
            args.append(x)
    NL = 16 * B
    sblk = pl.BlockSpec((RW_VH // 8, RW_HEAD, 8, NL), lambda i: (0, 0, 0, 0))
    yspecs = [pl.BlockSpec((B, tt, LANE_W), lambda i, d=d: (0, tblk(i, d), 0)) for d in range(2) for _ in range(2)]
    ysh = jax.ShapeDtypeStruct((B, T, LANE_W), F32)
    out = pl.pallas_call(
        functools.partial(_rw_scan_kernel, B=B, Tt=tt),
        grid=(nb,),
        in_specs=in_specs + [sblk],
        out_specs=yspecs + [sblk],
        out_shape=[ysh] * 4 + [jax.ShapeDtypeStruct((RW_VH // 8, RW_HEAD, 8, NL), F32)],
        scratch_shapes=([pltpu.VMEM((LANE_W, NL), F32)] * N_KEYED + [pltpu.VMEM((RW_VH, NL), F32)] * 2) * RW_UNROLL
        + [pltpu.VMEM((RW_VH // 8, 8, NL), F32)],
        compiler_params=_cparams("arbitrary"),
        name="rwkv_scan",
    )(*args, s0)
    return (out[0], out[1]), (out[2], out[3]), out[4]


def _gelu_tanh(x):
    return 0.5 * x * (1.0 + jnp.tanh(math.sqrt(2.0 / math.pi) * (x + 0.044715 * (x * x * x))))


def _lru_prep_kernel(p_ref, pp_ref, pn_ref, cw_ref, cb_ref, w_ref, bias_ref, lam_ref, a_ref, b_ref, gg_ref):
    i = pl.program_id(1)
    first = i == 0
    last = i == pl.num_programs(1) - 1
    x = p_ref[0][:, 0:BR_W]
    prev = pp_ref[0][:, 0:BR_W]
    nxt = pn_ref[0][:, 0:BR_W]
    xc = cb_ref[...] + cw_ref[0:1, :] * _shift_rows(x, prev, nxt, 2, first, last)
    xc = xc + cw_ref[1:2, :] * _shift_rows(x, prev, nxt, 1, first, last)
    xc = xc + cw_ref[2:3, :] * x
    xc = xc + cw_ref[3:4, :] * _shift_rows(x, prev, nxt, -1, first, last)
    z = _dot(xc.astype(BF16), w_ref[...])
    for d in range(2):
        r = _sigmoid(z[:, 2 * d * BR_W:(2 * d + 1) * BR_W] + bias_ref[2 * d:2 * d + 1, :])
        gi = _sigmoid(z[:, (2 * d + 1) * BR_W:(2 * d + 2) * BR_W] + bias_ref[2 * d + 1:2 * d + 2, :])
        log_a = -LRU_C * r * _softplus(-lam_ref[d:d + 1, :])
        a_ref[d, 0] = jnp.exp(log_a)
        b_ref[d, 0] = jnp.sqrt(1.0 - jnp.exp(2.0 * log_a)) * (gi * xc)
    gg_ref[0] = _gelu_tanh(p_ref[0][:, BR_W:2 * BR_W])


def lru_prep(p, conv_w, conv_b, w_blk, bias, lam, tm):
    B, T, _ = p.shape
    main, prev, nxt = _halo_specs(tm, T, LRU_IN, lambda b, i: OFF_LRU // LRU_IN, 8)
    two = pl.BlockSpec((2, 1, tm, BR_W), lambda b, i: (0, b, i, 0))
    s2 = jax.ShapeDtypeStruct((2, B, T, BR_W), F32)
    return pl.pallas_call(
        _lru_prep_kernel,
        grid=(B, T // tm),
        in_specs=[main, prev, nxt, _const_spec((4, BR_W)), _const_spec((1, BR_W)),
                  _const_spec((BR_W, 4 * BR_W)), _const_spec((4, BR_W)), _const_spec((2, BR_W))],
        out_specs=[two, two, pl.BlockSpec((1, tm, BR_W), lambda b, i: (b, i, 0))],
        out_shape=[s2, s2, jax.ShapeDtypeStruct((B, T, BR_W), F32)],
        compiler_params=_cparams("parallel", "parallel"),
        name="lru_prep",
    )(p, p, p, conv_w, conv_b, w_blk, bias, lam)


def _affine_scan(a, b, reverse):
    tb = a.shape[0]
    row = lax.broadcasted_iota(jnp.int32, (tb, 1), 0)
    s = 1
    while s < tb:
        sh = tb - s if reverse else s
        ok = (row < tb - s) if reverse else (row >= s)
        a_s = pltpu.roll(a, sh, 0)
        b_s = pltpu.roll(b, sh, 0)
        b = jnp.where(ok, a * b_s + b, b)
        a = jnp.where(ok, a * a_s, a)
        s *= 2
    return a, b


def _lru_scan_kernel(af_ref, bf_ref, ab_ref, bb_ref, h0_ref, hf_ref, hb_ref, fin_ref):
    @pl.when(pl.program_id(1) == 0)
    def _():
        fin_ref[...] = h0_ref[...]

    tb = af_ref.shape[2]
    a, b = _affine_scan(af_ref[0, 0], bf_ref[0, 0], False)
    h = b + a * fin_ref[0, 0]
    hf_ref[0] = h
    fin_ref[0, 0] = h[tb - 1:tb, :]
    a, b = _affine_scan(ab_ref[0, 0], bb_ref[0, 0], True)
    h = b + a * fin_ref[1, 0]
    hb_ref[0] = h
    fin_ref[1, 0] = h[0:1, :]


def lru_scan(a, b, h0, tb):
    _, B, T, C = a.shape
    nb = T // tb
    fwd = pl.BlockSpec((1, 1, tb, C), lambda bi, i: (0, bi, i, 0))
    bwd = pl.BlockSpec((1, 1, tb, C), lambda bi, i: (1, bi, nb - 1 - i, 0))
    st = pl.BlockSpec((2, 1, 1, C), lambda bi, i: (0, bi, 0, 0))
    return pl.pallas_call(
        _lru_scan_kernel,
        grid=(B, nb),
        in_specs=[fwd, fwd, bwd, bwd, st],
        out_specs=[pl.BlockSpec((1, tb, C), lambda bi, i: (bi, i, 0)),
                   pl.BlockSpec((1, tb, C), lambda bi, i: (bi, nb - 1 - i, 0)), st],
        out_shape=[jax.ShapeDtypeStruct((B, T, C), F32), jax.ShapeDtypeStruct((B, T, C), F32),
                   jax.ShapeDtypeStruct((2, B, 1, C), F32)],
        compiler_params=_cparams("parallel", "arbitrary"),
        name="lru_scan",
    )(a, b, a, b, h0)


def _rope(x, cos, sin):
    q4 = RET_HEAD // 4
    lane = lax.broadcasted_iota(jnp.int32, x.shape, 1) % (2 * q4)
    partner = jnp.where(lane < q4, pltpu.roll(x, x.shape[1] - q4, 1), pltpu.roll(x, q4, 1))
    return x * cos + partner * sin


def _ret_decay_tables(glane, gtile_ref, d, reverse, C, dm_ref, qd_ref, kd_ref):
    lg = -_softplus(-glane)
    idx = lax.broadcasted_iota(jnp.int32, (C, 1), 0).astype(F32)
    steps_in = (C - idx) if reverse else (idx + 1.0)
    steps_out = idx if reverse else (C - 1.0 - idx)
    qd_ref[d] = jnp.exp(steps_in * lg)
    kd_ref[d] = jnp.exp(steps_out * lg)
    ri = lax.broadcasted_iota(jnp.int32, (C, C), 0)
    ci = lax.broadcasted_iota(jnp.int32, (C, C), 1)
    diff = ((ci - ri) if reverse else (ri - ci)).astype(F32)
    for h in range(RET_HEADS):
        lg_h = -_softplus(-gtile_ref[d, h][0:1, :])
        dm_ref[d, h] = jnp.where(diff >= 0, jnp.exp(diff * lg_h), 0.0)


def _ret_dir(x, cos, sin, s, glane, d, dm_ref, qd_ref, kd_ref):
    C = x.shape[0]
    q = x[:, 0:BR_W]
    k = x[:, BR_W:2 * BR_W]
    v = x[:, 2 * BR_W:3 * BR_W].astype(BF16)
    if cos is not None:
        q = _rope(q, cos, sin)
        k = _rope(k, cos, sin)
    k = k * (RET_HEAD ** -0.5)
    lg = -_softplus(-glane)
    lane_head = lax.broadcasted_iota(jnp.int32, (1, BR_W), 1) // RET_HEAD
    qb = q.astype(BF16)
    kb = k.astype(BF16)
    y = _dot(qb, s.astype(BF16)) * qd_ref[d]
    for h in range(RET_HEADS):
        mh = lane_head == h
        sc = lax.dot_general(jnp.where(mh, qb, jnp.zeros_like(qb)), kb, (((1,), (1,)), ((), ())),
                             preferred_element_type=F32)
        y = y + jnp.where(mh, _dot((sc * dm_ref[d, h]).astype(BF16), v), 0.0)
    kd = (k * kd_ref[d]).astype(BF16)
    upd = lax.dot_general(kd, v, (((0,), (0,)), ((), ())), preferred_element_type=F32)
    rh = lax.broadcasted_iota(jnp.int32, (BR_W, BR_W), 0) // RET_HEAD
    ch = lax.broadcasted_iota(jnp.int32, (BR_W, BR_W), 1) // RET_HEAD
    s = s * jnp.exp(C * lg) + jnp.where(rh == ch, upd, 0.0)
    return y, s


def _ret_kernel(*refs, rope):
    if rope:
        xf_ref, xb_ref, cf_ref, sf_ref, cb_ref, sb_ref, gl_ref, gt_ref, s0_ref, yf_ref, yb_ref, s_ref = refs[:12]
    else:
        xf_ref, xb_ref, gl_ref, gt_ref, s0_ref, yf_ref, yb_ref, s_ref = refs[:8]
    tables = refs[-3:]
    C = xf_ref.shape[1]

    @pl.when(pl.program_id(1) == 0)
    def _():
        s_ref[...] = s0_ref[...]
        _ret_decay_tables(gl_ref[0:1, :], gt_ref, 0, False, C, *tables)
        _ret_decay_tables(gl_ref[1:2, :], gt_ref, 1, True, C, *tables)

    y, s = _ret_dir(xf_ref[0], cf_ref[...] if rope else None, sf_ref[...] if rope else None,
                    s_ref[0, 0], gl_ref[0:1, :], 0, *tables)
    yf_ref[0] = y
    s_ref[0, 0] = s
    y, s = _ret_dir(xb_ref[0], cb_ref[...] if rope else None, sb_ref[...] if rope else None,
                    s_ref[1, 0], gl_ref[1:2, :], 1, *tables)
    yb_ref[0] = y
    s_ref[1, 0] = s


def retention(p, cos, sin, glane, gtile, s0, rope):
    B, T, _ = p.shape
    C = RET_CHUNK
    nc = T // C
    cb = OFF_RET // RET_IN
    xf = pl.BlockSpec((1, C, RET_IN), lambda b, i: (b, i, cb))
    xb = pl.BlockSpec((1, C, RET_IN), lambda b, i: (b, nc - 1 - i, cb))
    tf = pl.BlockSpec((C, BR_W), lambda b, i: (i, 0))
    tb = pl.BlockSpec((C, BR_W), lambda b, i: (nc - 1 - i, 0))
    st = pl.BlockSpec((2, 1, BR_W, BR_W), lambda b, i: (0, b, 0, 0))
    ins = [xf, xb] + ([tf, tf, tb, tb] if rope else []) + [
        _const_spec((2, BR_W)), _const_spec((2, RET_HEADS, 8, C)), st]
    args = [p, p] + ([cos, sin, cos, sin] if rope else []) + [glane, gtile, s0]
    return pl.pallas_call(
        functools.partial(_ret_kernel, rope=rope),
        grid=(B, nc),
        in_specs=ins,
        out_specs=[pl.BlockSpec((1, C, BR_W), lambda b, i: (b, i, 0)),
                   pl.BlockSpec((1, C, BR_W), lambda b, i: (b, nc - 1 - i, 0)), st],
        out_shape=[jax.ShapeDtypeStruct((B, T, BR_W), F32), jax.ShapeDtypeStruct((B, T, BR_W), F32),
                   jax.ShapeDtypeStruct((2, B, BR_W, BR_W), F32)],
        scratch_shapes=[pltpu.VMEM((2, RET_HEADS, C, C), F32), pltpu.VMEM((2, C, BR_W), F32),
                        pltpu.VMEM((2, C, BR_W), F32)],
        compiler_params=_cparams("parallel", "arbitrary"),
        name="retention",
    )(*args)


@functools.lru_cache(maxsize=None)
def _rope_tables(T):
    pos = np.arange(T)
    q4 = RET_HEAD // 4
    inv = ROPE_BASE ** (-np.arange(q4, dtype=np.float64) / q4)
    cos = np.zeros((T, RET_HEAD))
    sin = np.zeros((T, RET_HEAD))
    for part, coord in enumerate((pos // GRID_W, pos % GRID_W)):
        ang = coord[:, None] * inv
        base = part * 2 * q4
        cos[:, base:base + q4] = np.cos(ang)
        cos[:, base + q4:base + 2 * q4] = np.cos(ang)
        sin[:, base:base + q4] = -np.sin(ang)
        sin[:, base + q4:base + 2 * q4] = np.sin(ang)
    tile = lambda t: np.asarray(np.tile(t, (1, RET_HEADS)), np.float32)
    return tile(cos), tile(sin)


def _head_norm(y, seg_ref, eps):
    mu = _head_sum(y, seg_ref) * (1.0 / RW_HEAD)
    yc = y - mu
    var = _head_sum(yc * yc, seg_ref) * (1.0 / RW_HEAD)
    return yc * lax.rsqrt(var + eps)


def _merge_kernel(x_ref, g_ref, sh_ref, sc_ref, gt_ref, hy_ref, rfl_ref, rfh_ref, rbl_ref, rbh_ref,
                  rbon_ref, rg_ref, lhf_ref, lhb_ref, lgg_ref, tyf_ref, tyb_ref, tg_ref, lng_ref, seg_ref,
                  wg_ref, br_ref, wo_ref, o_ref, m_ref):
    x = x_ref[0]
    u = _norm_mod(x, g_ref[...], sh_ref[0], sc_ref[0]).astype(BF16)
    wkv = jnp.concatenate([rfl_ref[0] + rbl_ref[0], rfh_ref[0] + rbh_ref[0]], axis=1)
    y_rw = (_head_norm(wkv, seg_ref, RW_LN_EPS) * lng_ref[...] + rbon_ref[0]) * rg_ref[0]
    y_lru = (lhf_ref[0] + lhb_ref[0]) * lgg_ref[0]
    y_ret = _head_norm(tyf_ref[0] + tyb_ref[0], seg_ref, RET_LN_EPS) * _silu(tg_ref[0])
    ys = [y.astype(BF16) for y in (hy_ref[0], y_rw, y_lru, y_ret)]
    D = x.shape[1]
    cw = 256
    for c in range(D // cw):
        acc = None
        for n in range(N_BRANCH):
            gate = _sigmoid(_dot(u, wg_ref[:, n * D + c * cw:n * D + (c + 1) * cw]))
            t = gate * _dot(ys[n], br_ref[n, :, c * cw:(c + 1) * cw])
            acc = t if acc is None else acc + t
        m_ref[:, c * cw:(c + 1) * cw] = acc.astype(BF16)
    o_ref[0] = x + gt_ref[0] * _dot(m_ref[...], wo_ref[...])


def merge(x, g, sh, sc, gt, p, y_hy, rw, lru, ret, ln_g, seg, w_gate, br, w_out, tm):
    B, T, D = x.shape
    vec = pl.BlockSpec((1, 1, D), lambda b, i: (b, 0, 0))
    row = pl.BlockSpec((1, tm, D), lambda b, i: (b, i, 0))
    brn = pl.BlockSpec((1, tm, BR_W), lambda b, i: (b, i, 0))
    half = pl.BlockSpec((1, tm, LANE_W), lambda b, i: (b, i, 0))
    tg = pl.BlockSpec((1, tm, BR_W), lambda b, i: (b, i, (OFF_RET + 3 * BR_W) // BR_W))
    return pl.pallas_call(
        _merge_kernel,
        grid=(B, T // tm),
        in_specs=[row, _const_spec((1, D)), vec, vec, vec, brn] + [half] * 4 + [brn] * 7 + [tg] + [
            _const_spec((1, BR_W)), _const_spec((BR_W, BR_W)), _const_spec((D, GATE_IN)),
            _const_spec((N_BRANCH, BR_W, D)), _const_spec((D, D))],
        out_specs=row,
        out_shape=jax.ShapeDtypeStruct((B, T, D), F32),
        scratch_shapes=[pltpu.VMEM((tm, D), BF16)],
        compiler_params=_cparams("parallel", "parallel"),
        name="merge",
    )(x, g, sh, sc, gt, y_hy, *rw, *lru, *ret, p, ln_g, seg, w_gate, br, w_out)


def _block_diag(w):
    G = w.shape[-3]
    eye = jnp.eye(G, dtype=w.dtype)
    full = w[..., :, :, None, :] * eye[:, None, :, None]
    return full.reshape(*w.shape[:-3], G * w.shape[-2], G * w.shape[-1])


def _mixers(p, lp, states, on_grid, with_output, tiles):
    B, T, _ = p.shape
    tm, tt, _ = tiles
    pre = rwkv_prep(p, *lp['rw'], on_grid, GRID_W)
    r, v, na, w, kt, b = pre[0:2] * 2, pre[2:4] * 2, pre[4:6] * 2, pre[6:10], pre[10:14], pre[14:18]
    bonus, g = pre[18], pre[19]
    y_f, y_b, rw_fin = rwkv_mix((r, w, kt, na, b, v), states[0], tt)
    a, bb, gg = lru_prep(p, *lp['lru'], tm)
    h_f, h_b, lru_fin = lru_scan(a, bb, states[1], tm)
    cos, sin = (jnp.asarray(t) for t in _rope_tables(T)) if on_grid else (None, None)
    t_f, t_b, ret_fin = retention(p, cos, sin, *lp['ret'], states[2], on_grid)
    fins = (rw_fin, lru_fin, ret_fin)
    if not with_output:
        return None, fins
    vg = hyena_prep(p, *lp['hy_conv'], min(T, 2048))
    y_hy = hyena_branch(vg[0], vg[1], vg[2], lp['hy'])
    return (y_hy, (y_f[0], y_f[1], y_b[0], y_b[1], bonus, g), (h_f, h_b, gg), (t_f, t_b)), fins


def kernel(x, c, ctx, c_ctx, w_mod, b_mod, norm1_g, norm2_g, w_in, hy_conv_w, hy_conv_b, hy_f_w1, hy_f_b1, hy_f_w2, hy_f_b2, hy_f_w3, hy_freq, hy_bias, rw_mu, rw_w0, rw_w2, rw_a0, rw_a2, rw_g2, rw_kk, rw_ka, rw_rk, rw_ln_g, lru_conv_w, lru_conv_b, lru_wa, lru_ba, lru_wx, lru_bx, lru_lam, ret_gamma, br_proj, w_out, ffn_w1, ffn_w2, final_g):
    B, T, D = x.shape
    TC = ctx.shape[1]
    L = w_in.shape[0]

    s0, s1, s2, s3 = HY_IN, HY_IN + RW_IN, HY_IN + RW_IN + LRU_IN, HY_IN + RW_IN + LRU_IN + RET_IN
    w_branch = jnp.concatenate([w_in[:, :, s0:s1], jnp.zeros((L, D, RW_PAD - RW_IN), w_in.dtype),
                                w_in[:, :, s2:s3], jnp.zeros((L, D, HY_GAP), w_in.dtype),
                                w_in[:, :, 0:s0], w_in[:, :, s1:s2]], axis=2).astype(BF16)
    w_gate = w_in[:, :, s3:].astype(BF16)
    mu = jnp.pad(rw_mu, ((0, 0), (0, RW_PAD - RW_IN)))[:, None, :]
    w_lora = jnp.zeros((L, BR_W, 5 * BR_W), F32)
    w_lora = w_lora.at[:, 0:RW_LORA, 0:BR_W].set(rw_w2[:, 0]).at[:, 0:RW_LORA, BR_W:2 * BR_W].set(rw_w2[:, 1])
    w_lora = w_lora.at[:, RW_LORA:2 * RW_LORA, 2 * BR_W:3 * BR_W].set(rw_a2[:, 0])
    w_lora = w_lora.at[:, RW_LORA:2 * RW_LORA, 3 * BR_W:4 * BR_W].set(rw_a2[:, 1])
    w_lora = w_lora.at[:, 2 * RW_LORA:3 * RW_LORA, 4 * BR_W:5 * BR_W].set(rw_g2).astype(BF16)
    kvec = jnp.stack([rw_kk, rw_ka, rw_rk], axis=1)
    seg = jnp.asarray(np.kron(np.eye(RW_HEADS), np.ones((RW_HEAD, RW_HEAD))), BF16)
    lru_w = jnp.concatenate([_block_diag(lru_wa[:, 0]), _block_diag(lru_wx[:, 0]),
                             _block_diag(lru_wa[:, 1]), _block_diag(lru_wx[:, 1])], axis=2).astype(BF16)
    lru_bias = jnp.stack([lru_ba[:, 0], lru_bx[:, 0], lru_ba[:, 1], lru_bx[:, 1]], axis=1)
    glane = jnp.repeat(ret_gamma, RET_HEAD, axis=2)
    gtile = jnp.broadcast_to(ret_gamma[:, :, :, None, None], (L, 2, RET_HEADS, 8, RET_CHUNK))
    f_w1 = jnp.pad(hy_f_w1, ((0, 0), (0, HY_FEAT_PAD - HY_FEAT), (0, 0)))
    br_b = br_proj.astype(BF16)
    w_out_b = w_out.astype(BF16)
    ffn_gate = ffn_w1[:, :, :D_FF].astype(BF16)
    ffn_up_w = ffn_w1[:, :, D_FF:].astype(BF16)
    ffn_w2_b = ffn_w2.astype(BF16)

    cc = jnp.concatenate([c, c_ctx[None, :], jnp.zeros((16 - B - 1, D), F32)], axis=0)
    mods = modulation(cc, w_mod.astype(BF16), b_mod[:, None, :])

    zero_states = (jnp.zeros((RW_VH // 8, RW_HEAD, 8, 16 * B), F32), jnp.zeros((2, B, 1, BR_W), F32),
                   jnp.zeros((2, B, BR_W, BR_W), F32))
    xc = ctx
    for l in range(L):
        last = l == L - 1
        lp = {
            'rw': (mu[l], w_lora[l], rw_w0[l], rw_a0[l], kvec[l], seg),
            'lru': (lru_conv_w[l], lru_conv_b[l][None, :], lru_w[l], lru_bias[l], lru_lam[l]),
            'ret': (glane[l], gtile[l]),
            'hy_conv': (hy_conv_w[l], hy_conv_b[l][None, :]),
            'hy': (f_w1[l], hy_f_b1[l][None, :], hy_f_w2[l], hy_f_b2[l][None, :], hy_f_w3[l], hy_freq[l],
                   hy_bias[l]),
        }
        g1 = norm1_g[l][None, :]
        g2 = norm2_g[l][None, :]
        m_lat = [m[:, None, :] for m in jnp.split(mods[l, :B], 6, axis=-1)]
        m_ctx = [jnp.broadcast_to(m[None, :, :], (B, 1, D)) for m in jnp.split(mods[l, B:B + 1], 6, axis=-1)]
        ln_g = rw_ln_g[l][None, :]

        def layer(xs, m, states, on_grid, with_output, tiles, final):
            p = in_projection(xs, g1, m[0], m[1], w_branch[l], tiles[0])
            br, fins = _mixers(p, lp, states, on_grid, with_output, tiles)
            if not with_output:
                return None, fins
            xs = merge(xs, g1, m[0], m[1], m[2], p, br[0], br[1], br[2], br[3], ln_g, seg,
                       w_gate[l], br_b[l], w_out_b[l], tiles[2])
            h = ffn_up(xs, g2, m[3], m[4], ffn_gate[l], ffn_up_w[l], tiles[0])
            xs = ffn_down(h, xs, m[5], ffn_w2_b[l], final_g[None, :], final, tiles[0])
            return xs, fins

        xc_new, ctx_states = layer(xc, m_ctx, zero_states, False, not last, (TC, 64, TC), False)
        x, _ = layer(x, m_lat, ctx_states, True, True, (512, 128, 512), last)
        if not last:
            xc = xc_new
    return x
```

```python
import functools
import math

import numpy as np
import jax
import jax.numpy as jnp
from jax import lax
from jax.experimental import pallas as pl
from jax.experimental.pallas import tpu as pltpu

F32 = jnp.float32
BF16 = jnp.bfloat16

D_MODEL = 1024
DEPTH = 4
GRID_W = 64
N_BRANCH = 4
BR_W = D_MODEL // N_BRANCH

HY_BANDS = 8
HY_FEAT = 1 + 2 * HY_BANDS
HY_FEAT_PAD = 32
HY_HID = 64
HY_TARGET = 1e-2
HY_FAST = 0.3
HY_SLOW = 1.5
HY_IN = 3 * BR_W

RW_HEAD = 64
RW_HEADS = BR_W // RW_HEAD
RW_LORA = 64
RW_IN = 3 * BR_W + 3 * RW_LORA
RW_PAD = 4 * BR_W
RW_LN_EPS = 64e-5

LRU_BLOCKS = 4
LRU_BLOCK = BR_W // LRU_BLOCKS
LRU_C = 8.0
LRU_IN = 2 * BR_W

RET_HEADS = 4
RET_HEAD = BR_W // RET_HEADS
RET_CHUNK = 128
ROPE_BASE = 10000.0
RET_IN = 4 * BR_W
RET_LN_EPS = 1e-5

GATE_IN = N_BRANCH * D_MODEL
D_FF = ((8 * D_MODEL // 3 + 255) // 256) * 256
EPS = 1e-6

OFF_RW = 0
OFF_RET = OFF_RW + RW_PAD
OFF_HY = 3 * HY_IN
OFF_LRU = OFF_HY + HY_IN
N_BR = OFF_LRU + LRU_IN
HY_GAP = OFF_HY - (OFF_RET + RET_IN)

VMEM_LIMIT = 56 * 1024 * 1024
HI = lax.Precision.HIGHEST


def _cparams(*sem):
    return pltpu.CompilerParams(dimension_semantics=sem, vmem_limit_bytes=VMEM_LIMIT)


def _const_spec(shape):
    nd = len(shape)
    return pl.BlockSpec(shape, lambda *_: (0,) * nd, pipeline_mode=pl.Buffered(1))


def _dot(a, b, **kw):
    return jnp.dot(a, b, preferred_element_type=F32, **kw)


def _norm_mod(x, g, sh, sc):
    ms = jnp.mean(x * x, axis=-1, keepdims=True)
    return x * lax.rsqrt(ms + EPS) * g * (1.0 + sc) + sh


def _sigmoid(x):
    return 1.0 / (1.0 + jnp.exp(-x))


def _silu(x):
    return x * _sigmoid(x)


def _softplus(x):
    return jnp.maximum(x, 0.0) + jnp.log(1.0 + jnp.exp(-jnp.abs(x)))


def _mod_kernel(c_ref, w_ref, b_ref, o_ref):
    c = c_ref[...]
    o_ref[0] = _dot(_silu(c).astype(BF16), w_ref[0]) + b_ref[0]


def modulation(cc, w_mod, b_mod):
    L, D, N = w_mod.shape
    tn = 1536
    return pl.pallas_call(
        _mod_kernel,
        grid=(L, N // tn),
        in_specs=[pl.BlockSpec((16, D), lambda l, j: (0, 0)),
                  pl.BlockSpec((1, D, tn), lambda l, j: (l, 0, j)),
                  pl.BlockSpec((1, 1, tn), lambda l, j: (l, 0, j))],
        out_specs=pl.BlockSpec((1, 16, tn), lambda l, j: (l, 0, j)),
        out_shape=jax.ShapeDtypeStruct((L, 16, N), F32),
        compiler_params=_cparams("parallel", "parallel"),
        name="modulation",
    )(cc, w_mod, b_mod)


def _inproj_kernel(x_ref, g_ref, sh_ref, sc_ref, w_ref, o_ref):
    u = _norm_mod(x_ref[0], g_ref[...], sh_ref[0], sc_ref[0]).astype(BF16)
    n = w_ref.shape[1]
    cw = 256
    for c in range(n // cw):
        o_ref[0, :, c * cw:(c + 1) * cw] = _dot(u, w_ref[:, c * cw:(c + 1) * cw])


def in_projection(x, g, sh, sc, w, tm):
    B, T, D = x.shape
    N = w.shape[1]
    vec = pl.BlockSpec((1, 1, D), lambda b, i: (b, 0, 0))
    return pl.pallas_call(
        _inproj_kernel,
        grid=(B, T // tm),
        in_specs=[pl.BlockSpec((1, tm, D), lambda b, i: (b, i, 0)),
                  _const_spec((1, D)), vec, vec, _const_spec((D, N))],
        out_specs=pl.BlockSpec((1, tm, N), lambda b, i: (b, i, 0)),
        out_shape=jax.ShapeDtypeStruct((B, T, N), F32),
        compiler_params=_cparams("parallel", "parallel"),
        name="in_projection",
    )(x, g, sh, sc, w)


def _ffn1_kernel(x_ref, g_ref, sh_ref, sc_ref, wg_ref, wu_ref, o_ref):
    u = _norm_mod(x_ref[0], g_ref[...], sh_ref[0], sc_ref[0]).astype(BF16)
    n = wg_ref.shape[1]
    cw = 256
    for c in range(n // cw):
        sl = slice(c * cw, (c + 1) * cw)
        gate = _dot(u, wg_ref[:, sl])
        up = _dot(u, wu_ref[:, sl])
        o_ref[0, :, sl] = (_silu(gate) * up).astype(BF16)


def ffn_up(x, g, sh, sc, w_gate, w_up, tm):
    B, T, D = x.shape
    N = w_gate.shape[1]
    vec = pl.BlockSpec((1, 1, D), lambda b, i: (b, 0, 0))
    return pl.pallas_call(
        _ffn1_kernel,
        grid=(B, T // tm),
        in_specs=[pl.BlockSpec((1, tm, D), lambda b, i: (b, i, 0)),
                  _const_spec((1, D)), vec, vec, _const_spec((D, N)), _const_spec((D, N))],
        out_specs=pl.BlockSpec((1, tm, N), lambda b, i: (b, i, 0)),
        out_shape=jax.ShapeDtypeStruct((B, T, N), BF16),
        compiler_params=_cparams("parallel", "parallel"),
        name="ffn_up",
    )(x, g, sh, sc, w_gate, w_up)


def _ffn2_kernel(h_ref, x_ref, gt_ref, w_ref, fg_ref, o_ref, *, final_norm):
    y = x_ref[0] + gt_ref[0] * _dot(h_ref[0], w_ref[...])
    if final_norm:
        ms = jnp.mean(y * y, axis=-1, keepdims=True)
        y = y * lax.rsqrt(ms + EPS) * fg_ref[...]
    o_ref[0] = y


def ffn_down(h, x, gate, w, final_g, final_norm, tm):
    B, T, D = x.shape
    N = h.shape[2]
    return pl.pallas_call(
        functools.partial(_ffn2_kernel, final_norm=final_norm),
        grid=(B, T // tm),
        in_specs=[pl.BlockSpec((1, tm, N), lambda b, i: (b, i, 0)),
                  pl.BlockSpec((1, tm, D), lambda b, i: (b, i, 0)),
                  pl.BlockSpec((1, 1, D), lambda b, i: (b, 0, 0)),
                  _const_spec((N, D)), _const_spec((1, D))],
        out_specs=pl.BlockSpec((1, tm, D), lambda b, i: (b, i, 0)),
        out_shape=jax.ShapeDtypeStruct((B, T, D), F32),
        compiler_params=_cparams("parallel", "parallel"),
        name="ffn_down",
    )(h, x, gate, w, final_g)


def _halo_specs(tm, T, width, col_fn, halo):
    r = tm // halo
    last = T // halo - 1
    main = pl.BlockSpec((1, tm, width), lambda *g: (g[0], g[1], col_fn(*g)))
    prev = pl.BlockSpec((1, halo, width), lambda *g: (g[0], jnp.maximum(g[1] * r - 1, 0), col_fn(*g)))
    nxt = pl.BlockSpec((1, halo, width), lambda *g: (g[0], jnp.minimum((g[1] + 1) * r, last), col_fn(*g)))
    return main, prev, nxt


def _shift_rows(x, prev, nxt, s, first, last):
    tm = x.shape[0]
    if s > 0:
        head = jnp.where(first, 0.0, prev[prev.shape[0] - s:, :])
        return head if s == tm else jnp.concatenate([head, x[:tm - s, :]], axis=0)
    s = -s
    tail = jnp.where(last, 0.0, nxt[:s, :])
    return tail if s == tm else jnp.concatenate([x[s:, :], tail], axis=0)


def _hy_prep_kernel(p_ref, pp_ref, pn_ref, w_ref, b_ref, v_ref, g1_ref, g2_ref):
    i = pl.program_id(1)
    first = i == 0
    last = i == pl.num_programs(1) - 1
    x = p_ref[0]
    xm = _shift_rows(x, pp_ref[0], pn_ref[0], 1, first, last)
    xp = _shift_rows(x, pp_ref[0], pn_ref[0], -1, first, last)
    u = b_ref[...] + w_ref[0:1, :] * xm + w_ref[1:2, :] * x + w_ref[2:3, :] * xp
    v_ref[0] = u[:, 0:BR_W]
    g1_ref[0] = u[:, BR_W:2 * BR_W]
    g2_ref[0] = u[:, 2 * BR_W:3 * BR_W]


def hyena_prep(p, conv_w, conv_b, tm):
    B, T, _ = p.shape
    main, prev, nxt = _halo_specs(tm, T, HY_IN, lambda b, i: OFF_HY // HY_IN, 8)
    out = pl.BlockSpec((1, tm, BR_W), lambda b, i: (b, i, 0))
    shp = jax.ShapeDtypeStruct((B, T, BR_W), F32)
    return pl.pallas_call(
        _hy_prep_kernel,
        grid=(B, T // tm),
        in_specs=[main, prev, nxt, _const_spec((3, HY_IN)), _const_spec((1, HY_IN))],
        out_specs=[out, out, out],
        out_shape=[shp, shp, shp],
        compiler_params=_cparams("parallel", "parallel"),
        name="hyena_prep",
    )(p, p, p, conv_w, conv_b)


def _hy_filter_kernel(feat_ref, w1_ref, b1_ref, w2_ref, b2_ref, w3_ref, fq_ref, rates_ref, h_ref, ss_ref):
    i = pl.program_id(0)
    feat = feat_ref[...]
    t = feat[:, 0:1]
    h = jnp.sin(fq_ref[0:1, :] * (_dot(feat, w1_ref[...], precision=HI) + b1_ref[...]))
    h = jnp.sin(fq_ref[1:2, :] * (_dot(h, w2_ref[...], precision=HI) + b2_ref[...]))
    h = _dot(h, w3_ref[...], precision=HI) * jnp.exp(-t * rates_ref[...])
    row = lax.broadcasted_iota(jnp.int32, h.shape, 0) + i * h.shape[0]
    col = lax.broadcasted_iota(jnp.int32, h.shape, 1)
    h = jnp.where((row == 0) & ((col // BR_W) % 2 == 1), 0.0, h)
    h_ref[...] = h

    @pl.when(i == 0)
    def _():
        ss_ref[...] = jnp.zeros_like(ss_ref)

    ss_ref[...] += jnp.sum(h * h, axis=0, keepdims=True)


def hyena_filter(feat, w1, b1, w2, b2, w3, freq, rates, tl):
    L = feat.shape[0]
    C = w3.shape[1]
    return pl.pallas_call(
        _hy_filter_kernel,
        grid=(L // tl,),
        in_specs=[pl.BlockSpec((tl, HY_FEAT_PAD), lambda i: (i, 0)),
                  _const_spec((HY_FEAT_PAD, HY_HID)), _const_spec((1, HY_HID)),
                  _const_spec((HY_HID, HY_HID)), _const_spec((1, HY_HID)),
                  _const_spec((HY_HID, C)), _const_spec((2, HY_HID)), _const_spec((1, C))],
        out_specs=[pl.BlockSpec((tl, C), lambda i: (i, 0)), pl.BlockSpec((1, C), lambda i: (0, 0))],
        out_shape=[jax.ShapeDtypeStruct((L, C), F32), jax.ShapeDtypeStruct((1, C), F32)],
        compiler_params=_cparams("arbitrary"),
        name="hyena_filter",
    )(feat, w1, b1, w2, b2, w3, freq, rates)


def _filter_scale(ss_ref, o):
    e = ss_ref[:, 2 * o * BR_W:(2 * o + 1) * BR_W] + ss_ref[:, (2 * o + 1) * BR_W:(2 * o + 2) * BR_W]
    return lax.rsqrt(e + EPS)


def _combine_spectrum(x, ss_ref, o, half):
    xf = x[:, 2 * o * BR_W:(2 * o + 1) * BR_W]
    xb = x[:, (2 * o + 1) * BR_W:(2 * o + 2) * BR_W]
    sc = _filter_scale(ss_ref, o)
    hr = (xf[:half] + xb[:half]) * sc
    hi = (xf[half:] - xb[half:]) * sc
    return jnp.concatenate([hr, hi], axis=0)


def _cmul(x, h, half):
    xr, xi = x[:half], x[half:]
    hr, hi = h[:half], h[half:]
    return jnp.concatenate([xr * hr - xi * hi, xr * hi + xi * hr], axis=0)


def _dft1_kernel(z_ref, f_ref, a_ref):
    a_ref[0] = _dot(f_ref[...], z_ref[0].astype(BF16)).astype(BF16)


def dft_stage1(z, f1, tn):
    B, n1, W = z.shape
    M = f1.shape[0]
    return pl.pallas_call(
        _dft1_kernel,
        grid=(B, W // tn),
        in_specs=[pl.BlockSpec((1, n1, tn), lambda b, j: (b, 0, j)), _const_spec((M, n1))],
        out_specs=pl.BlockSpec((1, M, tn), lambda b, j: (b, 0, j)),
        out_shape=jax.ShapeDtypeStruct((B, M, W), BF16),
        compiler_params=_cparams("parallel", "parallel"),
        name="dft_stage1",
    )(z, f1)


def _spec2_kernel(a_ref, g_ref, ss_ref, h_ref):
    kb = g_ref.shape[0]
    n2 = a_ref.shape[3]
    for k in range(kb):
        a = a_ref[0, :, k].reshape(2 * n2, a_ref.shape[4])
        x = _dot(g_ref[k], a)
        for o in range(2):
            h_ref[o, k] = _combine_spectrum(x, ss_ref, o, n2)


def filter_spectrum(a, g, ss, kb):
    _, _, N1, N2, C = a.shape
    return pl.pallas_call(
        _spec2_kernel,
        grid=(N1 // kb,),
        in_specs=[pl.BlockSpec((1, 2, kb, N2, C), lambda i: (0, 0, i, 0, 0)),
                  pl.BlockSpec((kb, 2 * N2, 2 * N2), lambda i: (i, 0, 0)),
                  _const_spec((1, C))],
        out_specs=pl.BlockSpec((2, kb, 2 * N2, BR_W), lambda i: (0, i, 0, 0)),
        out_shape=jax.ShapeDtypeStruct((2, N1, 2 * N2, BR_W), F32),
        compiler_params=_cparams("parallel"),
        name="filter_spectrum",
    )(a, g, ss)


def _conv2_kernel(a_ref, g_ref, gi_ref, h_ref, o_ref):
    kb = g_ref.shape[0]
    n2 = a_ref.shape[3]
    C = a_ref.shape[4]
    for k in range(kb):
        a = a_ref[0, :, k].reshape(2 * n2, C)
        y = _cmul(_dot(g_ref[k], a), h_ref[0, k], n2).astype(BF16)
        o_ref[0, :, k] = _dot(gi_ref[k], y).astype(BF16).reshape(2, n2, C)


def spectral_multiply(a, g, gi, h, o, kb):
    B, _, N1, N2, C = a.shape
    blk = pl.BlockSpec((1, 2, kb, N2, C), lambda i, b: (b, 0, i, 0, 0))
    mat = pl.BlockSpec((kb, 2 * N2, 2 * N2), lambda i, b: (i, 0, 0))
    return pl.pallas_call(
        _conv2_kernel,
        grid=(N1 // kb, B),
        in_specs=[blk, mat, mat, pl.BlockSpec((1, kb, 2 * N2, C), lambda i, b: (o, i, 0, 0))],
        out_specs=blk,
        out_shape=jax.ShapeDtypeStruct(a.shape, BF16),
        compiler_params=_cparams("parallel", "parallel"),
        name="spectral_multiply",
    )(a, g, gi, h)


def _idft1_kernel(b_ref, f_ref, z_ref, gate_ref, bias_ref, f1_ref, o_ref, *a_ref):
    y = _dot(f_ref[...], b_ref[0])
    z = z_ref[0]
    out = gate_ref[0] * (y + bias_ref[...] * z)
    o_ref[0] = out
    if a_ref:
        a_ref[0][0] = _dot(f1_ref[...], out.astype(BF16)).astype(BF16)


def idft_stage1(bm, fi, z, gate, bias, f1, tn, with_next):
    B, M, W = bm.shape
    n1 = fi.shape[0]
    blk = pl.BlockSpec((1, n1, tn), lambda b, j: (b, 0, j))
    wide = pl.BlockSpec((1, M, tn), lambda b, j: (b, 0, j))
    out_specs, out_shape = [blk], [jax.ShapeDtypeStruct((B, n1, W), F32)]
    if with_next:
        out_specs.append(wide)
        out_shape.append(jax.ShapeDtypeStruct((B, M, W), BF16))
    return pl.pallas_call(
        _idft1_kernel,
        grid=(B, W // tn),
        in_specs=[wide, _const_spec((n1, M)), blk, blk, _const_spec((1, tn)), _const_spec((M, n1))],
        out_specs=out_specs,
        out_shape=out_shape,
        compiler_params=_cparams("parallel", "parallel"),
        name="idft_stage1",
    )(bm, fi, z, gate, bias, f1)


@functools.lru_cache(maxsize=None)
def _dft_tables(L):
    N = 2 * L
    N2 = 128
    N1 = N // N2
    nz = L // N2
    k1 = np.arange(N1)[:, None]
    n1 = np.arange(nz)[None, :]
    th = 2 * np.pi * ((k1 * n1) % N1) / N1
    f1 = np.concatenate([np.cos(th), -np.sin(th)], axis=0)
    fi = np.concatenate([np.cos(th).T, -np.sin(th).T], axis=1) / N
    kk1 = np.arange(N1)[:, None, None]
    k2 = np.arange(N2)[None, :, None]
    n2 = np.arange(N2)[None, None, :]
    ph = 2 * np.pi * ((n2 * k2 * N1 + n2 * kk1) % N) / N
    gr, gim = np.cos(ph), -np.sin(ph)
    g = np.concatenate([np.concatenate([gr, -gim], axis=2), np.concatenate([gim, gr], axis=2)], axis=1)
    hr, him = np.swapaxes(gr, 1, 2), -np.swapaxes(gim, 1, 2)
    gi = np.concatenate([np.concatenate([hr, -him], axis=2), np.concatenate([him, hr], axis=2)], axis=1)
    return tuple(np.asarray(t, np.float32) for t in (f1, fi, g, gi))


def _spec_direct_kernel(hf_ref, f_ref, ss_ref, h_ref):
    x = _dot(f_ref[...], hf_ref[...].astype(BF16))
    half = x.shape[0] // 2
    for o in range(2):
        h_ref[o] = _combine_spectrum(x, ss_ref, o, half)


def filter_spectrum_direct(hf, f, ss):
    L, C = hf.shape
    return pl.pallas_call(
        _spec_direct_kernel,
        grid=(1,),
        in_specs=[_const_spec((L, C)), _const_spec((4 * L, L)), _const_spec((1, C))],
        out_specs=pl.BlockSpec((2, 4 * L, BR_W), lambda i: (0, 0, 0)),
        out_shape=jax.ShapeDtypeStruct((2, 4 * L, BR_W), F32),
        compiler_params=_cparams("arbitrary"),
        name="filter_spectrum_direct",
    )(hf, f, ss)


def _conv_direct_kernel(z_ref, gate_ref, bias_ref, f_ref, fi_ref, h_ref, o_ref):
    z = z_ref[0]
    x = _dot(f_ref[...], z.astype(BF16))
    y = _cmul(x, h_ref[0], x.shape[0] // 2).astype(BF16)
    o_ref[0] = gate_ref[0] * (_dot(fi_ref[...], y) + bias_ref[...] * z)


def conv_direct(z, gate, bias, f, fi, h, o):
    B, L, C = z.shape
    blk = pl.BlockSpec((1, L, C), lambda b: (b, 0, 0))
    return pl.pallas_call(
        _conv_direct_kernel,
        grid=(B,),
        in_specs=[blk, blk, _const_spec((1, C)), _const_spec((4 * L, L)), _const_spec((L, 4 * L)),
                  pl.BlockSpec((1, 4 * L, C), lambda b: (o, 0, 0))],
        out_specs=blk,
        out_shape=jax.ShapeDtypeStruct((B, L, C), F32),
        compiler_params=_cparams("parallel"),
        name="conv_direct",
    )(z, gate, bias, f, fi, h)


@functools.lru_cache(maxsize=None)
def _dft_direct_tables(L):
    N = 2 * L
    k = np.arange(N)[:, None]
    n = np.arange(L)[None, :]
    th = 2 * np.pi * ((k * n) % N) / N
    f = np.concatenate([np.cos(th), -np.sin(th)], axis=0)
    fi = np.concatenate([np.cos(th).T, -np.sin(th).T], axis=1) / N
    return np.asarray(f, np.float32), np.asarray(fi, np.float32)


@functools.lru_cache(maxsize=None)
def _filter_features(L):
    t = np.arange(L, dtype=np.float32) / np.float32(L)
    ang = (2.0 * math.pi) * t[:, None].astype(np.float64) * np.arange(1, HY_BANDS + 1)
    feat = np.zeros((L, HY_FEAT_PAD), np.float32)
    feat[:, 0] = t
    feat[:, 1:1 + HY_BANDS] = np.sin(ang)
    feat[:, 1 + HY_BANDS:HY_FEAT] = np.cos(ang)
    rates = np.abs(np.linspace(math.log(HY_TARGET) / HY_SLOW, math.log(HY_TARGET) / HY_FAST, BR_W))
    return feat, np.tile(np.asarray(rates, np.float32), 4)[None, :]


def hyena_branch(v, g1, g2, hp):
    f_w1, f_b1, f_w2, f_b2, f_w3, freq, bias = hp
    B, L, C = v.shape
    feat, rates = _filter_features(L)
    hf, ss = hyena_filter(jnp.asarray(feat), f_w1, f_b1, f_w2, f_b2, f_w3, freq, jnp.asarray(rates),
                          min(L, 512))
    if L <= 512:
        f, fi = (jnp.asarray(t).astype(BF16) for t in _dft_direct_tables(L))
        spec = filter_spectrum_direct(hf, f, ss)
        z = v
        for o, gate in enumerate((g1, g2)):
            z = conv_direct(z, gate, bias[o:o + 1], f, fi, spec, o)
        return z
    f1, fi1, g, gi = (jnp.asarray(t).astype(BF16) for t in _dft_tables(L))
    N2 = 128
    N1 = 2 * L // N2
    nz = L // N2
    kb = 8
    a = dft_stage1(hf.reshape(1, nz, N2 * 4 * C), f1, 8192)
    spec = filter_spectrum(a.reshape(1, 2, N1, N2, 4 * C), g, ss, kb)
    W = N2 * C
    tn = 8192
    z = v.reshape(B, nz, W)
    a = dft_stage1(z, f1, tn)
    for o, gate in enumerate((g1, g2)):
        bm = spectral_multiply(a.reshape(B, 2, N1, N2, C), g, gi, spec, o, kb).reshape(B, 2 * N1, W)
        out = idft_stage1(bm, fi1, z, gate.reshape(B, nz, W), jnp.tile(bias[o:o + 1], (1, tn // C)), f1, tn, o == 0)
        z = out[0]
        if o == 0:
            a = out[1]
    return z.reshape(B, L, C)


def _head_sum(x, seg_ref):
    hi = x.astype(BF16)
    lo = (x - hi.astype(F32)).astype(BF16)
    return _dot(hi, seg_ref[...]) + _dot(lo, seg_ref[...])


def _rw_prep_kernel(p_ref, pp_ref, pn_ref, mu_ref, wl_ref, w0_ref, a0_ref, kv_ref, seg_ref, *outs, on_grid):
    for bi in range(p_ref.shape[0]):
        _rw_prep_one(bi, p_ref, pp_ref, pn_ref, mu_ref, wl_ref, w0_ref, a0_ref, kv_ref, seg_ref, outs, on_grid)


def _rw_prep_one(bi, p_ref, pp_ref, pn_ref, mu_ref, wl_ref, w0_ref, a0_ref, kv_ref, seg_ref, outs, on_grid):
    def put(pair, val):
        pair[0][:, bi, :] = val[:, 0:LANE_W]
        pair[1][:, bi, :] = val[:, LANE_W:2 * LANE_W]

    r_out, v_out, na_out = outs[0:2], outs[2:4], outs[4:6]
    w_out, kt_out, b_out = outs[6:10], outs[10:14], outs[14:18]
    bonus_ref, g_ref = outs[18], outs[19]
    i = pl.program_id(0)
    first = i == 0
    last = i == pl.num_programs(0) - 1
    x = p_ref[bi]
    tm = x.shape[0]
    grp = lax.broadcasted_iota(jnp.int32, x.shape, 1) % 4
    prev, nxt = pp_ref[bi], pn_ref[bi]
    if on_grid:
        col = (lax.broadcasted_iota(jnp.int32, (tm, 1), 0) + i * tm) % GRID_W
        left = jnp.where(col == 0, 0.0, pltpu.roll(x, 1, 0))
        right = jnp.where(col == GRID_W - 1, 0.0, pltpu.roll(x, tm - 1, 0))
        up = _shift_rows(x, prev, nxt, GRID_W, first, last)
        down = _shift_rows(x, prev, nxt, -GRID_W, first, last)
        shifted = jnp.where(grp == 0, left, jnp.where(grp == 1, right, jnp.where(grp == 2, up, down)))
    else:
        before = _shift_rows(x, prev, nxt, 1, first, last)
        after = _shift_rows(x, prev, nxt, -1, first, last)
        shifted = jnp.where(grp % 2 == 0, before, after)
    xx = x + (shifted - x) * mu_ref[...]
    r = xx[:, 0:BR_W]
    k = xx[:, BR_W:2 * BR_W]
    v = xx[:, 2 * BR_W:3 * BR_W]
    lo = xx[:, 3 * BR_W:4 * BR_W]
    ll = lax.broadcasted_iota(jnp.int32, lo.shape, 1)
    act = jnp.where(ll < RW_LORA, jnp.tanh(lo), jnp.where(ll < 2 * RW_LORA, lo, _sigmoid(lo)))
    z = _dot(act.astype(BF16), wl_ref[...])
    kk = k * kv_ref[0:1, :]
    kk = kk * lax.rsqrt(_head_sum(kk * kk, seg_ref) + 1e-12)
    put(r_out, r)
    put(v_out, v)
    put(na_out, -kk)
    bonus_ref[bi] = _head_sum(r * k * kv_ref[2:3, :], seg_ref) * v
    g_ref[bi] = z[:, 4 * BR_W:5 * BR_W]
    for d in range(2):
        sig = _sigmoid(w0_ref[d:d + 1, :] + z[:, d * BR_W:(d + 1) * BR_W])
        put(w_out[2 * d:2 * d + 2], jnp.exp(-math.exp(-0.5) * sig))
        a = _sigmoid(a0_ref[d:d + 1, :] + z[:, (2 + d) * BR_W:(3 + d) * BR_W])
        put(kt_out[2 * d:2 * d + 2], k * (1.0 + (a - 1.0) * kv_ref[1:2, :]))
        put(b_out[2 * d:2 * d + 2], kk * a)


def rwkv_prep(p, mu, w_lora, w0, a0, kvec, seg, on_grid, tm):
    B, T, _ = p.shape
    halo = GRID_W if on_grid else 8
    r = tm // halo
    last = T // halo - 1
    main = pl.BlockSpec((B, tm, RW_PAD), lambda i: (0, i, OFF_RW // RW_PAD))
    prev = pl.BlockSpec((B, halo, RW_PAD), lambda i: (0, jnp.maximum(i * r - 1, 0), OFF_RW // RW_PAD))
    nxt = pl.BlockSpec((B, halo, RW_PAD), lambda i: (0, jnp.minimum((i + 1) * r, last), OFF_RW // RW_PAD))
    half = pl.BlockSpec((tm, B, LANE_W), lambda i: (i, 0, 0))
    full = pl.BlockSpec((B, tm, BR_W), lambda i: (0, i, 0))
    sh = jax.ShapeDtypeStruct((T, B, LANE_W), F32)
    sf = jax.ShapeDtypeStruct((B, T, BR_W), F32)
    return pl.pallas_call(
        functools.partial(_rw_prep_kernel, on_grid=on_grid),
        grid=(T // tm,),
        in_specs=[main, prev, nxt, _const_spec((1, RW_PAD)), _const_spec((BR_W, 5 * BR_W)),
                  _const_spec((2, BR_W)), _const_spec((2, BR_W)), _const_spec((3, BR_W)),
                  _const_spec((BR_W, BR_W))],
        out_specs=[half] * 18 + [full, full],
        out_shape=[sh] * 18 + [sf, sf],
        compiler_params=_cparams("parallel"),
        name="rwkv_prep",
    )(p, p, p, mu, w_lora, w0, a0, kvec, seg)


RW_VQ = RW_HEAD // 4
RW_VH = 2 * RW_VQ
LANE_W = 128
N_KEYED = 5
RW_UNROLL = 2
RW_PARTIALS = 1


def _chain_rows(refs, t, tb, shift):
    blocks = []
    for d in range(2):
        for half in range(2):
            x = refs[2 * d + half][t if d == 0 else tb]
            blocks.append(x if shift == 0 else pltpu.roll(x, LANE_W - shift, 1))
    return blocks


def _rw_scan_kernel(*refs, B, Tt):
    nin = 4 * (N_KEYED + 1)
    ins = refs[:nin]
    s0_ref = refs[nin]
    y_refs = refs[nin + 1:nin + 5]
    s_ref = refs[nin + 5]
    first_scratch = nin + 6
    per_set = N_KEYED + 2
    sets = [refs[first_scratch + per_set * u:first_scratch + per_set * (u + 1)] for u in range(RW_UNROLL)]
    bufs = [(st[:N_KEYED], st[N_KEYED], st[N_KEYED + 1]) for st in sets]
    sa_ref = refs[first_scratch + per_set * RW_UNROLL]
    NL = 16 * B
    A_TILE = 3

    @pl.when(pl.program_id(0) == 0)
    def _():
        s_ref[...] = s0_ref[...]
        bufs[RW_UNROLL - 1][2][...] = jnp.zeros_like(bufs[RW_UNROLL - 1][2])

    def keyed_job(t, slot, n):
        def issue():
            rows = _chain_rows(ins[4 * n:4 * n + 4], t, Tt - 1 - t, 0)
            return jnp.concatenate(rows * 4, axis=0).T

        def commit(val):
            bufs[slot][0][n][...] = val
        return issue, commit

    def values_job(t, slot):
        def issue():
            rows = []
            for vq in range(4):
                rows += _chain_rows(ins[4 * N_KEYED:], t, Tt - 1 - t, vq * RW_VQ)
            vt = jnp.concatenate(rows, axis=0).T
            return jnp.concatenate([vt[0:RW_VQ], vt[RW_HEAD:RW_HEAD + RW_VQ]], axis=0)

        def commit(val):
            bufs[slot][1][...] = val
        return issue, commit

    def output_job(t, slot):
        def issue():
            ys = bufs[slot][2][...]
            pad = jnp.zeros((RW_HEAD - RW_VQ, NL), F32)
            y = jnp.concatenate([ys[0:RW_VQ], pad, ys[RW_VQ:RW_VH], pad], axis=0).T
            out = []
            for d in range(2):
                for half in range(2):
                    acc = None
                    for vq in range(4):
                        row0 = ((vq * 2 + d) * 2 + half) * B
                        blk = y[row0:row0 + B, :]
                        blk = blk if vq == 0 else pltpu.roll(blk, vq * RW_VQ, 1)
                        acc = blk if acc is None else acc + blk
                    out.append(acc)
            return out

        def commit(val):
            for d in range(2):
                for half in range(2):
                    y_refs[2 * d + half][:, t if d == 0 else Tt - 1 - t, :] = val[2 * d + half]
        return issue, commit

    groups = range(RW_VH // 8)
    others = tuple(n for n in range(N_KEYED) if n != A_TILE)
    never = pl.program_id(0) < 0

    def step_pieces(slot, sa_in, sa_out):
        tiles, vt_ref, ys_ref = bufs[slot]
        a_next = bufs[(slot + 1) % RW_UNROLL][0][A_TILE]
        row = lambda ref, k: ref[pl.ds(k, 1), :]
        ys = [[None] * RW_PARTIALS for _ in groups]
        san = [[None] * RW_PARTIALS for _ in groups]
        vt = {}
        acc = lambda lst, i, v: lst.__setitem__(i, v if lst[i] is None else lst[i] + v)

        def update(k0, k1):
            if not vt:
                for g in groups:
                    vt[g] = vt_ref[g * 8:(g + 1) * 8, :]
            for k in range(k0, k1):
                for h2 in range(2):
                    kk = h2 * RW_HEAD + k
                    r_k, w_k, kt_k, b_k = (row(tiles[n], kk) for n in (0, 1, 2, 4))
                    an_k = row(a_next, kk)
                    for g in (2 * h2, 2 * h2 + 1):
                        s = s_ref[g, k] * w_k + sa_in[g] * b_k + vt[g] * kt_k
                        s_ref[g, k] = s
                        acc(ys[g], k % RW_PARTIALS, s * r_k)
                        acc(san[g], k % RW_PARTIALS, s * an_k)
            if k1 == RW_HEAD:
                for g in groups:
                    ys_ref[g * 8:(g + 1) * 8, :] = functools.reduce(lambda a, b: a + b, ys[g])
                    sa_out.append(functools.reduce(lambda a, b: a + b, san[g]))

        def anchor(i, val):
            g = i % len(groups)
            if val is not None and ys[g][0] is not None:
                ys[g][0] = jnp.where(never, val, ys[g][0])

        cuts = [0, 9, 18, 27, 36, 45, 54, RW_HEAD]
        return [functools.partial(update, cuts[i], cuts[i + 1]) for i in range(len(cuts) - 1)], anchor

    def run(step, jobs):
        pieces, anchor = step
        pending = None
        for i, piece in enumerate(pieces):
            val = jobs[i][0]() if i < len(jobs) else None
            piece()
            if pending is not None and i + 1 < len(pieces):
                anchor(*pending)
            pending = None
            if i < len(jobs):
                jobs[i][1](val)
                if not isinstance(val, list):
                    pending = (i, val[val.shape[0] - 8:, :])
                elif val[3].shape == (8, NL):
                    pending = (i, val[3])

    for issue, commit in ([keyed_job(0, 0, n) for n in range(N_KEYED)] + [values_job(0, 0)]
                          + [keyed_job(1, 1, A_TILE)]):
        commit(issue())
    first = [[None, None] for _ in groups]
    for k in range(RW_HEAD):
        for g in groups:
            p = s_ref[g, k] * bufs[0][0][A_TILE][pl.ds((g // 2) * RW_HEAD + k, 1), :]
            first[g][k % 2] = p if first[g][k % 2] is None else first[g][k % 2] + p
    for g in groups:
        sa_ref[g] = first[g][0] + first[g][1]

    def body(j, carry):
        t0 = RW_UNROLL * j
        sa = [sa_ref[g] for g in groups]
        for u in range(RW_UNROLL):
            t = t0 + u
            t1 = jnp.minimum(t + 1, Tt - 1)
            jobs = [output_job(jnp.maximum(t - 1, 0), (u - 1) % RW_UNROLL),
                    keyed_job(jnp.minimum(t + 2, Tt - 1), (u + 2) % RW_UNROLL, A_TILE)]
            jobs += [keyed_job(t1, (u + 1) % RW_UNROLL, n) for n in others] + [values_job(t1, (u + 1) % RW_UNROLL)]
            sa_next = []
            run(step_pieces(u, sa, sa_next), jobs)
            sa = sa_next
        for g in groups:
            sa_ref[g] = sa[g]
        return carry

    lax.fori_loop(0, Tt // RW_UNROLL, body, 0)
    issue, commit = output_job(Tt - 1, RW_UNROLL - 1)
    commit(issue())


def rwkv_mix(prep, s0, tt):
    T, B, _ = prep[0][0].shape
    nb = T // tt
    tblk = lambda i, d: i if d == 0 else nb - 1 - i

    in_specs, args = [], []
    for quad in prep:
        for j, x in enumerate(quad):
            in_specs.append(pl.BlockSpec((tt, B, LANE_W), lambda i, d=j // 2: (tblk(i, d), 0, 0)))
            args.append(x)
    NL = 16 * B
    sblk = pl.BlockSpec((RW_VH // 8, RW_HEAD, 8, NL), lambda i: (0, 0, 0, 0))
    yspecs = [pl.BlockSpec((B, tt, LANE_W), lambda i, d=d: (0, tblk(i, d), 0)) for d in range(2) for _ in range(2)]
    ysh = jax.ShapeDtypeStruct((B, T, LANE_W), F32)
    out = pl.pallas_call(
        functools.partial(_rw_scan_kernel, B=B, Tt=tt),
        grid=(nb,),
        in_specs=in_specs + [sblk],
        out_specs=yspecs + [sblk],
        out_shape=[ysh] * 4 + [jax.ShapeDtypeStruct((RW_VH // 8, RW_HEAD, 8, NL), F32)],
        scratch_shapes=([pltpu.VMEM((LANE_W, NL), F32)] * N_KEYED + [pltpu.VMEM((RW_VH, NL), F32)] * 2) * RW_UNROLL
        + [pltpu.VMEM((RW_VH // 8, 8, NL), F32)],
        compiler_params=_cparams("arbitrary"),
        name="rwkv_scan",
    )(*args, s0)
    return (out[0], out[1]), (out[2], out[3]), out[4]


def _gelu_tanh(x):
    return 0.5 * x * (1.0 + jnp.tanh(math.sqrt(2.0 / math.pi) * (x + 0.044715 * (x * x * x))))


def _lru_prep_kernel(p_ref, pp_ref, pn_ref, cw_ref, cb_ref, w_ref, bias_ref, lam_ref, a_ref, b_ref, gg_ref):
    i = pl.program_id(1)
    first = i == 0
    last = i == pl.num_programs(1) - 1
    x = p_ref[0][:, 0:BR_W]
    prev = pp_ref[0][:, 0:BR_W]
    nxt = pn_ref[0][:, 0:BR_W]
    xc = cb_ref[...] + cw_ref[0:1, :] * _shift_rows(x, prev, nxt, 2, first, last)
    xc = xc + cw_ref[1:2, :] * _shift_rows(x, prev, nxt, 1, first, last)
    xc = xc + cw_ref[2:3, :] * x
    xc = xc + cw_ref[3:4, :] * _shift_rows(x, prev, nxt, -1, first, last)
    z = _dot(xc.astype(BF16), w_ref[...])
    for d in range(2):
        r = _sigmoid(z[:, 2 * d * BR_W:(2 * d + 1) * BR_W] + bias_ref[2 * d:2 * d + 1, :])
        gi = _sigmoid(z[:, (2 * d + 1) * BR_W:(2 * d + 2) * BR_W] + bias_ref[2 * d + 1:2 * d + 2, :])
        log_a = -LRU_C * r * _softplus(-lam_ref[d:d + 1, :])
        a_ref[d, 0] = jnp.exp(log_a)
        b_ref[d, 0] = jnp.sqrt(1.0 - jnp.exp(2.0 * log_a)) * (gi * xc)
    gg_ref[0] = _gelu_tanh(p_ref[0][:, BR_W:2 * BR_W])


def lru_prep(p, conv_w, conv_b, w_blk, bias, lam, tm):
    B, T, _ = p.shape
    main, prev, nxt = _halo_specs(tm, T, LRU_IN, lambda b, i: OFF_LRU // LRU_IN, 8)
    two = pl.BlockSpec((2, 1, tm, BR_W), lambda b, i: (0, b, i, 0))
    s2 = jax.ShapeDtypeStruct((2, B, T, BR_W), F32)
    return pl.pallas_call(
        _lru_prep_kernel,
        grid=(B, T // tm),
        in_specs=[main, prev, nxt, _const_spec((4, BR_W)), _const_spec((1, BR_W)),
                  _const_spec((BR_W, 4 * BR_W)), _const_spec((4, BR_W)), _const_spec((2, BR_W))],
        out_specs=[two, two, pl.BlockSpec((1, tm, BR_W), lambda b, i: (b, i, 0))],
        out_shape=[s2, s2, jax.ShapeDtypeStruct((B, T, BR_W), F32)],
        compiler_params=_cparams("parallel", "parallel"),
        name="lru_prep",
    )(p, p, p, conv_w, conv_b, w_blk, bias, lam)


def _affine_scan(a, b, reverse):
    tb = a.shape[0]
    row = lax.broadcasted_iota(jnp.int32, (tb, 1), 0)
    s = 1
    while s < tb:
        sh = tb - s if reverse else s
        ok = (row < tb - s) if reverse else (row >= s)
        a_s = pltpu.roll(a, sh, 0)
        b_s = pltpu.roll(b, sh, 0)
        b = jnp.where(ok, a * b_s + b, b)
        a = jnp.where(ok, a * a_s, a)
        s *= 2
    return a, b


def _lru_scan_kernel(af_ref, bf_ref, ab_ref, bb_ref, h0_ref, hf_ref, hb_ref, fin_ref):
    @pl.when(pl.program_id(1) == 0)
    def _():
        fin_ref[...] = h0_ref[...]

    tb = af_ref.shape[2]
    a, b = _affine_scan(af_ref[0, 0], bf_ref[0, 0], False)
    h = b + a * fin_ref[0, 0]
    hf_ref[0] = h
    fin_ref[0, 0] = h[tb - 1:tb, :]
    a, b = _affine_scan(ab_ref[0, 0], bb_ref[0, 0], True)
    h = b + a * fin_ref[1, 0]
    hb_ref[0] = h
    fin_ref[1, 0] = h[0:1, :]


def lru_scan(a, b, h0, tb):
    _, B, T, C = a.shape
    nb = T // tb
    fwd = pl.BlockSpec((1, 1, tb, C), lambda bi, i: (0, bi, i, 0))
    bwd = pl.BlockSpec((1, 1, tb, C), lambda bi, i: (1, bi, nb - 1 - i, 0))
    st = pl.BlockSpec((2, 1, 1, C), lambda bi, i: (0, bi, 0, 0))
    return pl.pallas_call(
        _lru_scan_kernel,
        grid=(B, nb),
        in_specs=[fwd, fwd, bwd, bwd, st],
        out_specs=[pl.BlockSpec((1, tb, C), lambda bi, i: (bi, i, 0)),
                   pl.BlockSpec((1, tb, C), lambda bi, i: (bi, nb - 1 - i, 0)), st],
        out_shape=[jax.ShapeDtypeStruct((B, T, C), F32), jax.ShapeDtypeStruct((B, T, C), F32),
                   jax.ShapeDtypeStruct((2, B, 1, C), F32)],
        compiler_params=_cparams("parallel", "arbitrary"),
        name="lru_scan",
    )(a, b, a, b, h0)


def _rope(x, cos, sin):
    q4 = RET_HEAD // 4
    lane = lax.broadcasted_iota(jnp.int32, x.shape, 1) % (2 * q4)
    partner = jnp.where(lane < q4, pltpu.roll(x, x.shape[1] - q4, 1), pltpu.roll(x, q4, 1))
    return x * cos + partner * sin


def _ret_decay_tables(glane, gtile_ref, d, reverse, C, dm_ref, qd_ref, kd_ref):
    lg = -_softplus(-glane)
    idx = lax.broadcasted_iota(jnp.int32, (C, 1), 0).astype(F32)
    steps_in = (C - idx) if reverse else (idx + 1.0)
    steps_out = idx if reverse else (C - 1.0 - idx)
    qd_ref[d] = jnp.exp(steps_in * lg)
    kd_ref[d] = jnp.exp(steps_out * lg)
    ri = lax.broadcasted_iota(jnp.int32, (C, C), 0)
    ci = lax.broadcasted_iota(jnp.int32, (C, C), 1)
    diff = ((ci - ri) if reverse else (ri - ci)).astype(F32)
    for h in range(RET_HEADS):
        lg_h = -_softplus(-gtile_ref[d, h][0:1, :])
        dm_ref[d, h] = jnp.where(diff >= 0, jnp.exp(diff * lg_h), 0.0)


def _ret_dir(x, cos, sin, s, glane, d, dm_ref, qd_ref, kd_ref):
    C = x.shape[0]
    q = x[:, 0:BR_W]
    k = x[:, BR_W:2 * BR_W]
    v = x[:, 2 * BR_W:3 * BR_W].astype(BF16)
    if cos is not None:
        q = _rope(q, cos, sin)
        k = _rope(k, cos, sin)
    k = k * (RET_HEAD ** -0.5)
    lg = -_softplus(-glane)
    lane_head = lax.broadcasted_iota(jnp.int32, (1, BR_W), 1) // RET_HEAD
    qb = q.astype(BF16)
    kb = k.astype(BF16)
    y = _dot(qb, s.astype(BF16)) * qd_ref[d]
    for h in range(RET_HEADS):
        mh = lane_head == h
        sc = lax.dot_general(jnp.where(mh, qb, jnp.zeros_like(qb)), kb, (((1,), (1,)), ((), ())),
                             preferred_element_type=F32)
        y = y + jnp.where(mh, _dot((sc * dm_ref[d, h]).astype(BF16), v), 0.0)
    kd = (k * kd_ref[d]).astype(BF16)
    upd = lax.dot_general(kd, v, (((0,), (0,)), ((), ())), preferred_element_type=F32)
    rh = lax.broadcasted_iota(jnp.int32, (BR_W, BR_W), 0) // RET_HEAD
    ch = lax.broadcasted_iota(jnp.int32, (BR_W, BR_W), 1) // RET_HEAD
    s = s * jnp.exp(C * lg) + jnp.where(rh == ch, upd, 0.0)
    return y, s


def _ret_kernel(*refs, rope):
    if rope:
        xf_ref, xb_ref, cf_ref, sf_ref, cb_ref, sb_ref, gl_ref, gt_ref, s0_ref, yf_ref, yb_ref, s_ref = refs[:12]
    else:
        xf_ref, xb_ref, gl_ref, gt_ref, s0_ref, yf_ref, yb_ref, s_ref = refs[:8]
    tables = refs[-3:]
    C = xf_ref.shape[1]

    @pl.when(pl.program_id(1) == 0)
    def _():
        s_ref[...] = s0_ref[...]
        _ret_decay_tables(gl_ref[0:1, :], gt_ref, 0, False, C, *tables)
        _ret_decay_tables(gl_ref[1:2, :], gt_ref, 1, True, C, *tables)

    y, s = _ret_dir(xf_ref[0], cf_ref[...] if rope else None, sf_ref[...] if rope else None,
                    s_ref[0, 0], gl_ref[0:1, :], 0, *tables)
    yf_ref[0] = y
    s_ref[0, 0] = s
    y, s = _ret_dir(xb_ref[0], cb_ref[...] if rope else None, sb_ref[...] if rope else None,
                    s_ref[1, 0], gl_ref[1:2, :], 1, *tables)
    yb_ref[0] = y
    s_ref[1, 0] = s


def retention(p, cos, sin, glane, gtile, s0, rope):
    B, T, _ = p.shape
    C = RET_CHUNK
    nc = T // C
    cb = OFF_RET // RET_IN
    xf = pl.BlockSpec((1, C, RET_IN), lambda b, i: (b, i, cb))
    xb = pl.BlockSpec((1, C, RET_IN), lambda b, i: (b, nc - 1 - i, cb))
    tf = pl.BlockSpec((C, BR_W), lambda b, i: (i, 0))
    tb = pl.BlockSpec((C, BR_W), lambda b, i: (nc - 1 - i, 0))
    st = pl.BlockSpec((2, 1, BR_W, BR_W), lambda b, i: (0, b, 0, 0))
    ins = [xf, xb] + ([tf, tf, tb, tb] if rope else []) + [
        _const_spec((2, BR_W)), _const_spec((2, RET_HEADS, 8, C)), st]
    args = [p, p] + ([cos, sin, cos, sin] if rope else []) + [glane, gtile, s0]
    return pl.pallas_call(
        functools.partial(_ret_kernel, rope=rope),
        grid=(B, nc),
        in_specs=ins,
        out_specs=[pl.BlockSpec((1, C, BR_W), lambda b, i: (b, i, 0)),
                   pl.BlockSpec((1, C, BR_W), lambda b, i: (b, nc - 1 - i, 0)), st],
        out_shape=[jax.ShapeDtypeStruct((B, T, BR_W), F32), jax.ShapeDtypeStruct((B, T, BR_W), F32),
                   jax.ShapeDtypeStruct((2, B, BR_W, BR_W), F32)],
        scratch_shapes=[pltpu.VMEM((2, RET_HEADS, C, C), F32), pltpu.VMEM((2, C, BR_W), F32),
                        pltpu.VMEM((2, C, BR_W), F32)],
        compiler_params=_cparams("parallel", "arbitrary"),
        name="retention",
    )(*args)


@functools.lru_cache(maxsize=None)
def _rope_tables(T):
    pos = np.arange(T)
    q4 = RET_HEAD // 4
    inv = ROPE_BASE ** (-np.arange(q4, dtype=np.float64) / q4)
    cos = np.zeros((T, RET_HEAD))
    sin = np.zeros((T, RET_HEAD))
    for part, coord in enumerate((pos // GRID_W, pos % GRID_W)):
        ang = coord[:, None] * inv
        base = part * 2 * q4
        cos[:, base:base + q4] = np.cos(ang)
        cos[:, base + q4:base + 2 * q4] = np.cos(ang)
        sin[:, base:base + q4] = -np.sin(ang)
        sin[:, base + q4:base + 2 * q4] = np.sin(ang)
    tile = lambda t: np.asarray(np.tile(t, (1, RET_HEADS)), np.float32)
    return tile(cos), tile(sin)


def _head_norm(y, seg_ref, eps):
    mu = _head_sum(y, seg_ref) * (1.0 / RW_HEAD)
    yc = y - mu
    var = _head_sum(yc * yc, seg_ref) * (1.0 / RW_HEAD)
    return yc * lax.rsqrt(var + eps)


def _merge_kernel(x_ref, g_ref, sh_ref, sc_ref, gt_ref, hy_ref, rfl_ref, rfh_ref, rbl_ref, rbh_ref,
                  rbon_ref, rg_ref, lhf_ref, lhb_ref, lgg_ref, tyf_ref, tyb_ref, tg_ref, lng_ref, seg_ref,
                  wg_ref, br_ref, wo_ref, o_ref, m_ref):
    x = x_ref[0]
    u = _norm_mod(x, g_ref[...], sh_ref[0], sc_ref[0]).astype(BF16)
    wkv = jnp.concatenate([rfl_ref[0] + rbl_ref[0], rfh_ref[0] + rbh_ref[0]], axis=1)
    y_rw = (_head_norm(wkv, seg_ref, RW_LN_EPS) * lng_ref[...] + rbon_ref[0]) * rg_ref[0]
    y_lru = (lhf_ref[0] + lhb_ref[0]) * lgg_ref[0]
    y_ret = _head_norm(tyf_ref[0] + tyb_ref[0], seg_ref, RET_LN_EPS) * _silu(tg_ref[0])
    ys = [y.astype(BF16) for y in (hy_ref[0], y_rw, y_lru, y_ret)]
    D = x.shape[1]
    cw = 256
    for c in range(D // cw):
        acc = None
        for n in range(N_BRANCH):
            gate = _sigmoid(_dot(u, wg_ref[:, n * D + c * cw:n * D + (c + 1) * cw]))
            t = gate * _dot(ys[n], br_ref[n, :, c * cw:(c + 1) * cw])
            acc = t if acc is None else acc + t
        m_ref[:, c * cw:(c + 1) * cw] = acc.astype(BF16)
    o_ref[0] = x + gt_ref[0] * _dot(m_ref[...], wo_ref[...])


def merge(x, g, sh, sc, gt, p, y_hy, rw, lru, ret, ln_g, seg, w_gate, br, w_out, tm):
    B, T, D = x.shape
    vec = pl.BlockSpec((1, 1, D), lambda b, i: (b, 0, 0))
    row = pl.BlockSpec((1, tm, D), lambda b, i: (b, i, 0))
    brn = pl.BlockSpec((1, tm, BR_W), lambda b, i: (b, i, 0))
    half = pl.BlockSpec((1, tm, LANE_W), lambda b, i: (b, i, 0))
    tg = pl.BlockSpec((1, tm, BR_W), lambda b, i: (b, i, (OFF_RET + 3 * BR_W) // BR_W))
    return pl.pallas_call(
        _merge_kernel,
        grid=(B, T // tm),
        in_specs=[row, _const_spec((1, D)), vec, vec, vec, brn] + [half] * 4 + [brn] * 7 + [tg] + [
            _const_spec((1, BR_W)), _const_spec((BR_W, BR_W)), _const_spec((D, GATE_IN)),
            _const_spec((N_BRANCH, BR_W, D)), _const_spec((D, D))],
        out_specs=row,
        out_shape=jax.ShapeDtypeStruct((B, T, D), F32),
        scratch_shapes=[pltpu.VMEM((tm, D), BF16)],
        compiler_params=_cparams("parallel", "parallel"),
        name="merge",
    )(x, g, sh, sc, gt, y_hy, *rw, *lru, *ret, p, ln_g, seg, w_gate, br, w_out)


def _block_diag(w):
    G = w.shape[-3]
    eye = jnp.eye(G, dtype=w.dtype)
    full = w[..., :, :, None, :] * eye[:, None, :, None]
    return full.reshape(*w.shape[:-3], G * w.shape[-2], G * w.shape[-1])


def _mixers(p, lp, states, on_grid, with_output, tiles):
    B, T, _ = p.shape
    tm, tt = tiles[0], tiles[1]
    pre = rwkv_prep(p, *lp['rw'], on_grid, 2 * GRID_W)
    r, v, na, w, kt, b = pre[0:2] * 2, pre[2:4] * 2, pre[4:6] * 2, pre[6:10], pre[10:14], pre[14:18]
    bonus, g = pre[18], pre[19]
    y_f, y_b, rw_fin = rwkv_mix((r, w, kt, na, b, v), states[0], tt)
    a, bb, gg = lru_prep(p, *lp['lru'], tm)
    h_f, h_b, lru_fin = lru_scan(a, bb, states[1], tm)
    cos, sin = (jnp.asarray(t) for t in _rope_tables(T)) if on_grid else (None, None)
    t_f, t_b, ret_fin = retention(p, cos, sin, *lp['ret'], states[2], on_grid)
    fins = (rw_fin, lru_fin, ret_fin)
    if not with_output:
        return None, fins
    vg = hyena_prep(p, *lp['hy_conv'], min(T, 2048))
    y_hy = hyena_branch(vg[0], vg[1], vg[2], lp['hy'])
    return (y_hy, (y_f[0], y_f[1], y_b[0], y_b[1], bonus, g), (h_f, h_b, gg), (t_f, t_b)), fins


def kernel(x, c, ctx, c_ctx, w_mod, b_mod, norm1_g, norm2_g, w_in, hy_conv_w, hy_conv_b, hy_f_w1, hy_f_b1, hy_f_w2, hy_f_b2, hy_f_w3, hy_freq, hy_bias, rw_mu, rw_w0, rw_w2, rw_a0, rw_a2, rw_g2, rw_kk, rw_ka, rw_rk, rw_ln_g, lru_conv_w, lru_conv_b, lru_wa, lru_ba, lru_wx, lru_bx, lru_lam, ret_gamma, br_proj, w_out, ffn_w1, ffn_w2, final_g):
    B, T, D = x.shape
    TC = ctx.shape[1]
    L = w_in.shape[0]

    s0, s1, s2, s3 = HY_IN, HY_IN + RW_IN, HY_IN + RW_IN + LRU_IN, HY_IN + RW_IN + LRU_IN + RET_IN
    w_branch = jnp.concatenate([w_in[:, :, s0:s1], jnp.zeros((L, D, RW_PAD - RW_IN), w_in.dtype),
                                w_in[:, :, s2:s3], jnp.zeros((L, D, HY_GAP), w_in.dtype),
                                w_in[:, :, 0:s0], w_in[:, :, s1:s2]], axis=2).astype(BF16)
    w_gate = w_in[:, :, s3:].astype(BF16)
    mu = jnp.pad(rw_mu, ((0, 0), (0, RW_PAD - RW_IN)))[:, None, :]
    w_lora = jnp.zeros((L, BR_W, 5 * BR_W), F32)
    w_lora = w_lora.at[:, 0:RW_LORA, 0:BR_W].set(rw_w2[:, 0]).at[:, 0:RW_LORA, BR_W:2 * BR_W].set(rw_w2[:, 1])
    w_lora = w_lora.at[:, RW_LORA:2 * RW_LORA, 2 * BR_W:3 * BR_W].set(rw_a2[:, 0])
    w_lora = w_lora.at[:, RW_LORA:2 * RW_LORA, 3 * BR_W:4 * BR_W].set(rw_a2[:, 1])
    w_lora = w_lora.at[:, 2 * RW_LORA:3 * RW_LORA, 4 * BR_W:5 * BR_W].set(rw_g2).astype(BF16)
    kvec = jnp.stack([rw_kk, rw_ka, rw_rk], axis=1)
    seg = jnp.asarray(np.kron(np.eye(RW_HEADS), np.ones((RW_HEAD, RW_HEAD))), BF16)
    lru_w = jnp.concatenate([_block_diag(lru_wa[:, 0]), _block_diag(lru_wx[:, 0]),
                             _block_diag(lru_wa[:, 1]), _block_diag(lru_wx[:, 1])], axis=2).astype(BF16)
    lru_bias = jnp.stack([lru_ba[:, 0], lru_bx[:, 0], lru_ba[:, 1], lru_bx[:, 1]], axis=1)
    glane = jnp.repeat(ret_gamma, RET_HEAD, axis=2)
    gtile = jnp.broadcast_to(ret_gamma[:, :, :, None, None], (L, 2, RET_HEADS, 8, RET_CHUNK))
    f_w1 = jnp.pad(hy_f_w1, ((0, 0), (0, HY_FEAT_PAD - HY_FEAT), (0, 0)))
    br_b = br_proj.astype(BF16)
    w_out_b = w_out.astype(BF16)
    ffn_gate = ffn_w1[:, :, :D_FF].astype(BF16)
    ffn_up_w = ffn_w1[:, :, D_FF:].astype(BF16)
    ffn_w2_b = ffn_w2.astype(BF16)

    cc = jnp.concatenate([c, c_ctx[None, :], jnp.zeros((16 - B - 1, D), F32)], axis=0)
    mods = modulation(cc, w_mod.astype(BF16), b_mod[:, None, :])

    zero_states = (jnp.zeros((RW_VH // 8, RW_HEAD, 8, 16 * B), F32), jnp.zeros((2, B, 1, BR_W), F32),
                   jnp.zeros((2, B, BR_W, BR_W), F32))
    xc = ctx
    for l in range(L):
        last = l == L - 1
        lp = {
            'rw': (mu[l], w_lora[l], rw_w0[l], rw_a0[l], kvec[l], seg),
            'lru': (lru_conv_w[l], lru_conv_b[l][None, :], lru_w[l], lru_bias[l], lru_lam[l]),
            'ret': (glane[l], gtile[l]),
            'hy_conv': (hy_conv_w[l], hy_conv_b[l][None, :]),
            'hy': (f_w1[l], hy_f_b1[l][None, :], hy_f_w2[l], hy_f_b2[l][None, :], hy_f_w3[l], hy_freq[l],
                   hy_bias[l]),
        }
        g1 = norm1_g[l][None, :]
        g2 = norm2_g[l][None, :]
        m_lat = [m[:, None, :] for m in jnp.split(mods[l, :B], 6, axis=-1)]
        m_ctx = [jnp.broadcast_to(m[None, :, :], (B, 1, D)) for m in jnp.split(mods[l, B:B + 1], 6, axis=-1)]
        ln_g = rw_ln_g[l][None, :]

        def layer(xs, m, states, on_grid, with_output, tiles, final):
            p = in_projection(xs, g1, m[0], m[1], w_branch[l], tiles[0])
            br, fins = _mixers(p, lp, states, on_grid, with_output, tiles)
            if not with_output:
                return None, fins
            xs = merge(xs, g1, m[0], m[1], m[2], p, br[0], br[1], br[2], br[3], ln_g, seg,
                       w_gate[l], br_b[l], w_out_b[l], tiles[2])
            h = ffn_up(xs, g2, m[3], m[4], ffn_gate[l], ffn_up_w[l], tiles[3])
            xs = ffn_down(h, xs, m[5], ffn_w2_b[l], final_g[None, :], final, tiles[3])
            return xs, fins

        xc_new, ctx_states = layer(xc, m_ctx, zero_states, False, not last, (TC, 64, TC, TC), False)
        x, _ = layer(x, m_lat, ctx_states, True, True, (512, 128, 512, 1024), last)
        if not last:
            xc = xc_new
    return x
```

```python
import functools
import math

import numpy as np
import jax
import jax.numpy as jnp
from jax import lax
from jax.experimental import pallas as pl
from jax.experimental.pallas import tpu as pltpu

F32 = jnp.float32
BF16 = jnp.bfloat16

D_MODEL = 1024
DEPTH = 4
GRID_W = 64
N_BRANCH = 4
BR_W = D_MODEL // N_BRANCH

HY_BANDS = 8
HY_FEAT = 1 + 2 * HY_BANDS
HY_FEAT_PAD = 32
HY_HID = 64
HY_TARGET = 1e-2
HY_FAST = 0.3
HY_SLOW = 1.5
HY_IN = 3 * BR_W

RW_HEAD = 64
RW_HEADS = BR_W // RW_HEAD
RW_LORA = 64
RW_IN = 3 * BR_W + 3 * RW_LORA
RW_PAD = 4 * BR_W
RW_LN_EPS = 64e-5

LRU_BLOCKS = 4
LRU_BLOCK = BR_W // LRU_BLOCKS
LRU_C = 8.0
LRU_IN = 2 * BR_W

RET_HEADS = 4
RET_HEAD = BR_W // RET_HEADS
RET_CHUNK = 128
ROPE_BASE = 10000.0
RET_IN = 4 * BR_W
RET_LN_EPS = 1e-5

GATE_IN = N_BRANCH * D_MODEL
D_FF = ((8 * D_MODEL // 3 + 255) // 256) * 256
EPS = 1e-6

OFF_RW = 0
OFF_RET = OFF_RW + RW_PAD
OFF_HY = 3 * HY_IN
OFF_LRU = OFF_HY + HY_IN
N_BR = OFF_LRU + LRU_IN
HY_GAP = OFF_HY - (OFF_RET + RET_IN)

VMEM_LIMIT = 56 * 1024 * 1024
HI = lax.Precision.HIGHEST


def _cparams(*sem):
    return pltpu.CompilerParams(dimension_semantics=sem, vmem_limit_bytes=VMEM_LIMIT)


def _const_spec(shape):
    nd = len(shape)
    return pl.BlockSpec(shape, lambda *_: (0,) * nd, pipeline_mode=pl.Buffered(1))


def _dot(a, b, **kw):
    return jnp.dot(a, b, preferred_element_type=F32, **kw)


def _norm_mod(x, g, sh, sc):
    ms = jnp.mean(x * x, axis=-1, keepdims=True)
    return x * lax.rsqrt(ms + EPS) * g * (1.0 + sc) + sh


def _sigmoid(x):
    return 1.0 / (1.0 + jnp.exp(-x))


def _silu(x):
    return x * _sigmoid(x)


def _softplus(x):
    return jnp.maximum(x, 0.0) + jnp.log(1.0 + jnp.exp(-jnp.abs(x)))


def _mod_kernel(c_ref, w_ref, b_ref, o_ref):
    c = c_ref[...]
    o_ref[0] = _dot(_silu(c).astype(BF16), w_ref[0]) + b_ref[0]


def modulation(cc, w_mod, b_mod):
    L, D, N = w_mod.shape
    tn = 1536
    return pl.pallas_call(
        _mod_kernel,
        grid=(L, N // tn),
        in_specs=[pl.BlockSpec((16, D), lambda l, j: (0, 0)),
                  pl.BlockSpec((1, D, tn), lambda l, j: (l, 0, j)),
                  pl.BlockSpec((1, 1, tn), lambda l, j: (l, 0, j))],
        out_specs=pl.BlockSpec((1, 16, tn), lambda l, j: (l, 0, j)),
        out_shape=jax.ShapeDtypeStruct((L, 16, N), F32),
        compiler_params=_cparams("parallel", "parallel"),
        name="modulation",
    )(cc, w_mod, b_mod)


def _inproj_kernel(x_ref, g_ref, sh_ref, sc_ref, w_ref, o_ref):
    u = _norm_mod(x_ref[0], g_ref[...], sh_ref[0], sc_ref[0]).astype(BF16)
    n = w_ref.shape[1]
    cw = 256
    for c in range(n // cw):
        o_ref[0, :, c * cw:(c + 1) * cw] = _dot(u, w_ref[:, c * cw:(c + 1) * cw])


def in_projection(x, g, sh, sc, w, tm):
    B, T, D = x.shape
    N = w.shape[1]
    vec = pl.BlockSpec((1, 1, D), lambda b, i: (b, 0, 0))
    return pl.pallas_call(
        _inproj_kernel,
        grid=(B, T // tm),
        in_specs=[pl.BlockSpec((1, tm, D), lambda b, i: (b, i, 0)),
                  _const_spec((1, D)), vec, vec, _const_spec((D, N))],
        out_specs=pl.BlockSpec((1, tm, N), lambda b, i: (b, i, 0)),
        out_shape=jax.ShapeDtypeStruct((B, T, N), F32),
        compiler_params=_cparams("parallel", "parallel"),
        name="in_projection",
    )(x, g, sh, sc, w)


def _ffn_kernel(x_ref, g_ref, sh_ref, sc_ref, gt_ref, wg_ref, wu_ref, w2_ref, fg_ref, o_ref, h_ref, *, final_norm):
    x = x_ref[0]
    u = _norm_mod(x, g_ref[...], sh_ref[0], sc_ref[0]).astype(BF16)
    n = wg_ref.shape[1]
    cw = 256
    for c in range(n // cw):
        sl = slice(c * cw, (c + 1) * cw)
        h_ref[:, sl] = (_silu(_dot(u, wg_ref[:, sl])) * _dot(u, wu_ref[:, sl])).astype(BF16)
    y = x + gt_ref[0] * _dot(h_ref[...], w2_ref[...])
    if final_norm:
        ms = jnp.mean(y * y, axis=-1, keepdims=True)
        y = y * lax.rsqrt(ms + EPS) * fg_ref[...]
    o_ref[0] = y


def ffn_block(x, g, sh, sc, gate, w_gate, w_up, w_down, final_g, final_norm, tm):
    B, T, D = x.shape
    N = w_gate.shape[1]
    vec = pl.BlockSpec((1, 1, D), lambda b, i: (b, 0, 0))
    row = pl.BlockSpec((1, tm, D), lambda b, i: (b, i, 0))
    return pl.pallas_call(
        functools.partial(_ffn_kernel, final_norm=final_norm),
        grid=(B, T // tm),
        in_specs=[row, _const_spec((1, D)), vec, vec, vec, _const_spec((D, N)), _const_spec((D, N)),
                  _const_spec((N, D)), _const_spec((1, D))],
        out_specs=row,
        out_shape=jax.ShapeDtypeStruct((B, T, D), F32),
        scratch_shapes=[pltpu.VMEM((tm, N), BF16)],
        compiler_params=_cparams("parallel", "parallel"),
        name="ffn_block",
    )(x, g, sh, sc, gate, w_gate, w_up, w_down, final_g)


def _halo_specs(tm, T, width, col_fn, halo):
    r = tm // halo
    last = T // halo - 1
    main = pl.BlockSpec((1, tm, width), lambda *g: (g[0], g[1], col_fn(*g)))
    prev = pl.BlockSpec((1, halo, width), lambda *g: (g[0], jnp.maximum(g[1] * r - 1, 0), col_fn(*g)))
    nxt = pl.BlockSpec((1, halo, width), lambda *g: (g[0], jnp.minimum((g[1] + 1) * r, last), col_fn(*g)))
    return main, prev, nxt


def _shift_rows(x, prev, nxt, s, first, last):
    tm = x.shape[0]
    if s > 0:
        head = jnp.where(first, 0.0, prev[prev.shape[0] - s:, :])
        return head if s == tm else jnp.concatenate([head, x[:tm - s, :]], axis=0)
    s = -s
    tail = jnp.where(last, 0.0, nxt[:s, :])
    return tail if s == tm else jnp.concatenate([x[s:, :], tail], axis=0)


def _hy_prep_kernel(p_ref, pp_ref, pn_ref, w_ref, b_ref, v_ref, g1_ref, g2_ref):
    i = pl.program_id(1)
    first = i == 0
    last = i == pl.num_programs(1) - 1
    x = p_ref[0]
    xm = _shift_rows(x, pp_ref[0], pn_ref[0], 1, first, last)
    xp = _shift_rows(x, pp_ref[0], pn_ref[0], -1, first, last)
    u = b_ref[...] + w_ref[0:1, :] * xm + w_ref[1:2, :] * x + w_ref[2:3, :] * xp
    v_ref[0] = u[:, 0:BR_W]
    g1_ref[0] = u[:, BR_W:2 * BR_W]
    g2_ref[0] = u[:, 2 * BR_W:3 * BR_W]


def hyena_prep(p, conv_w, conv_b, tm):
    B, T, _ = p.shape
    main, prev, nxt = _halo_specs(tm, T, HY_IN, lambda b, i: OFF_HY // HY_IN, 8)
    out = pl.BlockSpec((1, tm, BR_W), lambda b, i: (b, i, 0))
    shp = jax.ShapeDtypeStruct((B, T, BR_W), F32)
    return pl.pallas_call(
        _hy_prep_kernel,
        grid=(B, T // tm),
        in_specs=[main, prev, nxt, _const_spec((3, HY_IN)), _const_spec((1, HY_IN))],
        out_specs=[out, out, out],
        out_shape=[shp, shp, shp],
        compiler_params=_cparams("parallel", "parallel"),
        name="hyena_prep",
    )(p, p, p, conv_w, conv_b)


def _hy_filter_kernel(feat_ref, w1_ref, b1_ref, w2_ref, b2_ref, w3_ref, fq_ref, rates_ref, h_ref, ss_ref):
    i = pl.program_id(0)
    feat = feat_ref[...]
    t = feat[:, 0:1]
    h = jnp.sin(fq_ref[0:1, :] * (_dot(feat, w1_ref[...], precision=HI) + b1_ref[...]))
    h = jnp.sin(fq_ref[1:2, :] * (_dot(h, w2_ref[...], precision=HI) + b2_ref[...]))
    h = _dot(h, w3_ref[...], precision=HI) * jnp.exp(-t * rates_ref[...])
    row = lax.broadcasted_iota(jnp.int32, h.shape, 0) + i * h.shape[0]
    col = lax.broadcasted_iota(jnp.int32, h.shape, 1)
    h = jnp.where((row == 0) & ((col // BR_W) % 2 == 1), 0.0, h)
    h_ref[...] = h

    @pl.when(i == 0)
    def _():
        ss_ref[...] = jnp.zeros_like(ss_ref)

    ss_ref[...] += jnp.sum(h * h, axis=0, keepdims=True)


def hyena_filter(feat, w1, b1, w2, b2, w3, freq, rates, tl):
    L = feat.shape[0]
    C = w3.shape[1]
    return pl.pallas_call(
        _hy_filter_kernel,
        grid=(L // tl,),
        in_specs=[pl.BlockSpec((tl, HY_FEAT_PAD), lambda i: (i, 0)),
                  _const_spec((HY_FEAT_PAD, HY_HID)), _const_spec((1, HY_HID)),
                  _const_spec((HY_HID, HY_HID)), _const_spec((1, HY_HID)),
                  _const_spec((HY_HID, C)), _const_spec((2, HY_HID)), _const_spec((1, C))],
        out_specs=[pl.BlockSpec((tl, C), lambda i: (i, 0)), pl.BlockSpec((1, C), lambda i: (0, 0))],
        out_shape=[jax.ShapeDtypeStruct((L, C), F32), jax.ShapeDtypeStruct((1, C), F32)],
        compiler_params=_cparams("arbitrary"),
        name="hyena_filter",
    )(feat, w1, b1, w2, b2, w3, freq, rates)


def _filter_scale(ss_ref, o):
    e = ss_ref[:, 2 * o * BR_W:(2 * o + 1) * BR_W] + ss_ref[:, (2 * o + 1) * BR_W:(2 * o + 2) * BR_W]
    return lax.rsqrt(e + EPS)


def _combine_spectrum(x, ss_ref, o, half):
    xf = x[:, 2 * o * BR_W:(2 * o + 1) * BR_W]
    xb = x[:, (2 * o + 1) * BR_W:(2 * o + 2) * BR_W]
    sc = _filter_scale(ss_ref, o)
    hr = (xf[:half] + xb[:half]) * sc
    hi = (xf[half:] - xb[half:]) * sc
    return jnp.concatenate([hr, hi], axis=0)


def _cmul(x, h, half):
    xr, xi = x[:half], x[half:]
    hr, hi = h[:half], h[half:]
    return jnp.concatenate([xr * hr - xi * hi, xr * hi + xi * hr], axis=0)


def _dft1_kernel(z_ref, f_ref, a_ref):
    a_ref[0] = _dot(f_ref[...], z_ref[0].astype(BF16)).astype(BF16)


def dft_stage1(z, f1, tn):
    B, n1, W = z.shape
    M = f1.shape[0]
    return pl.pallas_call(
        _dft1_kernel,
        grid=(B, W // tn),
        in_specs=[pl.BlockSpec((1, n1, tn), lambda b, j: (b, 0, j)), _const_spec((M, n1))],
        out_specs=pl.BlockSpec((1, M, tn), lambda b, j: (b, 0, j)),
        out_shape=jax.ShapeDtypeStruct((B, M, W), BF16),
        compiler_params=_cparams("parallel", "parallel"),
        name="dft_stage1",
    )(z, f1)


def _spec2_kernel(a_ref, g_ref, ss_ref, h_ref):
    kb = g_ref.shape[0]
    n2 = a_ref.shape[3]
    for k in range(kb):
        a = a_ref[0, :, k].reshape(2 * n2, a_ref.shape[4])
        x = _dot(g_ref[k], a)
        for o in range(2):
            h_ref[o, k] = _combine_spectrum(x, ss_ref, o, n2)


def filter_spectrum(a, g, ss, kb):
    _, _, N1, N2, C = a.shape
    return pl.pallas_call(
        _spec2_kernel,
        grid=(N1 // kb,),
        in_specs=[pl.BlockSpec((1, 2, kb, N2, C), lambda i: (0, 0, i, 0, 0)),
                  pl.BlockSpec((kb, 2 * N2, 2 * N2), lambda i: (i, 0, 0)),
                  _const_spec((1, C))],
        out_specs=pl.BlockSpec((2, kb, 2 * N2, BR_W), lambda i: (0, i, 0, 0)),
        out_shape=jax.ShapeDtypeStruct((2, N1, 2 * N2, BR_W), F32),
        compiler_params=_cparams("parallel"),
        name="filter_spectrum",
    )(a, g, ss)


def _conv2_kernel(a_ref, g_ref, gi_ref, h_ref, o_ref):
    kb = g_ref.shape[0]
    n2 = a_ref.shape[3]
    C = a_ref.shape[4]
    for k in range(kb):
        a = a_ref[0, :, k].reshape(2 * n2, C)
        y = _cmul(_dot(g_ref[k], a), h_ref[0, k], n2).astype(BF16)
        o_ref[0, :, k] = _dot(gi_ref[k], y).astype(BF16).reshape(2, n2, C)


def spectral_multiply(a, g, gi, h, o, kb):
    B, _, N1, N2, C = a.shape
    blk = pl.BlockSpec((1, 2, kb, N2, C), lambda i, b: (b, 0, i, 0, 0))
    mat = pl.BlockSpec((kb, 2 * N2, 2 * N2), lambda i, b: (i, 0, 0))
    return pl.pallas_call(
        _conv2_kernel,
        grid=(N1 // kb, B),
        in_specs=[blk, mat, mat, pl.BlockSpec((1, kb, 2 * N2, C), lambda i, b: (o, i, 0, 0))],
        out_specs=blk,
        out_shape=jax.ShapeDtypeStruct(a.shape, BF16),
        compiler_params=_cparams("parallel", "parallel"),
        name="spectral_multiply",
    )(a, g, gi, h)


def _idft1_kernel(b_ref, f_ref, z_ref, gate_ref, bias_ref, f1_ref, o_ref, *a_ref):
    y = _dot(f_ref[...], b_ref[0])
    z = z_ref[0]
    out = gate_ref[0] * (y + bias_ref[...] * z)
    o_ref[0] = out
    if a_ref:
        a_ref[0][0] = _dot(f1_ref[...], out.astype(BF16)).astype(BF16)


def idft_stage1(bm, fi, z, gate, bias, f1, tn, with_next):
    B, M, W = bm.shape
    n1 = fi.shape[0]
    blk = pl.BlockSpec((1, n1, tn), lambda b, j: (b, 0, j))
    wide = pl.BlockSpec((1, M, tn), lambda b, j: (b, 0, j))
    out_specs, out_shape = [blk], [jax.ShapeDtypeStruct((B, n1, W), F32)]
    if with_next:
        out_specs.append(wide)
        out_shape.append(jax.ShapeDtypeStruct((B, M, W), BF16))
    return pl.pallas_call(
        _idft1_kernel,
        grid=(B, W // tn),
        in_specs=[wide, _const_spec((n1, M)), blk, blk, _const_spec((1, tn)), _const_spec((M, n1))],
        out_specs=out_specs,
        out_shape=out_shape,
        compiler_params=_cparams("parallel", "parallel"),
        name="idft_stage1",
    )(bm, fi, z, gate, bias, f1)


@functools.lru_cache(maxsize=None)
def _dft_tables(L):
    N = 2 * L
    N2 = 128
    N1 = N // N2
    nz = L // N2
    k1 = np.arange(N1)[:, None]
    n1 = np.arange(nz)[None, :]
    th = 2 * np.pi * ((k1 * n1) % N1) / N1
    f1 = np.concatenate([np.cos(th), -np.sin(th)], axis=0)
    fi = np.concatenate([np.cos(th).T, -np.sin(th).T], axis=1) / N
    kk1 = np.arange(N1)[:, None, None]
    k2 = np.arange(N2)[None, :, None]
    n2 = np.arange(N2)[None, None, :]
    ph = 2 * np.pi * ((n2 * k2 * N1 + n2 * kk1) % N) / N
    gr, gim = np.cos(ph), -np.sin(ph)
    g = np.concatenate([np.concatenate([gr, -gim], axis=2), np.concatenate([gim, gr], axis=2)], axis=1)
    hr, him = np.swapaxes(gr, 1, 2), -np.swapaxes(gim, 1, 2)
    gi = np.concatenate([np.concatenate([hr, -him], axis=2), np.concatenate([him, hr], axis=2)], axis=1)
    return tuple(np.asarray(t, np.float32) for t in (f1, fi, g, gi))


def _spec_direct_kernel(hf_ref, f_ref, ss_ref, h_ref):
    x = _dot(f_ref[...], hf_ref[...].astype(BF16))
    half = x.shape[0] // 2
    for o in range(2):
        h_ref[o] = _combine_spectrum(x, ss_ref, o, half)


def filter_spectrum_direct(hf, f, ss):
    L, C = hf.shape
    return pl.pallas_call(
        _spec_direct_kernel,
        grid=(1,),
        in_specs=[_const_spec((L, C)), _const_spec((4 * L, L)), _const_spec((1, C))],
        out_specs=pl.BlockSpec((2, 4 * L, BR_W), lambda i: (0, 0, 0)),
        out_shape=jax.ShapeDtypeStruct((2, 4 * L, BR_W), F32),
        compiler_params=_cparams("arbitrary"),
        name="filter_spectrum_direct",
    )(hf, f, ss)


def _conv_direct_kernel(z_ref, gate_ref, bias_ref, f_ref, fi_ref, h_ref, o_ref):
    z = z_ref[0]
    x = _dot(f_ref[...], z.astype(BF16))
    y = _cmul(x, h_ref[0], x.shape[0] // 2).astype(BF16)
    o_ref[0] = gate_ref[0] * (_dot(fi_ref[...], y) + bias_ref[...] * z)


def conv_direct(z, gate, bias, f, fi, h, o):
    B, L, C = z.shape
    blk = pl.BlockSpec((1, L, C), lambda b: (b, 0, 0))
    return pl.pallas_call(
        _conv_direct_kernel,
        grid=(B,),
        in_specs=[blk, blk, _const_spec((1, C)), _const_spec((4 * L, L)), _const_spec((L, 4 * L)),
                  pl.BlockSpec((1, 4 * L, C), lambda b: (o, 0, 0))],
        out_specs=blk,
        out_shape=jax.ShapeDtypeStruct((B, L, C), F32),
        compiler_params=_cparams("parallel"),
        name="conv_direct",
    )(z, gate, bias, f, fi, h)


@functools.lru_cache(maxsize=None)
def _dft_direct_tables(L):
    N = 2 * L
    k = np.arange(N)[:, None]
    n = np.arange(L)[None, :]
    th = 2 * np.pi * ((k * n) % N) / N
    f = np.concatenate([np.cos(th), -np.sin(th)], axis=0)
    fi = np.concatenate([np.cos(th).T, -np.sin(th).T], axis=1) / N
    return np.asarray(f, np.float32), np.asarray(fi, np.float32)


@functools.lru_cache(maxsize=None)
def _filter_features(L):
    t = np.arange(L, dtype=np.float32) / np.float32(L)
    ang = (2.0 * math.pi) * t[:, None].astype(np.float64) * np.arange(1, HY_BANDS + 1)
    feat = np.zeros((L, HY_FEAT_PAD), np.float32)
    feat[:, 0] = t
    feat[:, 1:1 + HY_BANDS] = np.sin(ang)
    feat[:, 1 + HY_BANDS:HY_FEAT] = np.cos(ang)
    rates = np.abs(np.linspace(math.log(HY_TARGET) / HY_SLOW, math.log(HY_TARGET) / HY_FAST, BR_W))
    return feat, np.tile(np.asarray(rates, np.float32), 4)[None, :]


def hyena_branch(v, g1, g2, hp):
    f_w1, f_b1, f_w2, f_b2, f_w3, freq, bias = hp
    B, L, C = v.shape
    feat, rates = _filter_features(L)
    hf, ss = hyena_filter(jnp.asarray(feat), f_w1, f_b1, f_w2, f_b2, f_w3, freq, jnp.asarray(rates),
                          min(L, 512))
    if L <= 512:
        f, fi = (jnp.asarray(t).astype(BF16) for t in _dft_direct_tables(L))
        spec = filter_spectrum_direct(hf, f, ss)
        z = v
        for o, gate in enumerate((g1, g2)):
            z = conv_direct(z, gate, bias[o:o + 1], f, fi, spec, o)
        return z
    f1, fi1, g, gi = (jnp.asarray(t).astype(BF16) for t in _dft_tables(L))
    N2 = 128
    N1 = 2 * L // N2
    nz = L // N2
    kb = 8
    a = dft_stage1(hf.reshape(1, nz, N2 * 4 * C), f1, 8192)
    spec = filter_spectrum(a.reshape(1, 2, N1, N2, 4 * C), g, ss, kb)
    W = N2 * C
    tn = 8192
    z = v.reshape(B, nz, W)
    a = dft_stage1(z, f1, tn)
    for o, gate in enumerate((g1, g2)):
        bm = spectral_multiply(a.reshape(B, 2, N1, N2, C), g, gi, spec, o, kb).reshape(B, 2 * N1, W)
        out = idft_stage1(bm, fi1, z, gate.reshape(B, nz, W), jnp.tile(bias[o:o + 1], (1, tn // C)), f1, tn, o == 0)
        z = out[0]
        if o == 0:
            a = out[1]
    return z.reshape(B, L, C)


def _head_sum(x, seg_ref):
    hi = x.astype(BF16)
    lo = (x - hi.astype(F32)).astype(BF16)
    return _dot(hi, seg_ref[...]) + _dot(lo, seg_ref[...])


def _rw_prep_kernel(p_ref, pp_ref, pn_ref, mu_ref, wl_ref, w0_ref, a0_ref, kv_ref, seg_ref, *outs, on_grid):
    for bi in range(p_ref.shape[0]):
        _rw_prep_one(bi, p_ref, pp_ref, pn_ref, mu_ref, wl_ref, w0_ref, a0_ref, kv_ref, seg_ref, outs, on_grid)


def _rw_prep_one(bi, p_ref, pp_ref, pn_ref, mu_ref, wl_ref, w0_ref, a0_ref, kv_ref, seg_ref, outs, on_grid):
    def put(pair, val):
        pair[0][:, bi, :] = val[:, 0:LANE_W]
        pair[1][:, bi, :] = val[:, LANE_W:2 * LANE_W]

    r_out, v_out, na_out = outs[0:2], outs[2:4], outs[4:6]
    w_out, kt_out, b_out = outs[6:10], outs[10:14], outs[14:18]
    bonus_ref, g_ref = outs[18], outs[19]
    i = pl.program_id(0)
    first = i == 0
    last = i == pl.num_programs(0) - 1
    x = p_ref[bi]
    tm = x.shape[0]
    grp = lax.broadcasted_iota(jnp.int32, x.shape, 1) % 4
    prev, nxt = pp_ref[bi], pn_ref[bi]
    if on_grid:
        col = (lax.broadcasted_iota(jnp.int32, (tm, 1), 0) + i * tm) % GRID_W
        left = jnp.where(col == 0, 0.0, pltpu.roll(x, 1, 0))
        right = jnp.where(col == GRID_W - 1, 0.0, pltpu.roll(x, tm - 1, 0))
        up = _shift_rows(x, prev, nxt, GRID_W, first, last)
        down = _shift_rows(x, prev, nxt, -GRID_W, first, last)
        shifted = jnp.where(grp == 0, left, jnp.where(grp == 1, right, jnp.where(grp == 2, up, down)))
    else:
        before = _shift_rows(x, prev, nxt, 1, first, last)
        after = _shift_rows(x, prev, nxt, -1, first, last)
        shifted = jnp.where(grp % 2 == 0, before, after)
    xx = x + (shifted - x) * mu_ref[...]
    r = xx[:, 0:BR_W]
    k = xx[:, BR_W:2 * BR_W]
    v = xx[:, 2 * BR_W:3 * BR_W]
    lo = xx[:, 3 * BR_W:4 * BR_W]
    ll = lax.broadcasted_iota(jnp.int32, lo.shape, 1)
    act = jnp.where(ll < RW_LORA, jnp.tanh(lo), jnp.where(ll < 2 * RW_LORA, lo, _sigmoid(lo)))
    z = _dot(act.astype(BF16), wl_ref[...])
    kk = k * kv_ref[0:1, :]
    kk = kk * lax.rsqrt(_head_sum(kk * kk, seg_ref) + 1e-12)
    put(r_out, r)
    put(v_out, v)
    put(na_out, -kk)
    bonus_ref[bi] = _head_sum(r * k * kv_ref[2:3, :], seg_ref) * v
    g_ref[bi] = z[:, 4 * BR_W:5 * BR_W]
    for d in range(2):
        sig = _sigmoid(w0_ref[d:d + 1, :] + z[:, d * BR_W:(d + 1) * BR_W])
        put(w_out[2 * d:2 * d + 2], jnp.exp(-math.exp(-0.5) * sig))
        a = _sigmoid(a0_ref[d:d + 1, :] + z[:, (2 + d) * BR_W:(3 + d) * BR_W])
        put(kt_out[2 * d:2 * d + 2], k * (1.0 + (a - 1.0) * kv_ref[1:2, :]))
        put(b_out[2 * d:2 * d + 2], kk * a)


def rwkv_prep(p, mu, w_lora, w0, a0, kvec, seg, on_grid, tm):
    B, T, _ = p.shape
    halo = GRID_W if on_grid else 8
    r = tm // halo
    last = T // halo - 1
    main = pl.BlockSpec((B, tm, RW_PAD), lambda i: (0, i, OFF_RW // RW_PAD))
    prev = pl.BlockSpec((B, halo, RW_PAD), lambda i: (0, jnp.maximum(i * r - 1, 0), OFF_RW // RW_PAD))
    nxt = pl.BlockSpec((B, halo, RW_PAD), lambda i: (0, jnp.minimum((i + 1) * r, last), OFF_RW // RW_PAD))
    half = pl.BlockSpec((tm, B, LANE_W), lambda i: (i, 0, 0))
    full = pl.BlockSpec((B, tm, BR_W), lambda i: (0, i, 0))
    sh = jax.ShapeDtypeStruct((T, B, LANE_W), F32)
    sf = jax.ShapeDtypeStruct((B, T, BR_W), F32)
    return pl.pallas_call(
        functools.partial(_rw_prep_kernel, on_grid=on_grid),
        grid=(T // tm,),
        in_specs=[main, prev, nxt, _const_spec((1, RW_PAD)), _const_spec((BR_W, 5 * BR_W)),
                  _const_spec((2, BR_W)), _const_spec((2, BR_W)), _const_spec((3, BR_W)),
                  _const_spec((BR_W, BR_W))],
        out_specs=[half] * 18 + [full, full],
        out_shape=[sh] * 18 + [sf, sf],
        compiler_params=_cparams("parallel"),
        name="rwkv_prep",
    )(p, p, p, mu, w_lora, w0, a0, kvec, seg)


RW_VQ = RW_HEAD // 4
RW_VH = 2 * RW_VQ
LANE_W = 128
N_KEYED = 5
RW_UNROLL = 2
RW_PARTIALS = 1


def _chain_rows(refs, t, tb, shift):
    blocks = []
    for d in range(2):
        for half in range(2):
            x = refs[2 * d + half][t if d == 0 else tb]
            blocks.append(x if shift == 0 else pltpu.roll(x, LANE_W - shift, 1))
    return blocks


def _rw_scan_kernel(*refs, B, Tt):
    nin = 4 * (N_KEYED + 1)
    ins = refs[:nin]
    s0_ref = refs[nin]
    y_refs = refs[nin + 1:nin + 5]
    s_ref = refs[nin + 5]
    first_scratch = nin + 6
    per_set = N_KEYED + 2
    sets = [refs[first_scratch + per_set * u:first_scratch + per_set * (u + 1)] for u in range(RW_UNROLL)]
    bufs = [(st[:N_KEYED], st[N_KEYED], st[N_KEYED + 1]) for st in sets]
    sa_ref = refs[first_scratch + per_set * RW_UNROLL]
    NL = 16 * B
    A_TILE = 3

    @pl.when(pl.program_id(0) == 0)
    def _():
        s_ref[...] = s0_ref[...]
        bufs[RW_UNROLL - 1][2][...] = jnp.zeros_like(bufs[RW_UNROLL - 1][2])

    def keyed_job(t, slot, n):
        def issue():
            rows = _chain_rows(ins[4 * n:4 * n + 4], t, Tt - 1 - t, 0)
            return jnp.concatenate(rows * 4, axis=0).T

        def commit(val):
            bufs[slot][0][n][...] = val
        return issue, commit

    def values_job(t, slot):
        def issue():
            rows = []
            for vq in range(4):
                rows += _chain_rows(ins[4 * N_KEYED:], t, Tt - 1 - t, vq * RW_VQ)
            vt = jnp.concatenate(rows, axis=0).T
            return jnp.concatenate([vt[0:RW_VQ], vt[RW_HEAD:RW_HEAD + RW_VQ]], axis=0)

        def commit(val):
            bufs[slot][1][...] = val
        return issue, commit

    def output_job(t, slot):
        def issue():
            ys = bufs[slot][2][...]
            pad = jnp.zeros((RW_HEAD - RW_VQ, NL), F32)
            y = jnp.concatenate([ys[0:RW_VQ], pad, ys[RW_VQ:RW_VH], pad], axis=0).T
            out = []
            for d in range(2):
                for half in range(2):
                    acc = None
                    for vq in range(4):
                        row0 = ((vq * 2 + d) * 2 + half) * B
                        blk = y[row0:row0 + B, :]
                        blk = blk if vq == 0 else pltpu.roll(blk, vq * RW_VQ, 1)
                        acc = blk if acc is None else acc + blk
                    out.append(acc)
            return out

        def commit(val):
            for d in range(2):
                for half in range(2):
                    y_refs[2 * d + half][:, t if d == 0 else Tt - 1 - t, :] = val[2 * d + half]
        return issue, commit

    groups = range(RW_VH // 8)
    others = tuple(n for n in range(N_KEYED) if n != A_TILE)
    never = pl.program_id(0) < 0

    def step_pieces(slot, sa_in, sa_out):
        tiles, vt_ref, ys_ref = bufs[slot]
        a_next = bufs[(slot + 1) % RW_UNROLL][0][A_TILE]
        row = lambda ref, k: ref[pl.ds(k, 1), :]
        ys = [[None] * RW_PARTIALS for _ in groups]
        san = [[None] * RW_PARTIALS for _ in groups]
        vt = {}
        acc = lambda lst, i, v: lst.__setitem__(i, v if lst[i] is None else lst[i] + v)

        def update(k0, k1):
            if not vt:
                for g in groups:
                    vt[g] = vt_ref[g * 8:(g + 1) * 8, :]
            for k in range(k0, k1):
                for h2 in range(2):
                    kk = h2 * RW_HEAD + k
                    r_k, w_k, kt_k, b_k = (row(tiles[n], kk) for n in (0, 1, 2, 4))
                    an_k = row(a_next, kk)
                    for g in (2 * h2, 2 * h2 + 1):
                        s = s_ref[g, k] * w_k + sa_in[g] * b_k + vt[g] * kt_k
                        s_ref[g, k] = s
                        acc(ys[g], k % RW_PARTIALS, s * r_k)
                        acc(san[g], k % RW_PARTIALS, s * an_k)
            if k1 == RW_HEAD:
                for g in groups:
                    ys_ref[g * 8:(g + 1) * 8, :] = functools.reduce(lambda a, b: a + b, ys[g])
                    sa_out.append(functools.reduce(lambda a, b: a + b, san[g]))

        def anchor(i, val):
            g = i % len(groups)
            if val is not None and ys[g][0] is not None:
                ys[g][0] = jnp.where(never, val, ys[g][0])

        cuts = [0, 9, 18, 27, 36, 45, 54, RW_HEAD]
        return [functools.partial(update, cuts[i], cuts[i + 1]) for i in range(len(cuts) - 1)], anchor

    def run(step, jobs):
        pieces, anchor = step
        pending = None
        for i, piece in enumerate(pieces):
            val = jobs[i][0]() if i < len(jobs) else None
            piece()
            if pending is not None and i + 1 < len(pieces):
                anchor(*pending)
            pending = None
            if i < len(jobs):
                jobs[i][1](val)
                if not isinstance(val, list):
                    pending = (i, val[val.shape[0] - 8:, :])
                elif val[3].shape == (8, NL):
                    pending = (i, val[3])

    for issue, commit in ([keyed_job(0, 0, n) for n in range(N_KEYED)] + [values_job(0, 0)]
                          + [keyed_job(1, 1, A_TILE)]):
        commit(issue())
    first = [[None, None] for _ in groups]
    for k in range(RW_HEAD):
        for g in groups:
            p = s_ref[g, k] * bufs[0][0][A_TILE][pl.ds((g // 2) * RW_HEAD + k, 1), :]
            first[g][k % 2] = p if first[g][k % 2] is None else first[g][k % 2] + p
    for g in groups:
        sa_ref[g] = first[g][0] + first[g][1]

    def body(j, carry):
        t0 = RW_UNROLL * j
        sa = [sa_ref[g] for g in groups]
        for u in range(RW_UNROLL):
            t = t0 + u
            t1 = jnp.minimum(t + 1, Tt - 1)
            jobs = [output_job(jnp.maximum(t - 1, 0), (u - 1) % RW_UNROLL),
                    keyed_job(jnp.minimum(t + 2, Tt - 1), (u + 2) % RW_UNROLL, A_TILE)]
            jobs += [keyed_job(t1, (u + 1) % RW_UNROLL, n) for n in others] + [values_job(t1, (u + 1) % RW_UNROLL)]
            sa_next = []
            run(step_pieces(u, sa, sa_next), jobs)
            sa = sa_next
        for g in groups:
            sa_ref[g] = sa[g]
        return carry

    lax.fori_loop(0, Tt // RW_UNROLL, body, 0)
    issue, commit = output_job(Tt - 1, RW_UNROLL - 1)
    commit(issue())


def rwkv_mix(prep, s0, tt):
    T, B, _ = prep[0][0].shape
    nb = T // tt
    tblk = lambda i, d: i if d == 0 else nb - 1 - i

    in_specs, args = [], []
    for quad in prep:
        for j, x in enumerate(quad):
            in_specs.append(pl.BlockSpec((tt, B, LANE_W), lambda i, d=j // 2: (tblk(i, d), 0, 0)))
            args.append(x)
    NL = 16 * B
    sblk = pl.BlockSpec((RW_VH // 8, RW_HEAD, 8, NL), lambda i: (0, 0, 0, 0))
    yspecs = [pl.BlockSpec((B, tt, LANE_W), lambda i, d=d: (0, tblk(i, d), 0)) for d in range(2) for _ in range(2)]
    ysh = jax.ShapeDtypeStruct((B, T, LANE_W), F32)
    out = pl.pallas_call(
        functools.partial(_rw_scan_kernel, B=B, Tt=tt),
        grid=(nb,),
        in_specs=in_specs + [sblk],
        out_specs=yspecs + [sblk],
        out_shape=[ysh] * 4 + [jax.ShapeDtypeStruct((RW_VH // 8, RW_HEAD, 8, NL), F32)],
        scratch_shapes=([pltpu.VMEM((LANE_W, NL), F32)] * N_KEYED + [pltpu.VMEM((RW_VH, NL), F32)] * 2) * RW_UNROLL
        + [pltpu.VMEM((RW_VH // 8, 8, NL), F32)],
        compiler_params=_cparams("arbitrary"),
        name="rwkv_scan",
    )(*args, s0)
    return (out[0], out[1]), (out[2], out[3]), out[4]


def _gelu_tanh(x):
    return 0.5 * x * (1.0 + jnp.tanh(math.sqrt(2.0 / math.pi) * (x + 0.044715 * (x * x * x))))


def _lru_prep_kernel(p_ref, pp_ref, pn_ref, cw_ref, cb_ref, w_ref, bias_ref, lam_ref, a_ref, b_ref, gg_ref):
    i = pl.program_id(1)
    first = i == 0
    last = i == pl.num_programs(1) - 1
    x = p_ref[0][:, 0:BR_W]
    prev = pp_ref[0][:, 0:BR_W]
    nxt = pn_ref[0][:, 0:BR_W]
    xc = cb_ref[...] + cw_ref[0:1, :] * _shift_rows(x, prev, nxt, 2, first, last)
    xc = xc + cw_ref[1:2, :] * _shift_rows(x, prev, nxt, 1, first, last)
    xc = xc + cw_ref[2:3, :] * x
    xc = xc + cw_ref[3:4, :] * _shift_rows(x, prev, nxt, -1, first, last)
    z = _dot(xc.astype(BF16), w_ref[...])
    for d in range(2):
        r = _sigmoid(z[:, 2 * d * BR_W:(2 * d + 1) * BR_W] + bias_ref[2 * d:2 * d + 1, :])
        gi = _sigmoid(z[:, (2 * d + 1) * BR_W:(2 * d + 2) * BR_W] + bias_ref[2 * d + 1:2 * d + 2, :])
        log_a = -LRU_C * r * _softplus(-lam_ref[d:d + 1, :])
        a_ref[d, 0] = jnp.exp(log_a)
        b_ref[d, 0] = jnp.sqrt(1.0 - jnp.exp(2.0 * log_a)) * (gi * xc)
    gg_ref[0] = _gelu_tanh(p_ref[0][:, BR_W:2 * BR_W])


def lru_prep(p, conv_w, conv_b, w_blk, bias, lam, tm):
    B, T, _ = p.shape
    main, prev, nxt = _halo_specs(tm, T, LRU_IN, lambda b, i: OFF_LRU // LRU_IN, 8)
    two = pl.BlockSpec((2, 1, tm, BR_W), lambda b, i: (0, b, i, 0))
    s2 = jax.ShapeDtypeStruct((2, B, T, BR_W), F32)
    return pl.pallas_call(
        _lru_prep_kernel,
        grid=(B, T // tm),
        in_specs=[main, prev, nxt, _const_spec((4, BR_W)), _const_spec((1, BR_W)),
                  _const_spec((BR_W, 4 * BR_W)), _const_spec((4, BR_W)), _const_spec((2, BR_W))],
        out_specs=[two, two, pl.BlockSpec((1, tm, BR_W), lambda b, i: (b, i, 0))],
        out_shape=[s2, s2, jax.ShapeDtypeStruct((B, T, BR_W), F32)],
        compiler_params=_cparams("parallel", "parallel"),
        name="lru_prep",
    )(p, p, p, conv_w, conv_b, w_blk, bias, lam)


def _affine_scan(a, b, reverse):
    tb = a.shape[0]
    row = lax.broadcasted_iota(jnp.int32, (tb, 1), 0)
    s = 1
    while s < tb:
        sh = tb - s if reverse else s
        ok = (row < tb - s) if reverse else (row >= s)
        a_s = pltpu.roll(a, sh, 0)
        b_s = pltpu.roll(b, sh, 0)
        b = jnp.where(ok, a * b_s + b, b)
        a = jnp.where(ok, a * a_s, a)
        s *= 2
    return a, b


def _lru_scan_kernel(af_ref, bf_ref, ab_ref, bb_ref, h0_ref, hf_ref, hb_ref, fin_ref):
    @pl.when(pl.program_id(1) == 0)
    def _():
        fin_ref[...] = h0_ref[...]

    tb = af_ref.shape[2]
    a, b = _affine_scan(af_ref[0, 0], bf_ref[0, 0], False)
    h = b + a * fin_ref[0, 0]
    hf_ref[0] = h
    fin_ref[0, 0] = h[tb - 1:tb, :]
    a, b = _affine_scan(ab_ref[0, 0], bb_ref[0, 0], True)
    h = b + a * fin_ref[1, 0]
    hb_ref[0] = h
    fin_ref[1, 0] = h[0:1, :]


def lru_scan(a, b, h0, tb):
    _, B, T, C = a.shape
    nb = T // tb
    fwd = pl.BlockSpec((1, 1, tb, C), lambda bi, i: (0, bi, i, 0))
    bwd = pl.BlockSpec((1, 1, tb, C), lambda bi, i: (1, bi, nb - 1 - i, 0))
    st = pl.BlockSpec((2, 1, 1, C), lambda bi, i: (0, bi, 0, 0))
    return pl.pallas_call(
        _lru_scan_kernel,
        grid=(B, nb),
        in_specs=[fwd, fwd, bwd, bwd, st],
        out_specs=[pl.BlockSpec((1, tb, C), lambda bi, i: (bi, i, 0)),
                   pl.BlockSpec((1, tb, C), lambda bi, i: (bi, nb - 1 - i, 0)), st],
        out_shape=[jax.ShapeDtypeStruct((B, T, C), F32), jax.ShapeDtypeStruct((B, T, C), F32),
                   jax.ShapeDtypeStruct((2, B, 1, C), F32)],
        compiler_params=_cparams("parallel", "arbitrary"),
        name="lru_scan",
    )(a, b, a, b, h0)


def _rope(x, cos, sin):
    q4 = RET_HEAD // 4
    lane = lax.broadcasted_iota(jnp.int32, x.shape, 1) % (2 * q4)
    partner = jnp.where(lane < q4, pltpu.roll(x, x.shape[1] - q4, 1), pltpu.roll(x, q4, 1))
    return x * cos + partner * sin


def _ret_decay_tables(glane, gtile_ref, d, reverse, C, dm_ref, qd_ref, kd_ref):
    lg = -_softplus(-glane)
    idx = lax.broadcasted_iota(jnp.int32, (C, 1), 0).astype(F32)
    steps_in = (C - idx) if reverse else (idx + 1.0)
    steps_out = idx if reverse else (C - 1.0 - idx)
    qd_ref[d] = jnp.exp(steps_in * lg)
    kd_ref[d] = jnp.exp(steps_out * lg)
    ri = lax.broadcasted_iota(jnp.int32, (C, C), 0)
    ci = lax.broadcasted_iota(jnp.int32, (C, C), 1)
    diff = ((ci - ri) if reverse else (ri - ci)).astype(F32)
    for h in range(RET_HEADS):
        lg_h = -_softplus(-gtile_ref[d, h][0:1, :])
        dm_ref[d, h] = jnp.where(diff >= 0, jnp.exp(diff * lg_h), 0.0)


def _ret_dir(x, cos, sin, s, glane, d, dm_ref, qd_ref, kd_ref):
    C = x.shape[0]
    q = x[:, 0:BR_W]
    k = x[:, BR_W:2 * BR_W]
    v = x[:, 2 * BR_W:3 * BR_W].astype(BF16)
    if cos is not None:
        q = _rope(q, cos, sin)
        k = _rope(k, cos, sin)
    k = k * (RET_HEAD ** -0.5)
    lg = -_softplus(-glane)
    lane_head = lax.broadcasted_iota(jnp.int32, (1, BR_W), 1) // RET_HEAD
    qb = q.astype(BF16)
    kb = k.astype(BF16)
    y = _dot(qb, s.astype(BF16)) * qd_ref[d]
    for h in range(RET_HEADS):
        mh = lane_head == h
        sc = lax.dot_general(jnp.where(mh, qb, jnp.zeros_like(qb)), kb, (((1,), (1,)), ((), ())),
                             preferred_element_type=F32)
        y = y + jnp.where(mh, _dot((sc * dm_ref[d, h]).astype(BF16), v), 0.0)
    kd = (k * kd_ref[d]).astype(BF16)
    upd = lax.dot_general(kd, v, (((0,), (0,)), ((), ())), preferred_element_type=F32)
    rh = lax.broadcasted_iota(jnp.int32, (BR_W, BR_W), 0) // RET_HEAD
    ch = lax.broadcasted_iota(jnp.int32, (BR_W, BR_W), 1) // RET_HEAD
    s = s * jnp.exp(C * lg) + jnp.where(rh == ch, upd, 0.0)
    return y, s


def _ret_kernel(*refs, rope):
    if rope:
        xf_ref, xb_ref, cf_ref, sf_ref, cb_ref, sb_ref, gl_ref, gt_ref, s0_ref, yf_ref, yb_ref, s_ref = refs[:12]
    else:
        xf_ref, xb_ref, gl_ref, gt_ref, s0_ref, yf_ref, yb_ref, s_ref = refs[:8]
    tables = refs[-3:]
    C = xf_ref.shape[1]

    @pl.when(pl.program_id(1) == 0)
    def _():
        s_ref[...] = s0_ref[...]
        _ret_decay_tables(gl_ref[0:1, :], gt_ref, 0, False, C, *tables)
        _ret_decay_tables(gl_ref[1:2, :], gt_ref, 1, True, C, *tables)

    y, s = _ret_dir(xf_ref[0], cf_ref[...] if rope else None, sf_ref[...] if rope else None,
                    s_ref[0, 0], gl_ref[0:1, :], 0, *tables)
    yf_ref[0] = y
    s_ref[0, 0] = s
    y, s = _ret_dir(xb_ref[0], cb_ref[...] if rope else None, sb_ref[...] if rope else None,
                    s_ref[1, 0], gl_ref[1:2, :], 1, *tables)
    yb_ref[0] = y
    s_ref[1, 0] = s


def retention(p, cos, sin, glane, gtile, s0, rope):
    B, T, _ = p.shape
    C = RET_CHUNK
    nc = T // C
    cb = OFF_RET // RET_IN
    xf = pl.BlockSpec((1, C, RET_IN), lambda b, i: (b, i, cb))
    xb = pl.BlockSpec((1, C, RET_IN), lambda b, i: (b, nc - 1 - i, cb))
    tf = pl.BlockSpec((C, BR_W), lambda b, i: (i, 0))
    tb = pl.BlockSpec((C, BR_W), lambda b, i: (nc - 1 - i, 0))
    st = pl.BlockSpec((2, 1, BR_W, BR_W), lambda b, i: (0, b, 0, 0))
    ins = [xf, xb] + ([tf, tf, tb, tb] if rope else []) + [
        _const_spec((2, BR_W)), _const_spec((2, RET_HEADS, 8, C)), st]
    args = [p, p] + ([cos, sin, cos, sin] if rope else []) + [glane, gtile, s0]
    return pl.pallas_call(
        functools.partial(_ret_kernel, rope=rope),
        grid=(B, nc),
        in_specs=ins,
        out_specs=[pl.BlockSpec((1, C, BR_W), lambda b, i: (b, i, 0)),
                   pl.BlockSpec((1, C, BR_W), lambda b, i: (b, nc - 1 - i, 0)), st],
        out_shape=[jax.ShapeDtypeStruct((B, T, BR_W), F32), jax.ShapeDtypeStruct((B, T, BR_W), F32),
                   jax.ShapeDtypeStruct((2, B, BR_W, BR_W), F32)],
        scratch_shapes=[pltpu.VMEM((2, RET_HEADS, C, C), F32), pltpu.VMEM((2, C, BR_W), F32),
                        pltpu.VMEM((2, C, BR_W), F32)],
        compiler_params=_cparams("parallel", "arbitrary"),
        name="retention",
    )(*args)


@functools.lru_cache(maxsize=None)
def _rope_tables(T):
    pos = np.arange(T)
    q4 = RET_HEAD // 4
    inv = ROPE_BASE ** (-np.arange(q4, dtype=np.float64) / q4)
    cos = np.zeros((T, RET_HEAD))
    sin = np.zeros((T, RET_HEAD))
    for part, coord in enumerate((pos // GRID_W, pos % GRID_W)):
        ang = coord[:, None] * inv
        base = part * 2 * q4
        cos[:, base:base + q4] = np.cos(ang)
        cos[:, base + q4:base + 2 * q4] = np.cos(ang)
        sin[:, base:base + q4] = -np.sin(ang)
        sin[:, base + q4:base + 2 * q4] = np.sin(ang)
    tile = lambda t: np.asarray(np.tile(t, (1, RET_HEADS)), np.float32)
    return tile(cos), tile(sin)


def _head_norm(y, seg_ref, eps):
    mu = _head_sum(y, seg_ref) * (1.0 / RW_HEAD)
    yc = y - mu
    var = _head_sum(yc * yc, seg_ref) * (1.0 / RW_HEAD)
    return yc * lax.rsqrt(var + eps)


def _merge_kernel(x_ref, g_ref, sh_ref, sc_ref, gt_ref, hy_ref, rfl_ref, rfh_ref, rbl_ref, rbh_ref,
                  rbon_ref, rg_ref, lhf_ref, lhb_ref, lgg_ref, tyf_ref, tyb_ref, tg_ref, lng_ref, seg_ref,
                  wg_ref, br_ref, wo_ref, o_ref, m_ref):
    x = x_ref[0]
    u = _norm_mod(x, g_ref[...], sh_ref[0], sc_ref[0]).astype(BF16)
    wkv = jnp.concatenate([rfl_ref[0] + rbl_ref[0], rfh_ref[0] + rbh_ref[0]], axis=1)
    y_rw = (_head_norm(wkv, seg_ref, RW_LN_EPS) * lng_ref[...] + rbon_ref[0]) * rg_ref[0]
    y_lru = (lhf_ref[0] + lhb_ref[0]) * lgg_ref[0]
    y_ret = _head_norm(tyf_ref[0] + tyb_ref[0], seg_ref, RET_LN_EPS) * _silu(tg_ref[0])
    ys = [y.astype(BF16) for y in (hy_ref[0], y_rw, y_lru, y_ret)]
    D = x.shape[1]
    cw = 256
    for c in range(D // cw):
        acc = None
        for n in range(N_BRANCH):
            gate = _sigmoid(_dot(u, wg_ref[:, n * D + c * cw:n * D + (c + 1) * cw]))
            t = gate * _dot(ys[n], br_ref[n, :, c * cw:(c + 1) * cw])
            acc = t if acc is None else acc + t
        m_ref[:, c * cw:(c + 1) * cw] = acc.astype(BF16)
    o_ref[0] = x + gt_ref[0] * _dot(m_ref[...], wo_ref[...])


def merge(x, g, sh, sc, gt, p, y_hy, rw, lru, ret, ln_g, seg, w_gate, br, w_out, tm):
    B, T, D = x.shape
    vec = pl.BlockSpec((1, 1, D), lambda b, i: (b, 0, 0))
    row = pl.BlockSpec((1, tm, D), lambda b, i: (b, i, 0))
    brn = pl.BlockSpec((1, tm, BR_W), lambda b, i: (b, i, 0))
    half = pl.BlockSpec((1, tm, LANE_W), lambda b, i: (b, i, 0))
    tg = pl.BlockSpec((1, tm, BR_W), lambda b, i: (b, i, (OFF_RET + 3 * BR_W) // BR_W))
    return pl.pallas_call(
        _merge_kernel,
        grid=(B, T // tm),
        in_specs=[row, _const_spec((1, D)), vec, vec, vec, brn] + [half] * 4 + [brn] * 7 + [tg] + [
            _const_spec((1, BR_W)), _const_spec((BR_W, BR_W)), _const_spec((D, GATE_IN)),
            _const_spec((N_BRANCH, BR_W, D)), _const_spec((D, D))],
        out_specs=row,
        out_shape=jax.ShapeDtypeStruct((B, T, D), F32),
        scratch_shapes=[pltpu.VMEM((tm, D), BF16)],
        compiler_params=_cparams("parallel", "parallel"),
        name="merge",
    )(x, g, sh, sc, gt, y_hy, *rw, *lru, *ret, p, ln_g, seg, w_gate, br, w_out)


def _block_diag(w):
    G = w.shape[-3]
    eye = jnp.eye(G, dtype=w.dtype)
    full = w[..., :, :, None, :] * eye[:, None, :, None]
    return full.reshape(*w.shape[:-3], G * w.shape[-2], G * w.shape[-1])


def _mixers(p, lp, states, on_grid, with_output, tiles):
    B, T, _ = p.shape
    tm, tt = tiles[0], tiles[1]
    pre = rwkv_prep(p, *lp['rw'], on_grid, 2 * GRID_W)
    r, v, na, w, kt, b = pre[0:2] * 2, pre[2:4] * 2, pre[4:6] * 2, pre[6:10], pre[10:14], pre[14:18]
    bonus, g = pre[18], pre[19]
    y_f, y_b, rw_fin = rwkv_mix((r, w, kt, na, b, v), states[0], tt)
    a, bb, gg = lru_prep(p, *lp['lru'], tm)
    h_f, h_b, lru_fin = lru_scan(a, bb, states[1], tm)
    cos, sin = (jnp.asarray(t) for t in _rope_tables(T)) if on_grid else (None, None)
    t_f, t_b, ret_fin = retention(p, cos, sin, *lp['ret'], states[2], on_grid)
    fins = (rw_fin, lru_fin, ret_fin)
    if not with_output:
        return None, fins
    vg = hyena_prep(p, *lp['hy_conv'], min(T, 2048))
    y_hy = hyena_branch(vg[0], vg[1], vg[2], lp['hy'])
    return (y_hy, (y_f[0], y_f[1], y_b[0], y_b[1], bonus, g), (h_f, h_b, gg), (t_f, t_b)), fins


def kernel(x, c, ctx, c_ctx, w_mod, b_mod, norm1_g, norm2_g, w_in, hy_conv_w, hy_conv_b, hy_f_w1, hy_f_b1, hy_f_w2, hy_f_b2, hy_f_w3, hy_freq, hy_bias, rw_mu, rw_w0, rw_w2, rw_a0, rw_a2, rw_g2, rw_kk, rw_ka, rw_rk, rw_ln_g, lru_conv_w, lru_conv_b, lru_wa, lru_ba, lru_wx, lru_bx, lru_lam, ret_gamma, br_proj, w_out, ffn_w1, ffn_w2, final_g):
    B, T, D = x.shape
    TC = ctx.shape[1]
    L = w_in.shape[0]

    s0, s1, s2, s3 = HY_IN, HY_IN + RW_IN, HY_IN + RW_IN + LRU_IN, HY_IN + RW_IN + LRU_IN + RET_IN
    w_branch = jnp.concatenate([w_in[:, :, s0:s1], jnp.zeros((L, D, RW_PAD - RW_IN), w_in.dtype),
                                w_in[:, :, s2:s3], jnp.zeros((L, D, HY_GAP), w_in.dtype),
                                w_in[:, :, 0:s0], w_in[:, :, s1:s2]], axis=2).astype(BF16)
    w_gate = w_in[:, :, s3:].astype(BF16)
    mu = jnp.pad(rw_mu, ((0, 0), (0, RW_PAD - RW_IN)))[:, None, :]
    w_lora = jnp.zeros((L, BR_W, 5 * BR_W), F32)
    w_lora = w_lora.at[:, 0:RW_LORA, 0:BR_W].set(rw_w2[:, 0]).at[:, 0:RW_LORA, BR_W:2 * BR_W].set(rw_w2[:, 1])
    w_lora = w_lora.at[:, RW_LORA:2 * RW_LORA, 2 * BR_W:3 * BR_W].set(rw_a2[:, 0])
    w_lora = w_lora.at[:, RW_LORA:2 * RW_LORA, 3 * BR_W:4 * BR_W].set(rw_a2[:, 1])
    w_lora = w_lora.at[:, 2 * RW_LORA:3 * RW_LORA, 4 * BR_W:5 * BR_W].set(rw_g2).astype(BF16)
    kvec = jnp.stack([rw_kk, rw_ka, rw_rk], axis=1)
    seg = jnp.asarray(np.kron(np.eye(RW_HEADS), np.ones((RW_HEAD, RW_HEAD))), BF16)
    lru_w = jnp.concatenate([_block_diag(lru_wa[:, 0]), _block_diag(lru_wx[:, 0]),
                             _block_diag(lru_wa[:, 1]), _block_diag(lru_wx[:, 1])], axis=2).astype(BF16)
    lru_bias = jnp.stack([lru_ba[:, 0], lru_bx[:, 0], lru_ba[:, 1], lru_bx[:, 1]], axis=1)
    glane = jnp.repeat(ret_gamma, RET_HEAD, axis=2)
    gtile = jnp.broadcast_to(ret_gamma[:, :, :, None, None], (L, 2, RET_HEADS, 8, RET_CHUNK))
    f_w1 = jnp.pad(hy_f_w1, ((0, 0), (0, HY_FEAT_PAD - HY_FEAT), (0, 0)))
    br_b = br_proj.astype(BF16)
    w_out_b = w_out.astype(BF16)
    ffn_gate = ffn_w1[:, :, :D_FF].astype(BF16)
    ffn_up_w = ffn_w1[:, :, D_FF:].astype(BF16)
    ffn_w2_b = ffn_w2.astype(BF16)

    cc = jnp.concatenate([c, c_ctx[None, :], jnp.zeros((16 - B - 1, D), F32)], axis=0)
    mods = modulation(cc, w_mod.astype(BF16), b_mod[:, None, :])

    zero_states = (jnp.zeros((RW_VH // 8, RW_HEAD, 8, 16 * B), F32), jnp.zeros((2, B, 1, BR_W), F32),
                   jnp.zeros((2, B, BR_W, BR_W), F32))
    xc = ctx
    for l in range(L):
        last = l == L - 1
        lp = {
            'rw': (mu[l], w_lora[l], rw_w0[l], rw_a0[l], kvec[l], seg),
            'lru': (lru_conv_w[l], lru_conv_b[l][None, :], lru_w[l], lru_bias[l], lru_lam[l]),
            'ret': (glane[l], gtile[l]),
            'hy_conv': (hy_conv_w[l], hy_conv_b[l][None, :]),
            'hy': (f_w1[l], hy_f_b1[l][None, :], hy_f_w2[l], hy_f_b2[l][None, :], hy_f_w3[l], hy_freq[l],
                   hy_bias[l]),
        }
        g1 = norm1_g[l][None, :]
        g2 = norm2_g[l][None, :]
        m_lat = [m[:, None, :] for m in jnp.split(mods[l, :B], 6, axis=-1)]
        m_ctx = [jnp.broadcast_to(m[None, :, :], (B, 1, D)) for m in jnp.split(mods[l, B:B + 1], 6, axis=-1)]
        ln_g = rw_ln_g[l][None, :]

        def layer(xs, m, states, on_grid, with_output, tiles, final):
            p = in_projection(xs, g1, m[0], m[1], w_branch[l], tiles[0])
            br, fins = _mixers(p, lp, states, on_grid, with_output, tiles)
            if not with_output:
                return None, fins
            xs = merge(xs, g1, m[0], m[1], m[2], p, br[0], br[1], br[2], br[3], ln_g, seg,
                       w_gate[l], br_b[l], w_out_b[l], tiles[2])
            xs = ffn_block(xs, g2, m[3], m[4], m[5], ffn_gate[l], ffn_up_w[l], ffn_w2_b[l], final_g[None, :],
                           final, tiles[3])
            return xs, fins

        xc_new, ctx_states = layer(xc, m_ctx, zero_states, False, not last, (TC, 64, TC, TC), False)
        x, _ = layer(x, m_lat, ctx_states, True, True, (512, 128, 512, 1024), last)
        if not last:
            xc = xc_new
    return x
```
